```python
import math
import jax
import jax.numpy as jnp
from jax import lax
import numpy as np

D_MODEL = 2048
BATCH = 8
SEQ = 2048
DEPTH = 2

N_META = 16
BLOCK = 128
PREFIX = BLOCK
PAD = PREFIX - N_META
HEAD_DIM = 64
N_Q_HEADS = D_MODEL // 128
N_KV_HEADS = N_Q_HEADS // 4
GQA_GROUP = N_Q_HEADS // N_KV_HEADS
ATTN_WIDTH = N_Q_HEADS * HEAD_DIM
KV_WIDTH = N_KV_HEADS * HEAD_DIM
WINDOW = 128
NUM_BUCKETS = 32
MAX_DISTANCE = 128
ML_HEADS = 4
ML_V_WIDTH = D_MODEL // 2
ML_V_DIM = ML_V_WIDTH // ML_HEADS
ML_QK_DIM = ML_V_DIM // 2
ML_QK_WIDTH = ML_HEADS * ML_QK_DIM
CONV_WIDTH = 4
CHUNK = 64
D_FF = 11 * D_MODEL // 4
N_EXPERTS = 8
TOP_K = 2
D_FF_EXPERT = D_FF
MOE_BLOCK = 512
N_DENSE = (DEPTH + 1) // 2
N_MOE = DEPTH // 2
EPS = 1e-6
IN_SIZES = (ATTN_WIDTH, KV_WIDTH, KV_WIDTH, ML_QK_WIDTH, ML_QK_WIDTH, ML_V_WIDTH, ML_HEADS, ML_HEADS, ML_V_WIDTH, D_MODEL, D_MODEL)
D_IN = sum(IN_SIZES)
IN_OFFSETS = tuple(int(v) for v in np.cumsum(IN_SIZES)[:-1])

kernel_name = 'hybrid_swa_mlstm_moe_block'


def rmsnorm(x, g):
    xf = x.astype(jnp.float32)
    y = xf * lax.rsqrt(jnp.mean(xf * xf, axis=-1, keepdims=True) + EPS)
    return (y * g.astype(jnp.float32)).astype(x.dtype)


def t5_bucket(rel):
    n = jnp.maximum(rel, 0)
    max_exact = NUM_BUCKETS // 2
    large = max_exact + (jnp.log(jnp.maximum(n, 1).astype(jnp.float32) / max_exact)
                         / math.log(MAX_DISTANCE / max_exact) * (NUM_BUCKETS - max_exact)).astype(jnp.int32)
    large = jnp.minimum(large, NUM_BUCKETS - 1)
    return jnp.where(n < max_exact, n, large)


def swa_sink_attention(q, k, v, sinks, table):
    B, T, _ = q.shape
    NB = T // BLOCK
    q = q.reshape(B, NB, BLOCK, N_KV_HEADS, GQA_GROUP, HEAD_DIM)
    k = k.reshape(B, T, N_KV_HEADS, HEAD_DIM)
    v = v.reshape(B, T, N_KV_HEADS, HEAD_DIM)
    k_meta, v_meta = k[:, PAD:PREFIX], v[:, PAD:PREFIX]
    kb = k.reshape(B, NB, BLOCK, N_KV_HEADS, HEAD_DIM)
    vb = v.reshape(B, NB, BLOCK, N_KV_HEADS, HEAD_DIM)
    shift = ((0, 0), (1, 0), (0, 0), (0, 0), (0, 0))
    k_band = jnp.concatenate([jnp.pad(kb, shift)[:, :-1], kb], axis=2)
    v_band = jnp.concatenate([jnp.pad(vb, shift)[:, :-1], vb], axis=2)
    qi = jnp.arange(BLOCK)[:, None]
    ki = jnp.arange(2 * BLOCK)[None, :]
    blk = jnp.arange(NB)[:, None, None]
    rel_band = qi + BLOCK - ki
    mask_band = (rel_band >= 0) & (rel_band < WINDOW) & ((blk - 1) * BLOCK + ki >= PAD)
    bias_band = jnp.moveaxis(table.astype(jnp.float32)[t5_bucket(rel_band)], -1, 0)
    bias_band = bias_band.reshape(N_KV_HEADS, GQA_GROUP, BLOCK, 2 * BLOCK)
    rel_meta = blk * BLOCK + qi - (PAD + jnp.arange(N_META))
    mask_meta = rel_meta >= WINDOW
    bias_meta = jnp.moveaxis(table.astype(jnp.float32)[t5_bucket(rel_meta)], -1, 0)
    bias_meta = bias_meta.reshape(N_KV_HEADS, GQA_GROUP, NB, BLOCK, N_META)
    scale = HEAD_DIM ** -0.5
    s_band = jnp.einsum('bnqhgd,bnkhd->bhgnqk', q, k_band).astype(jnp.float32) * scale + bias_band[:, :, None]
    s_band = jnp.where(mask_band, s_band, -jnp.inf)
    s_meta = jnp.einsum('bnqhgd,bmhd->bhgnqm', q, k_meta).astype(jnp.float32) * scale + bias_meta
    s_meta = jnp.where(mask_meta, s_meta, -jnp.inf)
    sink = sinks.astype(jnp.float32).reshape(N_KV_HEADS, GQA_GROUP)[None, :, :, None, None, None]
    mx = jnp.maximum(jnp.maximum(s_band.max(-1, keepdims=True), s_meta.max(-1, keepdims=True)), sink)
    p_band = jnp.exp(s_band - mx)
    p_meta = jnp.exp(s_meta - mx)
    den = p_band.sum(-1, keepdims=True) + p_meta.sum(-1, keepdims=True) + jnp.exp(sink - mx)
    o = (jnp.einsum('bhgnqk,bnkhd->bnqhgd', (p_band / den).astype(v.dtype), v_band)
         + jnp.einsum('bhgnqm,bmhd->bnqhgd', (p_meta / den).astype(v.dtype), v_meta))
    return o.reshape(B, T, ATTN_WIDTH)


def causal_conv_silu(a, w, b):
    c = a.shape[-1]
    out = lax.conv_general_dilated(a, w[:, None, :].astype(a.dtype), window_strides=(1,),
                                   padding=[(CONV_WIDTH - 1, 0)],
                                   dimension_numbers=('NWC', 'WIO', 'NWC'), feature_group_count=c)
    return jax.nn.silu(out + b.astype(a.dtype))


def mlstm_chunkwise(q, k, v, ig, lf):
    B, H, T, dk = q.shape
    dv = v.shape[-1]
    NC = T // CHUNK

    def to_chunks(a):
        return jnp.moveaxis(a.reshape(B, H, NC, CHUNK, *a.shape[3:]), 2, 0)

    causal = jnp.tril(jnp.ones((CHUNK, CHUNK), dtype=bool))

    def step(carry, inp):
        C, n, m = carry
        qc, kc, vc, igc, lfc = inp
        b = jnp.cumsum(lfc, axis=-1)
        log_d = jnp.where(causal, b[..., :, None] - b[..., None, :] + igc[..., None, :], -jnp.inf)
        m_inter = b + m[..., None]
        m_out = jnp.maximum(m_inter, log_d.max(-1))
        d = jnp.exp(log_d - m_out[..., None])
        s = jnp.einsum('bhsk,bhrk->bhsr', qc, kc) * d
        inter = jnp.exp(m_inter - m_out)
        num = s @ vc + inter[..., None] * jnp.einsum('bhvk,bhsk->bhsv', C, qc)
        den = s.sum(-1) + inter * jnp.einsum('bhk,bhsk->bhs', n, qc)
        h = num / jnp.maximum(jnp.abs(den), jnp.exp(-m_out))[..., None]
        b_last = b[..., -1]
        log_w = b_last[..., None] - b + igc
        m_new = jnp.maximum(b_last + m, log_w.max(-1))
        decay = jnp.exp(b_last + m - m_new)
        w = jnp.exp(log_w - m_new[..., None])
        C_new = decay[..., None, None] * C + jnp.einsum('bhs,bhsv,bhsk->bhvk', w, vc, kc)
        n_new = decay[..., None] * n + jnp.einsum('bhs,bhsk->bhk', w, kc)
        return (C_new, n_new, m_new), h

    init = (jnp.zeros((B, H, dv, dk), jnp.float32), jnp.zeros((B, H, dk), jnp.float32),
            jnp.zeros((B, H), jnp.float32))
    _, hs = lax.scan(step, init, (to_chunks(q), to_chunks(k), to_chunks(v), to_chunks(ig), to_chunks(lf)))
    return jnp.moveaxis(hs, 0, 2).reshape(B, H, T, dv)


def hybrid_mixer(h, w_in, sinks, conv_w, conv_b, igate_b, fgate_b, norm_g, w_attn_up, w_mlstm_up, w_out, table):
    B, T, _ = h.shape
    z = h @ w_in
    aq, ak, av, mq, mk, mv, mi, mf, mo, ga, gm = jnp.split(z, IN_OFFSETS, axis=-1)
    attn = swa_sink_attention(aq, ak, av, sinks, table)
    qk = causal_conv_silu(jnp.concatenate([mq, mk], axis=-1), conv_w, conv_b)
    mq, mk = qk[..., :ML_QK_WIDTH], qk[..., ML_QK_WIDTH:]

    def heads(a, dim):
        return a.reshape(B, T, ML_HEADS, dim).transpose(0, 2, 1, 3).astype(jnp.float32)

    valid = (jnp.arange(T) >= PAD)[:, None]
    ig = jnp.where(valid, (mi + igate_b).astype(jnp.float32), -jnp.inf)
    lf = jnp.where(valid, jax.nn.log_sigmoid((mf + fgate_b).astype(jnp.float32)), 0.0)
    hm = mlstm_chunkwise(heads(mq, ML_QK_DIM), heads(mk, ML_QK_DIM) * ML_QK_DIM ** -0.5,
                         heads(mv, ML_V_DIM), ig.transpose(0, 2, 1), lf.transpose(0, 2, 1))
    hm = hm.transpose(0, 2, 1, 3)
    hm = hm * lax.rsqrt(jnp.mean(hm * hm, axis=-1, keepdims=True) + EPS)
    hm = (hm.reshape(B, T, ML_V_WIDTH) * norm_g.astype(jnp.float32)).astype(h.dtype)
    mlstm = jax.nn.sigmoid(mo) * hm
    y = jax.nn.sigmoid(ga) * (attn @ w_attn_up) + jax.nn.sigmoid(gm) * (mlstm @ w_mlstm_up)
    return y @ w_out


def swiglu(h, w_gate, w_up, w_down):
    return (jax.nn.silu(h @ w_gate) * (h @ w_up)) @ w_down


def moe_swiglu(h, w_router, b_router, w_gate, w_up, w_down):
    n_tok, d = h.shape
    logits = h.astype(jnp.float32) @ w_router.astype(jnp.float32) + b_router.astype(jnp.float32)
    top_logit, top_e = lax.top_k(logits, TOP_K)
    top_w = jax.nn.softmax(top_logit, axis=-1)
    n_assign = n_tok * TOP_K
    e_flat = top_e.reshape(n_assign)
    tok_flat = jnp.repeat(jnp.arange(n_tok, dtype=jnp.int32), TOP_K)
    w_flat = top_w.reshape(n_assign)
    order = jnp.argsort(e_flat)
    e_sorted = e_flat[order]
    counts = jnp.bincount(e_flat, length=N_EXPERTS)
    padded = (counts + MOE_BLOCK - 1) // MOE_BLOCK * MOE_BLOCK
    starts = jnp.cumsum(counts) - counts
    pad_ends = jnp.cumsum(padded)
    pad_starts = pad_ends - padded
    dest = pad_starts[e_sorted] + jnp.arange(n_assign) - starts[e_sorted]
    n_blocks = -(-n_assign // MOE_BLOCK) + N_EXPERTS
    n_rows = n_blocks * MOE_BLOCK
    row_tok = jnp.full((n_rows,), n_tok, jnp.int32).at[dest].set(tok_flat[order])
    row_w = jnp.zeros((n_rows,), jnp.float32).at[dest].set(w_flat[order])
    blk_e = jnp.minimum(jnp.searchsorted(pad_ends, jnp.arange(n_blocks) * MOE_BLOCK, side='right'), N_EXPERTS - 1)
    h_ext = jnp.concatenate([h, jnp.zeros((1, d), h.dtype)], axis=0)
    xb = h_ext[row_tok].reshape(n_blocks, MOE_BLOCK, d)

    def expert_block(args):
        xblk, e = args
        return (jax.nn.silu(xblk @ w_gate[e]) * (xblk @ w_up[e])) @ w_down[e]

    yb = lax.map(expert_block, (xb, blk_e)).reshape(n_rows, d)
    y = jnp.zeros((n_tok + 1, d), jnp.float32).at[row_tok].add(yb.astype(jnp.float32) * row_w[:, None])
    return y[:n_tok].astype(h.dtype)


def setup_inputs(seed: int = 0) -> dict:
    key = jax.random.key(seed)
    ks = jax.random.split(key, 24)
    f32 = jnp.float32

    def nrm(k, shape, fan_in):
        return jax.random.normal(k, shape, f32) * fan_in ** -0.5

    return {
        'x': jax.random.normal(ks[0], (BATCH, SEQ, D_MODEL), f32),
        'meta_tokens': jax.random.normal(ks[1], (N_META, D_MODEL), f32),
        'rel_bias_table': 0.5 * jax.random.normal(ks[2], (NUM_BUCKETS, N_Q_HEADS), f32),
        'w_in': nrm(ks[3], (DEPTH, D_MODEL, D_IN), D_MODEL),
        'attn_sinks': jax.random.normal(ks[4], (DEPTH, N_Q_HEADS), f32),
        'conv_w': nrm(ks[5], (DEPTH, CONV_WIDTH, 2 * ML_QK_WIDTH), CONV_WIDTH),
        'conv_b': 0.02 * jax.random.normal(ks[6], (DEPTH, 2 * ML_QK_WIDTH), f32),
        'igate_b': 0.1 * jax.random.normal(ks[7], (DEPTH, ML_HEADS), f32),
        'fgate_b': jnp.linspace(3.0, 6.0, ML_HEADS, dtype=f32)[None, :] + 0.1 * jax.random.normal(ks[8], (DEPTH, ML_HEADS), f32),
        'mlstm_norm_g': 1.0 + 0.02 * jax.random.normal(ks[9], (DEPTH, ML_V_WIDTH), f32),
        'w_attn_up': nrm(ks[10], (DEPTH, ATTN_WIDTH, D_MODEL), ATTN_WIDTH),
        'w_mlstm_up': nrm(ks[11], (DEPTH, ML_V_WIDTH, D_MODEL), ML_V_WIDTH),
        'w_out': nrm(ks[12], (DEPTH, D_MODEL, D_MODEL), D_MODEL),
        'norm_mix_g': 1.0 + 0.02 * jax.random.normal(ks[13], (DEPTH, D_MODEL), f32),
        'norm_ffn_g': 1.0 + 0.02 * jax.random.normal(ks[14], (DEPTH, D_MODEL), f32),
        'w_ffn_gate': nrm(ks[15], (N_DENSE, D_MODEL, D_FF), D_MODEL),
        'w_ffn_up': nrm(ks[16], (N_DENSE, D_MODEL, D_FF), D_MODEL),
        'w_ffn_down': nrm(ks[17], (N_DENSE, D_FF, D_MODEL), D_FF),
        'w_router': nrm(ks[18], (N_MOE, D_MODEL, N_EXPERTS), D_MODEL),
        'b_router': 0.01 * jax.random.normal(ks[19], (N_MOE, N_EXPERTS), f32),
        'w_moe_gate': nrm(ks[20], (N_MOE, N_EXPERTS, D_MODEL, D_FF_EXPERT), D_MODEL),
        'w_moe_up': nrm(ks[21], (N_MOE, N_EXPERTS, D_MODEL, D_FF_EXPERT), D_MODEL),
        'w_moe_down': nrm(ks[22], (N_MOE, N_EXPERTS, D_FF_EXPERT, D_MODEL), D_FF_EXPERT),
        'final_norm_g': 1.0 + 0.02 * jax.random.normal(ks[23], (D_MODEL,), f32),
    }


def reference(x, meta_tokens, rel_bias_table, w_in, attn_sinks, conv_w, conv_b, igate_b, fgate_b,
              mlstm_norm_g, w_attn_up, w_mlstm_up, w_out, norm_mix_g, norm_ffn_g, w_ffn_gate, w_ffn_up,
              w_ffn_down, w_router, b_router, w_moe_gate, w_moe_up, w_moe_down, final_norm_g):
    B = x.shape[0]
    prefix = jnp.concatenate([jnp.zeros((B, PAD, D_MODEL), x.dtype),
                              jnp.broadcast_to(meta_tokens.astype(x.dtype), (B, N_META, D_MODEL))], axis=1)
    xs = jnp.concatenate([prefix, x], axis=1)
    for layer in range(DEPTH):
        h = rmsnorm(xs, norm_mix_g[layer])
        xs = xs + hybrid_mixer(h, w_in[layer], attn_sinks[layer], conv_w[layer], conv_b[layer],
                               igate_b[layer], fgate_b[layer], mlstm_norm_g[layer], w_attn_up[layer],
                               w_mlstm_up[layer], w_out[layer], rel_bias_table)
        h = rmsnorm(xs, norm_ffn_g[layer])
        if layer % 2 == 0:
            i = layer // 2
            ffn = swiglu(h, w_ffn_gate[i], w_ffn_up[i], w_ffn_down[i])
        else:
            i = layer // 2
            ffn = moe_swiglu(h.reshape(-1, D_MODEL), w_router[i], b_router[i], w_moe_gate[i],
                             w_moe_up[i], w_moe_down[i]).reshape(h.shape)
        xs = xs + ffn
    return rmsnorm(xs, final_norm_g)[:, PREFIX:]
```

```python
import functools
import math

import jax
import jax.numpy as jnp
from jax import lax
from jax.experimental import pallas as pl
from jax.experimental.pallas import tpu as pltpu

D_MODEL = 2048
N_META = 16
BLOCK = 128
PREFIX = BLOCK
PAD = PREFIX - N_META
HEAD_DIM = 64
N_Q_HEADS = 16
N_KV_HEADS = 4
GQA_GROUP = 4
ATTN_WIDTH = N_Q_HEADS * HEAD_DIM
KV_WIDTH = N_KV_HEADS * HEAD_DIM
WINDOW = 128
NUM_BUCKETS = 32
MAX_DISTANCE = 128
ML_HEADS = 4
ML_V_WIDTH = D_MODEL // 2
ML_V_DIM = ML_V_WIDTH // ML_HEADS
ML_QK_DIM = ML_V_DIM // 2
ML_QK_WIDTH = ML_HEADS * ML_QK_DIM
CONV_WIDTH = 4
D_FF = 11 * D_MODEL // 4
N_EXPERTS = 8
TOP_K = 2
EPS = 1e-6

LANES = 128
SUBLANES = 8
BF16_ROWS = 16
VMEM_LIMIT = 56 * 1024 * 1024

Z_GA, Z_GM, Z_AQ, Z_MQK, Z_MV, Z_MO, Z_AK, Z_AV = 0, 2048, 4096, 5120, 6144, 7168, 8192, 8448
Z_WIDTH = 8704
ML_CHUNK = 128
CONV_HALO = 16
MOE_ROWS = 512
FFN_TILE = 512

F32 = jnp.float32
BF16 = jnp.bfloat16
NEG_INF = float("-inf")


def _tile(m, target):
    best = LANES
    for t in range(LANES, min(m, target) + 1, LANES):
        if m % t == 0:
            best = t
    assert m % best == 0
    return best


def _params(*sem):
    return pltpu.CompilerParams(dimension_semantics=sem, vmem_limit_bytes=VMEM_LIMIT)


def _rms(x, g):
    return x * lax.rsqrt(jnp.mean(x * x, axis=-1, keepdims=True) + EPS) * g


def _sigmoid(x):
    return 1.0 / (1.0 + jnp.exp(-x))


def _dot(a, b):
    return jnp.dot(a, b, preferred_element_type=F32)


def _dot_nt(a, b):
    return lax.dot_general(a, b, (((1,), (1,)), ((), ())), preferred_element_type=F32)


def _dot_tn(a, b):
    return lax.dot_general(a, b, (((0,), (0,)), ((), ())), preferred_element_type=F32)


def _norm_kernel(x_ref, g_ref, o_ref):
    o_ref[...] = _rms(x_ref[...], g_ref[...]).astype(o_ref.dtype)


def _rmsnorm(xs, g):
    m = xs.shape[0]
    bm = _tile(m, 512)
    return pl.pallas_call(
        _norm_kernel,
        out_shape=jax.ShapeDtypeStruct((m, D_MODEL), BF16),
        grid=(m // bm,),
        in_specs=[pl.BlockSpec((bm, D_MODEL), lambda i: (i, 0)),
                  pl.BlockSpec((1, D_MODEL), lambda i: (0, 0))],
        out_specs=pl.BlockSpec((bm, D_MODEL), lambda i: (i, 0)),
        compiler_params=_params("parallel"),
        name="rmsnorm",
    )(xs, g.reshape(1, D_MODEL))


def _mm_kernel(a_ref, w_ref, o_ref):
    o_ref[...] = _dot(a_ref[...], w_ref[...]).astype(o_ref.dtype)


def _matmul(a, w, out_dtype, bm, bn, name):
    m, k = a.shape
    n = w.shape[1]
    bm = _tile(m, bm)
    return pl.pallas_call(
        _mm_kernel,
        out_shape=jax.ShapeDtypeStruct((m, n), out_dtype),
        grid=(n // bn, m // bm),
        in_specs=[pl.BlockSpec((bm, k), lambda j, i: (i, 0)),
                  pl.BlockSpec((k, bn), lambda j, i: (0, j))],
        out_specs=pl.BlockSpec((bm, bn), lambda j, i: (i, j)),
        compiler_params=_params("parallel", "parallel"),
        name=name,
    )(a, w)


def _t5_bucket(rel):
    n = jnp.maximum(rel, 0)
    max_exact = NUM_BUCKETS // 2
    large = max_exact + (jnp.log(jnp.maximum(n, 1).astype(F32) / max_exact)
                         / math.log(MAX_DISTANCE / max_exact) * (NUM_BUCKETS - max_exact)).astype(jnp.int32)
    large = jnp.minimum(large, NUM_BUCKETS - 1)
    return jnp.where(n < max_exact, n, large)


def _attn_bias(table, nb):
    table = table.astype(F32)
    qi = jnp.arange(BLOCK)[:, None]
    ki = jnp.arange(2 * BLOCK)[None, :]
    rel_band = qi + BLOCK - ki
    blk3 = jnp.arange(3)[:, None, None]
    mask_band = (rel_band >= 0) & (rel_band < WINDOW) & ((blk3 - 1) * BLOCK + ki >= PAD)
    bias_band = jnp.moveaxis(table[_t5_bucket(rel_band)], -1, 0)
    band = jnp.where(mask_band[:, None], bias_band[None], NEG_INF)
    band = band.reshape(3, N_KV_HEADS, GQA_GROUP * BLOCK, 2 * BLOCK)
    blk = jnp.arange(nb)[:, None, None]
    key0 = jnp.arange(BLOCK)[None, None, :]
    rel_meta = blk * BLOCK + qi[None] - key0
    mask_meta = (rel_meta >= WINDOW) & (key0 >= PAD)
    bias_meta = jnp.moveaxis(table[_t5_bucket(rel_meta)], -1, 0)
    meta = jnp.where(mask_meta[None], bias_meta, NEG_INF)
    meta = jnp.moveaxis(meta, 1, 0).reshape(nb, N_KV_HEADS, GQA_GROUP * BLOCK, BLOCK)
    return band, meta


def _attn_kernel(q_ref, kp_ref, kc_ref, km_ref, vp_ref, vc_ref, vm_ref, bb_ref, bm_ref, sink_ref, o_ref):
    scale = HEAD_DIM ** -0.5
    q = q_ref[0]
    outs = [None] * N_Q_HEADS
    for h in range(N_KV_HEADS):
        cs = slice(h * HEAD_DIM, (h + 1) * HEAD_DIM)
        qh = jnp.concatenate(
            [q[:, (h * GQA_GROUP + g) * HEAD_DIM:(h * GQA_GROUP + g + 1) * HEAD_DIM] for g in range(GQA_GROUP)],
            axis=0)
        kb = jnp.concatenate([kp_ref[0, :, cs], kc_ref[0, :, cs]], axis=0)
        vb = jnp.concatenate([vp_ref[0, :, cs], vc_ref[0, :, cs]], axis=0)
        s_band = _dot_nt(qh, kb) * scale + bb_ref[0, h]
        s_meta = _dot_nt(qh, km_ref[0, :, cs]) * scale + bm_ref[0, h]
        sink = sink_ref[h]
        mx = jnp.maximum(jnp.maximum(s_band.max(-1, keepdims=True), s_meta.max(-1, keepdims=True)), sink)
        p_band = jnp.exp(s_band - mx)
        p_meta = jnp.exp(s_meta - mx)
        den = p_band.sum(-1, keepdims=True) + p_meta.sum(-1, keepdims=True) + jnp.exp(sink - mx)
        o = _dot(p_band.astype(BF16), vb) + _dot(p_meta.astype(BF16), vm_ref[0, :, cs])
        o = o / den
        for g in range(GQA_GROUP):
            outs[h * GQA_GROUP + g] = o[g * BLOCK:(g + 1) * BLOCK]
    o_ref[0] = jnp.concatenate(outs, axis=1).astype(o_ref.dtype)


def _attention(z3, band, meta, sinks):
    b, tp, _ = z3.shape
    nb = tp // BLOCK
    kcol, vcol = Z_AK // KV_WIDTH, Z_AV // KV_WIDTH
    sink_col = jnp.repeat(sinks.astype(F32).reshape(N_KV_HEADS, GQA_GROUP), BLOCK, axis=1)[..., None]
    kv = lambda col, f: pl.BlockSpec((1, BLOCK, KV_WIDTH), lambda bi, n: (bi, f(n), col))
    prev = lambda n: jnp.maximum(n - 1, 0)
    cur = lambda n: n
    first = lambda n: 0
    return pl.pallas_call(
        _attn_kernel,
        out_shape=jax.ShapeDtypeStruct((b, tp, ATTN_WIDTH), BF16),
        grid=(b, nb),
        in_specs=[pl.BlockSpec((1, BLOCK, ATTN_WIDTH), lambda bi, n: (bi, n, Z_AQ // ATTN_WIDTH)),
                  kv(kcol, prev), kv(kcol, cur), kv(kcol, first),
                  kv(vcol, prev), kv(vcol, cur), kv(vcol, first),
                  pl.BlockSpec((1, N_KV_HEADS, GQA_GROUP * BLOCK, 2 * BLOCK),
                               lambda bi, n: (jnp.minimum(n, 2), 0, 0, 0)),
                  pl.BlockSpec((1, N_KV_HEADS, GQA_GROUP * BLOCK, BLOCK), lambda bi, n: (n, 0, 0, 0)),
                  pl.BlockSpec((N_KV_HEADS, GQA_GROUP * BLOCK, 1), lambda bi, n: (0, 0, 0))],
        out_specs=pl.BlockSpec((1, BLOCK, ATTN_WIDTH), lambda bi, n: (bi, n, 0)),
        compiler_params=_params("parallel", "arbitrary"),
        name="swa_attention",
    )(z3, z3, z3, z3, z3, z3, z3, band, meta, sink_col)


def _mlstm_kernel(qk_ref, v_ref, gate_ref, mo_ref, cw_ref, cb_ref, gb_ref, ng_ref, o_ref,
                  xa_ref, ct_ref, n_ref, m_ref):
    L = ML_CHUNK
    n_chunks = qk_ref.shape[1] // L
    xa_ref[0:CONV_HALO, :] = jnp.zeros((CONV_HALO, 2 * ML_QK_WIDTH), F32)
    ct_ref[...] = jnp.zeros_like(ct_ref)
    n_ref[...] = jnp.zeros_like(n_ref)
    m_ref[...] = jnp.zeros_like(m_ref)
    ii = lax.broadcasted_iota(jnp.int32, (L, L), 0)
    jj = lax.broadcasted_iota(jnp.int32, (L, L), 1)
    causal = jj <= ii
    lane = lax.broadcasted_iota(jnp.int32, (L, LANES), 1)
    row = lax.broadcasted_iota(jnp.int32, (L, LANES), 0)
    k_scale = ML_QK_DIM ** -0.5

    def chunk(c, carry):
        t0 = pl.multiple_of(c * L, L)
        rows = pl.ds(t0, L)
        xa_ref[CONV_HALO:CONV_HALO + L, :] = qk_ref[0, rows, :].astype(F32)
        acc = cb_ref[...]
        for j in range(CONV_WIDTH):
            off = CONV_HALO - (CONV_WIDTH - 1) + j
            acc = acc + cw_ref[j:j + 1, :] * xa_ref[off:off + L, :]
        qk = acc * _sigmoid(acc)
        xa_ref[0:CONV_HALO, :] = xa_ref[L:L + CONV_HALO, :]
        gpb = gate_ref[0, rows, :] + gb_ref[...]
        valid = (row + t0) >= PAD
        log_sig = jnp.minimum(gpb, 0.0) - jnp.log1p(jnp.exp(-jnp.abs(gpb)))
        gx = jnp.where(lane < ML_HEADS, jnp.where(valid, gpb, NEG_INF), jnp.where(valid, log_sig, 0.0))
        gxt = gx.T
        for h in range(ML_HEADS):
            q = qk[:, h * ML_QK_DIM:(h + 1) * ML_QK_DIM].astype(BF16)
            k = qk[:, ML_QK_WIDTH + h * ML_QK_DIM:ML_QK_WIDTH + (h + 1) * ML_QK_DIM] * k_scale
            vcols = slice(h * ML_V_DIM, (h + 1) * ML_V_DIM)
            v = v_ref[0, rows, vcols]
            ig_col = gx[:, h:h + 1]
            lf_col = gx[:, ML_HEADS + h:ML_HEADS + h + 1]
            ig_row = gxt[h:h + 1, :]
            lf_row = gxt[ML_HEADS + h:ML_HEADS + h + 1, :]
            b_col = jnp.sum(jnp.where(causal, lf_row, 0.0), axis=-1, keepdims=True)
            b_row = jnp.sum(jnp.where(ii <= jj, lf_col, 0.0), axis=0, keepdims=True)
            m_prev = m_ref[h:h + 1, 0:1]
            log_d = jnp.where(causal, b_col - b_row + ig_row, NEG_INF)
            m_inter = b_col + m_prev
            m_out = jnp.maximum(m_inter, log_d.max(-1, keepdims=True))
            d = jnp.exp(log_d - m_out)
            s = _dot_nt(q, k.astype(BF16)) * d
            inter = jnp.exp(m_inter - m_out)
            ct = ct_ref[h]
            n_prev = n_ref[h:h + 1, :]
            num = _dot(s.astype(BF16), v) + inter * _dot(q, ct.astype(BF16))
            den = s.sum(-1, keepdims=True) + inter * jnp.sum(q.astype(F32) * n_prev, axis=-1, keepdims=True)
            hh = num / jnp.maximum(jnp.abs(den), jnp.exp(-m_out))
            b_last = b_col[L - 1:L, :]
            log_w = b_last - b_col + ig_col
            m_new = jnp.maximum(b_last + m_prev, log_w.max(0, keepdims=True))
            decay = jnp.exp(b_last + m_prev - m_new)
            kw = k * jnp.exp(log_w - m_new)
            ct_ref[h] = decay * ct + _dot_tn(kw.astype(BF16), v)
            n_ref[h:h + 1, :] = decay * n_prev + kw.sum(0, keepdims=True)
            m_ref[h:h + 1, :] = jnp.broadcast_to(m_new, (1, LANES))
            hn = hh * lax.rsqrt(jnp.mean(hh * hh, axis=-1, keepdims=True) + EPS) * ng_ref[:, vcols]
            o_ref[0, rows, vcols] = (_sigmoid(mo_ref[0, rows, vcols].astype(F32)) * hn).astype(o_ref.dtype)
        return carry

    lax.fori_loop(0, n_chunks, chunk, 0)


def _mlstm(z3, gates3, conv_w, conv_b, igate_b, fgate_b, norm_g):
    b, tp, _ = z3.shape
    gate_bias = jnp.zeros((1, LANES), F32).at[0, :ML_HEADS].set(igate_b).at[0, ML_HEADS:2 * ML_HEADS].set(fgate_b)
    col = lambda c: pl.BlockSpec((1, tp, ML_V_WIDTH), lambda bi: (bi, 0, c))
    full = lambda r, c: pl.BlockSpec((r, c), lambda bi: (0, 0))
    return pl.pallas_call(
        _mlstm_kernel,
        out_shape=jax.ShapeDtypeStruct((b, tp, ML_V_WIDTH), BF16),
        grid=(b,),
        in_specs=[col(Z_MQK // ML_V_WIDTH), col(Z_MV // ML_V_WIDTH),
                  pl.BlockSpec((1, tp, LANES), lambda bi: (bi, 0, 0)),
                  col(Z_MO // ML_V_WIDTH),
                  full(CONV_WIDTH, 2 * ML_QK_WIDTH), full(1, 2 * ML_QK_WIDTH), full(1, LANES), full(1, ML_V_WIDTH)],
        out_specs=pl.BlockSpec((1, tp, ML_V_WIDTH), lambda bi: (bi, 0, 0)),
        scratch_shapes=[pltpu.VMEM((ML_CHUNK + CONV_HALO, 2 * ML_QK_WIDTH), F32),
                        pltpu.VMEM((ML_HEADS, ML_QK_DIM, ML_V_DIM), F32),
                        pltpu.VMEM((SUBLANES, LANES), F32),
                        pltpu.VMEM((SUBLANES, LANES), F32)],
        compiler_params=_params("parallel"),
        name="mlstm",
    )(z3, z3, gates3, z3, conv_w.astype(F32), conv_b.reshape(1, -1).astype(F32), gate_bias,
      norm_g.reshape(1, -1).astype(F32))


def _merge_kernel(with_router, attn_ref, ml_ref, ga_ref, gm_ref, xs_ref, wa_ref, wm_ref, wo_ref, g_ref, *rest):
    if with_router:
        wr_ref, br_ref, xs_out, h_out, route_out = rest
    else:
        xs_out, h_out = rest
    a = _dot(attn_ref[...], wa_ref[...])
    m = _dot(ml_ref[...], wm_ref[...])
    y = _sigmoid(ga_ref[...].astype(F32)) * a + _sigmoid(gm_ref[...].astype(F32)) * m
    xs_new = xs_ref[...] + _dot(y.astype(BF16), wo_ref[...])
    xs_out[...] = xs_new
    hn = _rms(xs_new, g_ref[...])
    h_out[...] = hn.astype(h_out.dtype)
    if with_router:
        logits = jnp.dot(hn, wr_ref[...], preferred_element_type=F32, precision=lax.Precision.HIGHEST) + br_ref[...]
        lane = lax.broadcasted_iota(jnp.int32, logits.shape, 1).astype(F32)
        l1 = logits.max(-1, keepdims=True)
        i1 = jnp.min(jnp.where(logits == l1, lane, float(LANES)), axis=-1, keepdims=True)
        rest_logits = jnp.where(lane == i1, NEG_INF, logits)
        l2 = rest_logits.max(-1, keepdims=True)
        i2 = jnp.min(jnp.where(rest_logits == l2, lane, float(LANES)), axis=-1, keepdims=True)
        e = jnp.exp(l2 - l1)
        w1 = 1.0 / (1.0 + e)
        w2 = e / (1.0 + e)
        route_out[...] = jnp.where(lane == 0, i1, jnp.where(lane == 1, i2, jnp.where(lane == 2, w1,
                                   jnp.where(lane == 3, w2, 0.0))))


def _merge(attn, ml, z, xs, wa, wm, wo, g_next, router=None, h_dtype=BF16):
    m = xs.shape[0]
    bm = _tile(m, 256)
    row = lambda w, c: pl.BlockSpec((bm, w), lambda i: (i, c))
    const = lambda r, c: pl.BlockSpec((r, c), lambda i: (0, 0), pipeline_mode=pl.Buffered(1))
    in_specs = [row(ATTN_WIDTH, 0), row(ML_V_WIDTH, 0), row(D_MODEL, Z_GA // D_MODEL), row(D_MODEL, Z_GM // D_MODEL),
                row(D_MODEL, 0), const(ATTN_WIDTH, D_MODEL), const(ML_V_WIDTH, D_MODEL), const(D_MODEL, D_MODEL),
                const(1, D_MODEL)]
    args = [attn, ml, z, z, xs, wa, wm, wo, g_next.reshape(1, D_MODEL)]
    out_shape = [jax.ShapeDtypeStruct((m, D_MODEL), F32), jax.ShapeDtypeStruct((m, D_MODEL), h_dtype)]
    out_specs = [row(D_MODEL, 0), row(D_MODEL, 0)]
    if router is not None:
        w_router, b_router = router
        wr = jnp.zeros((D_MODEL, LANES), F32).at[:, :N_EXPERTS].set(w_router.astype(F32))
        br = jnp.full((1, LANES), NEG_INF, F32).at[0, :N_EXPERTS].set(b_router.astype(F32))
        in_specs += [const(D_MODEL, LANES), const(1, LANES)]
        args += [wr, br]
        out_shape.append(jax.ShapeDtypeStruct((m, LANES), F32))
        out_specs.append(row(LANES, 0))
    return pl.pallas_call(
        functools.partial(_merge_kernel, router is not None),
        out_shape=out_shape,
        grid=(m // bm,),
        in_specs=in_specs,
        out_specs=out_specs,
        compiler_params=_params("parallel"),
        name="merge_out_proj",
    )(*args)


def _ffn_kernel(h_ref, xs_ref, wg_ref, wu_ref, wd_ref, g_ref, xs_out, h_out):
    f = pl.program_id(1)

    @pl.when(f == 0)
    def _():
        xs_out[...] = xs_ref[...]

    hb = h_ref[...]
    g = _dot(hb, wg_ref[...])
    u = _dot(hb, wu_ref[...])
    act = (g * _sigmoid(g) * u).astype(BF16)
    xs_out[...] += _dot(act, wd_ref[...])

    @pl.when(f == pl.num_programs(1) - 1)
    def _():
        h_out[...] = _rms(xs_out[...], g_ref[...]).astype(h_out.dtype)


def _dense_ffn(h, xs, wg, wu, wd, g_next, bf=FFN_TILE):
    m = xs.shape[0]
    bm = _tile(m, 512)
    return pl.pallas_call(
        _ffn_kernel,
        out_shape=[jax.ShapeDtypeStruct((m, D_MODEL), F32), jax.ShapeDtypeStruct((m, D_MODEL), BF16)],
        grid=(m // bm, D_FF // bf),
        in_specs=[pl.BlockSpec((bm, D_MODEL), lambda i, f: (i, 0)),
                  pl.BlockSpec((bm, D_MODEL), lambda i, f: (i, 0)),
                  pl.BlockSpec((D_MODEL, bf), lambda i, f: (0, f)),
                  pl.BlockSpec((D_MODEL, bf), lambda i, f: (0, f)),
                  pl.BlockSpec((bf, D_MODEL), lambda i, f: (f, 0)),
                  pl.BlockSpec((1, D_MODEL), lambda i, f: (0, 0))],
        out_specs=[pl.BlockSpec((bm, D_MODEL), lambda i, f: (i, 0)),
                   pl.BlockSpec((bm, D_MODEL), lambda i, f: (i, 0))],
        compiler_params=_params("parallel", "arbitrary"),
        name="dense_swiglu",
    )(h, xs, wg, wu, wd, g_next.reshape(1, D_MODEL))


def _row_copy(src_hbm, dst_vmem, sem, src_row, dst_row):
    return pltpu.make_async_copy(src_hbm.at[pl.ds(src_row, 1), :], dst_vmem.at[pl.ds(dst_row, 1), :], sem)


def _moe_kernel(blk_e_ref, blk_valid_ref, row_tok_ref, h_hbm, wg_ref, wu_ref, wd_ref, y_ref, xbuf, xb16, sem):
    i = pl.program_id(0)
    f = pl.program_id(1)
    valid = blk_valid_ref[i] > 0

    @pl.when(jnp.logical_and(valid, f == 0))
    def _():
        def issue(r, c):
            _row_copy(h_hbm, xbuf, sem, row_tok_ref[i * MOE_ROWS + r], r).start()
            return c
        lax.fori_loop(0, MOE_ROWS, issue, 0)

        def drain(r, c):
            _row_copy(h_hbm, xbuf, sem, 0, r).wait()
            return c
        lax.fori_loop(0, MOE_ROWS, drain, 0)
        xb16[...] = xbuf[...].astype(BF16)

    @pl.when(jnp.logical_and(jnp.logical_not(valid), f == 0))
    def _():
        y_ref[...] = jnp.zeros_like(y_ref)

    @pl.when(valid)
    def _():
        xb = xb16[...]
        g = _dot(xb, wg_ref[0])
        u = _dot(xb, wu_ref[0])
        act = (g * _sigmoid(g) * u).astype(BF16)
        contrib = _dot(act, wd_ref[0])

        @pl.when(f == 0)
        def _():
            y_ref[...] = contrib

        @pl.when(f > 0)
        def _():
            y_ref[...] += contrib


def _moe_experts(h_f32, blk_e, blk_valid, row_tok, wg, wu, wd, n_blocks, bf=FFN_TILE):
    nf = D_FF // bf
    ftile = lambda i, f, be, bv, rt: jnp.where(bv[i] > 0, f, nf - 1)
    grid_spec = pltpu.PrefetchScalarGridSpec(
        num_scalar_prefetch=3,
        grid=(n_blocks, nf),
        in_specs=[pl.BlockSpec(memory_space=pl.ANY),
                  pl.BlockSpec((1, D_MODEL, bf), lambda i, f, be, bv, rt: (be[i], 0, ftile(i, f, be, bv, rt))),
                  pl.BlockSpec((1, D_MODEL, bf), lambda i, f, be, bv, rt: (be[i], 0, ftile(i, f, be, bv, rt))),
                  pl.BlockSpec((1, bf, D_MODEL), lambda i, f, be, bv, rt: (be[i], ftile(i, f, be, bv, rt), 0))],
        out_specs=pl.BlockSpec((MOE_ROWS, D_MODEL), lambda i, f, be, bv, rt: (i, 0)),
        scratch_shapes=[pltpu.VMEM((MOE_ROWS, D_MODEL), F32), pltpu.VMEM((MOE_ROWS, D_MODEL), BF16),
                        pltpu.SemaphoreType.DMA(())],
    )
    return pl.pallas_call(
        _moe_kernel,
        out_shape=jax.ShapeDtypeStruct((n_blocks * MOE_ROWS, D_MODEL), F32),
        grid_spec=grid_spec,
        compiler_params=_params("arbitrary", "arbitrary"),
        name="moe_experts",
    )(blk_e, blk_valid, row_tok, h_f32, wg, wu, wd)


def _combine_kernel(pos_ref, xs_ref, route_ref, y_hbm, g_ref, o_ref, ybuf, sem):
    bi = pl.program_id(0)
    t = pl.program_id(1)
    tp = pl.num_programs(1) * BLOCK + PREFIX
    base = bi * tp + PREFIX + t * BLOCK

    def issue(r, c):
        for k in range(TOP_K):
            _row_copy(y_hbm, ybuf.at[k], sem, pos_ref[TOP_K * (base + r) + k], r).start()
        return c
    lax.fori_loop(0, BLOCK, issue, 0)

    def drain(r, c):
        for k in range(TOP_K):
            _row_copy(y_hbm, ybuf.at[k], sem, 0, r).wait()
        return c
    lax.fori_loop(0, BLOCK, drain, 0)
    route = route_ref[0]
    moe = ybuf[0] * route[:, 2:3] + ybuf[1] * route[:, 3:4]
    o_ref[0] = _rms(xs_ref[0] + moe, g_ref[...])


def _moe_combine(pos, xs3, route3, yb, g_final):
    b, tp, _ = xs3.shape
    seq = tp - PREFIX
    grid_spec = pltpu.PrefetchScalarGridSpec(
        num_scalar_prefetch=1,
        grid=(b, seq // BLOCK),
        in_specs=[pl.BlockSpec((1, BLOCK, D_MODEL), lambda bi, t, p: (bi, t + 1, 0)),
                  pl.BlockSpec((1, BLOCK, LANES), lambda bi, t, p: (bi, t + 1, 0)),
                  pl.BlockSpec(memory_space=pl.ANY),
                  pl.BlockSpec((1, D_MODEL), lambda bi, t, p: (0, 0))],
        out_specs=pl.BlockSpec((1, BLOCK, D_MODEL), lambda bi, t, p: (bi, t, 0)),
        scratch_shapes=[pltpu.VMEM((TOP_K, BLOCK, D_MODEL), F32), pltpu.SemaphoreType.DMA(())],
    )
    return pl.pallas_call(
        _combine_kernel,
        out_shape=jax.ShapeDtypeStruct((b, seq, D_MODEL), F32),
        grid_spec=grid_spec,
        compiler_params=_params("arbitrary", "arbitrary"),
        name="moe_combine_final_norm",
    )(pos, xs3, route3, yb, g_final.reshape(1, D_MODEL))


def _moe_routing(route, n_tok):
    n_assign = n_tok * TOP_K
    n_blocks = -(-n_assign // MOE_ROWS) + N_EXPERTS
    e_flat = route[:, :TOP_K].astype(jnp.int32).reshape(n_assign)
    onehot = (e_flat[:, None] == jnp.arange(N_EXPERTS, dtype=jnp.int32)[None, :]).astype(jnp.int32)
    csum = jnp.cumsum(onehot, axis=0)
    rank = jnp.sum((csum - onehot) * onehot, axis=1)
    counts = csum[-1]
    padded = (counts + MOE_ROWS - 1) // MOE_ROWS * MOE_ROWS
    pad_ends = jnp.cumsum(padded)
    pad_starts = pad_ends - padded
    dest = (jnp.sum(pad_starts[None, :] * onehot, axis=1) + rank).astype(jnp.int32)
    row_tok = jnp.zeros((n_blocks * MOE_ROWS,), jnp.int32).at[dest].set(
        jnp.arange(n_assign, dtype=jnp.int32) // TOP_K)
    blk_start = jnp.arange(n_blocks, dtype=jnp.int32) * MOE_ROWS
    blk_valid = (blk_start < pad_ends[-1]).astype(jnp.int32)
    blk_e = jnp.sum((blk_start[:, None] >= pad_ends[None, :]).astype(jnp.int32), axis=1)
    last_e = jnp.sum((pad_ends[-1] - 1 >= pad_ends).astype(jnp.int32))
    blk_e = jnp.where(blk_valid > 0, blk_e, last_e).astype(jnp.int32)
    return dest, row_tok, blk_e, blk_valid, n_blocks


def _pack_w_in(w):
    o = {}
    off = 0
    for name, size in (("aq", ATTN_WIDTH), ("ak", KV_WIDTH), ("av", KV_WIDTH), ("mq", ML_QK_WIDTH),
                       ("mk", ML_QK_WIDTH), ("mv", ML_V_WIDTH), ("mi", ML_HEADS), ("mf", ML_HEADS),
                       ("mo", ML_V_WIDTH), ("ga", D_MODEL), ("gm", D_MODEL)):
        o[name] = w[:, off:off + size]
        off += size
    main = jnp.concatenate([o["ga"], o["gm"], o["aq"], o["mq"], o["mk"], o["mv"], o["mo"], o["ak"], o["av"]],
                           axis=1).astype(BF16)
    gates = jnp.concatenate([o["mi"], o["mf"], jnp.zeros((D_MODEL, LANES - 2 * ML_HEADS), w.dtype)], axis=1)
    return main, gates.astype(BF16)


def kernel(x, meta_tokens, rel_bias_table, w_in, attn_sinks, conv_w, conv_b, igate_b, fgate_b, mlstm_norm_g,
           w_attn_up, w_mlstm_up, w_out, norm_mix_g, norm_ffn_g, w_ffn_gate, w_ffn_up, w_ffn_down, w_router,
           b_router, w_moe_gate, w_moe_up, w_moe_down, final_norm_g):
    b, seq, _ = x.shape
    depth = w_in.shape[0]
    tp = PREFIX + seq
    m = b * tp
    assert depth == 2 and seq % BLOCK == 0 and tp % ML_CHUNK == 0
    prefix = jnp.concatenate([jnp.zeros((b, PAD, D_MODEL), x.dtype),
                              jnp.broadcast_to(meta_tokens.astype(x.dtype), (b, N_META, D_MODEL))], axis=1)
    xs = jnp.concatenate([prefix, x], axis=1).reshape(m, D_MODEL)
    band, meta = _attn_bias(rel_bias_table, tp // BLOCK)
    h = _rmsnorm(xs, norm_mix_g[0])
    out = None
    for layer in range(depth):
        w_main, w_gates = _pack_w_in(w_in[layer])
        z = _matmul(h, w_main, BF16, 512, Z_WIDTH // 4, "in_proj")
        gates = _matmul(h, w_gates, F32, 1024, LANES, "in_proj_gates")
        z3 = z.reshape(b, tp, Z_WIDTH)
        attn = _attention(z3, band, meta, attn_sinks[layer]).reshape(m, ATTN_WIDTH)
        ml = _mlstm(z3, gates.reshape(b, tp, LANES), conv_w[layer], conv_b[layer], igate_b[layer], fgate_b[layer],
                    mlstm_norm_g[layer]).reshape(m, ML_V_WIDTH)
        wa, wm, wo = (w_attn_up[layer].astype(BF16), w_mlstm_up[layer].astype(BF16), w_out[layer].astype(BF16))
        i = layer // 2
        if layer % 2 == 0:
            xs, h = _merge(attn, ml, z, xs, wa, wm, wo, norm_ffn_g[layer])
            xs, h = _dense_ffn(h, xs, w_ffn_gate[i].astype(BF16), w_ffn_up[i].astype(BF16),
                               w_ffn_down[i].astype(BF16), norm_mix_g[layer + 1])
        else:
            xs, hf, route = _merge(attn, ml, z, xs, wa, wm, wo, norm_ffn_g[layer],
                                   router=(w_router[i], b_router[i]), h_dtype=F32)
            dest, row_tok, blk_e, blk_valid, n_blocks = _moe_routing(route, m)
            yb = _moe_experts(hf, blk_e, blk_valid, row_tok, w_moe_gate[i].astype(BF16), w_moe_up[i].astype(BF16),
                              w_moe_down[i].astype(BF16), n_blocks)
            out = _moe_combine(dest, xs.reshape(b, tp, D_MODEL), route.reshape(b, tp, LANES), yb, final_norm_g)
    return out
```

```python
import functools
import math

import jax
import jax.numpy as jnp
from jax import lax
from jax.experimental import pallas as pl
from jax.experimental.pallas import tpu as pltpu

D_MODEL = 2048
N_META = 16
BLOCK = 128
PREFIX = BLOCK
PAD = PREFIX - N_META
HEAD_DIM = 64
N_Q_HEADS = 16
N_KV_HEADS = 4
GQA_GROUP = 4
ATTN_WIDTH = N_Q_HEADS * HEAD_DIM
KV_WIDTH = N_KV_HEADS * HEAD_DIM
WINDOW = 128
NUM_BUCKETS = 32
MAX_DISTANCE = 128
ML_HEADS = 4
ML_V_WIDTH = D_MODEL // 2
ML_V_DIM = ML_V_WIDTH // ML_HEADS
ML_QK_DIM = ML_V_DIM // 2
ML_QK_WIDTH = ML_HEADS * ML_QK_DIM
CONV_WIDTH = 4
D_FF = 11 * D_MODEL // 4
N_EXPERTS = 8
TOP_K = 2
EPS = 1e-6

LANES = 128
SUBLANES = 8
BF16_ROWS = 16
VMEM_LIMIT = 56 * 1024 * 1024

Z_GA, Z_GM, Z_AQ, Z_MQK, Z_MV, Z_MO, Z_AK, Z_AV = 0, 2048, 4096, 5120, 6144, 7168, 8192, 8448
Z_WIDTH = 8704
ML_CHUNK = 128
CONV_HALO = 16
FFN_TILE = 512
MOE_SB = 1024
MOE_HALF = MOE_SB // 2
MOE_STEP_ROWS = 96
MOE_GATHER_ROWS = MOE_STEP_ROWS * (D_FF // FFN_TILE)

F32 = jnp.float32
BF16 = jnp.bfloat16
NEG_INF = float("-inf")


def _tile(m, target):
    best = LANES
    for t in range(LANES, min(m, target) + 1, LANES):
        if m % t == 0:
            best = t
    assert m % best == 0
    return best


def _params(*sem):
    return pltpu.CompilerParams(dimension_semantics=sem, vmem_limit_bytes=VMEM_LIMIT)


def _rms(x, g):
    return x * lax.rsqrt(jnp.mean(x * x, axis=-1, keepdims=True) + EPS) * g


def _sigmoid(x):
    return 1.0 / (1.0 + jnp.exp(-x))


def _dot(a, b):
    return jnp.dot(a, b, preferred_element_type=F32)


def _dot_nt(a, b):
    return lax.dot_general(a, b, (((1,), (1,)), ((), ())), preferred_element_type=F32)


def _dot_tn(a, b):
    return lax.dot_general(a, b, (((0,), (0,)), ((), ())), preferred_element_type=F32)


def _norm_kernel(x_ref, g_ref, o_ref):
    o_ref[...] = _rms(x_ref[...], g_ref[...]).astype(o_ref.dtype)


def _rmsnorm(xs, g):
    m = xs.shape[0]
    bm = _tile(m, 512)
    return pl.pallas_call(
        _norm_kernel,
        out_shape=jax.ShapeDtypeStruct((m, D_MODEL), BF16),
        grid=(m // bm,),
        in_specs=[pl.BlockSpec((bm, D_MODEL), lambda i: (i, 0)),
                  pl.BlockSpec((1, D_MODEL), lambda i: (0, 0))],
        out_specs=pl.BlockSpec((bm, D_MODEL), lambda i: (i, 0)),
        compiler_params=_params("parallel"),
        name="rmsnorm",
    )(xs, g.reshape(1, D_MODEL))


def _mm_kernel(a_ref, w_ref, o_ref):
    o_ref[...] = _dot(a_ref[...], w_ref[...]).astype(o_ref.dtype)


def _matmul(a, w, out_dtype, bm, bn, name):
    m, k = a.shape
    n = w.shape[1]
    bm = _tile(m, bm)
    return pl.pallas_call(
        _mm_kernel,
        out_shape=jax.ShapeDtypeStruct((m, n), out_dtype),
        grid=(n // bn, m // bm),
        in_specs=[pl.BlockSpec((bm, k), lambda j, i: (i, 0)),
                  pl.BlockSpec((k, bn), lambda j, i: (0, j))],
        out_specs=pl.BlockSpec((bm, bn), lambda j, i: (i, j)),
        compiler_params=_params("parallel", "parallel"),
        name=name,
    )(a, w)


def _t5_bucket(rel):
    n = jnp.maximum(rel, 0)
    max_exact = NUM_BUCKETS // 2
    large = max_exact + (jnp.log(jnp.maximum(n, 1).astype(F32) / max_exact)
                         / math.log(MAX_DISTANCE / max_exact) * (NUM_BUCKETS - max_exact)).astype(jnp.int32)
    large = jnp.minimum(large, NUM_BUCKETS - 1)
    return jnp.where(n < max_exact, n, large)


def _bias_lookup(table, rel):
    onehot = (_t5_bucket(rel)[..., None] == jnp.arange(NUM_BUCKETS)).astype(F32)
    return jnp.einsum("...b,bh->h...", onehot, table.astype(F32), precision=lax.Precision.HIGHEST)


def _attn_bias(table, nb):
    qi = jnp.arange(BLOCK)[:, None]
    ki = jnp.arange(2 * BLOCK)[None, :]
    rel_band = qi + BLOCK - ki
    blk3 = jnp.arange(3)[:, None, None]
    mask_band = (rel_band >= 0) & (rel_band < WINDOW) & ((blk3 - 1) * BLOCK + ki >= PAD)
    band = jnp.where(mask_band[:, None], _bias_lookup(table, rel_band)[None], NEG_INF)
    band = band.reshape(3, N_KV_HEADS, GQA_GROUP * BLOCK, 2 * BLOCK)
    blk = jnp.arange(nb)[:, None, None]
    rel_meta = blk * BLOCK + qi[None] - (PAD + jnp.arange(N_META))
    meta = jnp.where((rel_meta >= WINDOW)[None], _bias_lookup(table, rel_meta), NEG_INF)
    meta = jnp.moveaxis(meta, 1, 0).reshape(nb, N_KV_HEADS, GQA_GROUP * BLOCK, N_META)
    meta = jnp.pad(meta, ((0, 0), (0, 0), (0, 0), (PAD, 0)), constant_values=NEG_INF)
    return band, meta


def _attn_kernel(q_ref, kp_ref, kc_ref, km_ref, vp_ref, vc_ref, vm_ref, bb_ref, bm_ref, sink_ref, o_ref):
    scale = HEAD_DIM ** -0.5
    q = q_ref[0]
    outs = [None] * N_Q_HEADS
    for h in range(N_KV_HEADS):
        cs = slice(h * HEAD_DIM, (h + 1) * HEAD_DIM)
        qh = jnp.concatenate(
            [q[:, (h * GQA_GROUP + g) * HEAD_DIM:(h * GQA_GROUP + g + 1) * HEAD_DIM] for g in range(GQA_GROUP)],
            axis=0)
        kb = jnp.concatenate([kp_ref[0, :, cs], kc_ref[0, :, cs]], axis=0)
        vb = jnp.concatenate([vp_ref[0, :, cs], vc_ref[0, :, cs]], axis=0)
        s_band = _dot_nt(qh, kb) * scale + bb_ref[0, h]
        s_meta = _dot_nt(qh, km_ref[0, :, cs]) * scale + bm_ref[0, h]
        sink = sink_ref[h]
        mx = jnp.maximum(jnp.maximum(s_band.max(-1, keepdims=True), s_meta.max(-1, keepdims=True)), sink)
        p_band = jnp.exp(s_band - mx)
        p_meta = jnp.exp(s_meta - mx)
        den = p_band.sum(-1, keepdims=True) + p_meta.sum(-1, keepdims=True) + jnp.exp(sink - mx)
        o = _dot(p_band.astype(BF16), vb) + _dot(p_meta.astype(BF16), vm_ref[0, :, cs])
        o = o / den
        for g in range(GQA_GROUP):
            outs[h * GQA_GROUP + g] = o[g * BLOCK:(g + 1) * BLOCK]
    o_ref[0] = jnp.concatenate(outs, axis=1).astype(o_ref.dtype)


def _attention(z3, band, meta, sinks):
    b, tp, _ = z3.shape
    nb = tp // BLOCK
    kcol, vcol = Z_AK // KV_WIDTH, Z_AV // KV_WIDTH
    sink_col = jnp.repeat(sinks.astype(F32).reshape(N_KV_HEADS, GQA_GROUP), BLOCK, axis=1)[..., None]
    kv = lambda col, f: pl.BlockSpec((1, BLOCK, KV_WIDTH), lambda bi, n: (bi, f(n), col))
    prev = lambda n: jnp.maximum(n - 1, 0)
    cur = lambda n: n
    first = lambda n: 0
    return pl.pallas_call(
        _attn_kernel,
        out_shape=jax.ShapeDtypeStruct((b, tp, ATTN_WIDTH), BF16),
        grid=(b, nb),
        in_specs=[pl.BlockSpec((1, BLOCK, ATTN_WIDTH), lambda bi, n: (bi, n, Z_AQ // ATTN_WIDTH)),
                  kv(kcol, prev), kv(kcol, cur), kv(kcol, first),
                  kv(vcol, prev), kv(vcol, cur), kv(vcol, first),
                  pl.BlockSpec((1, N_KV_HEADS, GQA_GROUP * BLOCK, 2 * BLOCK),
                               lambda bi, n: (jnp.minimum(n, 2), 0, 0, 0)),
                  pl.BlockSpec((1, N_KV_HEADS, GQA_GROUP * BLOCK, BLOCK), lambda bi, n: (n, 0, 0, 0)),
                  pl.BlockSpec((N_KV_HEADS, GQA_GROUP * BLOCK, 1), lambda bi, n: (0, 0, 0))],
        out_specs=pl.BlockSpec((1, BLOCK, ATTN_WIDTH), lambda bi, n: (bi, n, 0)),
        compiler_params=_params("parallel", "arbitrary"),
        name="swa_attention",
    )(z3, z3, z3, z3, z3, z3, z3, band, meta, sink_col)


def _mlstm_kernel(qk_ref, v_ref, gate_ref, mo_ref, cw_ref, cb_ref, gb_ref, ng_ref, o_ref,
                  xa_ref, ct_ref, n_ref, m_ref):
    L = ML_CHUNK
    n_chunks = qk_ref.shape[1] // L
    xa_ref[0:CONV_HALO, :] = jnp.zeros((CONV_HALO, 2 * ML_QK_WIDTH), F32)
    ct_ref[...] = jnp.zeros_like(ct_ref)
    n_ref[...] = jnp.zeros_like(n_ref)
    m_ref[...] = jnp.zeros_like(m_ref)
    ii = lax.broadcasted_iota(jnp.int32, (L, L), 0)
    jj = lax.broadcasted_iota(jnp.int32, (L, L), 1)
    causal = jj <= ii
    lane = lax.broadcasted_iota(jnp.int32, (L, LANES), 1)
    row = lax.broadcasted_iota(jnp.int32, (L, LANES), 0)
    k_scale = ML_QK_DIM ** -0.5

    def chunk(c, carry):
        t0 = pl.multiple_of(c * L, L)
        rows = pl.ds(t0, L)
        xa_ref[CONV_HALO:CONV_HALO + L, :] = qk_ref[0, rows, :].astype(F32)
        acc = cb_ref[...]
        for j in range(CONV_WIDTH):
            off = CONV_HALO - (CONV_WIDTH - 1) + j
            acc = acc + cw_ref[j:j + 1, :] * xa_ref[off:off + L, :]
        qk = acc * _sigmoid(acc)
        xa_ref[0:CONV_HALO, :] = xa_ref[L:L + CONV_HALO, :]
        gpb = gate_ref[0, rows, :] + gb_ref[...]
        valid = (row + t0) >= PAD
        log_sig = jnp.minimum(gpb, 0.0) - jnp.log1p(jnp.exp(-jnp.abs(gpb)))
        gx = jnp.where(lane < ML_HEADS, jnp.where(valid, gpb, NEG_INF), jnp.where(valid, log_sig, 0.0))
        gxt = gx.T
        for h in range(ML_HEADS):
            q = qk[:, h * ML_QK_DIM:(h + 1) * ML_QK_DIM].astype(BF16)
            k = qk[:, ML_QK_WIDTH + h * ML_QK_DIM:ML_QK_WIDTH + (h + 1) * ML_QK_DIM] * k_scale
            vcols = slice(h * ML_V_DIM, (h + 1) * ML_V_DIM)
            v = v_ref[0, rows, vcols]
            ig_col = gx[:, h:h + 1]
            lf_col = gx[:, ML_HEADS + h:ML_HEADS + h + 1]
            ig_row = gxt[h:h + 1, :]
            lf_row = gxt[ML_HEADS + h:ML_HEADS + h + 1, :]
            b_col = jnp.sum(jnp.where(causal, lf_row, 0.0), axis=-1, keepdims=True)
            b_row = jnp.sum(jnp.where(ii <= jj, lf_col, 0.0), axis=0, keepdims=True)
            m_prev = m_ref[h:h + 1, 0:1]
            log_d = jnp.where(causal, b_col - b_row + ig_row, NEG_INF)
            m_inter = b_col + m_prev
            m_out = jnp.maximum(m_inter, log_d.max(-1, keepdims=True))
            d = jnp.exp(log_d - m_out)
            s = _dot_nt(q, k.astype(BF16)) * d
            inter = jnp.exp(m_inter - m_out)
            ct = ct_ref[h]
            n_prev = n_ref[h:h + 1, :]
            num = _dot(s.astype(BF16), v) + inter * _dot(q, ct.astype(BF16))
            den = s.sum(-1, keepdims=True) + inter * jnp.sum(q.astype(F32) * n_prev, axis=-1, keepdims=True)
            hh = num / jnp.maximum(jnp.abs(den), jnp.exp(-m_out))
            b_last = b_col[L - 1:L, :]
            log_w = b_last - b_col + ig_col
            m_new = jnp.maximum(b_last + m_prev, log_w.max(0, keepdims=True))
            decay = jnp.exp(b_last + m_prev - m_new)
            kw = k * jnp.exp(log_w - m_new)
            ct_ref[h] = decay * ct + _dot_tn(kw.astype(BF16), v)
            n_ref[h:h + 1, :] = decay * n_prev + kw.sum(0, keepdims=True)
            m_ref[h:h + 1, :] = jnp.broadcast_to(m_new, (1, LANES))
            hn = hh * lax.rsqrt(jnp.mean(hh * hh, axis=-1, keepdims=True) + EPS) * ng_ref[:, vcols]
            o_ref[0, rows, vcols] = (_sigmoid(mo_ref[0, rows, vcols].astype(F32)) * hn).astype(o_ref.dtype)
        return carry

    lax.fori_loop(0, n_chunks, chunk, 0)


def _mlstm(z3, gates3, conv_w, conv_b, igate_b, fgate_b, norm_g):
    b, tp, _ = z3.shape
    gate_bias = jnp.zeros((1, LANES), F32).at[0, :ML_HEADS].set(igate_b).at[0, ML_HEADS:2 * ML_HEADS].set(fgate_b)
    col = lambda c: pl.BlockSpec((1, tp, ML_V_WIDTH), lambda bi: (bi, 0, c))
    full = lambda r, c: pl.BlockSpec((r, c), lambda bi: (0, 0))
    return pl.pallas_call(
        _mlstm_kernel,
        out_shape=jax.ShapeDtypeStruct((b, tp, ML_V_WIDTH), BF16),
        grid=(b,),
        in_specs=[col(Z_MQK // ML_V_WIDTH), col(Z_MV // ML_V_WIDTH),
                  pl.BlockSpec((1, tp, LANES), lambda bi: (bi, 0, 0)),
                  col(Z_MO // ML_V_WIDTH),
                  full(CONV_WIDTH, 2 * ML_QK_WIDTH), full(1, 2 * ML_QK_WIDTH), full(1, LANES), full(1, ML_V_WIDTH)],
        out_specs=pl.BlockSpec((1, tp, ML_V_WIDTH), lambda bi: (bi, 0, 0)),
        scratch_shapes=[pltpu.VMEM((ML_CHUNK + CONV_HALO, 2 * ML_QK_WIDTH), F32),
                        pltpu.VMEM((ML_HEADS, ML_QK_DIM, ML_V_DIM), F32),
                        pltpu.VMEM((SUBLANES, LANES), F32),
                        pltpu.VMEM((SUBLANES, LANES), F32)],
        compiler_params=_params("parallel"),
        name="mlstm",
    )(z3, z3, gates3, z3, conv_w.astype(F32), conv_b.reshape(1, -1).astype(F32), gate_bias,
      norm_g.reshape(1, -1).astype(F32))


def _merge_kernel(with_router, attn_ref, ml_ref, ga_ref, gm_ref, xs_ref, wa_ref, wm_ref, wo_ref, g_ref, *rest):
    if with_router:
        wr_hi_ref, wr_lo_ref, br_ref, xs_out, h_out, route_out = rest
    else:
        xs_out, h_out = rest
    a = _dot(attn_ref[...], wa_ref[...])
    m = _dot(ml_ref[...], wm_ref[...])
    y = _sigmoid(ga_ref[...].astype(F32)) * a + _sigmoid(gm_ref[...].astype(F32)) * m
    xs_new = xs_ref[...] + _dot(y.astype(BF16), wo_ref[...])
    xs_out[...] = xs_new
    hn = _rms(xs_new, g_ref[...])
    if not with_router:
        h_out[...] = hn.astype(h_out.dtype)
    else:
        half = D_MODEL // 2
        lo = lax.bitcast_convert_type(hn[:, :half].astype(BF16).astype(F32), jnp.uint32) >> 16
        hi = lax.bitcast_convert_type(hn[:, half:].astype(BF16).astype(F32), jnp.uint32) & jnp.uint32(0xFFFF0000)
        h_out[...] = lo | hi
        hn_hi = hn.astype(BF16)
        hn_lo = (hn - hn_hi.astype(F32)).astype(BF16)
        logits = (_dot(hn_hi, wr_hi_ref[...]) + (_dot(hn_lo, wr_hi_ref[...]) + _dot(hn_hi, wr_lo_ref[...]))
                  + br_ref[...])
        lane = lax.broadcasted_iota(jnp.int32, logits.shape, 1).astype(F32)
        l1 = logits.max(-1, keepdims=True)
        i1 = jnp.min(jnp.where(logits == l1, lane, float(LANES)), axis=-1, keepdims=True)
        rest_logits = jnp.where(lane == i1, NEG_INF, logits)
        l2 = rest_logits.max(-1, keepdims=True)
        i2 = jnp.min(jnp.where(rest_logits == l2, lane, float(LANES)), axis=-1, keepdims=True)
        e = jnp.exp(l2 - l1)
        w1 = 1.0 / (1.0 + e)
        w2 = e / (1.0 + e)
        route_out[...] = jnp.where(lane == 0, i1, jnp.where(lane == 1, i2, jnp.where(lane == 2, w1,
                                   jnp.where(lane == 3, w2, 0.0))))


def _merge(attn, ml, z, xs, wa, wm, wo, g_next, router=None):
    m = xs.shape[0]
    bm = _tile(m, 256)
    row = lambda w, c: pl.BlockSpec((bm, w), lambda i: (i, c))
    const = lambda r, c: pl.BlockSpec((r, c), lambda i: (0, 0), pipeline_mode=pl.Buffered(1))
    in_specs = [row(ATTN_WIDTH, 0), row(ML_V_WIDTH, 0), row(D_MODEL, Z_GA // D_MODEL), row(D_MODEL, Z_GM // D_MODEL),
                row(D_MODEL, 0), const(ATTN_WIDTH, D_MODEL), const(ML_V_WIDTH, D_MODEL), const(D_MODEL, D_MODEL),
                const(1, D_MODEL)]
    args = [attn, ml, z, z, xs, wa, wm, wo, g_next.reshape(1, D_MODEL)]
    if router is None:
        out_shape = [jax.ShapeDtypeStruct((m, D_MODEL), F32), jax.ShapeDtypeStruct((m, D_MODEL), BF16)]
        out_specs = [row(D_MODEL, 0), row(D_MODEL, 0)]
    else:
        out_shape = [jax.ShapeDtypeStruct((m, D_MODEL), F32), jax.ShapeDtypeStruct((m, D_MODEL // 2), jnp.uint32)]
        out_specs = [row(D_MODEL, 0), row(D_MODEL // 2, 0)]
    if router is not None:
        w_router, b_router = router
        wr = jnp.zeros((D_MODEL, LANES), F32).at[:, :N_EXPERTS].set(w_router.astype(F32))
        br = jnp.full((1, LANES), NEG_INF, F32).at[0, :N_EXPERTS].set(b_router.astype(F32))
        wr_hi = wr.astype(BF16)
        wr_lo = (wr - wr_hi.astype(F32)).astype(BF16)
        in_specs += [const(D_MODEL, LANES), const(D_MODEL, LANES), const(1, LANES)]
        args += [wr_hi, wr_lo, br]
        out_shape.append(jax.ShapeDtypeStruct((m, LANES), F32))
        out_specs.append(row(LANES, 0))
    return pl.pallas_call(
        functools.partial(_merge_kernel, router is not None),
        out_shape=out_shape,
        grid=(m // bm,),
        in_specs=in_specs,
        out_specs=out_specs,
        compiler_params=_params("parallel"),
        name="merge_out_proj",
    )(*args)


def _ffn_kernel(h_ref, xs_ref, wg_ref, wu_ref, wd_ref, g_ref, xs_out, h_out):
    f = pl.program_id(1)

    @pl.when(f == 0)
    def _():
        xs_out[...] = xs_ref[...]

    hb = h_ref[...]
    g = _dot(hb, wg_ref[...])
    u = _dot(hb, wu_ref[...])
    act = (g * _sigmoid(g) * u).astype(BF16)
    xs_out[...] += _dot(act, wd_ref[...])

    @pl.when(f == pl.num_programs(1) - 1)
    def _():
        h_out[...] = _rms(xs_out[...], g_ref[...]).astype(h_out.dtype)


def _dense_ffn(h, xs, wg, wu, wd, g_next, bf=FFN_TILE):
    m = xs.shape[0]
    bm = _tile(m, 512)
    return pl.pallas_call(
        _ffn_kernel,
        out_shape=[jax.ShapeDtypeStruct((m, D_MODEL), F32), jax.ShapeDtypeStruct((m, D_MODEL), BF16)],
        grid=(m // bm, D_FF // bf),
        in_specs=[pl.BlockSpec((bm, D_MODEL), lambda i, f: (i, 0)),
                  pl.BlockSpec((bm, D_MODEL), lambda i, f: (i, 0)),
                  pl.BlockSpec((D_MODEL, bf), lambda i, f: (0, f)),
                  pl.BlockSpec((D_MODEL, bf), lambda i, f: (0, f)),
                  pl.BlockSpec((bf, D_MODEL), lambda i, f: (f, 0)),
                  pl.BlockSpec((1, D_MODEL), lambda i, f: (0, 0))],
        out_specs=[pl.BlockSpec((bm, D_MODEL), lambda i, f: (i, 0)),
                   pl.BlockSpec((bm, D_MODEL), lambda i, f: (i, 0))],
        compiler_params=_params("parallel", "arbitrary"),
        name="dense_swiglu",
    )(h, xs, wg, wu, wd, g_next.reshape(1, D_MODEL))


def _row_copy(src_hbm, dst_vmem, sem, src_row, dst_row):
    return pltpu.make_async_copy(src_hbm.at[pl.ds(src_row, 1), :], dst_vmem.at[pl.ds(dst_row, 1), :], sem)


def _moe_gather_copy(h_hbm, gbuf, sem, slot, src_row, dst_row):
    return pltpu.make_async_copy(h_hbm.at[pl.ds(src_row, 1), :], gbuf.at[slot, pl.ds(dst_row, 1), :], sem.at[slot])


def _moe_gather_wait(h_hbm, gbuf, sem, slot):
    pltpu.make_async_copy(h_hbm.at[pl.ds(0, MOE_GATHER_ROWS), :], gbuf.at[slot], sem.at[slot]).wait()


def _moe_kernel(sb_e_ref, sb_rows_ref, row_tok_ref, h_hbm, wg_ref, wu_ref, wd_ref, y_ref, gbuf, xb16, sem):
    sb = pl.program_id(0)
    f = pl.program_id(1)
    n_sb = pl.num_programs(0)
    nf = pl.num_programs(1)
    kind = sb_rows_ref[sb]
    slot = sb % 2
    half = D_MODEL // 2

    @pl.when(jnp.logical_and(sb == 0, f == 0))
    def _():
        def issue(r, c):
            _moe_gather_copy(h_hbm, gbuf, sem, 0, row_tok_ref[r], r).start()
            return c
        lax.fori_loop(0, MOE_GATHER_ROWS, issue, 0)

    @pl.when(f == 0)
    def _():
        y_ref[...] = jnp.zeros_like(y_ref)
        prev_kind = sb_rows_ref[jnp.maximum(sb - 1, 0)]

        @pl.when(jnp.logical_or(sb == 0, prev_kind > 0))
        def _():
            _moe_gather_wait(h_hbm, gbuf, sem, slot)

        @pl.when(kind > 0)
        def _():
            w = gbuf[slot, 0:MOE_SB, :]
            xb16[:, :half] = lax.bitcast_convert_type(w << 16, F32).astype(BF16)
            xb16[:, half:] = lax.bitcast_convert_type(w & jnp.uint32(0xFFFF0000), F32).astype(BF16)

    def compute(rows):
        base = (sb + 1) * MOE_SB + f * MOE_STEP_ROWS
        for r in range(MOE_STEP_ROWS):
            _moe_gather_copy(h_hbm, gbuf, sem, 1 - slot, row_tok_ref[base + r], f * MOE_STEP_ROWS + r).start()
        xb = xb16[0:rows, :]
        g = _dot(xb, wg_ref[0])
        u = _dot(xb, wu_ref[0])
        act = (g * _sigmoid(g) * u).astype(BF16)
        y_ref[0:rows, :] += _dot(act, wd_ref[0])

    @pl.when(kind == 2)
    def _():
        compute(MOE_SB)

    @pl.when(kind == 1)
    def _():
        compute(MOE_HALF)

    @pl.when(jnp.logical_and(jnp.logical_and(sb == n_sb - 1, f == nf - 1), kind > 0))
    def _():
        _moe_gather_wait(h_hbm, gbuf, sem, 1 - slot)


def _moe_experts(h_packed, sb_e, sb_rows, row_tok, wg, wu, wd, n_sb, bf=FFN_TILE):
    nf = D_FF // bf
    assert nf * MOE_STEP_ROWS == MOE_GATHER_ROWS
    ftile = lambda i, f, rows: jnp.where(rows[i] > 0, f, nf - 1)
    grid_spec = pltpu.PrefetchScalarGridSpec(
        num_scalar_prefetch=3,
        grid=(n_sb, nf),
        in_specs=[pl.BlockSpec(memory_space=pl.ANY),
                  pl.BlockSpec((1, D_MODEL, bf), lambda i, f, se, sr, rt: (se[i], 0, ftile(i, f, sr))),
                  pl.BlockSpec((1, D_MODEL, bf), lambda i, f, se, sr, rt: (se[i], 0, ftile(i, f, sr))),
                  pl.BlockSpec((1, bf, D_MODEL), lambda i, f, se, sr, rt: (se[i], ftile(i, f, sr), 0))],
        out_specs=pl.BlockSpec((MOE_SB, D_MODEL), lambda i, f, se, sr, rt: (i, 0)),
        scratch_shapes=[pltpu.VMEM((2, MOE_GATHER_ROWS, D_MODEL // 2), jnp.uint32),
                        pltpu.VMEM((MOE_SB, D_MODEL), BF16),
                        pltpu.SemaphoreType.DMA((2,))],
    )
    return pl.pallas_call(
        _moe_kernel,
        out_shape=jax.ShapeDtypeStruct((n_sb * MOE_SB, D_MODEL), F32),
        grid_spec=grid_spec,
        compiler_params=_params("arbitrary", "arbitrary"),
        name="moe_experts",
    )(sb_e, sb_rows, row_tok, h_packed, wg, wu, wd)


def _combine_kernel(pos_ref, xs_ref, route_ref, y_hbm, g_ref, o_ref, ybuf, sem):
    bi = pl.program_id(0)
    t = pl.program_id(1)
    tp = pl.num_programs(1) * BLOCK + PREFIX
    base = bi * tp + PREFIX + t * BLOCK

    def issue(r, c):
        for k in range(TOP_K):
            _row_copy(y_hbm, ybuf.at[k], sem, pos_ref[TOP_K * (base + r) + k], r).start()
        return c
    lax.fori_loop(0, BLOCK, issue, 0)

    def drain(r, c):
        for k in range(TOP_K):
            _row_copy(y_hbm, ybuf.at[k], sem, 0, r).wait()
        return c
    lax.fori_loop(0, BLOCK, drain, 0)
    route = route_ref[0]
    moe = ybuf[0] * route[:, 2:3] + ybuf[1] * route[:, 3:4]
    o_ref[0] = _rms(xs_ref[0] + moe, g_ref[...])


def _moe_combine(pos, xs3, route3, yb, g_final):
    b, tp, _ = xs3.shape
    seq = tp - PREFIX
    grid_spec = pltpu.PrefetchScalarGridSpec(
        num_scalar_prefetch=1,
        grid=(b, seq // BLOCK),
        in_specs=[pl.BlockSpec((1, BLOCK, D_MODEL), lambda bi, t, p: (bi, t + 1, 0)),
                  pl.BlockSpec((1, BLOCK, LANES), lambda bi, t, p: (bi, t + 1, 0)),
                  pl.BlockSpec(memory_space=pl.ANY),
                  pl.BlockSpec((1, D_MODEL), lambda bi, t, p: (0, 0))],
        out_specs=pl.BlockSpec((1, BLOCK, D_MODEL), lambda bi, t, p: (bi, t, 0)),
        scratch_shapes=[pltpu.VMEM((TOP_K, BLOCK, D_MODEL), F32), pltpu.SemaphoreType.DMA(())],
    )
    return pl.pallas_call(
        _combine_kernel,
        out_shape=jax.ShapeDtypeStruct((b, seq, D_MODEL), F32),
        grid_spec=grid_spec,
        compiler_params=_params("arbitrary", "arbitrary"),
        name="moe_combine_final_norm",
    )(pos, xs3, route3, yb, g_final.reshape(1, D_MODEL))


def _moe_routing(route, n_tok):
    n_assign = n_tok * TOP_K
    n_sb = n_assign // MOE_SB + N_EXPERTS
    e_flat = route[:, :TOP_K].astype(jnp.int32).reshape(n_assign)
    onehot = (e_flat[:, None] == jnp.arange(N_EXPERTS, dtype=jnp.int32)[None, :]).astype(jnp.int32)
    csum = jnp.cumsum(onehot, axis=0)
    rank = jnp.sum((csum - onehot) * onehot, axis=1)
    counts = csum[-1]
    sb_count = (counts + MOE_SB - 1) // MOE_SB
    sb_end = jnp.cumsum(sb_count)
    sb_start = sb_end - sb_count
    dest = (jnp.sum((sb_start * MOE_SB)[None, :] * onehot, axis=1) + rank).astype(jnp.int32)
    row_tok = jnp.zeros(((n_sb + 2) * MOE_SB,), jnp.int32).at[dest].set(
        jnp.arange(n_assign, dtype=jnp.int32) // TOP_K)
    sb = jnp.arange(n_sb, dtype=jnp.int32)
    sb_e = jnp.sum((sb[:, None] >= sb_end[None, :]).astype(jnp.int32), axis=1)
    valid = sb < sb_end[-1]
    last_e = jnp.sum((sb_end[-1] - 1 >= sb_end).astype(jnp.int32))
    sb_e = jnp.where(valid, sb_e, last_e).astype(jnp.int32)
    rows_here = counts[sb_e] - (sb - sb_start[sb_e]) * MOE_SB
    sb_rows = jnp.where(valid, jnp.where(rows_here > MOE_HALF, 2, 1), 0).astype(jnp.int32)
    return dest, row_tok, sb_e, sb_rows, n_sb


def _pack_w_in(w):
    o = {}
    off = 0
    for name, size in (("aq", ATTN_WIDTH), ("ak", KV_WIDTH), ("av", KV_WIDTH), ("mq", ML_QK_WIDTH),
                       ("mk", ML_QK_WIDTH), ("mv", ML_V_WIDTH), ("mi", ML_HEADS), ("mf", ML_HEADS),
                       ("mo", ML_V_WIDTH), ("ga", D_MODEL), ("gm", D_MODEL)):
        o[name] = w[:, off:off + size]
        off += size
    main = jnp.concatenate([o["ga"], o["gm"], o["aq"], o["mq"], o["mk"], o["mv"], o["mo"], o["ak"], o["av"]],
                           axis=1).astype(BF16)
    gates = jnp.concatenate([o["mi"], o["mf"], jnp.zeros((D_MODEL, LANES - 2 * ML_HEADS), w.dtype)], axis=1)
    return main, gates.astype(BF16)


def kernel(x, meta_tokens, rel_bias_table, w_in, attn_sinks, conv_w, conv_b, igate_b, fgate_b, mlstm_norm_g,
           w_attn_up, w_mlstm_up, w_out, norm_mix_g, norm_ffn_g, w_ffn_gate, w_ffn_up, w_ffn_down, w_router,
           b_router, w_moe_gate, w_moe_up, w_moe_down, final_norm_g):
    b, seq, _ = x.shape
    depth = w_in.shape[0]
    tp = PREFIX + seq
    m = b * tp
    assert depth == 2 and seq % BLOCK == 0 and tp % ML_CHUNK == 0
    prefix = jnp.concatenate([jnp.zeros((b, PAD, D_MODEL), x.dtype),
                              jnp.broadcast_to(meta_tokens.astype(x.dtype), (b, N_META, D_MODEL))], axis=1)
    xs = jnp.concatenate([prefix, x], axis=1).reshape(m, D_MODEL)
    band, meta = _attn_bias(rel_bias_table, tp // BLOCK)
    h = _rmsnorm(xs, norm_mix_g[0])
    out = None
    for layer in range(depth):
        w_main, w_gates = _pack_w_in(w_in[layer])
        z = _matmul(h, w_main, BF16, 512, Z_WIDTH // 4, "in_proj")
        gates = _matmul(h, w_gates, F32, 1024, LANES, "in_proj_gates")
        z3 = z.reshape(b, tp, Z_WIDTH)
        attn = _attention(z3, band, meta, attn_sinks[layer]).reshape(m, ATTN_WIDTH)
        ml = _mlstm(z3, gates.reshape(b, tp, LANES), conv_w[layer], conv_b[layer], igate_b[layer], fgate_b[layer],
                    mlstm_norm_g[layer]).reshape(m, ML_V_WIDTH)
        wa, wm, wo = (w_attn_up[layer].astype(BF16), w_mlstm_up[layer].astype(BF16), w_out[layer].astype(BF16))
        i = layer // 2
        if layer % 2 == 0:
            xs, h = _merge(attn, ml, z, xs, wa, wm, wo, norm_ffn_g[layer])
            xs, h = _dense_ffn(h, xs, w_ffn_gate[i].astype(BF16), w_ffn_up[i].astype(BF16),
                               w_ffn_down[i].astype(BF16), norm_mix_g[layer + 1])
        else:
            xs, h_packed, route = _merge(attn, ml, z, xs, wa, wm, wo, norm_ffn_g[layer],
                                         router=(w_router[i], b_router[i]))
            dest, row_tok, sb_e, sb_rows, n_sb = _moe_routing(route, m)
            yb = _moe_experts(h_packed, sb_e, sb_rows, row_tok, w_moe_gate[i].astype(BF16),
                              w_moe_up[i].astype(BF16), w_moe_down[i].astype(BF16), n_sb)
            out = _moe_combine(dest, xs.reshape(b, tp, D_MODEL), route.reshape(b, tp, LANES), yb, final_norm_g)
    return out
```

```python
import functools
import math

import jax
import jax.numpy as jnp
from jax import lax
from jax.experimental import pallas as pl
from jax.experimental.pallas import tpu as pltpu

D_MODEL = 2048
N_META = 16
BLOCK = 128
PREFIX = BLOCK
PAD = PREFIX - N_META
HEAD_DIM = 64
N_Q_HEADS = 16
N_KV_HEADS = 4
GQA_GROUP = 4
ATTN_WIDTH = N_Q_HEADS * HEAD_DIM
KV_WIDTH = N_KV_HEADS * HEAD_DIM
WINDOW = 128
NUM_BUCKETS = 32
MAX_DISTANCE = 128
ML_HEADS = 4
ML_V_WIDTH = D_MODEL // 2
ML_V_DIM = ML_V_WIDTH // ML_HEADS
ML_QK_DIM = ML_V_DIM // 2
ML_QK_WIDTH = ML_HEADS * ML_QK_DIM
CONV_WIDTH = 4
D_FF = 11 * D_MODEL // 4
N_EXPERTS = 8
TOP_K = 2
EPS = 1e-6

LANES = 128
SUBLANES = 8
BF16_ROWS = 16
VMEM_LIMIT = 56 * 1024 * 1024

Z_GA, Z_GM, Z_AQ, Z_MQK, Z_MV, Z_MO, Z_AK, Z_AV = 0, 2048, 4096, 5120, 6144, 7168, 8192, 8448
Z_WIDTH = 8704
ML_CHUNK = 128
CONV_HALO = 16
FFN_TILE = 512
MOE_SB = 1024
MOE_HALF = MOE_SB // 2
MOE_STEP_ROWS = 96
MOE_GATHER_ROWS = MOE_STEP_ROWS * (D_FF // FFN_TILE)

F32 = jnp.float32
BF16 = jnp.bfloat16
NEG_INF = float("-inf")


def _tile(m, target):
    best = LANES
    for t in range(LANES, min(m, target) + 1, LANES):
        if m % t == 0:
            best = t
    assert m % best == 0
    return best


def _params(*sem):
    return pltpu.CompilerParams(dimension_semantics=sem, vmem_limit_bytes=VMEM_LIMIT)


def _rms(x, g):
    return x * lax.rsqrt(jnp.mean(x * x, axis=-1, keepdims=True) + EPS) * g


def _sigmoid(x):
    return 1.0 / (1.0 + jnp.exp(-x))


def _dot(a, b):
    return jnp.dot(a, b, preferred_element_type=F32)


def _dot_nt(a, b):
    return lax.dot_general(a, b, (((1,), (1,)), ((), ())), preferred_element_type=F32)


def _dot_tn(a, b):
    return lax.dot_general(a, b, (((0,), (0,)), ((), ())), preferred_element_type=F32)


def _norm_kernel(x_ref, g_ref, o_ref):
    o_ref[...] = _rms(x_ref[...], g_ref[...]).astype(o_ref.dtype)


def _rmsnorm(xs, g):
    m = xs.shape[0]
    bm = _tile(m, 512)
    return pl.pallas_call(
        _norm_kernel,
        out_shape=jax.ShapeDtypeStruct((m, D_MODEL), BF16),
        grid=(m // bm,),
        in_specs=[pl.BlockSpec((bm, D_MODEL), lambda i: (i, 0)),
                  pl.BlockSpec((1, D_MODEL), lambda i: (0, 0))],
        out_specs=pl.BlockSpec((bm, D_MODEL), lambda i: (i, 0)),
        compiler_params=_params("parallel"),
        name="rmsnorm",
    )(xs, g.reshape(1, D_MODEL))


def _mm_kernel(a_ref, w_ref, o_ref):
    o_ref[...] = _dot(a_ref[...], w_ref[...]).astype(o_ref.dtype)


def _matmul(a, w, out_dtype, bm, bn, name):
    m, k = a.shape
    n = w.shape[1]
    bm = _tile(m, bm)
    return pl.pallas_call(
        _mm_kernel,
        out_shape=jax.ShapeDtypeStruct((m, n), out_dtype),
        grid=(n // bn, m // bm),
        in_specs=[pl.BlockSpec((bm, k), lambda j, i: (i, 0)),
                  pl.BlockSpec((k, bn), lambda j, i: (0, j))],
        out_specs=pl.BlockSpec((bm, bn), lambda j, i: (i, j)),
        compiler_params=_params("parallel", "parallel"),
        name=name,
    )(a, w)


def _t5_bucket(rel):
    n = jnp.maximum(rel, 0)
    max_exact = NUM_BUCKETS // 2
    large = max_exact + (jnp.log(jnp.maximum(n, 1).astype(F32) / max_exact)
                         / math.log(MAX_DISTANCE / max_exact) * (NUM_BUCKETS - max_exact)).astype(jnp.int32)
    large = jnp.minimum(large, NUM_BUCKETS - 1)
    return jnp.where(n < max_exact, n, large)


def _bias_lookup(table, rel):
    onehot = (_t5_bucket(rel)[..., None] == jnp.arange(NUM_BUCKETS)).astype(F32)
    return jnp.einsum("...b,bh->h...", onehot, table.astype(F32), precision=lax.Precision.HIGHEST)


def _attn_bias(table, nb):
    qi = jnp.arange(BLOCK)[:, None]
    ki = jnp.arange(2 * BLOCK)[None, :]
    rel_band = qi + BLOCK - ki
    blk3 = jnp.arange(3)[:, None, None]
    mask_band = (rel_band >= 0) & (rel_band < WINDOW) & ((blk3 - 1) * BLOCK + ki >= PAD)
    band = jnp.where(mask_band[:, None], _bias_lookup(table, rel_band)[None], NEG_INF)
    band = band.reshape(3, N_KV_HEADS, GQA_GROUP * BLOCK, 2 * BLOCK)
    blk = jnp.arange(nb)[:, None, None]
    rel_meta = blk * BLOCK + qi[None] - (PAD + jnp.arange(N_META))
    meta = jnp.where((rel_meta >= WINDOW)[None], _bias_lookup(table, rel_meta), NEG_INF)
    meta = jnp.moveaxis(meta, 1, 0).reshape(nb, N_KV_HEADS, GQA_GROUP * BLOCK, N_META)
    meta = jnp.pad(meta, ((0, 0), (0, 0), (0, 0), (PAD, 0)), constant_values=NEG_INF)
    return band, meta


def _attn_kernel(q_ref, kp_ref, kc_ref, km_ref, vp_ref, vc_ref, vm_ref, bb_ref, bm_ref, sink_ref, o_ref):
    scale = HEAD_DIM ** -0.5
    assert math.frexp(scale)[0] == 0.5
    q = q_ref[0] * scale
    outs = [None] * N_Q_HEADS
    for h in range(N_KV_HEADS):
        cs = slice(h * HEAD_DIM, (h + 1) * HEAD_DIM)
        qh = jnp.concatenate(
            [q[:, (h * GQA_GROUP + g) * HEAD_DIM:(h * GQA_GROUP + g + 1) * HEAD_DIM] for g in range(GQA_GROUP)],
            axis=0)
        kb = jnp.concatenate([kp_ref[0, :, cs], kc_ref[0, :, cs]], axis=0)
        vb = jnp.concatenate([vp_ref[0, :, cs], vc_ref[0, :, cs]], axis=0)
        s_band = _dot_nt(qh, kb) + bb_ref[0, h]
        s_meta = _dot_nt(qh, km_ref[0, :, cs]) + bm_ref[0, h]
        sink = sink_ref[h]
        mx = jnp.maximum(jnp.maximum(s_band.max(-1, keepdims=True), s_meta.max(-1, keepdims=True)), sink)
        p_band = jnp.exp(s_band - mx)
        p_meta = jnp.exp(s_meta - mx)
        den = p_band.sum(-1, keepdims=True) + p_meta.sum(-1, keepdims=True) + jnp.exp(sink - mx)
        o = _dot(p_band.astype(BF16), vb) + _dot(p_meta.astype(BF16), vm_ref[0, :, cs])
        o = o / den
        for g in range(GQA_GROUP):
            outs[h * GQA_GROUP + g] = o[g * BLOCK:(g + 1) * BLOCK]
    o_ref[0] = jnp.concatenate(outs, axis=1).astype(o_ref.dtype)


def _attention(z3, band, meta, sinks):
    b, tp, _ = z3.shape
    nb = tp // BLOCK
    kcol, vcol = Z_AK // KV_WIDTH, Z_AV // KV_WIDTH
    sink_col = jnp.repeat(sinks.astype(F32).reshape(N_KV_HEADS, GQA_GROUP), BLOCK, axis=1)[..., None]
    kv = lambda col, f: pl.BlockSpec((1, BLOCK, KV_WIDTH), lambda bi, n: (bi, f(n), col))
    prev = lambda n: jnp.maximum(n - 1, 0)
    cur = lambda n: n
    first = lambda n: 0
    return pl.pallas_call(
        _attn_kernel,
        out_shape=jax.ShapeDtypeStruct((b, tp, ATTN_WIDTH), BF16),
        grid=(b, nb),
        in_specs=[pl.BlockSpec((1, BLOCK, ATTN_WIDTH), lambda bi, n: (bi, n, Z_AQ // ATTN_WIDTH)),
                  kv(kcol, prev), kv(kcol, cur), kv(kcol, first),
                  kv(vcol, prev), kv(vcol, cur), kv(vcol, first),
                  pl.BlockSpec((1, N_KV_HEADS, GQA_GROUP * BLOCK, 2 * BLOCK),
                               lambda bi, n: (jnp.minimum(n, 2), 0, 0, 0)),
                  pl.BlockSpec((1, N_KV_HEADS, GQA_GROUP * BLOCK, BLOCK), lambda bi, n: (n, 0, 0, 0)),
                  pl.BlockSpec((N_KV_HEADS, GQA_GROUP * BLOCK, 1), lambda bi, n: (0, 0, 0))],
        out_specs=pl.BlockSpec((1, BLOCK, ATTN_WIDTH), lambda bi, n: (bi, n, 0)),
        compiler_params=_params("parallel", "arbitrary"),
        name="swa_attention",
    )(z3, z3, z3, z3, z3, z3, z3, band, meta, sink_col)


def _mlstm_kernel(qk_ref, v_ref, gate_ref, mo_ref, cw_ref, cb_ref, gb_ref, ng_ref, o_ref,
                  xa_ref, ct_ref, n_ref, m_ref):
    L = ML_CHUNK
    n_chunks = qk_ref.shape[1] // L
    xa_ref[0:CONV_HALO, :] = jnp.zeros((CONV_HALO, 2 * ML_QK_WIDTH), F32)
    ct_ref[...] = jnp.zeros_like(ct_ref)
    n_ref[...] = jnp.zeros_like(n_ref)
    m_ref[...] = jnp.zeros_like(m_ref)
    ii = lax.broadcasted_iota(jnp.int32, (L, L), 0)
    jj = lax.broadcasted_iota(jnp.int32, (L, L), 1)
    causal = jj <= ii
    lane = lax.broadcasted_iota(jnp.int32, (L, LANES), 1)
    row = lax.broadcasted_iota(jnp.int32, (L, LANES), 0)
    k_scale = ML_QK_DIM ** -0.5

    def chunk(c, carry):
        t0 = pl.multiple_of(c * L, L)
        rows = pl.ds(t0, L)
        xa_ref[CONV_HALO:CONV_HALO + L, :] = qk_ref[0, rows, :].astype(F32)
        acc = cb_ref[...]
        for j in range(CONV_WIDTH):
            off = CONV_HALO - (CONV_WIDTH - 1) + j
            acc = acc + cw_ref[j:j + 1, :] * xa_ref[off:off + L, :]
        qk = acc * _sigmoid(acc)
        xa_ref[0:CONV_HALO, :] = xa_ref[L:L + CONV_HALO, :]
        gpb = gate_ref[0, rows, :] + gb_ref[...]
        valid = (row + t0) >= PAD
        log_sig = jnp.minimum(gpb, 0.0) - jnp.log1p(jnp.exp(-jnp.abs(gpb)))
        gx = jnp.where(lane < ML_HEADS, jnp.where(valid, gpb, NEG_INF), jnp.where(valid, log_sig, 0.0))
        gxt = gx.T
        for h in range(ML_HEADS):
            q = qk[:, h * ML_QK_DIM:(h + 1) * ML_QK_DIM].astype(BF16)
            k = qk[:, ML_QK_WIDTH + h * ML_QK_DIM:ML_QK_WIDTH + (h + 1) * ML_QK_DIM] * k_scale
            vcols = slice(h * ML_V_DIM, (h + 1) * ML_V_DIM)
            v = v_ref[0, rows, vcols]
            ig_col = gx[:, h:h + 1]
            lf_col = gx[:, ML_HEADS + h:ML_HEADS + h + 1]
            ig_row = gxt[h:h + 1, :]
            lf_row = gxt[ML_HEADS + h:ML_HEADS + h + 1, :]
            b_col = jnp.sum(jnp.where(causal, lf_row, 0.0), axis=-1, keepdims=True)
            b_row = jnp.sum(jnp.where(ii <= jj, lf_col, 0.0), axis=0, keepdims=True)
            m_prev = m_ref[h:h + 1, 0:1]
            log_d = jnp.where(causal, b_col - b_row + ig_row, NEG_INF)
            m_inter = b_col + m_prev
            m_out = jnp.maximum(m_inter, log_d.max(-1, keepdims=True))
            d = jnp.exp(log_d - m_out)
            s = _dot_nt(q, k.astype(BF16)) * d
            inter = jnp.exp(m_inter - m_out)
            ct = ct_ref[h]
            n_prev = n_ref[h:h + 1, :]
            num = _dot(s.astype(BF16), v) + inter * _dot(q, ct.astype(BF16))
            den = s.sum(-1, keepdims=True) + inter * jnp.sum(q.astype(F32) * n_prev, axis=-1, keepdims=True)
            hh = num / jnp.maximum(jnp.abs(den), jnp.exp(-m_out))
            b_last = b_col[L - 1:L, :]
            log_w = b_last - b_col + ig_col
            m_new = jnp.maximum(b_last + m_prev, log_w.max(0, keepdims=True))
            decay = jnp.exp(b_last + m_prev - m_new)
            kw = k * jnp.exp(log_w - m_new)
            ct_ref[h] = decay * ct + _dot_tn(kw.astype(BF16), v)
            n_ref[h:h + 1, :] = decay * n_prev + kw.sum(0, keepdims=True)
            m_ref[h:h + 1, :] = jnp.broadcast_to(m_new, (1, LANES))
            hn = hh * lax.rsqrt(jnp.mean(hh * hh, axis=-1, keepdims=True) + EPS) * ng_ref[:, vcols]
            o_ref[0, rows, vcols] = (_sigmoid(mo_ref[0, rows, vcols].astype(F32)) * hn).astype(o_ref.dtype)
        return carry

    lax.fori_loop(0, n_chunks, chunk, 0)


def _mlstm(z3, gates3, conv_w, conv_b, igate_b, fgate_b, norm_g):
    b, tp, _ = z3.shape
    gate_bias = jnp.zeros((1, LANES), F32).at[0, :ML_HEADS].set(igate_b).at[0, ML_HEADS:2 * ML_HEADS].set(fgate_b)
    col = lambda c: pl.BlockSpec((1, tp, ML_V_WIDTH), lambda bi: (bi, 0, c))
    full = lambda r, c: pl.BlockSpec((r, c), lambda bi: (0, 0))
    return pl.pallas_call(
        _mlstm_kernel,
        out_shape=jax.ShapeDtypeStruct((b, tp, ML_V_WIDTH), BF16),
        grid=(b,),
        in_specs=[col(Z_MQK // ML_V_WIDTH), col(Z_MV // ML_V_WIDTH),
                  pl.BlockSpec((1, tp, LANES), lambda bi: (bi, 0, 0)),
                  col(Z_MO // ML_V_WIDTH),
                  full(CONV_WIDTH, 2 * ML_QK_WIDTH), full(1, 2 * ML_QK_WIDTH), full(1, LANES), full(1, ML_V_WIDTH)],
        out_specs=pl.BlockSpec((1, tp, ML_V_WIDTH), lambda bi: (bi, 0, 0)),
        scratch_shapes=[pltpu.VMEM((ML_CHUNK + CONV_HALO, 2 * ML_QK_WIDTH), F32),
                        pltpu.VMEM((ML_HEADS, ML_QK_DIM, ML_V_DIM), F32),
                        pltpu.VMEM((SUBLANES, LANES), F32),
                        pltpu.VMEM((SUBLANES, LANES), F32)],
        compiler_params=_params("parallel"),
        name="mlstm",
    )(z3, z3, gates3, z3, conv_w.astype(F32), conv_b.reshape(1, -1).astype(F32), gate_bias,
      norm_g.reshape(1, -1).astype(F32))


def _merge_kernel(with_router, attn_ref, ml_ref, ga_ref, gm_ref, xs_ref, wa_ref, wm_ref, wo_ref, g_ref, *rest):
    if with_router:
        wr_hi_ref, wr_lo_ref, br_ref, xs_out, h_out, route_out = rest
    else:
        xs_out, h_out = rest
    a = _dot(attn_ref[...], wa_ref[...])
    m = _dot(ml_ref[...], wm_ref[...])
    y = _sigmoid(ga_ref[...].astype(F32)) * a + _sigmoid(gm_ref[...].astype(F32)) * m
    xs_new = xs_ref[...] + _dot(y.astype(BF16), wo_ref[...])
    xs_out[...] = xs_new
    hn = _rms(xs_new, g_ref[...])
    if not with_router:
        h_out[...] = hn.astype(h_out.dtype)
    else:
        half = D_MODEL // 2
        lo = lax.bitcast_convert_type(hn[:, :half].astype(BF16).astype(F32), jnp.uint32) >> 16
        hi = lax.bitcast_convert_type(hn[:, half:].astype(BF16).astype(F32), jnp.uint32) & jnp.uint32(0xFFFF0000)
        h_out[...] = lo | hi
        hn_hi = hn.astype(BF16)
        hn_lo = (hn - hn_hi.astype(F32)).astype(BF16)
        logits = (_dot(hn_hi, wr_hi_ref[...]) + (_dot(hn_lo, wr_hi_ref[...]) + _dot(hn_hi, wr_lo_ref[...]))
                  + br_ref[...])
        lane = lax.broadcasted_iota(jnp.int32, logits.shape, 1).astype(F32)
        l1 = logits.max(-1, keepdims=True)
        i1 = jnp.min(jnp.where(logits == l1, lane, float(LANES)), axis=-1, keepdims=True)
        rest_logits = jnp.where(lane == i1, NEG_INF, logits)
        l2 = rest_logits.max(-1, keepdims=True)
        i2 = jnp.min(jnp.where(rest_logits == l2, lane, float(LANES)), axis=-1, keepdims=True)
        e = jnp.exp(l2 - l1)
        w1 = 1.0 / (1.0 + e)
        w2 = e / (1.0 + e)
        route_out[...] = jnp.where(lane == 0, i1, jnp.where(lane == 1, i2, jnp.where(lane == 2, w1,
                                   jnp.where(lane == 3, w2, 0.0))))


def _merge(attn, ml, z, xs, wa, wm, wo, g_next, router=None):
    m = xs.shape[0]
    bm = _tile(m, 256)
    row = lambda w, c: pl.BlockSpec((bm, w), lambda i: (i, c))
    const = lambda r, c: pl.BlockSpec((r, c), lambda i: (0, 0), pipeline_mode=pl.Buffered(1))
    in_specs = [row(ATTN_WIDTH, 0), row(ML_V_WIDTH, 0), row(D_MODEL, Z_GA // D_MODEL), row(D_MODEL, Z_GM // D_MODEL),
                row(D_MODEL, 0), const(ATTN_WIDTH, D_MODEL), const(ML_V_WIDTH, D_MODEL), const(D_MODEL, D_MODEL),
                const(1, D_MODEL)]
    args = [attn, ml, z, z, xs, wa, wm, wo, g_next.reshape(1, D_MODEL)]
    if router is None:
        out_shape = [jax.ShapeDtypeStruct((m, D_MODEL), F32), jax.ShapeDtypeStruct((m, D_MODEL), BF16)]
        out_specs = [row(D_MODEL, 0), row(D_MODEL, 0)]
    else:
        out_shape = [jax.ShapeDtypeStruct((m, D_MODEL), F32), jax.ShapeDtypeStruct((m, D_MODEL // 2), jnp.uint32)]
        out_specs = [row(D_MODEL, 0), row(D_MODEL // 2, 0)]
    if router is not None:
        w_router, b_router = router
        wr = jnp.zeros((D_MODEL, LANES), F32).at[:, :N_EXPERTS].set(w_router.astype(F32))
        br = jnp.full((1, LANES), NEG_INF, F32).at[0, :N_EXPERTS].set(b_router.astype(F32))
        wr_hi = wr.astype(BF16)
        wr_lo = (wr - wr_hi.astype(F32)).astype(BF16)
        in_specs += [const(D_MODEL, LANES), const(D_MODEL, LANES), const(1, LANES)]
        args += [wr_hi, wr_lo, br]
        out_shape.append(jax.ShapeDtypeStruct((m, LANES), F32))
        out_specs.append(row(LANES, 0))
    return pl.pallas_call(
        functools.partial(_merge_kernel, router is not None),
        out_shape=out_shape,
        grid=(m // bm,),
        in_specs=in_specs,
        out_specs=out_specs,
        compiler_params=_params("parallel"),
        name="merge_out_proj",
    )(*args)


def _ffn_kernel(h_ref, xs_hbm, wg_ref, wu_ref, wd_ref, g_ref, xs_out, h_out, sem):
    i = pl.program_id(0)
    f = pl.program_id(1)
    bm = xs_out.shape[0]
    residual = pltpu.make_async_copy(xs_hbm.at[pl.ds(pl.multiple_of(i * bm, bm), bm), :], xs_out, sem)

    @pl.when(f == 0)
    def _():
        residual.start()

    hb = h_ref[...]
    g = _dot(hb, wg_ref[...])
    u = _dot(hb, wu_ref[...])
    act = (g * _sigmoid(g) * u).astype(BF16)

    @pl.when(f == 0)
    def _():
        residual.wait()

    xs_out[...] += _dot(act, wd_ref[...])

    @pl.when(f == pl.num_programs(1) - 1)
    def _():
        h_out[...] = _rms(xs_out[...], g_ref[...]).astype(h_out.dtype)


def _dense_ffn(h, xs, wg, wu, wd, g_next, bf=FFN_TILE):
    m = xs.shape[0]
    bm = _tile(m, 1024)
    return pl.pallas_call(
        _ffn_kernel,
        out_shape=[jax.ShapeDtypeStruct((m, D_MODEL), F32), jax.ShapeDtypeStruct((m, D_MODEL), BF16)],
        grid=(m // bm, D_FF // bf),
        in_specs=[pl.BlockSpec((bm, D_MODEL), lambda i, f: (i, 0)),
                  pl.BlockSpec(memory_space=pl.ANY),
                  pl.BlockSpec((D_MODEL, bf), lambda i, f: (0, f)),
                  pl.BlockSpec((D_MODEL, bf), lambda i, f: (0, f)),
                  pl.BlockSpec((bf, D_MODEL), lambda i, f: (f, 0)),
                  pl.BlockSpec((1, D_MODEL), lambda i, f: (0, 0))],
        out_specs=[pl.BlockSpec((bm, D_MODEL), lambda i, f: (i, 0)),
                   pl.BlockSpec((bm, D_MODEL), lambda i, f: (i, 0))],
        scratch_shapes=[pltpu.SemaphoreType.DMA(())],
        compiler_params=_params("arbitrary", "arbitrary"),
        name="dense_swiglu",
    )(h, xs, wg, wu, wd, g_next.reshape(1, D_MODEL))


def _row_copy(src_hbm, dst_vmem, sem, src_row, dst_row):
    return pltpu.make_async_copy(src_hbm.at[pl.ds(src_row, 1), :], dst_vmem.at[pl.ds(dst_row, 1), :], sem)


def _moe_gather_copy(h_hbm, gbuf, sem, slot, src_row, dst_row):
    return pltpu.make_async_copy(h_hbm.at[pl.ds(src_row, 1), :], gbuf.at[slot, pl.ds(dst_row, 1), :], sem.at[slot])


def _moe_gather_wait(h_hbm, gbuf, sem, slot):
    pltpu.make_async_copy(h_hbm.at[pl.ds(0, MOE_GATHER_ROWS), :], gbuf.at[slot], sem.at[slot]).wait()


def _moe_kernel(sb_e_ref, sb_rows_ref, row_tok_ref, h_hbm, wg_ref, wu_ref, wd_ref, y_ref, gbuf, xb16, sem):
    sb = pl.program_id(0)
    f = pl.program_id(1)
    n_sb = pl.num_programs(0)
    nf = pl.num_programs(1)
    kind = sb_rows_ref[sb]
    slot = sb % 2
    half = D_MODEL // 2

    @pl.when(jnp.logical_and(sb == 0, f == 0))
    def _():
        def issue(r, c):
            _moe_gather_copy(h_hbm, gbuf, sem, 0, row_tok_ref[r], r).start()
            return c
        lax.fori_loop(0, MOE_GATHER_ROWS, issue, 0)

    @pl.when(f == 0)
    def _():
        y_ref[...] = jnp.zeros_like(y_ref)
        prev_kind = sb_rows_ref[jnp.maximum(sb - 1, 0)]

        @pl.when(jnp.logical_or(sb == 0, prev_kind > 0))
        def _():
            _moe_gather_wait(h_hbm, gbuf, sem, slot)

        @pl.when(kind > 0)
        def _():
            w = gbuf[slot, 0:MOE_SB, :]
            xb16[:, :half] = lax.bitcast_convert_type(w << 16, F32).astype(BF16)
            xb16[:, half:] = lax.bitcast_convert_type(w & jnp.uint32(0xFFFF0000), F32).astype(BF16)

    def compute(rows):
        base = (sb + 1) * MOE_SB + f * MOE_STEP_ROWS
        for r in range(MOE_STEP_ROWS):
            _moe_gather_copy(h_hbm, gbuf, sem, 1 - slot, row_tok_ref[base + r], f * MOE_STEP_ROWS + r).start()
        xb = xb16[0:rows, :]
        g = _dot(xb, wg_ref[0])
        u = _dot(xb, wu_ref[0])
        act = (g * _sigmoid(g) * u).astype(BF16)
        y_ref[0:rows, :] += _dot(act, wd_ref[0])

    @pl.when(kind == 2)
    def _():
        compute(MOE_SB)

    @pl.when(kind == 1)
    def _():
        compute(MOE_HALF)

    @pl.when(jnp.logical_and(jnp.logical_and(sb == n_sb - 1, f == nf - 1), kind > 0))
    def _():
        _moe_gather_wait(h_hbm, gbuf, sem, 1 - slot)


def _moe_experts(h_packed, sb_e, sb_rows, row_tok, wg, wu, wd, n_sb, bf=FFN_TILE):
    nf = D_FF // bf
    assert nf * MOE_STEP_ROWS == MOE_GATHER_ROWS
    ftile = lambda i, f, rows: jnp.where(rows[i] > 0, f, nf - 1)
    grid_spec = pltpu.PrefetchScalarGridSpec(
        num_scalar_prefetch=3,
        grid=(n_sb, nf),
        in_specs=[pl.BlockSpec(memory_space=pl.ANY),
                  pl.BlockSpec((1, D_MODEL, bf), lambda i, f, se, sr, rt: (se[i], 0, ftile(i, f, sr))),
                  pl.BlockSpec((1, D_MODEL, bf), lambda i, f, se, sr, rt: (se[i], 0, ftile(i, f, sr))),
                  pl.BlockSpec((1, bf, D_MODEL), lambda i, f, se, sr, rt: (se[i], ftile(i, f, sr), 0))],
        out_specs=pl.BlockSpec((MOE_SB, D_MODEL), lambda i, f, se, sr, rt: (i, 0)),
        scratch_shapes=[pltpu.VMEM((2, MOE_GATHER_ROWS, D_MODEL // 2), jnp.uint32),
                        pltpu.VMEM((MOE_SB, D_MODEL), BF16),
                        pltpu.SemaphoreType.DMA((2,))],
    )
    return pl.pallas_call(
        _moe_kernel,
        out_shape=jax.ShapeDtypeStruct((n_sb * MOE_SB, D_MODEL), F32),
        grid_spec=grid_spec,
        compiler_params=_params("arbitrary", "arbitrary"),
        name="moe_experts",
    )(sb_e, sb_rows, row_tok, h_packed, wg, wu, wd)


def _combine_kernel(pos_ref, xs_ref, route_ref, y_hbm, g_ref, o_ref, ybuf, sem):
    nt = pl.num_programs(1)
    step = pl.program_id(0) * nt + pl.program_id(1)
    n_steps = pl.num_programs(0) * nt
    tp = nt * BLOCK + PREFIX
    slot = step % 2

    def gather(s, dst_slot):
        base = (s // nt) * tp + PREFIX + (s % nt) * BLOCK

        def issue(r, c):
            for k in range(TOP_K):
                _row_copy(y_hbm, ybuf.at[dst_slot, k], sem.at[dst_slot], pos_ref[TOP_K * (base + r) + k], r).start()
            return c
        lax.fori_loop(0, BLOCK, issue, 0, unroll=4)

    @pl.when(step == 0)
    def _():
        gather(step, slot)

    @pl.when(step + 1 < n_steps)
    def _():
        gather(step + 1, 1 - slot)

    for k in range(TOP_K):
        pltpu.make_async_copy(y_hbm.at[pl.ds(0, BLOCK), :], ybuf.at[slot, k], sem.at[slot]).wait()
    route = route_ref[0]
    moe = ybuf[slot, 0] * route[:, 2:3] + ybuf[slot, 1] * route[:, 3:4]
    o_ref[0] = _rms(xs_ref[0] + moe, g_ref[...])


def _moe_combine(pos, xs3, route3, yb, g_final):
    b, tp, _ = xs3.shape
    seq = tp - PREFIX
    grid_spec = pltpu.PrefetchScalarGridSpec(
        num_scalar_prefetch=1,
        grid=(b, seq // BLOCK),
        in_specs=[pl.BlockSpec((1, BLOCK, D_MODEL), lambda bi, t, p: (bi, t + 1, 0)),
                  pl.BlockSpec((1, BLOCK, LANES), lambda bi, t, p: (bi, t + 1, 0)),
                  pl.BlockSpec(memory_space=pl.ANY),
                  pl.BlockSpec((1, D_MODEL), lambda bi, t, p: (0, 0))],
        out_specs=pl.BlockSpec((1, BLOCK, D_MODEL), lambda bi, t, p: (bi, t, 0)),
        scratch_shapes=[pltpu.VMEM((2, TOP_K, BLOCK, D_MODEL), F32), pltpu.SemaphoreType.DMA((2,))],
    )
    return pl.pallas_call(
        _combine_kernel,
        out_shape=jax.ShapeDtypeStruct((b, seq, D_MODEL), F32),
        grid_spec=grid_spec,
        compiler_params=_params("arbitrary", "arbitrary"),
        name="moe_combine_final_norm",
    )(pos, xs3, route3, yb, g_final.reshape(1, D_MODEL))


def _moe_routing(route, n_tok):
    n_assign = n_tok * TOP_K
    n_sb = n_assign // MOE_SB + N_EXPERTS
    e_flat = route[:, :TOP_K].astype(jnp.int32).reshape(n_assign)
    onehot = (e_flat[:, None] == jnp.arange(N_EXPERTS, dtype=jnp.int32)[None, :]).astype(jnp.int32)
    csum = jnp.cumsum(onehot, axis=0)
    rank = jnp.sum((csum - onehot) * onehot, axis=1)
    counts = csum[-1]
    sb_count = (counts + MOE_SB - 1) // MOE_SB
    sb_end = jnp.cumsum(sb_count)
    sb_start = sb_end - sb_count
    dest = (jnp.sum((sb_start * MOE_SB)[None, :] * onehot, axis=1) + rank).astype(jnp.int32)
    row_tok = jnp.zeros(((n_sb + 2) * MOE_SB,), jnp.int32).at[dest].set(
        jnp.arange(n_assign, dtype=jnp.int32) // TOP_K, unique_indices=True)
    sb = jnp.arange(n_sb, dtype=jnp.int32)
    sb_e = jnp.sum((sb[:, None] >= sb_end[None, :]).astype(jnp.int32), axis=1)
    valid = sb < sb_end[-1]
    last_e = jnp.sum((sb_end[-1] - 1 >= sb_end).astype(jnp.int32))
    sb_e = jnp.where(valid, sb_e, last_e).astype(jnp.int32)
    rows_here = counts[sb_e] - (sb - sb_start[sb_e]) * MOE_SB
    sb_rows = jnp.where(valid, jnp.where(rows_here > MOE_HALF, 2, 1), 0).astype(jnp.int32)
    return dest, row_tok, sb_e, sb_rows, n_sb


def _pack_w_in(w):
    o = {}
    off = 0
    for name, size in (("aq", ATTN_WIDTH), ("ak", KV_WIDTH), ("av", KV_WIDTH), ("mq", ML_QK_WIDTH),
                       ("mk", ML_QK_WIDTH), ("mv", ML_V_WIDTH), ("mi", ML_HEADS), ("mf", ML_HEADS),
                       ("mo", ML_V_WIDTH), ("ga", D_MODEL), ("gm", D_MODEL)):
        o[name] = w[:, off:off + size]
        off += size
    main = jnp.concatenate([o["ga"], o["gm"], o["aq"], o["mq"], o["mk"], o["mv"], o["mo"], o["ak"], o["av"]],
                           axis=1).astype(BF16)
    gates = jnp.concatenate([o["mi"], o["mf"], jnp.zeros((D_MODEL, LANES - 2 * ML_HEADS), w.dtype)], axis=1)
    return main, gates.astype(BF16)


def kernel(x, meta_tokens, rel_bias_table, w_in, attn_sinks, conv_w, conv_b, igate_b, fgate_b, mlstm_norm_g,
           w_attn_up, w_mlstm_up, w_out, norm_mix_g, norm_ffn_g, w_ffn_gate, w_ffn_up, w_ffn_down, w_router,
           b_router, w_moe_gate, w_moe_up, w_moe_down, final_norm_g):
    b, seq, _ = x.shape
    depth = w_in.shape[0]
    tp = PREFIX + seq
    m = b * tp
    assert depth == 2 and seq % BLOCK == 0 and tp % ML_CHUNK == 0
    prefix = jnp.concatenate([jnp.zeros((b, PAD, D_MODEL), x.dtype),
                              jnp.broadcast_to(meta_tokens.astype(x.dtype), (b, N_META, D_MODEL))], axis=1)
    xs = jnp.concatenate([prefix, x], axis=1).reshape(m, D_MODEL)
    band, meta = _attn_bias(rel_bias_table, tp // BLOCK)
    h = _rmsnorm(xs, norm_mix_g[0])
    out = None
    for layer in range(depth):
        w_main, w_gates = _pack_w_in(w_in[layer])
        z = _matmul(h, w_main, BF16, 1024, Z_WIDTH // 4, "in_proj")
        gates = _matmul(h, w_gates, F32, 1024, LANES, "in_proj_gates")
        z3 = z.reshape(b, tp, Z_WIDTH)
        attn = _attention(z3, band, meta, attn_sinks[layer]).reshape(m, ATTN_WIDTH)
        ml = _mlstm(z3, gates.reshape(b, tp, LANES), conv_w[layer], conv_b[layer], igate_b[layer], fgate_b[layer],
                    mlstm_norm_g[layer]).reshape(m, ML_V_WIDTH)
        wa, wm, wo = (w_attn_up[layer].astype(BF16), w_mlstm_up[layer].astype(BF16), w_out[layer].astype(BF16))
        i = layer // 2
        if layer % 2 == 0:
            xs, h = _merge(attn, ml, z, xs, wa, wm, wo, norm_ffn_g[layer])
            xs, h = _dense_ffn(h, xs, w_ffn_gate[i].astype(BF16), w_ffn_up[i].astype(BF16),
                               w_ffn_down[i].astype(BF16), norm_mix_g[layer + 1])
        else:
            xs, h_packed, route = _merge(attn, ml, z, xs, wa, wm, wo, norm_ffn_g[layer],
                                         router=(w_router[i], b_router[i]))
            dest, row_tok, sb_e, sb_rows, n_sb = _moe_routing(route, m)
            yb = _moe_experts(h_packed, sb_e, sb_rows, row_tok, w_moe_gate[i].astype(BF16),
                              w_moe_up[i].astype(BF16), w_moe_down[i].astype(BF16), n_sb)
            out = _moe_combine(dest, xs.reshape(b, tp, D_MODEL), route.reshape(b, tp, LANES), yb, final_norm_g)
    return out
```

```python
import functools
import math

import jax
import jax.numpy as jnp
from jax import lax
from jax.experimental import pallas as pl
from jax.experimental.pallas import tpu as pltpu

D_MODEL = 2048
N_META = 16
BLOCK = 128
PREFIX = BLOCK
PAD = PREFIX - N_META
HEAD_DIM = 64
N_Q_HEADS = 16
N_KV_HEADS = 4
GQA_GROUP = 4
ATTN_WIDTH = N_Q_HEADS * HEAD_DIM
KV_WIDTH = N_KV_HEADS * HEAD_DIM
WINDOW = 128
NUM_BUCKETS = 32
MAX_DISTANCE = 128
ML_HEADS = 4
ML_V_WIDTH = D_MODEL // 2
ML_V_DIM = ML_V_WIDTH // ML_HEADS
ML_QK_DIM = ML_V_DIM // 2
ML_QK_WIDTH = ML_HEADS * ML_QK_DIM
CONV_WIDTH = 4
D_FF = 11 * D_MODEL // 4
N_EXPERTS = 8
TOP_K = 2
EPS = 1e-6

LANES = 128
SUBLANES = 8
BF16_ROWS = 16
VMEM_LIMIT = 56 * 1024 * 1024

Z_GA, Z_GM, Z_AQ, Z_MQK, Z_MV, Z_MO, Z_AK, Z_AV = 0, 2048, 4096, 5120, 6144, 7168, 8192, 8448
Z_WIDTH = 8704
ML_CHUNK = 128
CONV_HALO = 16
FFN_TILE = 512
MOE_SB = 1024
MOE_HALF = MOE_SB // 2
MOE_FF_TILE = 256
MOE_STEP_ROWS = 48
MOE_GATHER_ROWS = MOE_STEP_ROWS * (D_FF // MOE_FF_TILE)

F32 = jnp.float32
BF16 = jnp.bfloat16
NEG_INF = float("-inf")


def _tile(m, target):
    best = LANES
    for t in range(LANES, min(m, target) + 1, LANES):
        if m % t == 0:
            best = t
    assert m % best == 0
    return best


def _params(*sem):
    return pltpu.CompilerParams(dimension_semantics=sem, vmem_limit_bytes=VMEM_LIMIT)


def _rms(x, g):
    return x * lax.rsqrt(jnp.mean(x * x, axis=-1, keepdims=True) + EPS) * g


def _sigmoid(x):
    return 1.0 / (1.0 + jnp.exp(-x))


def _dot(a, b):
    return jnp.dot(a, b, preferred_element_type=F32)


def _dot_nt(a, b):
    return lax.dot_general(a, b, (((1,), (1,)), ((), ())), preferred_element_type=F32)


def _dot_tn(a, b):
    return lax.dot_general(a, b, (((0,), (0,)), ((), ())), preferred_element_type=F32)


def _norm_kernel(x_ref, g_ref, o_ref):
    o_ref[...] = _rms(x_ref[...], g_ref[...]).astype(o_ref.dtype)


def _rmsnorm(xs, g):
    m = xs.shape[0]
    bm = _tile(m, 512)
    return pl.pallas_call(
        _norm_kernel,
        out_shape=jax.ShapeDtypeStruct((m, D_MODEL), BF16),
        grid=(m // bm,),
        in_specs=[pl.BlockSpec((bm, D_MODEL), lambda i: (i, 0)),
                  pl.BlockSpec((1, D_MODEL), lambda i: (0, 0))],
        out_specs=pl.BlockSpec((bm, D_MODEL), lambda i: (i, 0)),
        compiler_params=_params("parallel"),
        name="rmsnorm",
    )(xs, g.reshape(1, D_MODEL))


def _mm_kernel(a_ref, w_ref, o_ref):
    o_ref[...] = _dot(a_ref[...], w_ref[...]).astype(o_ref.dtype)


def _matmul(a, w, out_dtype, bm, bn, name):
    m, k = a.shape
    n = w.shape[1]
    bm = _tile(m, bm)
    return pl.pallas_call(
        _mm_kernel,
        out_shape=jax.ShapeDtypeStruct((m, n), out_dtype),
        grid=(n // bn, m // bm),
        in_specs=[pl.BlockSpec((bm, k), lambda j, i: (i, 0)),
                  pl.BlockSpec((k, bn), lambda j, i: (0, j))],
        out_specs=pl.BlockSpec((bm, bn), lambda j, i: (i, j)),
        compiler_params=_params("parallel", "parallel"),
        name=name,
    )(a, w)


def _t5_bucket(rel):
    n = jnp.maximum(rel, 0)
    max_exact = NUM_BUCKETS // 2
    large = max_exact + (jnp.log(jnp.maximum(n, 1).astype(F32) / max_exact)
                         / math.log(MAX_DISTANCE / max_exact) * (NUM_BUCKETS - max_exact)).astype(jnp.int32)
    large = jnp.minimum(large, NUM_BUCKETS - 1)
    return jnp.where(n < max_exact, n, large)


def _bias_lookup(table, rel):
    onehot = (_t5_bucket(rel)[..., None] == jnp.arange(NUM_BUCKETS)).astype(F32)
    return jnp.einsum("...b,bh->h...", onehot, table.astype(F32), precision=lax.Precision.HIGHEST)


ATTN_GROUP_ORDER = (0, 2, 1, 3)


def _stack_group_rows(a):
    lead = a.shape[:-3]
    a = a.reshape(*lead, N_KV_HEADS, GQA_GROUP, BLOCK, a.shape[-1])
    a = jnp.take(a, jnp.array(ATTN_GROUP_ORDER), axis=len(lead) + 1)
    return a.reshape(*lead, N_KV_HEADS, GQA_GROUP * BLOCK, a.shape[-1])


def _attn_bias(table, nb):
    qi = jnp.arange(BLOCK)[:, None]
    ki = jnp.arange(2 * BLOCK)[None, :]
    rel_band = qi + BLOCK - ki
    blk3 = jnp.arange(3)[:, None, None]
    mask_band = (rel_band >= 0) & (rel_band < WINDOW) & ((blk3 - 1) * BLOCK + ki >= PAD)
    band = jnp.where(mask_band[:, None], _bias_lookup(table, rel_band)[None], NEG_INF)
    blk = jnp.arange(nb)[:, None, None]
    rel_meta = blk * BLOCK + qi[None] - (PAD + jnp.arange(N_META))
    meta = jnp.where((rel_meta >= WINDOW)[None], _bias_lookup(table, rel_meta), NEG_INF)
    meta = jnp.pad(jnp.moveaxis(meta, 1, 0), ((0, 0), (0, 0), (0, 0), (PAD, 0)), constant_values=NEG_INF)
    return _stack_group_rows(band), _stack_group_rows(meta)


def _swap_halves(x):
    return pltpu.roll(x.astype(F32), HEAD_DIM, axis=1).astype(x.dtype)


def _attn_kernel(q_ref, kp_ref, kc_ref, km_ref, vp_ref, vc_ref, vm_ref, bb_ref, bm_ref, sink_ref, o_ref):
    scale = HEAD_DIM ** -0.5
    assert math.frexp(scale)[0] == 0.5 and 2 * HEAD_DIM == LANES
    lane = lax.broadcasted_iota(jnp.int32, (1, LANES), 1)
    keep = (jnp.where(lane < HEAD_DIM, scale, 0.0).astype(BF16),
            jnp.where(lane < HEAD_DIM, 0.0, scale).astype(BF16))
    low = lax.broadcasted_iota(jnp.int32, (2 * BLOCK, LANES), 1) < HEAD_DIM
    chains = []
    for col in range(N_KV_HEADS // 2):
        kcols = slice(col * LANES, (col + 1) * LANES)
        k_nat = jnp.concatenate([kp_ref[0, :, kcols], kc_ref[0, :, kcols], km_ref[0, :, kcols]], axis=0)
        v_nat = jnp.concatenate([vp_ref[0, :, kcols], vc_ref[0, :, kcols], vm_ref[0, :, kcols]], axis=0)
        keys = (k_nat, _swap_halves(k_nat))
        vals = (v_nat, _swap_halves(v_nat))
        for half in range(2):
            h = 2 * col + half
            q0 = h * GQA_GROUP * HEAD_DIM
            q2 = jnp.concatenate([q_ref[0, :, q0:q0 + LANES], q_ref[0, :, q0 + LANES:q0 + 2 * LANES]], axis=0)
            for lane_half in range(2):
                which = 0 if lane_half == half else 1
                chains.append((h, lane_half, _dot_nt(q2 * keep[lane_half], keys[which]), vals[which]))
    probs = []
    for h, lane_half, s, _ in chains:
        rows = slice(lane_half * 2 * BLOCK, (lane_half + 1) * 2 * BLOCK)
        s0 = s[:, :BLOCK] + bb_ref[0, h, rows, :BLOCK]
        s1 = s[:, BLOCK:2 * BLOCK] + bb_ref[0, h, rows, BLOCK:]
        s2 = s[:, 2 * BLOCK:] + bm_ref[0, h, rows, :]
        sink = sink_ref[h, rows, :]
        mx = jnp.maximum(jnp.maximum(jnp.maximum(s0, s1), s2).max(-1, keepdims=True), sink)
        p0 = jnp.exp(s0 - mx)
        p1 = jnp.exp(s1 - mx)
        p2 = jnp.exp(s2 - mx)
        den = (p0 + p1 + p2).sum(-1, keepdims=True) + jnp.exp(sink - mx)
        probs.append((jnp.concatenate([p0, p1, p2], axis=1).astype(BF16), 1.0 / den))
    outs = [_dot(p, chain[3]) * rden for (p, rden), chain in zip(probs, chains)]
    out_cols = []
    for h in range(N_KV_HEADS):
        o = jnp.where(low, outs[2 * h], outs[2 * h + 1]).astype(o_ref.dtype)
        out_cols += [o[:BLOCK], o[BLOCK:]]
    o_ref[0] = jnp.concatenate(out_cols, axis=1)


def _attention(z3, band, meta, sinks):
    b, tp, _ = z3.shape
    nb = tp // BLOCK
    kcol, vcol = Z_AK // KV_WIDTH, Z_AV // KV_WIDTH
    sink_col = _stack_group_rows(jnp.broadcast_to(sinks.astype(F32)[:, None, None], (N_Q_HEADS, BLOCK, LANES)))
    kv = lambda col, f: pl.BlockSpec((1, BLOCK, KV_WIDTH), lambda bi, n: (bi, f(n), col))
    prev = lambda n: jnp.maximum(n - 1, 0)
    cur = lambda n: n
    first = lambda n: 0
    return pl.pallas_call(
        _attn_kernel,
        out_shape=jax.ShapeDtypeStruct((b, tp, ATTN_WIDTH), BF16),
        grid=(b, nb),
        in_specs=[pl.BlockSpec((1, BLOCK, ATTN_WIDTH), lambda bi, n: (bi, n, Z_AQ // ATTN_WIDTH)),
                  kv(kcol, prev), kv(kcol, cur), kv(kcol, first),
                  kv(vcol, prev), kv(vcol, cur), kv(vcol, first),
                  pl.BlockSpec((1, N_KV_HEADS, GQA_GROUP * BLOCK, 2 * BLOCK),
                               lambda bi, n: (jnp.minimum(n, 2), 0, 0, 0)),
                  pl.BlockSpec((1, N_KV_HEADS, GQA_GROUP * BLOCK, BLOCK), lambda bi, n: (n, 0, 0, 0)),
                  pl.BlockSpec((N_KV_HEADS, GQA_GROUP * BLOCK, LANES), lambda bi, n: (0, 0, 0))],
        out_specs=pl.BlockSpec((1, BLOCK, ATTN_WIDTH), lambda bi, n: (bi, n, 0)),
        compiler_params=_params("parallel", "arbitrary"),
        name="swa_attention",
    )(z3, z3, z3, z3, z3, z3, z3, band, meta, sink_col)


def _mlstm_kernel(qk_ref, v_ref, gate_ref, mo_ref, cw_ref, cb_ref, gb_ref, ng_ref, o_ref,
                  xa_ref, ct_ref, n_ref, m_ref):
    L = ML_CHUNK
    n_chunks = qk_ref.shape[1] // L
    xa_ref[0:CONV_HALO, :] = jnp.zeros((CONV_HALO, 2 * ML_QK_WIDTH), F32)
    ct_ref[...] = jnp.zeros_like(ct_ref)
    n_ref[...] = jnp.zeros_like(n_ref)
    m_ref[...] = jnp.zeros_like(m_ref)
    ii = lax.broadcasted_iota(jnp.int32, (L, L), 0)
    jj = lax.broadcasted_iota(jnp.int32, (L, L), 1)
    causal = jj <= ii
    lane = lax.broadcasted_iota(jnp.int32, (L, LANES), 1)
    row = lax.broadcasted_iota(jnp.int32, (L, LANES), 0)
    k_scale = ML_QK_DIM ** -0.5

    def chunk(c, carry):
        t0 = pl.multiple_of(c * L, L)
        rows = pl.ds(t0, L)
        xa_ref[CONV_HALO:CONV_HALO + L, :] = qk_ref[0, rows, :].astype(F32)
        acc = cb_ref[...]
        for j in range(CONV_WIDTH):
            off = CONV_HALO - (CONV_WIDTH - 1) + j
            acc = acc + cw_ref[j:j + 1, :] * xa_ref[off:off + L, :]
        qk = acc * _sigmoid(acc)
        xa_ref[0:CONV_HALO, :] = xa_ref[L:L + CONV_HALO, :]
        gpb = gate_ref[0, rows, :] + gb_ref[...]
        valid = (row + t0) >= PAD
        log_sig = jnp.minimum(gpb, 0.0) - jnp.log1p(jnp.exp(-jnp.abs(gpb)))
        gx = jnp.where(lane < ML_HEADS, jnp.where(valid, gpb, NEG_INF), jnp.where(valid, log_sig, 0.0))
        gxt = gx.T
        for h in range(ML_HEADS):
            q = qk[:, h * ML_QK_DIM:(h + 1) * ML_QK_DIM].astype(BF16)
            k = qk[:, ML_QK_WIDTH + h * ML_QK_DIM:ML_QK_WIDTH + (h + 1) * ML_QK_DIM] * k_scale
            vcols = slice(h * ML_V_DIM, (h + 1) * ML_V_DIM)
            v = v_ref[0, rows, vcols]
            ig_col = gx[:, h:h + 1]
            lf_col = gx[:, ML_HEADS + h:ML_HEADS + h + 1]
            ig_row = gxt[h:h + 1, :]
            lf_row = gxt[ML_HEADS + h:ML_HEADS + h + 1, :]
            b_col = jnp.sum(jnp.where(causal, lf_row, 0.0), axis=-1, keepdims=True)
            b_row = jnp.sum(jnp.where(ii <= jj, lf_col, 0.0), axis=0, keepdims=True)
            m_prev = m_ref[h:h + 1, 0:1]
            log_d = jnp.where(causal, b_col - b_row + ig_row, NEG_INF)
            m_inter = b_col + m_prev
            m_out = jnp.maximum(m_inter, log_d.max(-1, keepdims=True))
            d = jnp.exp(log_d - m_out)
            s = _dot_nt(q, k.astype(BF16)) * d
            inter = jnp.exp(m_inter - m_out)
            ct = ct_ref[h]
            n_prev = n_ref[h:h + 1, :]
            num = _dot(s.astype(BF16), v) + inter * _dot(q, ct.astype(BF16))
            den = s.sum(-1, keepdims=True) + inter * jnp.sum(q.astype(F32) * n_prev, axis=-1, keepdims=True)
            hh = num / jnp.maximum(jnp.abs(den), jnp.exp(-m_out))
            b_last = b_col[L - 1:L, :]
            log_w = b_last - b_col + ig_col
            m_new = jnp.maximum(b_last + m_prev, log_w.max(0, keepdims=True))
            decay = jnp.exp(b_last + m_prev - m_new)
            kw = k * jnp.exp(log_w - m_new)
            ct_ref[h] = decay * ct + _dot_tn(kw.astype(BF16), v)
            n_ref[h:h + 1, :] = decay * n_prev + kw.sum(0, keepdims=True)
            m_ref[h:h + 1, :] = jnp.broadcast_to(m_new, (1, LANES))
            hn = hh * lax.rsqrt(jnp.mean(hh * hh, axis=-1, keepdims=True) + EPS) * ng_ref[:, vcols]
            o_ref[0, rows, vcols] = (_sigmoid(mo_ref[0, rows, vcols].astype(F32)) * hn).astype(o_ref.dtype)
        return carry

    lax.fori_loop(0, n_chunks, chunk, 0)


def _mlstm(z3, gates3, conv_w, conv_b, igate_b, fgate_b, norm_g):
    b, tp, _ = z3.shape
    gate_bias = jnp.zeros((1, LANES), F32).at[0, :ML_HEADS].set(igate_b).at[0, ML_HEADS:2 * ML_HEADS].set(fgate_b)
    col = lambda c: pl.BlockSpec((1, tp, ML_V_WIDTH), lambda bi: (bi, 0, c))
    full = lambda r, c: pl.BlockSpec((r, c), lambda bi: (0, 0))
    return pl.pallas_call(
        _mlstm_kernel,
        out_shape=jax.ShapeDtypeStruct((b, tp, ML_V_WIDTH), BF16),
        grid=(b,),
        in_specs=[col(Z_MQK // ML_V_WIDTH), col(Z_MV // ML_V_WIDTH),
                  pl.BlockSpec((1, tp, LANES), lambda bi: (bi, 0, 0)),
                  col(Z_MO // ML_V_WIDTH),
                  full(CONV_WIDTH, 2 * ML_QK_WIDTH), full(1, 2 * ML_QK_WIDTH), full(1, LANES), full(1, ML_V_WIDTH)],
        out_specs=pl.BlockSpec((1, tp, ML_V_WIDTH), lambda bi: (bi, 0, 0)),
        scratch_shapes=[pltpu.VMEM((ML_CHUNK + CONV_HALO, 2 * ML_QK_WIDTH), F32),
                        pltpu.VMEM((ML_HEADS, ML_QK_DIM, ML_V_DIM), F32),
                        pltpu.VMEM((SUBLANES, LANES), F32),
                        pltpu.VMEM((SUBLANES, LANES), F32)],
        compiler_params=_params("parallel"),
        name="mlstm",
    )(z3, z3, gates3, z3, conv_w.astype(F32), conv_b.reshape(1, -1).astype(F32), gate_bias,
      norm_g.reshape(1, -1).astype(F32))


def _merge_kernel(with_router, attn_ref, ml_ref, ga_ref, gm_ref, xs_ref, wa_ref, wm_ref, wo_ref, g_ref, *rest):
    if with_router:
        wr_hi_ref, wr_lo_ref, br_ref, xs_out, h_out, route_out = rest
    else:
        xs_out, h_out = rest
    a = _dot(attn_ref[...], wa_ref[...])
    m = _dot(ml_ref[...], wm_ref[...])
    y = _sigmoid(ga_ref[...].astype(F32)) * a + _sigmoid(gm_ref[...].astype(F32)) * m
    xs_new = xs_ref[...] + _dot(y.astype(BF16), wo_ref[...])
    xs_out[...] = xs_new
    hn = _rms(xs_new, g_ref[...])
    if not with_router:
        h_out[...] = hn.astype(h_out.dtype)
    else:
        half = D_MODEL // 2
        lo = lax.bitcast_convert_type(hn[:, :half].astype(BF16).astype(F32), jnp.uint32) >> 16
        hi = lax.bitcast_convert_type(hn[:, half:].astype(BF16).astype(F32), jnp.uint32) & jnp.uint32(0xFFFF0000)
        h_out[...] = lo | hi
        hn_hi = hn.astype(BF16)
        hn_lo = (hn - hn_hi.astype(F32)).astype(BF16)
        logits = (_dot(hn_hi, wr_hi_ref[...]) + (_dot(hn_lo, wr_hi_ref[...]) + _dot(hn_hi, wr_lo_ref[...]))
                  + br_ref[...])
        lane = lax.broadcasted_iota(jnp.int32, logits.shape, 1).astype(F32)
        l1 = logits.max(-1, keepdims=True)
        i1 = jnp.min(jnp.where(logits == l1, lane, float(LANES)), axis=-1, keepdims=True)
        rest_logits = jnp.where(lane == i1, NEG_INF, logits)
        l2 = rest_logits.max(-1, keepdims=True)
        i2 = jnp.min(jnp.where(rest_logits == l2, lane, float(LANES)), axis=-1, keepdims=True)
        e = jnp.exp(l2 - l1)
        w1 = 1.0 / (1.0 + e)
        w2 = e / (1.0 + e)
        route_out[...] = jnp.where(lane == 0, i1, jnp.where(lane == 1, i2, jnp.where(lane == 2, w1,
                                   jnp.where(lane == 3, w2, 0.0))))


def _merge(attn, ml, z, xs, wa, wm, wo, g_next, router=None):
    m = xs.shape[0]
    bm = _tile(m, 256)
    row = lambda w, c: pl.BlockSpec((bm, w), lambda i: (i, c))
    const = lambda r, c: pl.BlockSpec((r, c), lambda i: (0, 0), pipeline_mode=pl.Buffered(1))
    in_specs = [row(ATTN_WIDTH, 0), row(ML_V_WIDTH, 0), row(D_MODEL, Z_GA // D_MODEL), row(D_MODEL, Z_GM // D_MODEL),
                row(D_MODEL, 0), const(ATTN_WIDTH, D_MODEL), const(ML_V_WIDTH, D_MODEL), const(D_MODEL, D_MODEL),
                const(1, D_MODEL)]
    args = [attn, ml, z, z, xs, wa, wm, wo, g_next.reshape(1, D_MODEL)]
    if router is None:
        out_shape = [jax.ShapeDtypeStruct((m, D_MODEL), F32), jax.ShapeDtypeStruct((m, D_MODEL), BF16)]
        out_specs = [row(D_MODEL, 0), row(D_MODEL, 0)]
    else:
        out_shape = [jax.ShapeDtypeStruct((m, D_MODEL), F32), jax.ShapeDtypeStruct((m, D_MODEL // 2), jnp.uint32)]
        out_specs = [row(D_MODEL, 0), row(D_MODEL // 2, 0)]
    if router is not None:
        w_router, b_router = router
        wr = jnp.zeros((D_MODEL, LANES), F32).at[:, :N_EXPERTS].set(w_router.astype(F32))
        br = jnp.full((1, LANES), NEG_INF, F32).at[0, :N_EXPERTS].set(b_router.astype(F32))
        wr_hi = wr.astype(BF16)
        wr_lo = (wr - wr_hi.astype(F32)).astype(BF16)
        in_specs += [const(D_MODEL, LANES), const(D_MODEL, LANES), const(1, LANES)]
        args += [wr_hi, wr_lo, br]
        out_shape.append(jax.ShapeDtypeStruct((m, LANES), F32))
        out_specs.append(row(LANES, 0))
    return pl.pallas_call(
        functools.partial(_merge_kernel, router is not None),
        out_shape=out_shape,
        grid=(m // bm,),
        in_specs=in_specs,
        out_specs=out_specs,
        compiler_params=_params("parallel"),
        name="merge_out_proj",
    )(*args)


def _ffn_kernel(h_ref, xs_hbm, wg_ref, wu_ref, wd_ref, g_ref, xs_out, h_out, sem):
    i = pl.program_id(0)
    f = pl.program_id(1)
    bm = xs_out.shape[0]
    residual = pltpu.make_async_copy(xs_hbm.at[pl.ds(pl.multiple_of(i * bm, bm), bm), :], xs_out, sem)

    @pl.when(f == 0)
    def _():
        residual.start()

    hb = h_ref[...]
    g = _dot(hb, wg_ref[...])
    u = _dot(hb, wu_ref[...])
    act = (g * _sigmoid(g) * u).astype(BF16)

    @pl.when(f == 0)
    def _():
        residual.wait()

    xs_out[...] += _dot(act, wd_ref[...])

    @pl.when(f == pl.num_programs(1) - 1)
    def _():
        h_out[...] = _rms(xs_out[...], g_ref[...]).astype(h_out.dtype)


def _dense_ffn(h, xs, wg, wu, wd, g_next, bf=FFN_TILE):
    m = xs.shape[0]
    bm = _tile(m, 1024)
    return pl.pallas_call(
        _ffn_kernel,
        out_shape=[jax.ShapeDtypeStruct((m, D_MODEL), F32), jax.ShapeDtypeStruct((m, D_MODEL), BF16)],
        grid=(m // bm, D_FF // bf),
        in_specs=[pl.BlockSpec((bm, D_MODEL), lambda i, f: (i, 0)),
                  pl.BlockSpec(memory_space=pl.ANY),
                  pl.BlockSpec((D_MODEL, bf), lambda i, f: (0, f)),
                  pl.BlockSpec((D_MODEL, bf), lambda i, f: (0, f)),
                  pl.BlockSpec((bf, D_MODEL), lambda i, f: (f, 0)),
                  pl.BlockSpec((1, D_MODEL), lambda i, f: (0, 0))],
        out_specs=[pl.BlockSpec((bm, D_MODEL), lambda i, f: (i, 0)),
                   pl.BlockSpec((bm, D_MODEL), lambda i, f: (i, 0))],
        scratch_shapes=[pltpu.SemaphoreType.DMA(())],
        compiler_params=_params("arbitrary", "arbitrary"),
        name="dense_swiglu",
    )(h, xs, wg, wu, wd, g_next.reshape(1, D_MODEL))


def _row_copy(src_hbm, dst_vmem, sem, src_row, dst_row):
    return pltpu.make_async_copy(src_hbm.at[pl.ds(src_row, 1), :], dst_vmem.at[pl.ds(dst_row, 1), :], sem)


def _moe_gather_copy(h_hbm, gbuf, sem, slot, src_row, dst_row):
    return pltpu.make_async_copy(h_hbm.at[pl.ds(src_row, 1), :], gbuf.at[slot, pl.ds(dst_row, 1), :], sem.at[slot])


def _moe_gather_wait(h_hbm, gbuf, sem, slot):
    pltpu.make_async_copy(h_hbm.at[pl.ds(0, MOE_GATHER_ROWS), :], gbuf.at[slot], sem.at[slot]).wait()


def _moe_kernel(sb_e_ref, sb_rows_ref, row_tok_ref, h_hbm, wg_ref, wu_ref, wd_ref, y_ref, gbuf, xb16, sem):
    sb = pl.program_id(0)
    f = pl.program_id(1)
    n_sb = pl.num_programs(0)
    nf = pl.num_programs(1)
    kind = sb_rows_ref[sb]
    slot = sb % 2
    half = D_MODEL // 2

    @pl.when(jnp.logical_and(sb == 0, f == 0))
    def _():
        def issue(r, c):
            _moe_gather_copy(h_hbm, gbuf, sem, 0, row_tok_ref[r], r).start()
            return c
        lax.fori_loop(0, MOE_GATHER_ROWS, issue, 0)

    @pl.when(f == 0)
    def _():
        y_ref[...] = jnp.zeros_like(y_ref)
        prev_kind = sb_rows_ref[jnp.maximum(sb - 1, 0)]

        @pl.when(jnp.logical_or(sb == 0, prev_kind > 0))
        def _():
            _moe_gather_wait(h_hbm, gbuf, sem, slot)

        @pl.when(kind > 0)
        def _():
            w = gbuf[slot, 0:MOE_SB, :]
            xb16[:, :half] = lax.bitcast_convert_type(w << 16, F32).astype(BF16)
            xb16[:, half:] = lax.bitcast_convert_type(w & jnp.uint32(0xFFFF0000), F32).astype(BF16)

    def compute(rows):
        base = (sb + 1) * MOE_SB + f * MOE_STEP_ROWS
        for r in range(MOE_STEP_ROWS):
            _moe_gather_copy(h_hbm, gbuf, sem, 1 - slot, row_tok_ref[base + r], f * MOE_STEP_ROWS + r).start()
        xb = xb16[0:rows, :]
        g = _dot(xb, wg_ref[0].astype(BF16))
        u = _dot(xb, wu_ref[0].astype(BF16))
        act = (g * _sigmoid(g) * u).astype(BF16)
        y_ref[0:rows, :] += _dot(act, wd_ref[0].astype(BF16))

    @pl.when(kind == 2)
    def _():
        compute(MOE_SB)

    @pl.when(kind == 1)
    def _():
        compute(MOE_HALF)

    @pl.when(jnp.logical_and(jnp.logical_and(sb == n_sb - 1, f == nf - 1), kind > 0))
    def _():
        _moe_gather_wait(h_hbm, gbuf, sem, 1 - slot)


def _moe_experts(h_packed, sb_e, sb_rows, row_tok, wg, wu, wd, n_sb, bf=MOE_FF_TILE):
    nf = D_FF // bf
    assert nf * MOE_STEP_ROWS == MOE_GATHER_ROWS
    ftile = lambda i, f, rows: jnp.where(rows[i] > 0, f, nf - 1)
    grid_spec = pltpu.PrefetchScalarGridSpec(
        num_scalar_prefetch=3,
        grid=(n_sb, nf),
        in_specs=[pl.BlockSpec(memory_space=pl.ANY),
                  pl.BlockSpec((1, D_MODEL, bf), lambda i, f, se, sr, rt: (se[i], 0, ftile(i, f, sr))),
                  pl.BlockSpec((1, D_MODEL, bf), lambda i, f, se, sr, rt: (se[i], 0, ftile(i, f, sr))),
                  pl.BlockSpec((1, bf, D_MODEL), lambda i, f, se, sr, rt: (se[i], ftile(i, f, sr), 0))],
        out_specs=pl.BlockSpec((MOE_SB, D_MODEL), lambda i, f, se, sr, rt: (i, 0)),
        scratch_shapes=[pltpu.VMEM((2, MOE_GATHER_ROWS, D_MODEL // 2), jnp.uint32),
                        pltpu.VMEM((MOE_SB, D_MODEL), BF16),
                        pltpu.SemaphoreType.DMA((2,))],
    )
    return pl.pallas_call(
        _moe_kernel,
        out_shape=jax.ShapeDtypeStruct((n_sb * MOE_SB, D_MODEL), F32),
        grid_spec=grid_spec,
        compiler_params=_params("arbitrary", "arbitrary"),
        name="moe_experts",
    )(sb_e, sb_rows, row_tok, h_packed, wg, wu, wd)


def _combine_kernel(pos_ref, xs_ref, route_ref, y_hbm, g_ref, o_ref, ybuf, sem):
    nt = pl.num_programs(1)
    step = pl.program_id(0) * nt + pl.program_id(1)
    n_steps = pl.num_programs(0) * nt
    tp = nt * BLOCK + PREFIX
    slot = step % 2

    def gather(s, dst_slot):
        base = (s // nt) * tp + PREFIX + (s % nt) * BLOCK

        def issue(r, c):
            for k in range(TOP_K):
                _row_copy(y_hbm, ybuf.at[dst_slot, k], sem.at[dst_slot], pos_ref[TOP_K * (base + r) + k], r).start()
            return c
        lax.fori_loop(0, BLOCK, issue, 0, unroll=4)

    @pl.when(step == 0)
    def _():
        gather(step, slot)

    @pl.when(step + 1 < n_steps)
    def _():
        gather(step + 1, 1 - slot)

    for k in range(TOP_K):
        pltpu.make_async_copy(y_hbm.at[pl.ds(0, BLOCK), :], ybuf.at[slot, k], sem.at[slot]).wait()
    route = route_ref[0]
    moe = ybuf[slot, 0] * route[:, 2:3] + ybuf[slot, 1] * route[:, 3:4]
    o_ref[0] = _rms(xs_ref[0] + moe, g_ref[...])


def _moe_combine(pos, xs3, route3, yb, g_final):
    b, tp, _ = xs3.shape
    seq = tp - PREFIX
    grid_spec = pltpu.PrefetchScalarGridSpec(
        num_scalar_prefetch=1,
        grid=(b, seq // BLOCK),
        in_specs=[pl.BlockSpec((1, BLOCK, D_MODEL), lambda bi, t, p: (bi, t + 1, 0)),
                  pl.BlockSpec((1, BLOCK, LANES), lambda bi, t, p: (bi, t + 1, 0)),
                  pl.BlockSpec(memory_space=pl.ANY),
                  pl.BlockSpec((1, D_MODEL), lambda bi, t, p: (0, 0))],
        out_specs=pl.BlockSpec((1, BLOCK, D_MODEL), lambda bi, t, p: (bi, t, 0)),
        scratch_shapes=[pltpu.VMEM((2, TOP_K, BLOCK, D_MODEL), F32), pltpu.SemaphoreType.DMA((2,))],
    )
    return pl.pallas_call(
        _combine_kernel,
        out_shape=jax.ShapeDtypeStruct((b, seq, D_MODEL), F32),
        grid_spec=grid_spec,
        compiler_params=_params("arbitrary", "arbitrary"),
        name="moe_combine_final_norm",
    )(pos, xs3, route3, yb, g_final.reshape(1, D_MODEL))


def _moe_routing(route, n_tok):
    n_assign = n_tok * TOP_K
    n_sb = n_assign // MOE_SB + N_EXPERTS
    e_flat = route[:, :TOP_K].astype(jnp.int32).reshape(n_assign)
    onehot = (e_flat[:, None] == jnp.arange(N_EXPERTS, dtype=jnp.int32)[None, :]).astype(jnp.int32)
    csum = jnp.cumsum(onehot, axis=0)
    rank = jnp.sum((csum - onehot) * onehot, axis=1)
    counts = csum[-1]
    sb_count = (counts + MOE_SB - 1) // MOE_SB
    sb_end = jnp.cumsum(sb_count)
    sb_start = sb_end - sb_count
    dest = (jnp.sum((sb_start * MOE_SB)[None, :] * onehot, axis=1) + rank).astype(jnp.int32)
    row_tok = jnp.zeros(((n_sb + 2) * MOE_SB,), jnp.int32).at[dest].set(
        jnp.arange(n_assign, dtype=jnp.int32) // TOP_K, unique_indices=True)
    sb = jnp.arange(n_sb, dtype=jnp.int32)
    sb_e = jnp.sum((sb[:, None] >= sb_end[None, :]).astype(jnp.int32), axis=1)
    valid = sb < sb_end[-1]
    last_e = jnp.sum((sb_end[-1] - 1 >= sb_end).astype(jnp.int32))
    sb_e = jnp.where(valid, sb_e, last_e).astype(jnp.int32)
    rows_here = counts[sb_e] - (sb - sb_start[sb_e]) * MOE_SB
    sb_rows = jnp.where(valid, jnp.where(rows_here > MOE_HALF, 2, 1), 0).astype(jnp.int32)
    return dest, row_tok, sb_e, sb_rows, n_sb


def _pack_w_in(w):
    o = {}
    off = 0
    for name, size in (("aq", ATTN_WIDTH), ("ak", KV_WIDTH), ("av", KV_WIDTH), ("mq", ML_QK_WIDTH),
                       ("mk", ML_QK_WIDTH), ("mv", ML_V_WIDTH), ("mi", ML_HEADS), ("mf", ML_HEADS),
                       ("mo", ML_V_WIDTH), ("ga", D_MODEL), ("gm", D_MODEL)):
        o[name] = w[:, off:off + size]
        off += size
    main = jnp.concatenate([o["ga"], o["gm"], o["aq"], o["mq"], o["mk"], o["mv"], o["mo"], o["ak"], o["av"]],
                           axis=1).astype(BF16)
    gates = jnp.concatenate([o["mi"], o["mf"], jnp.zeros((D_MODEL, LANES - 2 * ML_HEADS), w.dtype)], axis=1)
    return main, gates.astype(BF16)


def kernel(x, meta_tokens, rel_bias_table, w_in, attn_sinks, conv_w, conv_b, igate_b, fgate_b, mlstm_norm_g,
           w_attn_up, w_mlstm_up, w_out, norm_mix_g, norm_ffn_g, w_ffn_gate, w_ffn_up, w_ffn_down, w_router,
           b_router, w_moe_gate, w_moe_up, w_moe_down, final_norm_g):
    b, seq, _ = x.shape
    depth = w_in.shape[0]
    tp = PREFIX + seq
    m = b * tp
    assert depth == 2 and seq % BLOCK == 0 and tp % ML_CHUNK == 0
    prefix = jnp.concatenate([jnp.zeros((b, PAD, D_MODEL), x.dtype),
                              jnp.broadcast_to(meta_tokens.astype(x.dtype), (b, N_META, D_MODEL))], axis=1)
    xs = jnp.concatenate([prefix, x], axis=1).reshape(m, D_MODEL)
    band, meta = _attn_bias(rel_bias_table, tp // BLOCK)
    h = _rmsnorm(xs, norm_mix_g[0])
    out = None
    for layer in range(depth):
        w_main, w_gates = _pack_w_in(w_in[layer])
        z = _matmul(h, w_main, BF16, 1024, Z_WIDTH // 4, "in_proj")
        gates = _matmul(h, w_gates, F32, 1024, LANES, "in_proj_gates")
        z3 = z.reshape(b, tp, Z_WIDTH)
        attn = _attention(z3, band, meta, attn_sinks[layer]).reshape(m, ATTN_WIDTH)
        ml = _mlstm(z3, gates.reshape(b, tp, LANES), conv_w[layer], conv_b[layer], igate_b[layer], fgate_b[layer],
                    mlstm_norm_g[layer]).reshape(m, ML_V_WIDTH)
        wa, wm, wo = (w_attn_up[layer].astype(BF16), w_mlstm_up[layer].astype(BF16), w_out[layer].astype(BF16))
        i = layer // 2
        if layer % 2 == 0:
            xs, h = _merge(attn, ml, z, xs, wa, wm, wo, norm_ffn_g[layer])
            xs, h = _dense_ffn(h, xs, w_ffn_gate[i].astype(BF16), w_ffn_up[i].astype(BF16),
                               w_ffn_down[i].astype(BF16), norm_mix_g[layer + 1])
        else:
            xs, h_packed, route = _merge(attn, ml, z, xs, wa, wm, wo, norm_ffn_g[layer],
                                         router=(w_router[i], b_router[i]))
            dest, row_tok, sb_e, sb_rows, n_sb = _moe_routing(route, m)
            yb = _moe_experts(h_packed, sb_e, sb_rows, row_tok, w_moe_gate[i], w_moe_up[i], w_moe_down[i], n_sb)
            out = _moe_combine(dest, xs.reshape(b, tp, D_MODEL), route.reshape(b, tp, LANES), yb, final_norm_g)
    return out
```

```python
import functools
import math

import jax
import jax.numpy as jnp
from jax import lax
from jax.experimental import pallas as pl
from jax.experimental.pallas import tpu as pltpu

D_MODEL = 2048
N_META = 16
BLOCK = 128
PREFIX = BLOCK
PAD = PREFIX - N_META
HEAD_DIM = 64
N_Q_HEADS = 16
N_KV_HEADS = 4
GQA_GROUP = 4
ATTN_WIDTH = N_Q_HEADS * HEAD_DIM
KV_WIDTH = N_KV_HEADS * HEAD_DIM
WINDOW = 128
NUM_BUCKETS = 32
MAX_DISTANCE = 128
ML_HEADS = 4
ML_V_WIDTH = D_MODEL // 2
ML_V_DIM = ML_V_WIDTH // ML_HEADS
ML_QK_DIM = ML_V_DIM // 2
ML_QK_WIDTH = ML_HEADS * ML_QK_DIM
CONV_WIDTH = 4
D_FF = 11 * D_MODEL // 4
N_EXPERTS = 8
TOP_K = 2
EPS = 1e-6

LANES = 128
SUBLANES = 8
BF16_ROWS = 16
VMEM_LIMIT = 56 * 1024 * 1024

Z_GA, Z_GM, Z_AQ, Z_MQK, Z_MV, Z_MO, Z_AK, Z_AV = 0, 2048, 4096, 5120, 6144, 7168, 8192, 8448
Z_WIDTH = 8704
TAIL_ROWS = 512
ML_CHUNK = 128
CONV_HALO = 16
FFN_TILE = 512
MOE_SB = 1024
MOE_HALF = MOE_SB // 2
MOE_FF_TILE = 256
MOE_STEP_ROWS = 48
MOE_GATHER_ROWS = MOE_STEP_ROWS * (D_FF // MOE_FF_TILE)

F32 = jnp.float32
BF16 = jnp.bfloat16
NEG_INF = float("-inf")


def _tile(m, target):
    best = LANES
    for t in range(LANES, min(m, target) + 1, LANES):
        if m % t == 0:
            best = t
    assert m % best == 0
    return best


def _params(*sem):
    return pltpu.CompilerParams(dimension_semantics=sem, vmem_limit_bytes=VMEM_LIMIT)


def _rms(x, g):
    return x * lax.rsqrt(jnp.mean(x * x, axis=-1, keepdims=True) + EPS) * g


def _sigmoid(x):
    return 1.0 / (1.0 + jnp.exp(-x))


def _dot(a, b):
    return jnp.dot(a, b, preferred_element_type=F32)


def _dot_nt(a, b):
    return lax.dot_general(a, b, (((1,), (1,)), ((), ())), preferred_element_type=F32)


def _dot_tn(a, b):
    return lax.dot_general(a, b, (((0,), (0,)), ((), ())), preferred_element_type=F32)


def _norm_kernel(x_ref, g_ref, o_ref):
    o_ref[...] = _rms(x_ref[...], g_ref[...]).astype(o_ref.dtype)


def _rmsnorm(xs, g):
    m = xs.shape[0]
    bm = _tile(m, 512)
    return pl.pallas_call(
        _norm_kernel,
        out_shape=jax.ShapeDtypeStruct((m, D_MODEL), BF16),
        grid=(m // bm,),
        in_specs=[pl.BlockSpec((bm, D_MODEL), lambda i: (i, 0)),
                  pl.BlockSpec((1, D_MODEL), lambda i: (0, 0))],
        out_specs=pl.BlockSpec((bm, D_MODEL), lambda i: (i, 0)),
        compiler_params=_params("parallel"),
        name="rmsnorm",
    )(xs, g.reshape(1, D_MODEL))


def _mm_kernel(a_ref, w_ref, o_ref):
    o_ref[...] = _dot(a_ref[...], w_ref[...]).astype(o_ref.dtype)


def _matmul(a, w, out_dtype, bm, bn, name):
    m, k = a.shape
    n = w.shape[1]
    bm = _tile(m, bm)
    return pl.pallas_call(
        _mm_kernel,
        out_shape=jax.ShapeDtypeStruct((m, n), out_dtype),
        grid=(n // bn, m // bm),
        in_specs=[pl.BlockSpec((bm, k), lambda j, i: (i, 0)),
                  pl.BlockSpec((k, bn), lambda j, i: (0, j))],
        out_specs=pl.BlockSpec((bm, bn), lambda j, i: (i, j)),
        compiler_params=_params("parallel", "parallel"),
        name=name,
    )(a, w)


def _t5_bucket(rel):
    n = jnp.maximum(rel, 0)
    max_exact = NUM_BUCKETS // 2
    large = max_exact + (jnp.log(jnp.maximum(n, 1).astype(F32) / max_exact)
                         / math.log(MAX_DISTANCE / max_exact) * (NUM_BUCKETS - max_exact)).astype(jnp.int32)
    large = jnp.minimum(large, NUM_BUCKETS - 1)
    return jnp.where(n < max_exact, n, large)


def _bias_lookup(table, rel):
    onehot = (_t5_bucket(rel)[..., None] == jnp.arange(NUM_BUCKETS)).astype(F32)
    return jnp.einsum("...b,bh->h...", onehot, table.astype(F32), precision=lax.Precision.HIGHEST)


ATTN_GROUP_ORDER = (0, 2, 1, 3)


def _stack_group_rows(a):
    lead = a.shape[:-3]
    a = a.reshape(*lead, N_KV_HEADS, GQA_GROUP, BLOCK, a.shape[-1])
    a = jnp.take(a, jnp.array(ATTN_GROUP_ORDER), axis=len(lead) + 1)
    return a.reshape(*lead, N_KV_HEADS, GQA_GROUP * BLOCK, a.shape[-1])


def _attn_bias(table, nb):
    qi = jnp.arange(BLOCK)[:, None]
    ki = jnp.arange(2 * BLOCK)[None, :]
    rel_band = qi + BLOCK - ki
    blk3 = jnp.arange(3)[:, None, None]
    mask_band = (rel_band >= 0) & (rel_band < WINDOW) & ((blk3 - 1) * BLOCK + ki >= PAD)
    band = jnp.where(mask_band[:, None], _bias_lookup(table, rel_band)[None], NEG_INF)
    blk = jnp.arange(nb)[:, None, None]
    rel_meta = blk * BLOCK + qi[None] - (PAD + jnp.arange(N_META))
    meta = jnp.where((rel_meta >= WINDOW)[None], _bias_lookup(table, rel_meta), NEG_INF)
    meta = jnp.pad(jnp.moveaxis(meta, 1, 0), ((0, 0), (0, 0), (0, 0), (PAD, 0)), constant_values=NEG_INF)
    return _stack_group_rows(band), _stack_group_rows(meta)


def _swap_halves(x):
    return pltpu.roll(x.astype(F32), HEAD_DIM, axis=1).astype(x.dtype)


def _attn_kernel(q_ref, kp_ref, kc_ref, km_ref, vp_ref, vc_ref, vm_ref, bb_ref, bm_ref, sink_ref, o_ref):
    scale = HEAD_DIM ** -0.5
    assert math.frexp(scale)[0] == 0.5 and 2 * HEAD_DIM == LANES
    lane = lax.broadcasted_iota(jnp.int32, (1, LANES), 1)
    keep = (jnp.where(lane < HEAD_DIM, scale, 0.0).astype(BF16),
            jnp.where(lane < HEAD_DIM, 0.0, scale).astype(BF16))
    low = lax.broadcasted_iota(jnp.int32, (2 * BLOCK, LANES), 1) < HEAD_DIM
    chains = []
    for col in range(N_KV_HEADS // 2):
        kcols = slice(col * LANES, (col + 1) * LANES)
        k_nat = jnp.concatenate([kp_ref[:, kcols], kc_ref[:, kcols], km_ref[:, kcols]], axis=0)
        v_nat = jnp.concatenate([vp_ref[:, kcols], vc_ref[:, kcols], vm_ref[:, kcols]], axis=0)
        keys = (k_nat, _swap_halves(k_nat))
        vals = (v_nat, _swap_halves(v_nat))
        for half in range(2):
            h = 2 * col + half
            q0 = h * GQA_GROUP * HEAD_DIM
            q2 = jnp.concatenate([q_ref[:, q0:q0 + LANES], q_ref[:, q0 + LANES:q0 + 2 * LANES]], axis=0)
            for lane_half in range(2):
                which = 0 if lane_half == half else 1
                chains.append((h, lane_half, _dot_nt(q2 * keep[lane_half], keys[which]), vals[which]))
    probs = []
    for h, lane_half, s, _ in chains:
        rows = slice(lane_half * 2 * BLOCK, (lane_half + 1) * 2 * BLOCK)
        s0 = s[:, :BLOCK] + bb_ref[0, h, rows, :BLOCK]
        s1 = s[:, BLOCK:2 * BLOCK] + bb_ref[0, h, rows, BLOCK:]
        s2 = s[:, 2 * BLOCK:] + bm_ref[0, h, rows, :]
        sink = sink_ref[h, rows, :]
        mx = jnp.maximum(jnp.maximum(jnp.maximum(s0, s1), s2).max(-1, keepdims=True), sink)
        p0 = jnp.exp(s0 - mx)
        p1 = jnp.exp(s1 - mx)
        p2 = jnp.exp(s2 - mx)
        den = (p0 + p1 + p2).sum(-1, keepdims=True) + jnp.exp(sink - mx)
        probs.append((jnp.concatenate([p0, p1, p2], axis=1).astype(BF16), 1.0 / den))
    outs = [_dot(p, chain[3]) * rden for (p, rden), chain in zip(probs, chains)]
    out_cols = []
    for h in range(N_KV_HEADS):
        o = jnp.where(low, outs[2 * h], outs[2 * h + 1]).astype(o_ref.dtype)
        out_cols += [o[:BLOCK], o[BLOCK:]]
    o_ref[...] = jnp.concatenate(out_cols, axis=1)


def _attention(z, band, meta, sinks, b, seq):
    m = z.shape[0]
    per_seq = seq // BLOCK
    n_real = b * per_seq
    kcol, vcol = Z_AK // KV_WIDTH, Z_AV // KV_WIDTH
    sink_col = _stack_group_rows(jnp.broadcast_to(sinks.astype(F32)[:, None, None], (N_Q_HEADS, BLOCK, LANES)))
    kv = lambda col, f: pl.BlockSpec((BLOCK, KV_WIDTH), lambda s: (f(s), col))
    cur = lambda s: s
    prev = lambda s: jnp.where(s < n_real, jnp.where(s % per_seq == 0, n_real, s - 1), s)
    first = lambda s: n_real
    query_block = lambda s: jnp.where(s < n_real, s % per_seq + 1, 0)
    return pl.pallas_call(
        _attn_kernel,
        out_shape=jax.ShapeDtypeStruct((m, ATTN_WIDTH), BF16),
        grid=(m // BLOCK,),
        in_specs=[pl.BlockSpec((BLOCK, ATTN_WIDTH), lambda s: (s, Z_AQ // ATTN_WIDTH)),
                  kv(kcol, prev), kv(kcol, cur), kv(kcol, first),
                  kv(vcol, prev), kv(vcol, cur), kv(vcol, first),
                  pl.BlockSpec((1, N_KV_HEADS, GQA_GROUP * BLOCK, 2 * BLOCK),
                               lambda s: (jnp.minimum(query_block(s), 2), 0, 0, 0)),
                  pl.BlockSpec((1, N_KV_HEADS, GQA_GROUP * BLOCK, BLOCK), lambda s: (query_block(s), 0, 0, 0)),
                  pl.BlockSpec((N_KV_HEADS, GQA_GROUP * BLOCK, LANES), lambda s: (0, 0, 0))],
        out_specs=pl.BlockSpec((BLOCK, ATTN_WIDTH), lambda s: (s, 0)),
        compiler_params=_params("parallel"),
        name="swa_attention",
    )(z, z, z, z, z, z, z, band, meta, sink_col)


def _mlstm_chunk(qk_in, v_in, gate_in, mo_in, prefix_chunk, cw_ref, cb_ref, gb_ref, ng_ref, xa_ref, ct_ref, n_ref, m_ref):
    L = ML_CHUNK
    ii = lax.broadcasted_iota(jnp.int32, (L, L), 0)
    jj = lax.broadcasted_iota(jnp.int32, (L, L), 1)
    causal = jj <= ii
    lane = lax.broadcasted_iota(jnp.int32, (L, LANES), 1)
    k_scale = ML_QK_DIM ** -0.5
    xa_ref[CONV_HALO:CONV_HALO + L, :] = qk_in.astype(F32)
    acc = cb_ref[...]
    for j in range(CONV_WIDTH):
        off = CONV_HALO - (CONV_WIDTH - 1) + j
        acc = acc + cw_ref[j:j + 1, :] * xa_ref[off:off + L, :]
    qk = acc * _sigmoid(acc)
    xa_ref[0:CONV_HALO, :] = xa_ref[L:L + CONV_HALO, :]
    gpb = gate_in + gb_ref[...]
    log_sig = jnp.minimum(gpb, 0.0) - jnp.log1p(jnp.exp(-jnp.abs(gpb)))
    if prefix_chunk:
        valid = lax.broadcasted_iota(jnp.int32, (L, LANES), 0) >= PAD
        gx = jnp.where(lane < ML_HEADS, jnp.where(valid, gpb, NEG_INF), jnp.where(valid, log_sig, 0.0))
    else:
        gx = jnp.where(lane < ML_HEADS, gpb, log_sig)
    gxt = gx.T
    outs = []
    for h in range(ML_HEADS):
        q = qk[:, h * ML_QK_DIM:(h + 1) * ML_QK_DIM].astype(BF16)
        k = qk[:, ML_QK_WIDTH + h * ML_QK_DIM:ML_QK_WIDTH + (h + 1) * ML_QK_DIM] * k_scale
        v = v_in[:, h * ML_V_DIM:(h + 1) * ML_V_DIM]
        ig_col = gx[:, h:h + 1]
        lf_col = gx[:, ML_HEADS + h:ML_HEADS + h + 1]
        ig_row = gxt[h:h + 1, :]
        lf_row = gxt[ML_HEADS + h:ML_HEADS + h + 1, :]
        b_col = jnp.sum(jnp.where(causal, lf_row, 0.0), axis=-1, keepdims=True)
        b_row = jnp.sum(jnp.where(ii <= jj, lf_col, 0.0), axis=0, keepdims=True)
        m_prev = m_ref[h:h + 1, 0:1]
        log_d = jnp.where(causal, b_col - b_row + ig_row, NEG_INF)
        m_inter = b_col + m_prev
        m_out = jnp.maximum(m_inter, log_d.max(-1, keepdims=True))
        d = jnp.exp(log_d - m_out)
        inter = jnp.exp(m_inter - m_out)
        ct = ct_ref[h]
        n_prev = n_ref[h:h + 1, :]
        qk_scores = _dot_nt(q, k.astype(BF16))
        q_state = _dot(q, ct.astype(BF16))
        q_norm = jnp.sum(q.astype(F32) * n_prev, axis=-1, keepdims=True)
        b_last = b_col[L - 1:L, :]
        log_w = b_last - b_col + ig_col
        m_new = jnp.maximum(b_last + m_prev, log_w.max(0, keepdims=True))
        decay = jnp.exp(b_last + m_prev - m_new)
        kw = k * jnp.exp(log_w - m_new)
        s = qk_scores * d
        vcols = slice(h * ML_V_DIM, (h + 1) * ML_V_DIM)
        num = _dot(s.astype(BF16), v) + inter * q_state
        den = s.sum(-1, keepdims=True) + inter * q_norm
        hh = num / jnp.maximum(jnp.abs(den), jnp.exp(-m_out))
        ct_ref[h] = decay * ct + _dot_tn(kw.astype(BF16), v)
        n_ref[h:h + 1, :] = decay * n_prev + kw.sum(0, keepdims=True)
        m_ref[h:h + 1, :] = jnp.broadcast_to(m_new, (1, LANES))
        hn = hh * lax.rsqrt(jnp.mean(hh * hh, axis=-1, keepdims=True) + EPS) * ng_ref[:, vcols]
        outs.append(_sigmoid(mo_in[:, vcols].astype(F32)) * hn)
    return jnp.concatenate(outs, axis=1)


def _mlstm_kernel(blocks_per_seq, qk_ref, v_ref, gate_ref, mo_ref, cw_ref, cb_ref, gb_ref, ng_ref, o_ref,
                  xa_ref, ct_ref, n_ref, m_ref, xa0_ref, ct0_ref, n0_ref, m0_ref):
    L = ML_CHUNK
    step = pl.program_id(0)
    params = (cw_ref, cb_ref, gb_ref, ng_ref, xa_ref, ct_ref, n_ref, m_ref)

    @pl.when(step == 0)
    def _():
        xa_ref[0:CONV_HALO, :] = jnp.zeros((CONV_HALO, 2 * ML_QK_WIDTH), F32)
        ct_ref[...] = jnp.zeros_like(ct_ref)
        n_ref[...] = jnp.zeros_like(n_ref)
        m_ref[...] = jnp.zeros_like(m_ref)
        out = _mlstm_chunk(qk_ref[0:L, :], v_ref[0:L, :], gate_ref[0:L, :], mo_ref[0:L, :], True, *params)
        o_ref[0:L, :] = out.astype(o_ref.dtype)
        o_ref[L:, :] = jnp.zeros((o_ref.shape[0] - L, o_ref.shape[1]), o_ref.dtype)
        xa0_ref[...] = xa_ref[0:CONV_HALO, :]
        ct0_ref[...] = ct_ref[...]
        n0_ref[...] = n_ref[...]
        m0_ref[...] = m_ref[...]

    @pl.when(step > 0)
    def _():
        @pl.when((step - 1) % blocks_per_seq == 0)
        def _():
            xa_ref[0:CONV_HALO, :] = xa0_ref[...]
            ct_ref[...] = ct0_ref[...]
            n_ref[...] = n0_ref[...]
            m_ref[...] = m0_ref[...]

        def chunk(c, carry):
            rows = pl.ds(pl.multiple_of(c * L, L), L)
            out = _mlstm_chunk(qk_ref[rows, :], v_ref[rows, :], gate_ref[rows, :], mo_ref[rows, :], False, *params)
            o_ref[rows, :] = out.astype(o_ref.dtype)
            return carry

        lax.fori_loop(0, TAIL_ROWS // L, chunk, 0)


def _mlstm(z, gates, conv_w, conv_b, igate_b, fgate_b, norm_g, b, seq):
    m = z.shape[0]
    assert seq % TAIL_ROWS == 0 and TAIL_ROWS % ML_CHUNK == 0
    n_seq_blocks = b * seq // TAIL_ROWS
    gate_bias = jnp.zeros((1, LANES), F32).at[0, :ML_HEADS].set(igate_b).at[0, ML_HEADS:2 * ML_HEADS].set(fgate_b)
    row_blk = lambda s: jnp.where(s == 0, n_seq_blocks, s - 1)
    rows = lambda w, c: pl.BlockSpec((TAIL_ROWS, w), lambda s: (row_blk(s), c))
    full = lambda r, c: pl.BlockSpec((r, c), lambda s: (0, 0))
    return pl.pallas_call(
        functools.partial(_mlstm_kernel, seq // TAIL_ROWS),
        out_shape=jax.ShapeDtypeStruct((m, ML_V_WIDTH), BF16),
        grid=(n_seq_blocks + 1,),
        in_specs=[rows(ML_V_WIDTH, Z_MQK // ML_V_WIDTH), rows(ML_V_WIDTH, Z_MV // ML_V_WIDTH), rows(LANES, 0),
                  rows(ML_V_WIDTH, Z_MO // ML_V_WIDTH),
                  full(CONV_WIDTH, 2 * ML_QK_WIDTH), full(1, 2 * ML_QK_WIDTH), full(1, LANES), full(1, ML_V_WIDTH)],
        out_specs=rows(ML_V_WIDTH, 0),
        scratch_shapes=[pltpu.VMEM((ML_CHUNK + CONV_HALO, 2 * ML_QK_WIDTH), F32),
                        pltpu.VMEM((ML_HEADS, ML_QK_DIM, ML_V_DIM), F32),
                        pltpu.VMEM((SUBLANES, LANES), F32),
                        pltpu.VMEM((SUBLANES, LANES), F32),
                        pltpu.VMEM((CONV_HALO, 2 * ML_QK_WIDTH), F32),
                        pltpu.VMEM((ML_HEADS, ML_QK_DIM, ML_V_DIM), F32),
                        pltpu.VMEM((SUBLANES, LANES), F32),
                        pltpu.VMEM((SUBLANES, LANES), F32)],
        compiler_params=_params("arbitrary"),
        name="mlstm",
    )(z, z, gates, z, conv_w.astype(F32), conv_b.reshape(1, -1).astype(F32), gate_bias,
      norm_g.reshape(1, -1).astype(F32))


def _merge_kernel(with_router, attn_ref, ml_ref, ga_ref, gm_ref, xs_ref, wa_ref, wm_ref, wo_ref, g_ref, *rest):
    if with_router:
        wr_hi_ref, wr_lo_ref, br_ref, xs_out, h_out, route_out = rest
    else:
        xs_out, h_out = rest
    a = _dot(attn_ref[...], wa_ref[...])
    m = _dot(ml_ref[...], wm_ref[...])
    y = _sigmoid(ga_ref[...].astype(F32)) * a + _sigmoid(gm_ref[...].astype(F32)) * m
    xs_new = xs_ref[...] + _dot(y.astype(BF16), wo_ref[...])
    xs_out[...] = xs_new
    hn = _rms(xs_new, g_ref[...])
    if not with_router:
        h_out[...] = hn.astype(h_out.dtype)
    else:
        half = D_MODEL // 2
        lo = lax.bitcast_convert_type(hn[:, :half].astype(BF16).astype(F32), jnp.uint32) >> 16
        hi = lax.bitcast_convert_type(hn[:, half:].astype(BF16).astype(F32), jnp.uint32) & jnp.uint32(0xFFFF0000)
        h_out[...] = lo | hi
        hn_hi = hn.astype(BF16)
        hn_lo = (hn - hn_hi.astype(F32)).astype(BF16)
        logits = (_dot(hn_hi, wr_hi_ref[...]) + (_dot(hn_lo, wr_hi_ref[...]) + _dot(hn_hi, wr_lo_ref[...]))
                  + br_ref[...])
        lane = lax.broadcasted_iota(jnp.int32, logits.shape, 1).astype(F32)
        l1 = logits.max(-1, keepdims=True)
        i1 = jnp.min(jnp.where(logits == l1, lane, float(LANES)), axis=-1, keepdims=True)
        rest_logits = jnp.where(lane == i1, NEG_INF, logits)
        l2 = rest_logits.max(-1, keepdims=True)
        i2 = jnp.min(jnp.where(rest_logits == l2, lane, float(LANES)), axis=-1, keepdims=True)
        e = jnp.exp(l2 - l1)
        w1 = 1.0 / (1.0 + e)
        w2 = e / (1.0 + e)
        route_out[...] = jnp.where(lane == 0, i1, jnp.where(lane == 1, i2, jnp.where(lane == 2, w1,
                                   jnp.where(lane == 3, w2, 0.0))))


def _merge(attn, ml, z, xs, wa, wm, wo, g_next, router=None):
    m = xs.shape[0]
    bm = _tile(m, 256)
    row = lambda w, c: pl.BlockSpec((bm, w), lambda i: (i, c))
    const = lambda r, c: pl.BlockSpec((r, c), lambda i: (0, 0), pipeline_mode=pl.Buffered(1))
    in_specs = [row(ATTN_WIDTH, 0), row(ML_V_WIDTH, 0), row(D_MODEL, Z_GA // D_MODEL), row(D_MODEL, Z_GM // D_MODEL),
                row(D_MODEL, 0), const(ATTN_WIDTH, D_MODEL), const(ML_V_WIDTH, D_MODEL), const(D_MODEL, D_MODEL),
                const(1, D_MODEL)]
    args = [attn, ml, z, z, xs, wa, wm, wo, g_next.reshape(1, D_MODEL)]
    if router is None:
        out_shape = [jax.ShapeDtypeStruct((m, D_MODEL), F32), jax.ShapeDtypeStruct((m, D_MODEL), BF16)]
        out_specs = [row(D_MODEL, 0), row(D_MODEL, 0)]
    else:
        out_shape = [jax.ShapeDtypeStruct((m, D_MODEL), F32), jax.ShapeDtypeStruct((m, D_MODEL // 2), jnp.uint32)]
        out_specs = [row(D_MODEL, 0), row(D_MODEL // 2, 0)]
    if router is not None:
        w_router, b_router = router
        wr = jnp.zeros((D_MODEL, LANES), F32).at[:, :N_EXPERTS].set(w_router.astype(F32))
        br = jnp.full((1, LANES), NEG_INF, F32).at[0, :N_EXPERTS].set(b_router.astype(F32))
        wr_hi = wr.astype(BF16)
        wr_lo = (wr - wr_hi.astype(F32)).astype(BF16)
        in_specs += [const(D_MODEL, LANES), const(D_MODEL, LANES), const(1, LANES)]
        args += [wr_hi, wr_lo, br]
        out_shape.append(jax.ShapeDtypeStruct((m, LANES), F32))
        out_specs.append(row(LANES, 0))
    return pl.pallas_call(
        functools.partial(_merge_kernel, router is not None),
        out_shape=out_shape,
        grid=(m // bm,),
        in_specs=in_specs,
        out_specs=out_specs,
        compiler_params=_params("parallel"),
        name="merge_out_proj",
    )(*args)


def _ffn_kernel(h_ref, xs_hbm, wg_ref, wu_ref, wd_ref, g_ref, xs_out, h_out, sem):
    i = pl.program_id(0)
    f = pl.program_id(1)
    bm = xs_out.shape[0]
    residual = pltpu.make_async_copy(xs_hbm.at[pl.ds(pl.multiple_of(i * bm, bm), bm), :], xs_out, sem)

    @pl.when(f == 0)
    def _():
        residual.start()

    hb = h_ref[...]
    g = _dot(hb, wg_ref[...])
    u = _dot(hb, wu_ref[...])
    act = (g * _sigmoid(g) * u).astype(BF16)

    @pl.when(f == 0)
    def _():
        residual.wait()

    xs_out[...] += _dot(act, wd_ref[...])

    @pl.when(f == pl.num_programs(1) - 1)
    def _():
        h_out[...] = _rms(xs_out[...], g_ref[...]).astype(h_out.dtype)


def _dense_ffn(h, xs, wg, wu, wd, g_next, bf=FFN_TILE):
    m = xs.shape[0]
    bm = _tile(m, 1024)
    return pl.pallas_call(
        _ffn_kernel,
        out_shape=[jax.ShapeDtypeStruct((m, D_MODEL), F32), jax.ShapeDtypeStruct((m, D_MODEL), BF16)],
        grid=(m // bm, D_FF // bf),
        in_specs=[pl.BlockSpec((bm, D_MODEL), lambda i, f: (i, 0)),
                  pl.BlockSpec(memory_space=pl.ANY),
                  pl.BlockSpec((D_MODEL, bf), lambda i, f: (0, f)),
                  pl.BlockSpec((D_MODEL, bf), lambda i, f: (0, f)),
                  pl.BlockSpec((bf, D_MODEL), lambda i, f: (f, 0)),
                  pl.BlockSpec((1, D_MODEL), lambda i, f: (0, 0))],
        out_specs=[pl.BlockSpec((bm, D_MODEL), lambda i, f: (i, 0)),
                   pl.BlockSpec((bm, D_MODEL), lambda i, f: (i, 0))],
        scratch_shapes=[pltpu.SemaphoreType.DMA(())],
        compiler_params=_params("arbitrary", "arbitrary"),
        name="dense_swiglu",
    )(h, xs, wg, wu, wd, g_next.reshape(1, D_MODEL))


def _row_copy(src_hbm, dst_vmem, sem, src_row, dst_row):
    return pltpu.make_async_copy(src_hbm.at[pl.ds(src_row, 1), :], dst_vmem.at[pl.ds(dst_row, 1), :], sem)


def _moe_gather_copy(h_hbm, gbuf, sem, slot, src_row, dst_row):
    return pltpu.make_async_copy(h_hbm.at[pl.ds(src_row, 1), :], gbuf.at[slot, pl.ds(dst_row, 1), :], sem.at[slot])


def _moe_gather_wait(h_hbm, gbuf, sem, slot):
    pltpu.make_async_copy(h_hbm.at[pl.ds(0, MOE_GATHER_ROWS), :], gbuf.at[slot], sem.at[slot]).wait()


def _moe_kernel(sb_e_ref, sb_rows_ref, row_tok_ref, h_hbm, wg_ref, wu_ref, wd_ref, y_ref, gbuf, xb16, sem):
    sb = pl.program_id(0)
    f = pl.program_id(1)
    n_sb = pl.num_programs(0)
    nf = pl.num_programs(1)
    kind = sb_rows_ref[sb]
    slot = sb % 2
    half = D_MODEL // 2

    @pl.when(jnp.logical_and(sb == 0, f == 0))
    def _():
        def issue(r, c):
            _moe_gather_copy(h_hbm, gbuf, sem, 0, row_tok_ref[r], r).start()
            return c
        lax.fori_loop(0, MOE_GATHER_ROWS, issue, 0)

    @pl.when(f == 0)
    def _():
        y_ref[...] = jnp.zeros_like(y_ref)
        prev_kind = sb_rows_ref[jnp.maximum(sb - 1, 0)]

        @pl.when(jnp.logical_or(sb == 0, prev_kind > 0))
        def _():
            _moe_gather_wait(h_hbm, gbuf, sem, slot)

        @pl.when(kind > 0)
        def _():
            w = gbuf[slot, 0:MOE_SB, :]
            xb16[:, :half] = lax.bitcast_convert_type(w << 16, F32).astype(BF16)
            xb16[:, half:] = lax.bitcast_convert_type(w & jnp.uint32(0xFFFF0000), F32).astype(BF16)

    def compute(rows):
        base = (sb + 1) * MOE_SB + f * MOE_STEP_ROWS
        for r in range(MOE_STEP_ROWS):
            _moe_gather_copy(h_hbm, gbuf, sem, 1 - slot, row_tok_ref[base + r], f * MOE_STEP_ROWS + r).start()
        xb = xb16[0:rows, :]
        g = _dot(xb, wg_ref[0].astype(BF16))
        u = _dot(xb, wu_ref[0].astype(BF16))
        act = (g * _sigmoid(g) * u).astype(BF16)
        y_ref[0:rows, :] += _dot(act, wd_ref[0].astype(BF16))

    @pl.when(kind == 2)
    def _():
        compute(MOE_SB)

    @pl.when(kind == 1)
    def _():
        compute(MOE_HALF)

    @pl.when(jnp.logical_and(jnp.logical_and(sb == n_sb - 1, f == nf - 1), kind > 0))
    def _():
        _moe_gather_wait(h_hbm, gbuf, sem, 1 - slot)


def _moe_experts(h_packed, sb_e, sb_rows, row_tok, wg, wu, wd, n_sb, bf=MOE_FF_TILE):
    nf = D_FF // bf
    assert nf * MOE_STEP_ROWS == MOE_GATHER_ROWS
    ftile = lambda i, f, rows: jnp.where(rows[i] > 0, f, nf - 1)
    grid_spec = pltpu.PrefetchScalarGridSpec(
        num_scalar_prefetch=3,
        grid=(n_sb, nf),
        in_specs=[pl.BlockSpec(memory_space=pl.ANY),
                  pl.BlockSpec((1, D_MODEL, bf), lambda i, f, se, sr, rt: (se[i], 0, ftile(i, f, sr))),
                  pl.BlockSpec((1, D_MODEL, bf), lambda i, f, se, sr, rt: (se[i], 0, ftile(i, f, sr))),
                  pl.BlockSpec((1, bf, D_MODEL), lambda i, f, se, sr, rt: (se[i], ftile(i, f, sr), 0))],
        out_specs=pl.BlockSpec((MOE_SB, D_MODEL), lambda i, f, se, sr, rt: (i, 0)),
        scratch_shapes=[pltpu.VMEM((2, MOE_GATHER_ROWS, D_MODEL // 2), jnp.uint32),
                        pltpu.VMEM((MOE_SB, D_MODEL), BF16),
                        pltpu.SemaphoreType.DMA((2,))],
    )
    return pl.pallas_call(
        _moe_kernel,
        out_shape=jax.ShapeDtypeStruct((n_sb * MOE_SB, D_MODEL), F32),
        grid_spec=grid_spec,
        compiler_params=_params("arbitrary", "arbitrary"),
        name="moe_experts",
    )(sb_e, sb_rows, row_tok, h_packed, wg, wu, wd)


def _combine_kernel(pos_ref, xs_ref, route_ref, y_hbm, g_ref, o_ref, ybuf, sem):
    step = pl.program_id(0)
    n_steps = pl.num_programs(0)
    slot = step % 2

    def gather(s, dst_slot):
        base = s * BLOCK

        def issue(r, c):
            for k in range(TOP_K):
                _row_copy(y_hbm, ybuf.at[dst_slot, k], sem.at[dst_slot], pos_ref[TOP_K * (base + r) + k], r).start()
            return c
        lax.fori_loop(0, BLOCK, issue, 0, unroll=4)

    @pl.when(step == 0)
    def _():
        gather(step, slot)

    @pl.when(step + 1 < n_steps)
    def _():
        gather(step + 1, 1 - slot)

    for k in range(TOP_K):
        pltpu.make_async_copy(y_hbm.at[pl.ds(0, BLOCK), :], ybuf.at[slot, k], sem.at[slot]).wait()
    route = route_ref[...]
    moe = ybuf[slot, 0] * route[:, 2:3] + ybuf[slot, 1] * route[:, 3:4]
    o_ref[...] = _rms(xs_ref[...] + moe, g_ref[...])


def _moe_combine(pos, xs, route, yb, g_final, m_real):
    grid_spec = pltpu.PrefetchScalarGridSpec(
        num_scalar_prefetch=1,
        grid=(m_real // BLOCK,),
        in_specs=[pl.BlockSpec((BLOCK, D_MODEL), lambda t, p: (t, 0)),
                  pl.BlockSpec((BLOCK, LANES), lambda t, p: (t, 0)),
                  pl.BlockSpec(memory_space=pl.ANY),
                  pl.BlockSpec((1, D_MODEL), lambda t, p: (0, 0))],
        out_specs=pl.BlockSpec((BLOCK, D_MODEL), lambda t, p: (t, 0)),
        scratch_shapes=[pltpu.VMEM((2, TOP_K, BLOCK, D_MODEL), F32), pltpu.SemaphoreType.DMA((2,))],
    )
    return pl.pallas_call(
        _combine_kernel,
        out_shape=jax.ShapeDtypeStruct((m_real, D_MODEL), F32),
        grid_spec=grid_spec,
        compiler_params=_params("arbitrary"),
        name="moe_combine_final_norm",
    )(pos, xs, route, yb, g_final.reshape(1, D_MODEL))


def _moe_routing(route, n_tok):
    n_assign = n_tok * TOP_K
    n_sb = n_assign // MOE_SB + N_EXPERTS
    e_flat = route[:, :TOP_K].astype(jnp.int32).reshape(n_assign)
    onehot = (e_flat[:, None] == jnp.arange(N_EXPERTS, dtype=jnp.int32)[None, :]).astype(jnp.int32)
    csum = jnp.cumsum(onehot, axis=0)
    rank = jnp.sum((csum - onehot) * onehot, axis=1)
    counts = csum[-1]
    sb_count = (counts + MOE_SB - 1) // MOE_SB
    sb_end = jnp.cumsum(sb_count)
    sb_start = sb_end - sb_count
    dest = (jnp.sum((sb_start * MOE_SB)[None, :] * onehot, axis=1) + rank).astype(jnp.int32)
    row_tok = jnp.zeros(((n_sb + 2) * MOE_SB,), jnp.int32).at[dest].set(
        jnp.arange(n_assign, dtype=jnp.int32) // TOP_K, unique_indices=True)
    sb = jnp.arange(n_sb, dtype=jnp.int32)
    sb_e = jnp.sum((sb[:, None] >= sb_end[None, :]).astype(jnp.int32), axis=1)
    valid = sb < sb_end[-1]
    last_e = jnp.sum((sb_end[-1] - 1 >= sb_end).astype(jnp.int32))
    sb_e = jnp.where(valid, sb_e, last_e).astype(jnp.int32)
    rows_here = counts[sb_e] - (sb - sb_start[sb_e]) * MOE_SB
    sb_rows = jnp.where(valid, jnp.where(rows_here > MOE_HALF, 2, 1), 0).astype(jnp.int32)
    return dest, row_tok, sb_e, sb_rows, n_sb


def _pack_w_in(w):
    o = {}
    off = 0
    for name, size in (("aq", ATTN_WIDTH), ("ak", KV_WIDTH), ("av", KV_WIDTH), ("mq", ML_QK_WIDTH),
                       ("mk", ML_QK_WIDTH), ("mv", ML_V_WIDTH), ("mi", ML_HEADS), ("mf", ML_HEADS),
                       ("mo", ML_V_WIDTH), ("ga", D_MODEL), ("gm", D_MODEL)):
        o[name] = w[:, off:off + size]
        off += size
    main = jnp.concatenate([o["ga"], o["gm"], o["aq"], o["mq"], o["mk"], o["mv"], o["mo"], o["ak"], o["av"]],
                           axis=1).astype(BF16)
    gates = jnp.concatenate([o["mi"], o["mf"], jnp.zeros((D_MODEL, LANES - 2 * ML_HEADS), w.dtype)], axis=1)
    return main, gates.astype(BF16)


def kernel(x, meta_tokens, rel_bias_table, w_in, attn_sinks, conv_w, conv_b, igate_b, fgate_b, mlstm_norm_g,
           w_attn_up, w_mlstm_up, w_out, norm_mix_g, norm_ffn_g, w_ffn_gate, w_ffn_up, w_ffn_down, w_router,
           b_router, w_moe_gate, w_moe_up, w_moe_down, final_norm_g):
    b, seq, _ = x.shape
    depth = w_in.shape[0]
    m_real = b * seq
    m = m_real + TAIL_ROWS
    assert depth == 2 and seq % ML_CHUNK == 0 and seq % BLOCK == 0 and m_real % TAIL_ROWS == 0
    tail = jnp.concatenate([jnp.zeros((PAD, D_MODEL), x.dtype), meta_tokens.astype(x.dtype),
                            jnp.zeros((TAIL_ROWS - PREFIX, D_MODEL), x.dtype)], axis=0)
    xs = jnp.concatenate([x.reshape(m_real, D_MODEL), tail], axis=0)
    band, meta = _attn_bias(rel_bias_table, seq // BLOCK + 1)
    h = _rmsnorm(xs, norm_mix_g[0])
    out = None
    for layer in range(depth):
        w_main, w_gates = _pack_w_in(w_in[layer])
        z = _matmul(h, w_main, BF16, 1024, Z_WIDTH // 4, "in_proj")
        gates = _matmul(h, w_gates, F32, 1024, LANES, "in_proj_gates")
        attn = _attention(z, band, meta, attn_sinks[layer], b, seq)
        ml = _mlstm(z, gates, conv_w[layer], conv_b[layer], igate_b[layer], fgate_b[layer], mlstm_norm_g[layer],
                    b, seq)
        wa, wm, wo = (w_attn_up[layer].astype(BF16), w_mlstm_up[layer].astype(BF16), w_out[layer].astype(BF16))
        i = layer // 2
        if layer % 2 == 0:
            xs, h = _merge(attn, ml, z, xs, wa, wm, wo, norm_ffn_g[layer])
            xs, h = _dense_ffn(h, xs, w_ffn_gate[i].astype(BF16), w_ffn_up[i].astype(BF16),
                               w_ffn_down[i].astype(BF16), norm_mix_g[layer + 1])
        else:
            xs, h_packed, route = _merge(attn, ml, z, xs, wa, wm, wo, norm_ffn_g[layer],
                                         router=(w_router[i], b_router[i]))
            n_tok = m_real + PREFIX
            dest, row_tok, sb_e, sb_rows, n_sb = _moe_routing(route[:n_tok], n_tok)
            yb = _moe_experts(h_packed, sb_e, sb_rows, row_tok, w_moe_gate[i], w_moe_up[i], w_moe_down[i], n_sb)
            out = _moe_combine(dest, xs, route, yb, final_norm_g, m_real).reshape(b, seq, D_MODEL)
    return out
```

```python
import functools
import math

import jax
import jax.numpy as jnp
from jax import lax
from jax.experimental import pallas as pl
from jax.experimental.pallas import tpu as pltpu

D_MODEL = 2048
N_META = 16
BLOCK = 128
PREFIX = BLOCK
PAD = PREFIX - N_META
HEAD_DIM = 64
N_Q_HEADS = 16
N_KV_HEADS = 4
GQA_GROUP = 4
ATTN_WIDTH = N_Q_HEADS * HEAD_DIM
KV_WIDTH = N_KV_HEADS * HEAD_DIM
WINDOW = 128
NUM_BUCKETS = 32
MAX_DISTANCE = 128
ML_HEADS = 4
ML_V_WIDTH = D_MODEL // 2
ML_V_DIM = ML_V_WIDTH // ML_HEADS
ML_QK_DIM = ML_V_DIM // 2
ML_QK_WIDTH = ML_HEADS * ML_QK_DIM
CONV_WIDTH = 4
D_FF = 11 * D_MODEL // 4
N_EXPERTS = 8
TOP_K = 2
EPS = 1e-6

LANES = 128
SUBLANES = 8
BF16_ROWS = 16
VMEM_LIMIT = 56 * 1024 * 1024

Z_GA, Z_GM, Z_AQ, Z_MQK, Z_MV, Z_MO, Z_AK, Z_AV = 0, 2048, 4096, 5120, 6144, 7168, 8192, 8448
Z_WIDTH = 8704
TAIL_ROWS = 512
ML_CHUNK = 128
CONV_HALO = 16
FFN_TILE = 512
MOE_SB = 1024
MOE_PARTS = 4
MOE_PART_ROWS = MOE_SB // MOE_PARTS
MOE_FF_TILE = 256
MOE_STEP_ROWS = 48
MOE_GATHER_ROWS = MOE_STEP_ROWS * (D_FF // MOE_FF_TILE)

F32 = jnp.float32
BF16 = jnp.bfloat16
NEG_INF = float("-inf")


def _tile(m, target):
    best = LANES
    for t in range(LANES, min(m, target) + 1, LANES):
        if m % t == 0:
            best = t
    assert m % best == 0
    return best


def _params(*sem):
    return pltpu.CompilerParams(dimension_semantics=sem, vmem_limit_bytes=VMEM_LIMIT)


def _rms(x, g):
    return x * lax.rsqrt(jnp.mean(x * x, axis=-1, keepdims=True) + EPS) * g


def _sigmoid(x):
    return 1.0 / (1.0 + jnp.exp(-x))


def _dot(a, b):
    return jnp.dot(a, b, preferred_element_type=F32)


def _dot_nt(a, b):
    return lax.dot_general(a, b, (((1,), (1,)), ((), ())), preferred_element_type=F32)


def _dot_tn(a, b):
    return lax.dot_general(a, b, (((0,), (0,)), ((), ())), preferred_element_type=F32)


def _split_rows(i, n_body_tiles, body_ref, tail_ref):
    return jnp.where(i < n_body_tiles, body_ref[...], tail_ref[...])


def _split_specs(bm, n_body_tiles):
    return [pl.BlockSpec((bm, D_MODEL), lambda i: (jnp.minimum(i, n_body_tiles - 1), 0)),
            pl.BlockSpec((bm, D_MODEL), lambda i: (jnp.maximum(i - n_body_tiles, 0), 0))]


def _norm_kernel(n_body_tiles, x_ref, tail_ref, g_ref, o_ref):
    x = _split_rows(pl.program_id(0), n_body_tiles, x_ref, tail_ref)
    o_ref[...] = _rms(x, g_ref[...]).astype(o_ref.dtype)


def _rmsnorm(x_body, x_tail, g):
    m = x_body.shape[0] + x_tail.shape[0]
    bm = _tile(x_tail.shape[0], 512)
    assert x_body.shape[0] % bm == 0
    n_body_tiles = x_body.shape[0] // bm
    return pl.pallas_call(
        functools.partial(_norm_kernel, n_body_tiles),
        out_shape=jax.ShapeDtypeStruct((m, D_MODEL), BF16),
        grid=(m // bm,),
        in_specs=_split_specs(bm, n_body_tiles) + [pl.BlockSpec((1, D_MODEL), lambda i: (0, 0))],
        out_specs=pl.BlockSpec((bm, D_MODEL), lambda i: (i, 0)),
        compiler_params=_params("parallel"),
        name="rmsnorm",
    )(x_body, x_tail, g.reshape(1, D_MODEL))


def _mm_kernel(a_ref, w_ref, o_ref):
    o_ref[...] = _dot(a_ref[...], w_ref[...]).astype(o_ref.dtype)


def _matmul(a, w, out_dtype, bm, bn, name):
    m, k = a.shape
    n = w.shape[1]
    bm = _tile(m, bm)
    return pl.pallas_call(
        _mm_kernel,
        out_shape=jax.ShapeDtypeStruct((m, n), out_dtype),
        grid=(n // bn, m // bm),
        in_specs=[pl.BlockSpec((bm, k), lambda j, i: (i, 0)),
                  pl.BlockSpec((k, bn), lambda j, i: (0, j))],
        out_specs=pl.BlockSpec((bm, bn), lambda j, i: (i, j)),
        compiler_params=_params("parallel", "parallel"),
        name=name,
    )(a, w)


def _t5_bucket(rel):
    n = jnp.maximum(rel, 0)
    max_exact = NUM_BUCKETS // 2
    large = max_exact + (jnp.log(jnp.maximum(n, 1).astype(F32) / max_exact)
                         / math.log(MAX_DISTANCE / max_exact) * (NUM_BUCKETS - max_exact)).astype(jnp.int32)
    large = jnp.minimum(large, NUM_BUCKETS - 1)
    return jnp.where(n < max_exact, n, large)


def _bias_lookup(table, rel):
    onehot = (_t5_bucket(rel)[..., None] == jnp.arange(NUM_BUCKETS)).astype(F32)
    return jnp.einsum("...b,bh->h...", onehot, table.astype(F32), precision=lax.Precision.HIGHEST)


ATTN_GROUP_ORDER = (0, 2, 1, 3)


def _stack_group_rows(a):
    lead = a.shape[:-3]
    a = a.reshape(*lead, N_KV_HEADS, GQA_GROUP, BLOCK, a.shape[-1])
    a = jnp.take(a, jnp.array(ATTN_GROUP_ORDER), axis=len(lead) + 1)
    return a.reshape(*lead, N_KV_HEADS, GQA_GROUP * BLOCK, a.shape[-1])


def _attn_bias(table, nb):
    qi = jnp.arange(BLOCK)[:, None]
    ki = jnp.arange(2 * BLOCK)[None, :]
    rel_band = qi + BLOCK - ki
    blk3 = jnp.arange(3)[:, None, None]
    mask_band = (rel_band >= 0) & (rel_band < WINDOW) & ((blk3 - 1) * BLOCK + ki >= PAD)
    band = jnp.where(mask_band[:, None], _bias_lookup(table, rel_band)[None], NEG_INF)
    blk = jnp.arange(nb)[:, None, None]
    rel_meta = blk * BLOCK + qi[None] - (PAD + jnp.arange(N_META))
    meta = jnp.where((rel_meta >= WINDOW)[None], _bias_lookup(table, rel_meta), NEG_INF)
    meta = jnp.pad(jnp.moveaxis(meta, 1, 0), ((0, 0), (0, 0), (0, 0), (PAD, 0)), constant_values=NEG_INF)
    return _stack_group_rows(band), _stack_group_rows(meta)


def _swap_halves(x):
    return pltpu.roll(x.astype(F32), HEAD_DIM, axis=1).astype(x.dtype)


def _attn_kernel(q_ref, kp_ref, kc_ref, km_ref, vp_ref, vc_ref, vm_ref, bb_ref, bm_ref, sink_ref, o_ref):
    scale = HEAD_DIM ** -0.5
    assert math.frexp(scale)[0] == 0.5 and 2 * HEAD_DIM == LANES
    lane = lax.broadcasted_iota(jnp.int32, (1, LANES), 1)
    keep = (jnp.where(lane < HEAD_DIM, scale, 0.0).astype(BF16),
            jnp.where(lane < HEAD_DIM, 0.0, scale).astype(BF16))
    low = lax.broadcasted_iota(jnp.int32, (2 * BLOCK, LANES), 1) < HEAD_DIM
    chains = []
    for col in range(N_KV_HEADS // 2):
        kcols = slice(col * LANES, (col + 1) * LANES)
        k_nat = jnp.concatenate([kp_ref[:, kcols], kc_ref[:, kcols], km_ref[:, kcols]], axis=0)
        v_nat = jnp.concatenate([vp_ref[:, kcols], vc_ref[:, kcols], vm_ref[:, kcols]], axis=0)
        keys = (k_nat, _swap_halves(k_nat))
        vals = (v_nat, _swap_halves(v_nat))
        for half in range(2):
            h = 2 * col + half
            q0 = h * GQA_GROUP * HEAD_DIM
            q2 = jnp.concatenate([q_ref[:, q0:q0 + LANES], q_ref[:, q0 + LANES:q0 + 2 * LANES]], axis=0)
            for lane_half in range(2):
                which = 0 if lane_half == half else 1
                chains.append((h, lane_half, _dot_nt(q2 * keep[lane_half], keys[which]), vals[which]))
    probs = []
    for h, lane_half, s, _ in chains:
        rows = slice(lane_half * 2 * BLOCK, (lane_half + 1) * 2 * BLOCK)
        s0 = s[:, :BLOCK] + bb_ref[0, h, rows, :BLOCK]
        s1 = s[:, BLOCK:2 * BLOCK] + bb_ref[0, h, rows, BLOCK:]
        s2 = s[:, 2 * BLOCK:] + bm_ref[0, h, rows, :]
        sink = sink_ref[h, rows, :]
        mx = jnp.maximum(jnp.maximum(jnp.maximum(s0, s1), s2).max(-1, keepdims=True), sink)
        p0 = jnp.exp(s0 - mx)
        p1 = jnp.exp(s1 - mx)
        p2 = jnp.exp(s2 - mx)
        den = (p0 + p1 + p2).sum(-1, keepdims=True) + jnp.exp(sink - mx)
        probs.append((jnp.concatenate([p0, p1, p2], axis=1).astype(BF16), 1.0 / den))
    outs = [_dot(p, chain[3]) * rden for (p, rden), chain in zip(probs, chains)]
    out_cols = []
    for h in range(N_KV_HEADS):
        o = jnp.where(low, outs[2 * h], outs[2 * h + 1]).astype(o_ref.dtype)
        out_cols += [o[:BLOCK], o[BLOCK:]]
    o_ref[...] = jnp.concatenate(out_cols, axis=1)


def _attention(z, band, meta, sinks, b, seq):
    m = z.shape[0]
    per_seq = seq // BLOCK
    n_real = b * per_seq
    kcol, vcol = Z_AK // KV_WIDTH, Z_AV // KV_WIDTH
    sink_col = _stack_group_rows(jnp.broadcast_to(sinks.astype(F32)[:, None, None], (N_Q_HEADS, BLOCK, LANES)))
    kv = lambda col, f: pl.BlockSpec((BLOCK, KV_WIDTH), lambda s: (f(s), col))
    cur = lambda s: s
    prev = lambda s: jnp.where(s < n_real, jnp.where(s % per_seq == 0, n_real, s - 1), s)
    first = lambda s: n_real
    query_block = lambda s: jnp.where(s < n_real, s % per_seq + 1, 0)
    return pl.pallas_call(
        _attn_kernel,
        out_shape=jax.ShapeDtypeStruct((m, ATTN_WIDTH), BF16),
        grid=(m // BLOCK,),
        in_specs=[pl.BlockSpec((BLOCK, ATTN_WIDTH), lambda s: (s, Z_AQ // ATTN_WIDTH)),
                  kv(kcol, prev), kv(kcol, cur), kv(kcol, first),
                  kv(vcol, prev), kv(vcol, cur), kv(vcol, first),
                  pl.BlockSpec((1, N_KV_HEADS, GQA_GROUP * BLOCK, 2 * BLOCK),
                               lambda s: (jnp.minimum(query_block(s), 2), 0, 0, 0)),
                  pl.BlockSpec((1, N_KV_HEADS, GQA_GROUP * BLOCK, BLOCK), lambda s: (query_block(s), 0, 0, 0)),
                  pl.BlockSpec((N_KV_HEADS, GQA_GROUP * BLOCK, LANES), lambda s: (0, 0, 0))],
        out_specs=pl.BlockSpec((BLOCK, ATTN_WIDTH), lambda s: (s, 0)),
        compiler_params=_params("parallel"),
        name="swa_attention",
    )(z, z, z, z, z, z, z, band, meta, sink_col)


def _mlstm_chunk(qk_in, v_in, gate_in, mo_in, prefix_chunk, cw_ref, cb_ref, gb_ref, ng_ref, xa_ref, ct_ref, n_ref, m_ref):
    L = ML_CHUNK
    ii = lax.broadcasted_iota(jnp.int32, (L, L), 0)
    jj = lax.broadcasted_iota(jnp.int32, (L, L), 1)
    causal = jj <= ii
    lane = lax.broadcasted_iota(jnp.int32, (L, LANES), 1)
    k_scale = ML_QK_DIM ** -0.5
    xa_ref[CONV_HALO:CONV_HALO + L, :] = qk_in
    window = xa_ref[...]
    sel_row = lax.broadcasted_iota(jnp.int32, (L, CONV_HALO + L), 0)
    sel_col = lax.broadcasted_iota(jnp.int32, (L, CONV_HALO + L), 1)
    acc = cb_ref[...] + cw_ref[CONV_WIDTH - 1:CONV_WIDTH, :] * qk_in.astype(F32)
    for j in range(CONV_WIDTH - 1):
        back = CONV_WIDTH - 1 - j
        shift = jnp.where(sel_col == sel_row + (CONV_HALO - back), 1.0, 0.0).astype(BF16)
        acc = acc + cw_ref[j:j + 1, :] * _dot(shift, window)
    qk = acc * _sigmoid(acc)
    xa_ref[0:CONV_HALO, :] = xa_ref[L:L + CONV_HALO, :]
    gpb = gate_in + gb_ref[...]
    log_sig = jnp.minimum(gpb, 0.0) - jnp.log1p(jnp.exp(-jnp.abs(gpb)))
    if prefix_chunk:
        valid = lax.broadcasted_iota(jnp.int32, (L, LANES), 0) >= PAD
        gx = jnp.where(lane < ML_HEADS, jnp.where(valid, gpb, NEG_INF), jnp.where(valid, log_sig, 0.0))
    else:
        gx = jnp.where(lane < ML_HEADS, gpb, log_sig)
    gxt = gx.T
    outs = []
    for h in range(ML_HEADS):
        q = qk[:, h * ML_QK_DIM:(h + 1) * ML_QK_DIM].astype(BF16)
        k = qk[:, ML_QK_WIDTH + h * ML_QK_DIM:ML_QK_WIDTH + (h + 1) * ML_QK_DIM] * k_scale
        v = v_in[:, h * ML_V_DIM:(h + 1) * ML_V_DIM]
        ig_col = gx[:, h:h + 1]
        lf_col = gx[:, ML_HEADS + h:ML_HEADS + h + 1]
        ig_row = gxt[h:h + 1, :]
        lf_row = gxt[ML_HEADS + h:ML_HEADS + h + 1, :]
        b_col = jnp.sum(jnp.where(causal, lf_row, 0.0), axis=-1, keepdims=True)
        b_row = jnp.sum(jnp.where(ii <= jj, lf_col, 0.0), axis=0, keepdims=True)
        m_prev = m_ref[h:h + 1, 0:1]
        log_d = jnp.where(causal, b_col - b_row + ig_row, NEG_INF)
        m_inter = b_col + m_prev
        m_out = jnp.maximum(m_inter, log_d.max(-1, keepdims=True))
        d = jnp.exp(log_d - m_out)
        inter = jnp.exp(m_inter - m_out)
        ct = ct_ref[h]
        n_prev = n_ref[h:h + 1, :]
        qk_scores = _dot_nt(q, k.astype(BF16))
        q_state = _dot(q, ct.astype(BF16))
        q_norm = jnp.sum(q.astype(F32) * n_prev, axis=-1, keepdims=True)
        b_last = b_col[L - 1:L, :]
        log_w = b_last - b_col + ig_col
        m_new = jnp.maximum(b_last + m_prev, log_w.max(0, keepdims=True))
        decay = jnp.exp(b_last + m_prev - m_new)
        kw = k * jnp.exp(log_w - m_new)
        s = qk_scores * d
        vcols = slice(h * ML_V_DIM, (h + 1) * ML_V_DIM)
        num = _dot(s.astype(BF16), v) + inter * q_state
        den = s.sum(-1, keepdims=True) + inter * q_norm
        hh = num / jnp.maximum(jnp.abs(den), jnp.exp(-m_out))
        ct_ref[h] = decay * ct + _dot_tn(kw.astype(BF16), v)
        n_ref[h:h + 1, :] = decay * n_prev + kw.sum(0, keepdims=True)
        m_ref[h:h + 1, :] = jnp.broadcast_to(m_new, (1, LANES))
        hn = hh * lax.rsqrt(jnp.mean(hh * hh, axis=-1, keepdims=True) + EPS) * ng_ref[:, vcols]
        outs.append(_sigmoid(mo_in[:, vcols].astype(F32)) * hn)
    return jnp.concatenate(outs, axis=1)


def _mlstm_kernel(blocks_per_seq, qk_ref, v_ref, gate_ref, mo_ref, cw_ref, cb_ref, gb_ref, ng_ref, o_ref,
                  xa_ref, ct_ref, n_ref, m_ref, xa0_ref, ct0_ref, n0_ref, m0_ref):
    L = ML_CHUNK
    step = pl.program_id(0)
    params = (cw_ref, cb_ref, gb_ref, ng_ref, xa_ref, ct_ref, n_ref, m_ref)

    @pl.when(step == 0)
    def _():
        xa_ref[0:CONV_HALO, :] = jnp.zeros((CONV_HALO, 2 * ML_QK_WIDTH), xa_ref.dtype)
        ct_ref[...] = jnp.zeros_like(ct_ref)
        n_ref[...] = jnp.zeros_like(n_ref)
        m_ref[...] = jnp.zeros_like(m_ref)
        out = _mlstm_chunk(qk_ref[0:L, :], v_ref[0:L, :], gate_ref[0:L, :], mo_ref[0:L, :], True, *params)
        o_ref[0:L, :] = out.astype(o_ref.dtype)
        o_ref[L:, :] = jnp.zeros((o_ref.shape[0] - L, o_ref.shape[1]), o_ref.dtype)
        xa0_ref[...] = xa_ref[0:CONV_HALO, :]
        ct0_ref[...] = ct_ref[...]
        n0_ref[...] = n_ref[...]
        m0_ref[...] = m_ref[...]

    @pl.when(step > 0)
    def _():
        @pl.when((step - 1) % blocks_per_seq == 0)
        def _():
            xa_ref[0:CONV_HALO, :] = xa0_ref[...]
            ct_ref[...] = ct0_ref[...]
            n_ref[...] = n0_ref[...]
            m_ref[...] = m0_ref[...]

        def chunk(c, carry):
            rows = pl.ds(pl.multiple_of(c * L, L), L)
            out = _mlstm_chunk(qk_ref[rows, :], v_ref[rows, :], gate_ref[rows, :], mo_ref[rows, :], False, *params)
            o_ref[rows, :] = out.astype(o_ref.dtype)
            return carry

        lax.fori_loop(0, TAIL_ROWS // L, chunk, 0)


def _mlstm(z, gates, conv_w, conv_b, igate_b, fgate_b, norm_g, b, seq):
    m = z.shape[0]
    assert seq % TAIL_ROWS == 0 and TAIL_ROWS % ML_CHUNK == 0
    n_seq_blocks = b * seq // TAIL_ROWS
    gate_bias = jnp.zeros((1, LANES), F32).at[0, :ML_HEADS].set(igate_b).at[0, ML_HEADS:2 * ML_HEADS].set(fgate_b)
    row_blk = lambda s: jnp.where(s == 0, n_seq_blocks, s - 1)
    rows = lambda w, c: pl.BlockSpec((TAIL_ROWS, w), lambda s: (row_blk(s), c))
    full = lambda r, c: pl.BlockSpec((r, c), lambda s: (0, 0))
    return pl.pallas_call(
        functools.partial(_mlstm_kernel, seq // TAIL_ROWS),
        out_shape=jax.ShapeDtypeStruct((m, ML_V_WIDTH), BF16),
        grid=(n_seq_blocks + 1,),
        in_specs=[rows(ML_V_WIDTH, Z_MQK // ML_V_WIDTH), rows(ML_V_WIDTH, Z_MV // ML_V_WIDTH), rows(LANES, 0),
                  rows(ML_V_WIDTH, Z_MO // ML_V_WIDTH),
                  full(CONV_WIDTH, 2 * ML_QK_WIDTH), full(1, 2 * ML_QK_WIDTH), full(1, LANES), full(1, ML_V_WIDTH)],
        out_specs=rows(ML_V_WIDTH, 0),
        scratch_shapes=[pltpu.VMEM((ML_CHUNK + CONV_HALO, 2 * ML_QK_WIDTH), BF16),
                        pltpu.VMEM((ML_HEADS, ML_QK_DIM, ML_V_DIM), F32),
                        pltpu.VMEM((SUBLANES, LANES), F32),
                        pltpu.VMEM((SUBLANES, LANES), F32),
                        pltpu.VMEM((CONV_HALO, 2 * ML_QK_WIDTH), BF16),
                        pltpu.VMEM((ML_HEADS, ML_QK_DIM, ML_V_DIM), F32),
                        pltpu.VMEM((SUBLANES, LANES), F32),
                        pltpu.VMEM((SUBLANES, LANES), F32)],
        compiler_params=_params("arbitrary"),
        name="mlstm",
    )(z, z, gates, z, conv_w.astype(F32), conv_b.reshape(1, -1).astype(F32), gate_bias,
      norm_g.reshape(1, -1).astype(F32))


def _merge_kernel(with_router, n_body_tiles, attn_ref, ml_ref, ga_ref, gm_ref, *rest):
    if n_body_tiles is None:
        xs_ref, wa_ref, wm_ref, wo_ref, g_ref, *rest = rest
        residual = xs_ref[...]
    else:
        xs_ref, tail_ref, wa_ref, wm_ref, wo_ref, g_ref, *rest = rest
        residual = _split_rows(pl.program_id(0), n_body_tiles, xs_ref, tail_ref)
    if with_router:
        wr_hi_ref, wr_lo_ref, br_ref, xs_out, h_out, route_out = rest
    else:
        xs_out, h_out = rest
    a = _dot(attn_ref[...], wa_ref[...])
    m = _dot(ml_ref[...], wm_ref[...])
    y = _sigmoid(ga_ref[...].astype(F32)) * a + _sigmoid(gm_ref[...].astype(F32)) * m
    xs_new = residual + _dot(y.astype(BF16), wo_ref[...])
    xs_out[...] = xs_new
    hn = _rms(xs_new, g_ref[...])
    if not with_router:
        h_out[...] = hn.astype(h_out.dtype)
    else:
        half = D_MODEL // 2
        lo = lax.bitcast_convert_type(hn[:, :half].astype(BF16).astype(F32), jnp.uint32) >> 16
        hi = lax.bitcast_convert_type(hn[:, half:].astype(BF16).astype(F32), jnp.uint32) & jnp.uint32(0xFFFF0000)
        h_out[...] = lo | hi
        hn_hi = hn.astype(BF16)
        hn_lo = (hn - hn_hi.astype(F32)).astype(BF16)
        logits = (_dot(hn_hi, wr_hi_ref[...]) + (_dot(hn_lo, wr_hi_ref[...]) + _dot(hn_hi, wr_lo_ref[...]))
                  + br_ref[...])
        lane = lax.broadcasted_iota(jnp.int32, logits.shape, 1).astype(F32)
        l1 = logits.max(-1, keepdims=True)
        i1 = jnp.min(jnp.where(logits == l1, lane, float(LANES)), axis=-1, keepdims=True)
        rest_logits = jnp.where(lane == i1, NEG_INF, logits)
        l2 = rest_logits.max(-1, keepdims=True)
        i2 = jnp.min(jnp.where(rest_logits == l2, lane, float(LANES)), axis=-1, keepdims=True)
        e = jnp.exp(l2 - l1)
        w1 = 1.0 / (1.0 + e)
        w2 = e / (1.0 + e)
        route_out[...] = jnp.where(lane == 0, i1, jnp.where(lane == 1, i2, jnp.where(lane == 2, w1,
                                   jnp.where(lane == 3, w2, 0.0))))


def _merge(attn, ml, z, xs, wa, wm, wo, g_next, router=None):
    m = z.shape[0]
    bm = _tile(TAIL_ROWS, 256)
    row = lambda w, c: pl.BlockSpec((bm, w), lambda i: (i, c))
    const = lambda r, c: pl.BlockSpec((r, c), lambda i: (0, 0), pipeline_mode=pl.Buffered(1))
    if isinstance(xs, tuple):
        n_body_tiles = xs[0].shape[0] // bm
        xs_specs, xs_args = _split_specs(bm, n_body_tiles), list(xs)
    else:
        n_body_tiles = None
        xs_specs, xs_args = [row(D_MODEL, 0)], [xs]
    in_specs = [row(ATTN_WIDTH, 0), row(ML_V_WIDTH, 0), row(D_MODEL, Z_GA // D_MODEL), row(D_MODEL, Z_GM // D_MODEL),
                *xs_specs, const(ATTN_WIDTH, D_MODEL), const(ML_V_WIDTH, D_MODEL), const(D_MODEL, D_MODEL),
                const(1, D_MODEL)]
    args = [attn, ml, z, z, *xs_args, wa, wm, wo, g_next.reshape(1, D_MODEL)]
    if router is None:
        out_shape = [jax.ShapeDtypeStruct((m, D_MODEL), F32), jax.ShapeDtypeStruct((m, D_MODEL), BF16)]
        out_specs = [row(D_MODEL, 0), row(D_MODEL, 0)]
    else:
        out_shape = [jax.ShapeDtypeStruct((m, D_MODEL), F32), jax.ShapeDtypeStruct((m, D_MODEL // 2), jnp.uint32)]
        out_specs = [row(D_MODEL, 0), row(D_MODEL // 2, 0)]
    if router is not None:
        w_router, b_router = router
        wr = jnp.zeros((D_MODEL, LANES), F32).at[:, :N_EXPERTS].set(w_router.astype(F32))
        br = jnp.full((1, LANES), NEG_INF, F32).at[0, :N_EXPERTS].set(b_router.astype(F32))
        wr_hi = wr.astype(BF16)
        wr_lo = (wr - wr_hi.astype(F32)).astype(BF16)
        in_specs += [const(D_MODEL, LANES), const(D_MODEL, LANES), const(1, LANES)]
        args += [wr_hi, wr_lo, br]
        out_shape.append(jax.ShapeDtypeStruct((m, LANES), F32))
        out_specs.append(row(LANES, 0))
    return pl.pallas_call(
        functools.partial(_merge_kernel, router is not None, n_body_tiles),
        out_shape=out_shape,
        grid=(m // bm,),
        in_specs=in_specs,
        out_specs=out_specs,
        compiler_params=_params("parallel"),
        name="merge_out_proj",
    )(*args)


def _ffn_kernel(h_ref, xs_hbm, wg_ref, wu_ref, wd_ref, g_ref, xs_out, h_out, sem):
    i = pl.program_id(0)
    f = pl.program_id(1)
    bm = xs_out.shape[0]
    residual = pltpu.make_async_copy(xs_hbm.at[pl.ds(pl.multiple_of(i * bm, bm), bm), :], xs_out, sem)

    @pl.when(f == 0)
    def _():
        residual.start()

    hb = h_ref[...]
    g = _dot(hb, wg_ref[...])
    u = _dot(hb, wu_ref[...])
    act = (g * _sigmoid(g) * u).astype(BF16)

    @pl.when(f == 0)
    def _():
        residual.wait()

    xs_out[...] += _dot(act, wd_ref[...])

    @pl.when(f == pl.num_programs(1) - 1)
    def _():
        h_out[...] = _rms(xs_out[...], g_ref[...]).astype(h_out.dtype)


def _dense_ffn(h, xs, wg, wu, wd, g_next, bf=FFN_TILE):
    m = xs.shape[0]
    bm = _tile(m, 1024)
    return pl.pallas_call(
        _ffn_kernel,
        out_shape=[jax.ShapeDtypeStruct((m, D_MODEL), F32), jax.ShapeDtypeStruct((m, D_MODEL), BF16)],
        grid=(m // bm, D_FF // bf),
        in_specs=[pl.BlockSpec((bm, D_MODEL), lambda i, f: (i, 0)),
                  pl.BlockSpec(memory_space=pl.ANY),
                  pl.BlockSpec((D_MODEL, bf), lambda i, f: (0, f)),
                  pl.BlockSpec((D_MODEL, bf), lambda i, f: (0, f)),
                  pl.BlockSpec((bf, D_MODEL), lambda i, f: (f, 0)),
                  pl.BlockSpec((1, D_MODEL), lambda i, f: (0, 0))],
        out_specs=[pl.BlockSpec((bm, D_MODEL), lambda i, f: (i, 0)),
                   pl.BlockSpec((bm, D_MODEL), lambda i, f: (i, 0))],
        scratch_shapes=[pltpu.SemaphoreType.DMA(())],
        compiler_params=_params("arbitrary", "arbitrary"),
        name="dense_swiglu",
    )(h, xs, wg, wu, wd, g_next.reshape(1, D_MODEL))


def _row_copy(src_hbm, dst_vmem, sem, src_row, dst_row):
    return pltpu.make_async_copy(src_hbm.at[pl.ds(src_row, 1), :], dst_vmem.at[pl.ds(dst_row, 1), :], sem)


def _moe_gather_copy(h_hbm, gbuf, sem, slot, src_row, dst_row):
    return pltpu.make_async_copy(h_hbm.at[pl.ds(src_row, 1), :], gbuf.at[slot, pl.ds(dst_row, 1), :], sem.at[slot])


def _moe_gather_wait(h_hbm, gbuf, sem, slot):
    pltpu.make_async_copy(h_hbm.at[pl.ds(0, MOE_GATHER_ROWS), :], gbuf.at[slot], sem.at[slot]).wait()


def _moe_kernel(sb_e_ref, sb_rows_ref, row_tok_ref, h_hbm, wg_ref, wu_ref, wd_ref, y_ref, gbuf, xb16, sem):
    sb = pl.program_id(0)
    f = pl.program_id(1)
    n_sb = pl.num_programs(0)
    nf = pl.num_programs(1)
    kind = sb_rows_ref[sb]
    slot = sb % 2
    half = D_MODEL // 2

    @pl.when(jnp.logical_and(sb == 0, f == 0))
    def _():
        def issue(r, c):
            _moe_gather_copy(h_hbm, gbuf, sem, 0, row_tok_ref[r], r).start()
            return c
        lax.fori_loop(0, MOE_GATHER_ROWS, issue, 0)

    @pl.when(f == 0)
    def _():
        y_ref[...] = jnp.zeros_like(y_ref)
        prev_kind = sb_rows_ref[jnp.maximum(sb - 1, 0)]

        @pl.when(jnp.logical_or(sb == 0, prev_kind > 0))
        def _():
            _moe_gather_wait(h_hbm, gbuf, sem, slot)

        @pl.when(kind > 0)
        def _():
            w = gbuf[slot, 0:MOE_SB, :]
            xb16[:, :half] = lax.bitcast_convert_type(w << 16, F32).astype(BF16)
            xb16[:, half:] = lax.bitcast_convert_type(w & jnp.uint32(0xFFFF0000), F32).astype(BF16)

    def compute(row0, rows, prefetch):
        if prefetch:
            base = (sb + 1) * MOE_SB + f * MOE_STEP_ROWS
            for r in range(MOE_STEP_ROWS):
                _moe_gather_copy(h_hbm, gbuf, sem, 1 - slot, row_tok_ref[base + r], f * MOE_STEP_ROWS + r).start()
        xb = xb16[row0:row0 + rows, :]
        g = _dot(xb, wg_ref[0].astype(BF16))
        u = _dot(xb, wu_ref[0].astype(BF16))
        act = (g * _sigmoid(g) * u).astype(BF16)
        y_ref[row0:row0 + rows, :] += _dot(act, wd_ref[0].astype(BF16))

    @pl.when(kind == MOE_PARTS)
    def _():
        compute(0, MOE_SB, True)

    for part in range(MOE_PARTS - 1):
        @pl.when(jnp.logical_and(kind > part, kind < MOE_PARTS))
        def _():
            compute(part * MOE_PART_ROWS, MOE_PART_ROWS, part == 0)

    @pl.when(jnp.logical_and(jnp.logical_and(sb == n_sb - 1, f == nf - 1), kind > 0))
    def _():
        _moe_gather_wait(h_hbm, gbuf, sem, 1 - slot)


def _moe_experts(h_packed, sb_e, sb_rows, row_tok, wg, wu, wd, n_sb, bf=MOE_FF_TILE):
    nf = D_FF // bf
    assert nf * MOE_STEP_ROWS == MOE_GATHER_ROWS
    ftile = lambda i, f, rows: jnp.where(rows[i] > 0, f, nf - 1)
    grid_spec = pltpu.PrefetchScalarGridSpec(
        num_scalar_prefetch=3,
        grid=(n_sb, nf),
        in_specs=[pl.BlockSpec(memory_space=pl.ANY),
                  pl.BlockSpec((1, D_MODEL, bf), lambda i, f, se, sr, rt: (se[i], 0, ftile(i, f, sr))),
                  pl.BlockSpec((1, D_MODEL, bf), lambda i, f, se, sr, rt: (se[i], 0, ftile(i, f, sr))),
                  pl.BlockSpec((1, bf, D_MODEL), lambda i, f, se, sr, rt: (se[i], ftile(i, f, sr), 0))],
        out_specs=pl.BlockSpec((MOE_SB, D_MODEL), lambda i, f, se, sr, rt: (i, 0)),
        scratch_shapes=[pltpu.VMEM((2, MOE_GATHER_ROWS, D_MODEL // 2), jnp.uint32),
                        pltpu.VMEM((MOE_SB, D_MODEL), BF16),
                        pltpu.SemaphoreType.DMA((2,))],
    )
    return pl.pallas_call(
        _moe_kernel,
        out_shape=jax.ShapeDtypeStruct((n_sb * MOE_SB, D_MODEL), F32),
        grid_spec=grid_spec,
        compiler_params=_params("arbitrary", "arbitrary"),
        name="moe_experts",
    )(sb_e, sb_rows, row_tok, h_packed, wg, wu, wd)


def _combine_kernel(pos_ref, xs_ref, route_ref, y_hbm, g_ref, o_ref, ybuf, sem):
    step = pl.program_id(0)
    n_steps = pl.num_programs(0)
    slot = step % 2

    def gather(s, dst_slot):
        base = s * BLOCK

        def issue(r, c):
            for k in range(TOP_K):
                _row_copy(y_hbm, ybuf.at[dst_slot, k], sem.at[dst_slot], pos_ref[TOP_K * (base + r) + k], r).start()
            return c
        lax.fori_loop(0, BLOCK, issue, 0, unroll=4)

    @pl.when(step == 0)
    def _():
        gather(step, slot)

    @pl.when(step + 1 < n_steps)
    def _():
        gather(step + 1, 1 - slot)

    for k in range(TOP_K):
        pltpu.make_async_copy(y_hbm.at[pl.ds(0, BLOCK), :], ybuf.at[slot, k], sem.at[slot]).wait()
    route = route_ref[...]
    moe = ybuf[slot, 0] * route[:, 2:3] + ybuf[slot, 1] * route[:, 3:4]
    o_ref[...] = _rms(xs_ref[...] + moe, g_ref[...])


def _moe_combine(pos, xs, route, yb, g_final, m_real):
    grid_spec = pltpu.PrefetchScalarGridSpec(
        num_scalar_prefetch=1,
        grid=(m_real // BLOCK,),
        in_specs=[pl.BlockSpec((BLOCK, D_MODEL), lambda t, p: (t, 0)),
                  pl.BlockSpec((BLOCK, LANES), lambda t, p: (t, 0)),
                  pl.BlockSpec(memory_space=pl.ANY),
                  pl.BlockSpec((1, D_MODEL), lambda t, p: (0, 0))],
        out_specs=pl.BlockSpec((BLOCK, D_MODEL), lambda t, p: (t, 0)),
        scratch_shapes=[pltpu.VMEM((2, TOP_K, BLOCK, D_MODEL), F32), pltpu.SemaphoreType.DMA((2,))],
    )
    return pl.pallas_call(
        _combine_kernel,
        out_shape=jax.ShapeDtypeStruct((m_real, D_MODEL), F32),
        grid_spec=grid_spec,
        compiler_params=_params("arbitrary"),
        name="moe_combine_final_norm",
    )(pos, xs, route, yb, g_final.reshape(1, D_MODEL))


def _moe_routing(route, n_tok):
    n_assign = n_tok * TOP_K
    n_sb = n_assign // MOE_SB + N_EXPERTS
    e_flat = route[:, :TOP_K].astype(jnp.int32).reshape(n_assign)
    onehot = (e_flat[:, None] == jnp.arange(N_EXPERTS, dtype=jnp.int32)[None, :]).astype(jnp.int32)
    csum = jnp.cumsum(onehot, axis=0)
    rank = jnp.sum((csum - onehot) * onehot, axis=1)
    counts = csum[-1]
    sb_count = (counts + MOE_SB - 1) // MOE_SB
    sb_end = jnp.cumsum(sb_count)
    sb_start = sb_end - sb_count
    dest = (jnp.sum((sb_start * MOE_SB)[None, :] * onehot, axis=1) + rank).astype(jnp.int32)
    row_tok = jnp.zeros(((n_sb + 2) * MOE_SB,), jnp.int32).at[dest].set(
        jnp.arange(n_assign, dtype=jnp.int32) // TOP_K, unique_indices=True)
    sb = jnp.arange(n_sb, dtype=jnp.int32)
    sb_e = jnp.sum((sb[:, None] >= sb_end[None, :]).astype(jnp.int32), axis=1)
    valid = sb < sb_end[-1]
    last_e = jnp.sum((sb_end[-1] - 1 >= sb_end).astype(jnp.int32))
    sb_e = jnp.where(valid, sb_e, last_e).astype(jnp.int32)
    rows_here = counts[sb_e] - (sb - sb_start[sb_e]) * MOE_SB
    parts = (jnp.minimum(rows_here, MOE_SB) + MOE_PART_ROWS - 1) // MOE_PART_ROWS
    sb_rows = jnp.where(valid, parts, 0).astype(jnp.int32)
    return dest, row_tok, sb_e, sb_rows, n_sb


def _pack_w_in(w):
    o = {}
    off = 0
    for name, size in (("aq", ATTN_WIDTH), ("ak", KV_WIDTH), ("av", KV_WIDTH), ("mq", ML_QK_WIDTH),
                       ("mk", ML_QK_WIDTH), ("mv", ML_V_WIDTH), ("mi", ML_HEADS), ("mf", ML_HEADS),
                       ("mo", ML_V_WIDTH), ("ga", D_MODEL), ("gm", D_MODEL)):
        o[name] = w[:, off:off + size]
        off += size
    main = jnp.concatenate([o["ga"], o["gm"], o["aq"], o["mq"], o["mk"], o["mv"], o["mo"], o["ak"], o["av"]],
                           axis=1).astype(BF16)
    gates = jnp.concatenate([o["mi"], o["mf"], jnp.zeros((D_MODEL, LANES - 2 * ML_HEADS), w.dtype)], axis=1)
    return main, gates.astype(BF16)


def kernel(x, meta_tokens, rel_bias_table, w_in, attn_sinks, conv_w, conv_b, igate_b, fgate_b, mlstm_norm_g,
           w_attn_up, w_mlstm_up, w_out, norm_mix_g, norm_ffn_g, w_ffn_gate, w_ffn_up, w_ffn_down, w_router,
           b_router, w_moe_gate, w_moe_up, w_moe_down, final_norm_g):
    b, seq, _ = x.shape
    depth = w_in.shape[0]
    m_real = b * seq
    m = m_real + TAIL_ROWS
    assert depth == 2 and seq % ML_CHUNK == 0 and seq % BLOCK == 0 and m_real % TAIL_ROWS == 0
    tail = jnp.concatenate([jnp.zeros((PAD, D_MODEL), x.dtype), meta_tokens.astype(x.dtype),
                            jnp.zeros((TAIL_ROWS - PREFIX, D_MODEL), x.dtype)], axis=0)
    xs = (x.reshape(m_real, D_MODEL), tail)
    band, meta = _attn_bias(rel_bias_table, seq // BLOCK + 1)
    h = _rmsnorm(*xs, norm_mix_g[0])
    out = None
    for layer in range(depth):
        w_main, w_gates = _pack_w_in(w_in[layer])
        z = _matmul(h, w_main, BF16, 1024, Z_WIDTH // 4, "in_proj")
        gates = _matmul(h, w_gates, F32, 1024, LANES, "in_proj_gates")
        attn = _attention(z, band, meta, attn_sinks[layer], b, seq)
        ml = _mlstm(z, gates, conv_w[layer], conv_b[layer], igate_b[layer], fgate_b[layer], mlstm_norm_g[layer],
                    b, seq)
        wa, wm, wo = (w_attn_up[layer].astype(BF16), w_mlstm_up[layer].astype(BF16), w_out[layer].astype(BF16))
        i = layer // 2
        if layer % 2 == 0:
            xs, h = _merge(attn, ml, z, xs, wa, wm, wo, norm_ffn_g[layer])
            xs, h = _dense_ffn(h, xs, w_ffn_gate[i].astype(BF16), w_ffn_up[i].astype(BF16),
                               w_ffn_down[i].astype(BF16), norm_mix_g[layer + 1])
        else:
            xs, h_packed, route = _merge(attn, ml, z, xs, wa, wm, wo, norm_ffn_g[layer],
                                         router=(w_router[i], b_router[i]))
            n_tok = m_real + PREFIX
            dest, row_tok, sb_e, sb_rows, n_sb = _moe_routing(route[:n_tok], n_tok)
            yb = _moe_experts(h_packed, sb_e, sb_rows, row_tok, w_moe_gate[i], w_moe_up[i], w_moe_down[i], n_sb)
            out = _moe_combine(dest, xs, route, yb, final_norm_g, m_real).reshape(b, seq, D_MODEL)
    return out
```

```python
import functools
import math

import jax
import jax.numpy as jnp
from jax import lax
from jax.experimental import pallas as pl
from jax.experimental.pallas import tpu as pltpu

D_MODEL = 2048
N_META = 16
BLOCK = 128
PREFIX = BLOCK
PAD = PREFIX - N_META
HEAD_DIM = 64
N_Q_HEADS = 16
N_KV_HEADS = 4
GQA_GROUP = 4
ATTN_WIDTH = N_Q_HEADS * HEAD_DIM
KV_WIDTH = N_KV_HEADS * HEAD_DIM
WINDOW = 128
NUM_BUCKETS = 32
MAX_DISTANCE = 128
ML_HEADS = 4
ML_V_WIDTH = D_MODEL // 2
ML_V_DIM = ML_V_WIDTH // ML_HEADS
ML_QK_DIM = ML_V_DIM // 2
ML_QK_WIDTH = ML_HEADS * ML_QK_DIM
CONV_WIDTH = 4
D_FF = 11 * D_MODEL // 4
N_EXPERTS = 8
TOP_K = 2
EPS = 1e-6

LANES = 128
SUBLANES = 8
BF16_ROWS = 16
VMEM_LIMIT = 56 * 1024 * 1024

Z_GA, Z_GM, Z_AQ, Z_MQK, Z_MV, Z_MO, Z_AK, Z_AV = 0, 2048, 4096, 5120, 6144, 7168, 8192, 8448
Z_WIDTH = 8704
TAIL_ROWS = 512
ML_CHUNK = 128
CONV_HALO = 16
FFN_TILE = 512
MOE_SB = 1024
MOE_PARTS = 4
MOE_PART_ROWS = MOE_SB // MOE_PARTS
MOE_FF_TILE = 512
MOE_STEP_ROWS = 96
MOE_GATHER_ROWS = MOE_STEP_ROWS * (D_FF // MOE_FF_TILE)
MOE_VMEM_LIMIT = 60 * 1024 * 1024

F32 = jnp.float32
BF16 = jnp.bfloat16
NEG_INF = float("-inf")


def _tile(m, target):
    best = LANES
    for t in range(LANES, min(m, target) + 1, LANES):
        if m % t == 0:
            best = t
    assert m % best == 0
    return best


def _params(*sem):
    return pltpu.CompilerParams(dimension_semantics=sem, vmem_limit_bytes=VMEM_LIMIT)


def _rms(x, g):
    return x * lax.rsqrt(jnp.mean(x * x, axis=-1, keepdims=True) + EPS) * g


def _sigmoid(x):
    return 1.0 / (1.0 + jnp.exp(-x))


def _dot(a, b):
    return jnp.dot(a, b, preferred_element_type=F32)


def _dot_nt(a, b):
    return lax.dot_general(a, b, (((1,), (1,)), ((), ())), preferred_element_type=F32)


def _dot_tn(a, b):
    return lax.dot_general(a, b, (((0,), (0,)), ((), ())), preferred_element_type=F32)


def _split_rows(i, n_body_tiles, body_ref, tail_ref):
    return jnp.where(i < n_body_tiles, body_ref[...], tail_ref[...])


def _split_specs(bm, n_body_tiles):
    return [pl.BlockSpec((bm, D_MODEL), lambda i: (jnp.minimum(i, n_body_tiles - 1), 0)),
            pl.BlockSpec((bm, D_MODEL), lambda i: (jnp.maximum(i - n_body_tiles, 0), 0))]


def _norm_kernel(n_body_tiles, x_ref, tail_ref, g_ref, o_ref):
    x = _split_rows(pl.program_id(0), n_body_tiles, x_ref, tail_ref)
    o_ref[...] = _rms(x, g_ref[...]).astype(o_ref.dtype)


def _rmsnorm(x_body, x_tail, g):
    m = x_body.shape[0] + x_tail.shape[0]
    bm = _tile(x_tail.shape[0], 512)
    assert x_body.shape[0] % bm == 0
    n_body_tiles = x_body.shape[0] // bm
    return pl.pallas_call(
        functools.partial(_norm_kernel, n_body_tiles),
        out_shape=jax.ShapeDtypeStruct((m, D_MODEL), BF16),
        grid=(m // bm,),
        in_specs=_split_specs(bm, n_body_tiles) + [pl.BlockSpec((1, D_MODEL), lambda i: (0, 0))],
        out_specs=pl.BlockSpec((bm, D_MODEL), lambda i: (i, 0)),
        compiler_params=_params("parallel"),
        name="rmsnorm",
    )(x_body, x_tail, g.reshape(1, D_MODEL))


def _mm_kernel(a_ref, w_ref, o_ref):
    o_ref[...] = _dot(a_ref[...], w_ref[...]).astype(o_ref.dtype)


def _matmul(a, w, out_dtype, bm, bn, name):
    m, k = a.shape
    n = w.shape[1]
    bm = _tile(m, bm)
    return pl.pallas_call(
        _mm_kernel,
        out_shape=jax.ShapeDtypeStruct((m, n), out_dtype),
        grid=(n // bn, m // bm),
        in_specs=[pl.BlockSpec((bm, k), lambda j, i: (i, 0)),
                  pl.BlockSpec((k, bn), lambda j, i: (0, j))],
        out_specs=pl.BlockSpec((bm, bn), lambda j, i: (i, j)),
        compiler_params=_params("parallel", "parallel"),
        name=name,
    )(a, w)


def _t5_bucket(rel):
    n = jnp.maximum(rel, 0)
    max_exact = NUM_BUCKETS // 2
    large = max_exact + (jnp.log(jnp.maximum(n, 1).astype(F32) / max_exact)
                         / math.log(MAX_DISTANCE / max_exact) * (NUM_BUCKETS - max_exact)).astype(jnp.int32)
    large = jnp.minimum(large, NUM_BUCKETS - 1)
    return jnp.where(n < max_exact, n, large)


def _bias_lookup(table, rel):
    onehot = (_t5_bucket(rel)[..., None] == jnp.arange(NUM_BUCKETS)).astype(F32)
    return jnp.einsum("...b,bh->h...", onehot, table.astype(F32), precision=lax.Precision.HIGHEST)


ATTN_GROUP_ORDER = (0, 2, 1, 3)


def _stack_group_rows(a):
    lead = a.shape[:-3]
    a = a.reshape(*lead, N_KV_HEADS, GQA_GROUP, BLOCK, a.shape[-1])
    a = jnp.take(a, jnp.array(ATTN_GROUP_ORDER), axis=len(lead) + 1)
    return a.reshape(*lead, N_KV_HEADS, GQA_GROUP * BLOCK, a.shape[-1])


def _attn_bias(table, nb):
    qi = jnp.arange(BLOCK)[:, None]
    ki = jnp.arange(2 * BLOCK)[None, :]
    rel_band = qi + BLOCK - ki
    blk3 = jnp.arange(3)[:, None, None]
    mask_band = (rel_band >= 0) & (rel_band < WINDOW) & ((blk3 - 1) * BLOCK + ki >= PAD)
    band = jnp.where(mask_band[:, None], _bias_lookup(table, rel_band)[None], NEG_INF)
    blk = jnp.arange(nb)[:, None, None]
    rel_meta = blk * BLOCK + qi[None] - (PAD + jnp.arange(N_META))
    meta = jnp.where((rel_meta >= WINDOW)[None], _bias_lookup(table, rel_meta), NEG_INF)
    meta = jnp.pad(jnp.moveaxis(meta, 1, 0), ((0, 0), (0, 0), (0, 0), (PAD, 0)), constant_values=NEG_INF)
    return _stack_group_rows(band), _stack_group_rows(meta)


def _swap_halves(x):
    return pltpu.roll(x.astype(F32), HEAD_DIM, axis=1).astype(x.dtype)


def _attn_kernel(q_ref, kp_ref, kc_ref, km_ref, vp_ref, vc_ref, vm_ref, bb_ref, bm_ref, sink_ref, o_ref):
    scale = HEAD_DIM ** -0.5
    assert math.frexp(scale)[0] == 0.5 and 2 * HEAD_DIM == LANES
    lane = lax.broadcasted_iota(jnp.int32, (1, LANES), 1)
    keep = (jnp.where(lane < HEAD_DIM, scale, 0.0).astype(BF16),
            jnp.where(lane < HEAD_DIM, 0.0, scale).astype(BF16))
    low = lax.broadcasted_iota(jnp.int32, (2 * BLOCK, LANES), 1) < HEAD_DIM
    chains = []
    for col in range(N_KV_HEADS // 2):
        kcols = slice(col * LANES, (col + 1) * LANES)
        k_nat = jnp.concatenate([kp_ref[:, kcols], kc_ref[:, kcols], km_ref[:, kcols]], axis=0)
        v_nat = jnp.concatenate([vp_ref[:, kcols], vc_ref[:, kcols], vm_ref[:, kcols]], axis=0)
        keys = (k_nat, _swap_halves(k_nat))
        vals = (v_nat, _swap_halves(v_nat))
        for half in range(2):
            h = 2 * col + half
            q0 = h * GQA_GROUP * HEAD_DIM
            q2 = jnp.concatenate([q_ref[:, q0:q0 + LANES], q_ref[:, q0 + LANES:q0 + 2 * LANES]], axis=0)
            for lane_half in range(2):
                which = 0 if lane_half == half else 1
                chains.append((h, lane_half, _dot_nt(q2 * keep[lane_half], keys[which]), vals[which]))
    probs = []
    for h, lane_half, s, _ in chains:
        rows = slice(lane_half * 2 * BLOCK, (lane_half + 1) * 2 * BLOCK)
        s0 = s[:, :BLOCK] + bb_ref[0, h, rows, :BLOCK]
        s1 = s[:, BLOCK:2 * BLOCK] + bb_ref[0, h, rows, BLOCK:]
        s2 = s[:, 2 * BLOCK:] + bm_ref[0, h, rows, :]
        sink = sink_ref[h, rows, :]
        mx = jnp.maximum(jnp.maximum(jnp.maximum(s0, s1), s2).max(-1, keepdims=True), sink)
        p0 = jnp.exp(s0 - mx)
        p1 = jnp.exp(s1 - mx)
        p2 = jnp.exp(s2 - mx)
        den = (p0 + p1 + p2).sum(-1, keepdims=True) + jnp.exp(sink - mx)
        probs.append((jnp.concatenate([p0, p1, p2], axis=1).astype(BF16), 1.0 / den))
    outs = [_dot(p, chain[3]) * rden for (p, rden), chain in zip(probs, chains)]
    out_cols = []
    for h in range(N_KV_HEADS):
        o = jnp.where(low, outs[2 * h], outs[2 * h + 1]).astype(o_ref.dtype)
        out_cols += [o[:BLOCK], o[BLOCK:]]
    o_ref[...] = jnp.concatenate(out_cols, axis=1)


def _attention(z, band, meta, sinks, b, seq):
    m = z.shape[0]
    per_seq = seq // BLOCK
    n_real = b * per_seq
    kcol, vcol = Z_AK // KV_WIDTH, Z_AV // KV_WIDTH
    sink_col = _stack_group_rows(jnp.broadcast_to(sinks.astype(F32)[:, None, None], (N_Q_HEADS, BLOCK, LANES)))
    kv = lambda col, f: pl.BlockSpec((BLOCK, KV_WIDTH), lambda s: (f(s), col))
    cur = lambda s: s
    prev = lambda s: jnp.where(s < n_real, jnp.where(s % per_seq == 0, n_real, s - 1), s)
    first = lambda s: n_real
    query_block = lambda s: jnp.where(s < n_real, s % per_seq + 1, 0)
    return pl.pallas_call(
        _attn_kernel,
        out_shape=jax.ShapeDtypeStruct((m, ATTN_WIDTH), BF16),
        grid=(m // BLOCK,),
        in_specs=[pl.BlockSpec((BLOCK, ATTN_WIDTH), lambda s: (s, Z_AQ // ATTN_WIDTH)),
                  kv(kcol, prev), kv(kcol, cur), kv(kcol, first),
                  kv(vcol, prev), kv(vcol, cur), kv(vcol, first),
                  pl.BlockSpec((1, N_KV_HEADS, GQA_GROUP * BLOCK, 2 * BLOCK),
                               lambda s: (jnp.minimum(query_block(s), 2), 0, 0, 0)),
                  pl.BlockSpec((1, N_KV_HEADS, GQA_GROUP * BLOCK, BLOCK), lambda s: (query_block(s), 0, 0, 0)),
                  pl.BlockSpec((N_KV_HEADS, GQA_GROUP * BLOCK, LANES), lambda s: (0, 0, 0))],
        out_specs=pl.BlockSpec((BLOCK, ATTN_WIDTH), lambda s: (s, 0)),
        compiler_params=_params("parallel"),
        name="swa_attention",
    )(z, z, z, z, z, z, z, band, meta, sink_col)


def _mlstm_chunk(qk_in, v_in, gate_in, mo_in, prefix_chunk, cw_ref, cb_ref, gb_ref, ng_ref, xa_ref, ct_ref, n_ref, m_ref):
    L = ML_CHUNK
    ii = lax.broadcasted_iota(jnp.int32, (L, L), 0)
    jj = lax.broadcasted_iota(jnp.int32, (L, L), 1)
    causal = jj <= ii
    lane = lax.broadcasted_iota(jnp.int32, (L, LANES), 1)
    k_scale = ML_QK_DIM ** -0.5
    xa_ref[CONV_HALO:CONV_HALO + L, :] = qk_in
    window = xa_ref[...]
    sel_row = lax.broadcasted_iota(jnp.int32, (L, CONV_HALO + L), 0)
    sel_col = lax.broadcasted_iota(jnp.int32, (L, CONV_HALO + L), 1)
    acc = cb_ref[...] + cw_ref[CONV_WIDTH - 1:CONV_WIDTH, :] * qk_in.astype(F32)
    for j in range(CONV_WIDTH - 1):
        back = CONV_WIDTH - 1 - j
        shift = jnp.where(sel_col == sel_row + (CONV_HALO - back), 1.0, 0.0).astype(BF16)
        acc = acc + cw_ref[j:j + 1, :] * _dot(shift, window)
    qk = acc * _sigmoid(acc)
    xa_ref[0:CONV_HALO, :] = xa_ref[L:L + CONV_HALO, :]
    gpb = gate_in + gb_ref[...]
    log_sig = jnp.minimum(gpb, 0.0) - jnp.log1p(jnp.exp(-jnp.abs(gpb)))
    if prefix_chunk:
        valid = lax.broadcasted_iota(jnp.int32, (L, LANES), 0) >= PAD
        gx = jnp.where(lane < ML_HEADS, jnp.where(valid, gpb, NEG_INF), jnp.where(valid, log_sig, 0.0))
    else:
        gx = jnp.where(lane < ML_HEADS, gpb, log_sig)
    gxt = gx.T
    outs = []
    for h in range(ML_HEADS):
        q = qk[:, h * ML_QK_DIM:(h + 1) * ML_QK_DIM].astype(BF16)
        k = qk[:, ML_QK_WIDTH + h * ML_QK_DIM:ML_QK_WIDTH + (h + 1) * ML_QK_DIM] * k_scale
        v = v_in[:, h * ML_V_DIM:(h + 1) * ML_V_DIM]
        ig_col = gx[:, h:h + 1]
        lf_col = gx[:, ML_HEADS + h:ML_HEADS + h + 1]
        ig_row = gxt[h:h + 1, :]
        lf_row = gxt[ML_HEADS + h:ML_HEADS + h + 1, :]
        b_col = jnp.sum(jnp.where(causal, lf_row, 0.0), axis=-1, keepdims=True)
        b_row = jnp.sum(jnp.where(ii <= jj, lf_col, 0.0), axis=0, keepdims=True)
        m_prev = m_ref[h:h + 1, 0:1]
        log_d = jnp.where(causal, b_col - b_row + ig_row, NEG_INF)
        m_inter = b_col + m_prev
        m_out = jnp.maximum(m_inter, log_d.max(-1, keepdims=True))
        d = jnp.exp(log_d - m_out)
        inter = jnp.exp(m_inter - m_out)
        ct = ct_ref[h]
        n_prev = n_ref[h:h + 1, :]
        qk_scores = _dot_nt(q, k.astype(BF16))
        q_state = _dot(q, ct.astype(BF16))
        q_norm = jnp.sum(q.astype(F32) * n_prev, axis=-1, keepdims=True)
        b_last = b_col[L - 1:L, :]
        log_w = b_last - b_col + ig_col
        m_new = jnp.maximum(b_last + m_prev, log_w.max(0, keepdims=True))
        decay = jnp.exp(b_last + m_prev - m_new)
        kw = k * jnp.exp(log_w - m_new)
        s = qk_scores * d
        vcols = slice(h * ML_V_DIM, (h + 1) * ML_V_DIM)
        num = _dot(s.astype(BF16), v) + inter * q_state
        den = s.sum(-1, keepdims=True) + inter * q_norm
        hh = num / jnp.maximum(jnp.abs(den), jnp.exp(-m_out))
        ct_ref[h] = decay * ct + _dot_tn(kw.astype(BF16), v)
        n_ref[h:h + 1, :] = decay * n_prev + kw.sum(0, keepdims=True)
        m_ref[h:h + 1, :] = jnp.broadcast_to(m_new, (1, LANES))
        hn = hh * lax.rsqrt(jnp.mean(hh * hh, axis=-1, keepdims=True) + EPS) * ng_ref[:, vcols]
        outs.append(_sigmoid(mo_in[:, vcols].astype(F32)) * hn)
    return jnp.concatenate(outs, axis=1)


def _mlstm_kernel(blocks_per_seq, qk_ref, v_ref, h_ref, mo_ref, wgate_ref, cw_ref, cb_ref, gb_ref, ng_ref, o_ref,
                  gate_ref, xa_ref, ct_ref, n_ref, m_ref, xa0_ref, ct0_ref, n0_ref, m0_ref):
    L = ML_CHUNK
    step = pl.program_id(0)
    params = (cw_ref, cb_ref, gb_ref, ng_ref, xa_ref, ct_ref, n_ref, m_ref)
    gate_ref[...] = _dot(h_ref[...], wgate_ref[...])

    @pl.when(step == 0)
    def _():
        xa_ref[0:CONV_HALO, :] = jnp.zeros((CONV_HALO, 2 * ML_QK_WIDTH), xa_ref.dtype)
        ct_ref[...] = jnp.zeros_like(ct_ref)
        n_ref[...] = jnp.zeros_like(n_ref)
        m_ref[...] = jnp.zeros_like(m_ref)
        out = _mlstm_chunk(qk_ref[0:L, :], v_ref[0:L, :], gate_ref[0:L, :], mo_ref[0:L, :], True, *params)
        o_ref[0:L, :] = out.astype(o_ref.dtype)
        o_ref[L:, :] = jnp.zeros((o_ref.shape[0] - L, o_ref.shape[1]), o_ref.dtype)
        xa0_ref[...] = xa_ref[0:CONV_HALO, :]
        ct0_ref[...] = ct_ref[...]
        n0_ref[...] = n_ref[...]
        m0_ref[...] = m_ref[...]

    @pl.when(step > 0)
    def _():
        @pl.when((step - 1) % blocks_per_seq == 0)
        def _():
            xa_ref[0:CONV_HALO, :] = xa0_ref[...]
            ct_ref[...] = ct0_ref[...]
            n_ref[...] = n0_ref[...]
            m_ref[...] = m0_ref[...]

        def chunk(c, carry):
            rows = pl.ds(pl.multiple_of(c * L, L), L)
            out = _mlstm_chunk(qk_ref[rows, :], v_ref[rows, :], gate_ref[rows, :], mo_ref[rows, :], False, *params)
            o_ref[rows, :] = out.astype(o_ref.dtype)
            return carry

        lax.fori_loop(0, TAIL_ROWS // L, chunk, 0)


def _mlstm(z, h, w_gates, conv_w, conv_b, igate_b, fgate_b, norm_g, b, seq):
    m = z.shape[0]
    assert seq % TAIL_ROWS == 0 and TAIL_ROWS % ML_CHUNK == 0
    n_seq_blocks = b * seq // TAIL_ROWS
    gate_bias = jnp.zeros((1, LANES), F32).at[0, :ML_HEADS].set(igate_b).at[0, ML_HEADS:2 * ML_HEADS].set(fgate_b)
    row_blk = lambda s: jnp.where(s == 0, n_seq_blocks, s - 1)
    rows = lambda w, c: pl.BlockSpec((TAIL_ROWS, w), lambda s: (row_blk(s), c))
    full = lambda r, c: pl.BlockSpec((r, c), lambda s: (0, 0))
    return pl.pallas_call(
        functools.partial(_mlstm_kernel, seq // TAIL_ROWS),
        out_shape=jax.ShapeDtypeStruct((m, ML_V_WIDTH), BF16),
        grid=(n_seq_blocks + 1,),
        in_specs=[rows(ML_V_WIDTH, Z_MQK // ML_V_WIDTH), rows(ML_V_WIDTH, Z_MV // ML_V_WIDTH), rows(D_MODEL, 0),
                  rows(ML_V_WIDTH, Z_MO // ML_V_WIDTH), full(D_MODEL, LANES),
                  full(CONV_WIDTH, 2 * ML_QK_WIDTH), full(1, 2 * ML_QK_WIDTH), full(1, LANES), full(1, ML_V_WIDTH)],
        out_specs=rows(ML_V_WIDTH, 0),
        scratch_shapes=[pltpu.VMEM((TAIL_ROWS, LANES), F32),
                        pltpu.VMEM((ML_CHUNK + CONV_HALO, 2 * ML_QK_WIDTH), BF16),
                        pltpu.VMEM((ML_HEADS, ML_QK_DIM, ML_V_DIM), F32),
                        pltpu.VMEM((SUBLANES, LANES), F32),
                        pltpu.VMEM((SUBLANES, LANES), F32),
                        pltpu.VMEM((CONV_HALO, 2 * ML_QK_WIDTH), BF16),
                        pltpu.VMEM((ML_HEADS, ML_QK_DIM, ML_V_DIM), F32),
                        pltpu.VMEM((SUBLANES, LANES), F32),
                        pltpu.VMEM((SUBLANES, LANES), F32)],
        compiler_params=_params("arbitrary"),
        name="mlstm",
    )(z, z, h, z, w_gates, conv_w.astype(F32), conv_b.reshape(1, -1).astype(F32), gate_bias,
      norm_g.reshape(1, -1).astype(F32))


def _merge_kernel(n_routed, n_body_tiles, attn_ref, ml_ref, ga_ref, gm_ref, *rest):
    with_router = n_routed is not None
    if n_body_tiles is None:
        xs_ref, wa_ref, wm_ref, wo_ref, g_ref, *rest = rest
        residual = xs_ref[...]
    else:
        xs_ref, tail_ref, wa_ref, wm_ref, wo_ref, g_ref, *rest = rest
        residual = _split_rows(pl.program_id(0), n_body_tiles, xs_ref, tail_ref)
    if with_router:
        wr_hi_ref, wr_lo_ref, br_ref, xs_out, h_out, route_out, counts_out = rest
    else:
        xs_out, h_out = rest
    a = _dot(attn_ref[...], wa_ref[...])
    m = _dot(ml_ref[...], wm_ref[...])
    y = _sigmoid(ga_ref[...].astype(F32)) * a + _sigmoid(gm_ref[...].astype(F32)) * m
    xs_new = residual + _dot(y.astype(BF16), wo_ref[...])
    xs_out[...] = xs_new
    hn = _rms(xs_new, g_ref[...])
    if not with_router:
        h_out[...] = hn.astype(h_out.dtype)
    else:
        half = D_MODEL // 2
        lo = lax.bitcast_convert_type(hn[:, :half].astype(BF16).astype(F32), jnp.uint32) >> 16
        hi = lax.bitcast_convert_type(hn[:, half:].astype(BF16).astype(F32), jnp.uint32) & jnp.uint32(0xFFFF0000)
        h_out[...] = lo | hi
        hn_hi = hn.astype(BF16)
        hn_lo = (hn - hn_hi.astype(F32)).astype(BF16)
        logits = (_dot(hn_hi, wr_hi_ref[...]) + (_dot(hn_lo, wr_hi_ref[...]) + _dot(hn_hi, wr_lo_ref[...]))
                  + br_ref[...])
        lane = lax.broadcasted_iota(jnp.int32, logits.shape, 1).astype(F32)
        l1 = logits.max(-1, keepdims=True)
        i1 = jnp.min(jnp.where(logits == l1, lane, float(LANES)), axis=-1, keepdims=True)
        rest_logits = jnp.where(lane == i1, NEG_INF, logits)
        l2 = rest_logits.max(-1, keepdims=True)
        i2 = jnp.min(jnp.where(rest_logits == l2, lane, float(LANES)), axis=-1, keepdims=True)
        e = jnp.exp(l2 - l1)
        w1 = 1.0 / (1.0 + e)
        w2 = e / (1.0 + e)
        bm = logits.shape[0]
        row = lax.broadcasted_iota(jnp.int32, (bm, 1), 0) + pl.program_id(0) * bm
        routed = row < n_routed
        pick1 = jnp.where(jnp.logical_and(lane == i1, routed), 1.0, 0.0)
        pick2 = jnp.where(jnp.logical_and(lane == i2, routed), 1.0, 0.0)
        picks = pick1 + pick2
        earlier = (lax.broadcasted_iota(jnp.int32, (bm, bm), 1) < lax.broadcasted_iota(jnp.int32, (bm, bm), 0))
        before = _dot(jnp.where(earlier, 1.0, 0.0).astype(BF16), picks.astype(BF16))
        r1 = jnp.sum(before * pick1, axis=-1, keepdims=True)
        r2 = jnp.sum(before * pick2, axis=-1, keepdims=True)
        route_out[...] = jnp.where(lane == 0, i1, jnp.where(lane == 1, i2, jnp.where(lane == 2, w1,
                                   jnp.where(lane == 3, w2, jnp.where(lane == 4, r1, jnp.where(lane == 5, r2, 0.0))))))
        counts_out[0] = jnp.broadcast_to(jnp.sum(picks, axis=0, keepdims=True), (SUBLANES, LANES))


def _merge(attn, ml, z, xs, wa, wm, wo, g_next, router=None):
    m = z.shape[0]
    bm = _tile(TAIL_ROWS, 256)
    row = lambda w, c: pl.BlockSpec((bm, w), lambda i: (i, c))
    const = lambda r, c: pl.BlockSpec((r, c), lambda i: (0, 0), pipeline_mode=pl.Buffered(1))
    if isinstance(xs, tuple):
        n_body_tiles = xs[0].shape[0] // bm
        xs_specs, xs_args = _split_specs(bm, n_body_tiles), list(xs)
    else:
        n_body_tiles = None
        xs_specs, xs_args = [row(D_MODEL, 0)], [xs]
    in_specs = [row(ATTN_WIDTH, 0), row(ML_V_WIDTH, 0), row(D_MODEL, Z_GA // D_MODEL), row(D_MODEL, Z_GM // D_MODEL),
                *xs_specs, const(ATTN_WIDTH, D_MODEL), const(ML_V_WIDTH, D_MODEL), const(D_MODEL, D_MODEL),
                const(1, D_MODEL)]
    args = [attn, ml, z, z, *xs_args, wa, wm, wo, g_next.reshape(1, D_MODEL)]
    if router is None:
        out_shape = [jax.ShapeDtypeStruct((m, D_MODEL), F32), jax.ShapeDtypeStruct((m, D_MODEL), BF16)]
        out_specs = [row(D_MODEL, 0), row(D_MODEL, 0)]
    else:
        out_shape = [jax.ShapeDtypeStruct((m, D_MODEL), F32), jax.ShapeDtypeStruct((m, D_MODEL // 2), jnp.uint32)]
        out_specs = [row(D_MODEL, 0), row(D_MODEL // 2, 0)]
    n_routed = None
    if router is not None:
        w_router, b_router, n_routed = router
        wr = jnp.zeros((D_MODEL, LANES), F32).at[:, :N_EXPERTS].set(w_router.astype(F32))
        br = jnp.full((1, LANES), NEG_INF, F32).at[0, :N_EXPERTS].set(b_router.astype(F32))
        wr_hi = wr.astype(BF16)
        wr_lo = (wr - wr_hi.astype(F32)).astype(BF16)
        in_specs += [const(D_MODEL, LANES), const(D_MODEL, LANES), const(1, LANES)]
        args += [wr_hi, wr_lo, br]
        out_shape += [jax.ShapeDtypeStruct((m, LANES), F32), jax.ShapeDtypeStruct((m // bm, SUBLANES, LANES), F32)]
        out_specs += [row(LANES, 0), pl.BlockSpec((1, SUBLANES, LANES), lambda i: (i, 0, 0))]
    return pl.pallas_call(
        functools.partial(_merge_kernel, n_routed, n_body_tiles),
        out_shape=out_shape,
        grid=(m // bm,),
        in_specs=in_specs,
        out_specs=out_specs,
        compiler_params=_params("parallel"),
        name="merge_out_proj",
    )(*args)


def _ffn_kernel(h_ref, xs_hbm, wg_ref, wu_ref, wd_ref, g_ref, xs_out, h_out, sem):
    i = pl.program_id(0)
    f = pl.program_id(1)
    bm = xs_out.shape[0]
    residual = pltpu.make_async_copy(xs_hbm.at[pl.ds(pl.multiple_of(i * bm, bm), bm), :], xs_out, sem)

    @pl.when(f == 0)
    def _():
        residual.start()

    hb = h_ref[...]
    g = _dot(hb, wg_ref[...])
    u = _dot(hb, wu_ref[...])
    act = (g * _sigmoid(g) * u).astype(BF16)

    @pl.when(f == 0)
    def _():
        residual.wait()

    xs_out[...] += _dot(act, wd_ref[...])

    @pl.when(f == pl.num_programs(1) - 1)
    def _():
        h_out[...] = _rms(xs_out[...], g_ref[...]).astype(h_out.dtype)


def _dense_ffn(h, xs, wg, wu, wd, g_next, bf=FFN_TILE):
    m = xs.shape[0]
    bm = _tile(m, 1024)
    return pl.pallas_call(
        _ffn_kernel,
        out_shape=[jax.ShapeDtypeStruct((m, D_MODEL), F32), jax.ShapeDtypeStruct((m, D_MODEL), BF16)],
        grid=(m // bm, D_FF // bf),
        in_specs=[pl.BlockSpec((bm, D_MODEL), lambda i, f: (i, 0)),
                  pl.BlockSpec(memory_space=pl.ANY),
                  pl.BlockSpec((D_MODEL, bf), lambda i, f: (0, f)),
                  pl.BlockSpec((D_MODEL, bf), lambda i, f: (0, f)),
                  pl.BlockSpec((bf, D_MODEL), lambda i, f: (f, 0)),
                  pl.BlockSpec((1, D_MODEL), lambda i, f: (0, 0))],
        out_specs=[pl.BlockSpec((bm, D_MODEL), lambda i, f: (i, 0)),
                   pl.BlockSpec((bm, D_MODEL), lambda i, f: (i, 0))],
        scratch_shapes=[pltpu.SemaphoreType.DMA(())],
        compiler_params=_params("arbitrary", "arbitrary"),
        name="dense_swiglu",
    )(h, xs, wg, wu, wd, g_next.reshape(1, D_MODEL))


def _row_copy(src_hbm, dst_vmem, sem, src_row, dst_row):
    return pltpu.make_async_copy(src_hbm.at[pl.ds(src_row, 1), :], dst_vmem.at[pl.ds(dst_row, 1), :], sem)


def _moe_gather_copy(h_hbm, gbuf, sem, src_row, dst_row):
    return pltpu.make_async_copy(h_hbm.at[pl.ds(src_row, 1), :], gbuf.at[pl.ds(dst_row, 1), :], sem)


def _moe_gather_wait(h_hbm, gbuf, sem):
    pltpu.make_async_copy(h_hbm.at[pl.ds(0, MOE_GATHER_ROWS), :], gbuf, sem).wait()


def _moe_kernel(sb_e_ref, sb_rows_ref, row_tok_ref, h_hbm, wg_ref, wu_ref, wd_ref, y_ref, gbuf, xb16, sem):
    sb = pl.program_id(0)
    f = pl.program_id(1)
    n_sb = pl.num_programs(0)
    nf = pl.num_programs(1)
    kind = sb_rows_ref[sb]
    half = D_MODEL // 2

    @pl.when(jnp.logical_and(sb == 0, f == 0))
    def _():
        def issue(r, c):
            _moe_gather_copy(h_hbm, gbuf, sem, row_tok_ref[r], r).start()
            return c
        lax.fori_loop(0, MOE_GATHER_ROWS, issue, 0)

    @pl.when(f == 0)
    def _():
        y_ref[...] = jnp.zeros_like(y_ref)
        prev_kind = sb_rows_ref[jnp.maximum(sb - 1, 0)]

        @pl.when(jnp.logical_or(sb == 0, prev_kind > 0))
        def _():
            _moe_gather_wait(h_hbm, gbuf, sem)

        @pl.when(kind > 0)
        def _():
            w = gbuf[0:MOE_SB, :]
            xb16[:, :half] = lax.bitcast_convert_type(w << 16, F32).astype(BF16)
            xb16[:, half:] = lax.bitcast_convert_type(w & jnp.uint32(0xFFFF0000), F32).astype(BF16)

    def compute(row0, rows, prefetch):
        if prefetch:
            base = (sb + 1) * MOE_SB + f * MOE_STEP_ROWS
            for r in range(MOE_STEP_ROWS):
                _moe_gather_copy(h_hbm, gbuf, sem, row_tok_ref[base + r], f * MOE_STEP_ROWS + r).start()
        xb = xb16[row0:row0 + rows, :]
        g = _dot(xb, wg_ref[0].astype(BF16))
        u = _dot(xb, wu_ref[0].astype(BF16))
        act = (g * _sigmoid(g) * u).astype(BF16)
        y_ref[row0:row0 + rows, :] += _dot(act, wd_ref[0].astype(BF16))

    @pl.when(kind == MOE_PARTS)
    def _():
        compute(0, MOE_SB, True)

    for part in range(MOE_PARTS - 1):
        @pl.when(jnp.logical_and(kind > part, kind < MOE_PARTS))
        def _():
            compute(part * MOE_PART_ROWS, MOE_PART_ROWS, part == 0)

    @pl.when(jnp.logical_and(jnp.logical_and(sb == n_sb - 1, f == nf - 1), kind > 0))
    def _():
        _moe_gather_wait(h_hbm, gbuf, sem)


def _moe_experts(h_packed, sb_e, sb_rows, row_tok, wg, wu, wd, n_sb, bf=MOE_FF_TILE):
    nf = D_FF // bf
    assert nf * MOE_STEP_ROWS == MOE_GATHER_ROWS
    ftile = lambda i, f, rows: jnp.where(rows[i] > 0, f, nf - 1)
    grid_spec = pltpu.PrefetchScalarGridSpec(
        num_scalar_prefetch=3,
        grid=(n_sb, nf),
        in_specs=[pl.BlockSpec(memory_space=pl.ANY),
                  pl.BlockSpec((1, D_MODEL, bf), lambda i, f, se, sr, rt: (se[i], 0, ftile(i, f, sr))),
                  pl.BlockSpec((1, D_MODEL, bf), lambda i, f, se, sr, rt: (se[i], 0, ftile(i, f, sr))),
                  pl.BlockSpec((1, bf, D_MODEL), lambda i, f, se, sr, rt: (se[i], ftile(i, f, sr), 0))],
        out_specs=pl.BlockSpec((MOE_SB, D_MODEL), lambda i, f, se, sr, rt: (i, 0)),
        scratch_shapes=[pltpu.VMEM((MOE_GATHER_ROWS, D_MODEL // 2), jnp.uint32),
                        pltpu.VMEM((MOE_SB, D_MODEL), BF16),
                        pltpu.SemaphoreType.DMA(())],
    )
    return pl.pallas_call(
        _moe_kernel,
        out_shape=jax.ShapeDtypeStruct((n_sb * MOE_SB, D_MODEL), F32),
        grid_spec=grid_spec,
        compiler_params=pltpu.CompilerParams(dimension_semantics=("arbitrary", "arbitrary"),
                                             vmem_limit_bytes=MOE_VMEM_LIMIT),
        name="moe_experts",
    )(sb_e, sb_rows, row_tok, h_packed, wg, wu, wd)


def _combine_kernel(pos_ref, xs_ref, route_ref, y_hbm, g_ref, o_ref, ybuf, sem):
    step = pl.program_id(0)
    n_steps = pl.num_programs(0)
    slot = step % 2

    def gather(s, dst_slot):
        base = s * BLOCK

        def issue(r, c):
            for k in range(TOP_K):
                _row_copy(y_hbm, ybuf.at[dst_slot, k], sem.at[dst_slot], pos_ref[TOP_K * (base + r) + k], r).start()
            return c
        lax.fori_loop(0, BLOCK, issue, 0, unroll=4)

    @pl.when(step == 0)
    def _():
        gather(step, slot)

    @pl.when(step + 1 < n_steps)
    def _():
        gather(step + 1, 1 - slot)

    for k in range(TOP_K):
        pltpu.make_async_copy(y_hbm.at[pl.ds(0, BLOCK), :], ybuf.at[slot, k], sem.at[slot]).wait()
    route = route_ref[...]
    moe = ybuf[slot, 0] * route[:, 2:3] + ybuf[slot, 1] * route[:, 3:4]
    o_ref[...] = _rms(xs_ref[...] + moe, g_ref[...])


def _moe_combine(pos, xs, route, yb, g_final, m_real):
    grid_spec = pltpu.PrefetchScalarGridSpec(
        num_scalar_prefetch=1,
        grid=(m_real // BLOCK,),
        in_specs=[pl.BlockSpec((BLOCK, D_MODEL), lambda t, p: (t, 0)),
                  pl.BlockSpec((BLOCK, LANES), lambda t, p: (t, 0)),
                  pl.BlockSpec(memory_space=pl.ANY),
                  pl.BlockSpec((1, D_MODEL), lambda t, p: (0, 0))],
        out_specs=pl.BlockSpec((BLOCK, D_MODEL), lambda t, p: (t, 0)),
        scratch_shapes=[pltpu.VMEM((2, TOP_K, BLOCK, D_MODEL), F32), pltpu.SemaphoreType.DMA((2,))],
    )
    return pl.pallas_call(
        _combine_kernel,
        out_shape=jax.ShapeDtypeStruct((m_real, D_MODEL), F32),
        grid_spec=grid_spec,
        compiler_params=_params("arbitrary"),
        name="moe_combine_final_norm",
    )(pos, xs, route, yb, g_final.reshape(1, D_MODEL))


def _moe_routing(route, tile_counts, n_tok):
    n_assign = n_tok * TOP_K
    n_sb = n_assign // MOE_SB + N_EXPERTS
    rows_per_tile = route.shape[0] // tile_counts.shape[0]
    e_flat = route[:n_tok, :TOP_K].astype(jnp.int32).reshape(n_assign)
    onehot = (e_flat[:, None] == jnp.arange(N_EXPERTS, dtype=jnp.int32)[None, :]).astype(jnp.int32)
    tile_counts = tile_counts[:, 0, :N_EXPERTS].astype(jnp.int32)
    tile_base = jnp.cumsum(tile_counts, axis=0) - tile_counts
    base = jnp.repeat(tile_base, rows_per_tile * TOP_K, axis=0)[:n_assign]
    rank = jnp.sum(base * onehot, axis=1) + route[:n_tok, 4:4 + TOP_K].astype(jnp.int32).reshape(n_assign)
    counts = jnp.sum(tile_counts, axis=0)
    sb_count = (counts + MOE_SB - 1) // MOE_SB
    sb_end = jnp.cumsum(sb_count)
    sb_start = sb_end - sb_count
    dest = (jnp.sum((sb_start * MOE_SB)[None, :] * onehot, axis=1) + rank).astype(jnp.int32)
    row_tok = jnp.zeros(((n_sb + 2) * MOE_SB,), jnp.int32).at[dest].set(
        jnp.arange(n_assign, dtype=jnp.int32) // TOP_K, unique_indices=True)
    sb = jnp.arange(n_sb, dtype=jnp.int32)
    sb_e = jnp.sum((sb[:, None] >= sb_end[None, :]).astype(jnp.int32), axis=1)
    valid = sb < sb_end[-1]
    last_e = jnp.sum((sb_end[-1] - 1 >= sb_end).astype(jnp.int32))
    sb_e = jnp.where(valid, sb_e, last_e).astype(jnp.int32)
    rows_here = counts[sb_e] - (sb - sb_start[sb_e]) * MOE_SB
    parts = (jnp.minimum(rows_here, MOE_SB) + MOE_PART_ROWS - 1) // MOE_PART_ROWS
    sb_rows = jnp.where(valid, parts, 0).astype(jnp.int32)
    return dest, row_tok, sb_e, sb_rows, n_sb


def _pack_w_in(w):
    o = {}
    off = 0
    for name, size in (("aq", ATTN_WIDTH), ("ak", KV_WIDTH), ("av", KV_WIDTH), ("mq", ML_QK_WIDTH),
                       ("mk", ML_QK_WIDTH), ("mv", ML_V_WIDTH), ("mi", ML_HEADS), ("mf", ML_HEADS),
                       ("mo", ML_V_WIDTH), ("ga", D_MODEL), ("gm", D_MODEL)):
        o[name] = w[:, off:off + size]
        off += size
    main = jnp.concatenate([o["ga"], o["gm"], o["aq"], o["mq"], o["mk"], o["mv"], o["mo"], o["ak"], o["av"]],
                           axis=1).astype(BF16)
    gates = jnp.concatenate([o["mi"], o["mf"], jnp.zeros((D_MODEL, LANES - 2 * ML_HEADS), w.dtype)], axis=1)
    return main, gates.astype(BF16)


def kernel(x, meta_tokens, rel_bias_table, w_in, attn_sinks, conv_w, conv_b, igate_b, fgate_b, mlstm_norm_g,
           w_attn_up, w_mlstm_up, w_out, norm_mix_g, norm_ffn_g, w_ffn_gate, w_ffn_up, w_ffn_down, w_router,
           b_router, w_moe_gate, w_moe_up, w_moe_down, final_norm_g):
    b, seq, _ = x.shape
    depth = w_in.shape[0]
    m_real = b * seq
    m = m_real + TAIL_ROWS
    assert depth == 2 and seq % ML_CHUNK == 0 and seq % BLOCK == 0 and m_real % TAIL_ROWS == 0
    tail = jnp.concatenate([jnp.zeros((PAD, D_MODEL), x.dtype), meta_tokens.astype(x.dtype),
                            jnp.zeros((TAIL_ROWS - PREFIX, D_MODEL), x.dtype)], axis=0)
    xs = (x.reshape(m_real, D_MODEL), tail)
    band, meta = _attn_bias(rel_bias_table, seq // BLOCK + 1)
    h = _rmsnorm(*xs, norm_mix_g[0])
    out = None
    for layer in range(depth):
        w_main, w_gates = _pack_w_in(w_in[layer])
        z = _matmul(h, w_main, BF16, 1024, Z_WIDTH // 4, "in_proj")
        attn = _attention(z, band, meta, attn_sinks[layer], b, seq)
        ml = _mlstm(z, h, w_gates, conv_w[layer], conv_b[layer], igate_b[layer], fgate_b[layer],
                    mlstm_norm_g[layer], b, seq)
        wa, wm, wo = (w_attn_up[layer].astype(BF16), w_mlstm_up[layer].astype(BF16), w_out[layer].astype(BF16))
        i = layer // 2
        if layer % 2 == 0:
            xs, h = _merge(attn, ml, z, xs, wa, wm, wo, norm_ffn_g[layer])
            xs, h = _dense_ffn(h, xs, w_ffn_gate[i].astype(BF16), w_ffn_up[i].astype(BF16),
                               w_ffn_down[i].astype(BF16), norm_mix_g[layer + 1])
        else:
            n_tok = m_real + PREFIX
            xs, h_packed, route, tile_counts = _merge(attn, ml, z, xs, wa, wm, wo, norm_ffn_g[layer],
                                                      router=(w_router[i], b_router[i], n_tok))
            dest, row_tok, sb_e, sb_rows, n_sb = _moe_routing(route, tile_counts, n_tok)
            yb = _moe_experts(h_packed, sb_e, sb_rows, row_tok, w_moe_gate[i], w_moe_up[i], w_moe_down[i], n_sb)
            out = _moe_combine(dest, xs, route, yb, final_norm_g, m_real).reshape(b, seq, D_MODEL)
    return out
```

```python
import functools
import math

import jax
import jax.numpy as jnp
from jax import lax
from jax.experimental import pallas as pl
from jax.experimental.pallas import tpu as pltpu

D_MODEL = 2048
N_META = 16
BLOCK = 128
PREFIX = BLOCK
PAD = PREFIX - N_META
HEAD_DIM = 64
N_Q_HEADS = 16
N_KV_HEADS = 4
GQA_GROUP = 4
ATTN_WIDTH = N_Q_HEADS * HEAD_DIM
KV_WIDTH = N_KV_HEADS * HEAD_DIM
WINDOW = 128
NUM_BUCKETS = 32
MAX_DISTANCE = 128
ML_HEADS = 4
ML_V_WIDTH = D_MODEL // 2
ML_V_DIM = ML_V_WIDTH // ML_HEADS
ML_QK_DIM = ML_V_DIM // 2
ML_QK_WIDTH = ML_HEADS * ML_QK_DIM
CONV_WIDTH = 4
D_FF = 11 * D_MODEL // 4
N_EXPERTS = 8
TOP_K = 2
EPS = 1e-6

LANES = 128
SUBLANES = 8
BF16_ROWS = 16
VMEM_LIMIT = 56 * 1024 * 1024

ZA_AQ, ZA_AK, ZA_AV, ZA_MQ, ZA_MK, ZA_MV, ZA_WIDTH = 0, 1024, 1280, 1536, 2048, 2560, 3584
ZB_MO, ZB_GA, ZB_GM, ZB_WIDTH = 0, 1024, 3072, 5120
W_IN_GATES = ZA_WIDTH
HALF_D = D_MODEL // 2
TAIL_ROWS = 512
ML_CHUNK = 128
CONV_HALO = 16
FFN_TILE = 512
MOE_SB = 1024
MOE_PARTS = 4
MOE_PART_ROWS = MOE_SB // MOE_PARTS
MOE_FF_TILE = 512
MOE_STEP_ROWS = 96
MOE_GATHER_ROWS = MOE_STEP_ROWS * (D_FF // MOE_FF_TILE)
MOE_VMEM_LIMIT = 60 * 1024 * 1024

F32 = jnp.float32
BF16 = jnp.bfloat16
NEG_INF = float("-inf")


def _tile(m, target):
    best = LANES
    for t in range(LANES, min(m, target) + 1, LANES):
        if m % t == 0:
            best = t
    assert m % best == 0
    return best


def _params(*sem):
    return pltpu.CompilerParams(dimension_semantics=sem, vmem_limit_bytes=VMEM_LIMIT)


def _rms(x, g):
    return x * lax.rsqrt(jnp.mean(x * x, axis=-1, keepdims=True) + EPS) * g


def _sigmoid(x):
    return 1.0 / (1.0 + jnp.exp(-x))


def _dot(a, b):
    return jnp.dot(a, b, preferred_element_type=F32)


def _dot_nt(a, b):
    return lax.dot_general(a, b, (((1,), (1,)), ((), ())), preferred_element_type=F32)


def _dot_tn(a, b):
    return lax.dot_general(a, b, (((0,), (0,)), ((), ())), preferred_element_type=F32)


def _split_rows(i, n_body_tiles, body_ref, tail_ref):
    return jnp.where(i < n_body_tiles, body_ref[...], tail_ref[...])


def _split_specs(bm, n_body_tiles):
    return [pl.BlockSpec((bm, D_MODEL), lambda i: (jnp.minimum(i, n_body_tiles - 1), 0)),
            pl.BlockSpec((bm, D_MODEL), lambda i: (jnp.maximum(i - n_body_tiles, 0), 0))]


def _norm_kernel(n_body_tiles, x_ref, tail_ref, g_ref, o_ref):
    x = _split_rows(pl.program_id(0), n_body_tiles, x_ref, tail_ref)
    o_ref[...] = _rms(x, g_ref[...]).astype(o_ref.dtype)


def _rmsnorm(x_body, x_tail, g):
    m = x_body.shape[0] + x_tail.shape[0]
    bm = _tile(x_tail.shape[0], 512)
    assert x_body.shape[0] % bm == 0
    n_body_tiles = x_body.shape[0] // bm
    return pl.pallas_call(
        functools.partial(_norm_kernel, n_body_tiles),
        out_shape=jax.ShapeDtypeStruct((m, D_MODEL), BF16),
        grid=(m // bm,),
        in_specs=_split_specs(bm, n_body_tiles) + [pl.BlockSpec((1, D_MODEL), lambda i: (0, 0))],
        out_specs=pl.BlockSpec((bm, D_MODEL), lambda i: (i, 0)),
        compiler_params=_params("parallel"),
        name="rmsnorm",
    )(x_body, x_tail, g.reshape(1, D_MODEL))


def _mm_nt_kernel(a_ref, wt_ref, o_ref):
    o_ref[...] = _dot_nt(a_ref[...], wt_ref[...]).astype(o_ref.dtype)


def _matmul_nt(a, wt, out_dtype, bm, bn, name):
    m, k = a.shape
    n = wt.shape[0]
    bm = _tile(m, bm)
    return pl.pallas_call(
        _mm_nt_kernel,
        out_shape=jax.ShapeDtypeStruct((m, n), out_dtype),
        grid=(n // bn, m // bm),
        in_specs=[pl.BlockSpec((bm, k), lambda j, i: (i, 0)),
                  pl.BlockSpec((bn, k), lambda j, i: (j, 0))],
        out_specs=pl.BlockSpec((bm, bn), lambda j, i: (i, j)),
        compiler_params=_params("parallel", "parallel"),
        name=name,
    )(a, wt)


def _t5_bucket(rel):
    n = jnp.maximum(rel, 0)
    max_exact = NUM_BUCKETS // 2
    large = max_exact + (jnp.log(jnp.maximum(n, 1).astype(F32) / max_exact)
                         / math.log(MAX_DISTANCE / max_exact) * (NUM_BUCKETS - max_exact)).astype(jnp.int32)
    large = jnp.minimum(large, NUM_BUCKETS - 1)
    return jnp.where(n < max_exact, n, large)


def _bias_lookup(table, rel):
    onehot = (_t5_bucket(rel)[..., None] == jnp.arange(NUM_BUCKETS)).astype(F32)
    return jnp.einsum("...b,bh->h...", onehot, table.astype(F32), precision=lax.Precision.HIGHEST)


ATTN_GROUP_ORDER = (0, 2, 1, 3)


def _stack_group_rows(a):
    lead = a.shape[:-3]
    a = a.reshape(*lead, N_KV_HEADS, GQA_GROUP, BLOCK, a.shape[-1])
    a = jnp.take(a, jnp.array(ATTN_GROUP_ORDER), axis=len(lead) + 1)
    return a.reshape(*lead, N_KV_HEADS, GQA_GROUP * BLOCK, a.shape[-1])


def _attn_bias(table, nb):
    qi = jnp.arange(BLOCK)[:, None]
    ki = jnp.arange(2 * BLOCK)[None, :]
    rel_band = qi + BLOCK - ki
    blk3 = jnp.arange(3)[:, None, None]
    mask_band = (rel_band >= 0) & (rel_band < WINDOW) & ((blk3 - 1) * BLOCK + ki >= PAD)
    band = jnp.where(mask_band[:, None], _bias_lookup(table, rel_band)[None], NEG_INF)
    blk = jnp.arange(nb)[:, None, None]
    rel_meta = blk * BLOCK + qi[None] - (PAD + jnp.arange(N_META))
    meta = jnp.where((rel_meta >= WINDOW)[None], _bias_lookup(table, rel_meta), NEG_INF)
    meta = jnp.pad(jnp.moveaxis(meta, 1, 0), ((0, 0), (0, 0), (0, 0), (PAD, 0)), constant_values=NEG_INF)
    return _stack_group_rows(band), _stack_group_rows(meta)


def _swap_halves(x):
    return pltpu.roll(x.astype(F32), HEAD_DIM, axis=1).astype(x.dtype)


def _attn_kernel(q_ref, kp_ref, kc_ref, km_ref, vp_ref, vc_ref, vm_ref, bb_ref, bm_ref, sink_ref, o_ref):
    scale = HEAD_DIM ** -0.5
    assert math.frexp(scale)[0] == 0.5 and 2 * HEAD_DIM == LANES
    lane = lax.broadcasted_iota(jnp.int32, (1, LANES), 1)
    keep = (jnp.where(lane < HEAD_DIM, scale, 0.0).astype(BF16),
            jnp.where(lane < HEAD_DIM, 0.0, scale).astype(BF16))
    low = lax.broadcasted_iota(jnp.int32, (2 * BLOCK, LANES), 1) < HEAD_DIM
    chains = []
    for col in range(N_KV_HEADS // 2):
        kcols = slice(col * LANES, (col + 1) * LANES)
        k_nat = jnp.concatenate([kp_ref[:, kcols], kc_ref[:, kcols], km_ref[:, kcols]], axis=0)
        v_nat = jnp.concatenate([vp_ref[:, kcols], vc_ref[:, kcols], vm_ref[:, kcols]], axis=0)
        keys = (k_nat, _swap_halves(k_nat))
        vals = (v_nat, _swap_halves(v_nat))
        for half in range(2):
            h = 2 * col + half
            q0 = h * GQA_GROUP * HEAD_DIM
            q2 = jnp.concatenate([q_ref[:, q0:q0 + LANES], q_ref[:, q0 + LANES:q0 + 2 * LANES]], axis=0)
            for lane_half in range(2):
                which = 0 if lane_half == half else 1
                chains.append((h, lane_half, _dot_nt(q2 * keep[lane_half], keys[which]), vals[which]))
    probs = []
    for h, lane_half, s, _ in chains:
        rows = slice(lane_half * 2 * BLOCK, (lane_half + 1) * 2 * BLOCK)
        s0 = s[:, :BLOCK] + bb_ref[0, h, rows, :BLOCK]
        s1 = s[:, BLOCK:2 * BLOCK] + bb_ref[0, h, rows, BLOCK:]
        s2 = s[:, 2 * BLOCK:] + bm_ref[0, h, rows, :]
        sink = sink_ref[h, rows, :]
        mx = jnp.maximum(jnp.maximum(jnp.maximum(s0, s1), s2).max(-1, keepdims=True), sink)
        p0 = jnp.exp(s0 - mx)
        p1 = jnp.exp(s1 - mx)
        p2 = jnp.exp(s2 - mx)
        den = (p0 + p1 + p2).sum(-1, keepdims=True) + jnp.exp(sink - mx)
        probs.append((jnp.concatenate([p0, p1, p2], axis=1).astype(BF16), 1.0 / den))
    outs = [_dot(p, chain[3]) * rden for (p, rden), chain in zip(probs, chains)]
    out_cols = []
    for h in range(N_KV_HEADS):
        o = jnp.where(low, outs[2 * h], outs[2 * h + 1]).astype(o_ref.dtype)
        out_cols += [o[:BLOCK], o[BLOCK:]]
    o_ref[...] = jnp.concatenate(out_cols, axis=1)


def _attention(z, band, meta, sinks, b, seq):
    m = z.shape[0]
    per_seq = seq // BLOCK
    n_real = b * per_seq
    kcol, vcol = ZA_AK // KV_WIDTH, ZA_AV // KV_WIDTH
    sink_col = _stack_group_rows(jnp.broadcast_to(sinks.astype(F32)[:, None, None], (N_Q_HEADS, BLOCK, LANES)))
    kv = lambda col, f: pl.BlockSpec((BLOCK, KV_WIDTH), lambda s: (f(s), col))
    cur = lambda s: s
    prev = lambda s: jnp.where(s < n_real, jnp.where(s % per_seq == 0, n_real, s - 1), s)
    first = lambda s: n_real
    query_block = lambda s: jnp.where(s < n_real, s % per_seq + 1, 0)
    return pl.pallas_call(
        _attn_kernel,
        out_shape=jax.ShapeDtypeStruct((m, ATTN_WIDTH), BF16),
        grid=(m // BLOCK,),
        in_specs=[pl.BlockSpec((BLOCK, ATTN_WIDTH), lambda s: (s, ZA_AQ // ATTN_WIDTH)),
                  kv(kcol, prev), kv(kcol, cur), kv(kcol, first),
                  kv(vcol, prev), kv(vcol, cur), kv(vcol, first),
                  pl.BlockSpec((1, N_KV_HEADS, GQA_GROUP * BLOCK, 2 * BLOCK),
                               lambda s: (jnp.minimum(query_block(s), 2), 0, 0, 0)),
                  pl.BlockSpec((1, N_KV_HEADS, GQA_GROUP * BLOCK, BLOCK), lambda s: (query_block(s), 0, 0, 0)),
                  pl.BlockSpec((N_KV_HEADS, GQA_GROUP * BLOCK, LANES), lambda s: (0, 0, 0))],
        out_specs=pl.BlockSpec((BLOCK, ATTN_WIDTH), lambda s: (s, 0)),
        compiler_params=_params("parallel"),
        name="swa_attention",
    )(z, z, z, z, z, z, z, band, meta, sink_col)


def _mlstm_chunk(qk_in, v_in, gate_in, mo_in, prefix_chunk, cw_ref, cb_ref, gb_ref, ng_ref, xa_ref, ct_ref, n_ref, m_ref):
    L = ML_CHUNK
    ii = lax.broadcasted_iota(jnp.int32, (L, L), 0)
    jj = lax.broadcasted_iota(jnp.int32, (L, L), 1)
    causal = jj <= ii
    lane = lax.broadcasted_iota(jnp.int32, (L, LANES), 1)
    k_scale = ML_QK_DIM ** -0.5
    xa_ref[CONV_HALO:CONV_HALO + L, :] = qk_in
    window = xa_ref[...]
    sel_row = lax.broadcasted_iota(jnp.int32, (L, CONV_HALO + L), 0)
    sel_col = lax.broadcasted_iota(jnp.int32, (L, CONV_HALO + L), 1)
    acc = cb_ref[...] + cw_ref[CONV_WIDTH - 1:CONV_WIDTH, :] * qk_in.astype(F32)
    for j in range(CONV_WIDTH - 1):
        back = CONV_WIDTH - 1 - j
        shift = jnp.where(sel_col == sel_row + (CONV_HALO - back), 1.0, 0.0).astype(BF16)
        acc = acc + cw_ref[j:j + 1, :] * _dot(shift, window)
    qk = acc * _sigmoid(acc)
    xa_ref[0:CONV_HALO, :] = xa_ref[L:L + CONV_HALO, :]
    gpb = gate_in + gb_ref[...]
    log_sig = jnp.minimum(gpb, 0.0) - jnp.log1p(jnp.exp(-jnp.abs(gpb)))
    if prefix_chunk:
        valid = lax.broadcasted_iota(jnp.int32, (L, LANES), 0) >= PAD
        gx = jnp.where(lane < ML_HEADS, jnp.where(valid, gpb, NEG_INF), jnp.where(valid, log_sig, 0.0))
    else:
        gx = jnp.where(lane < ML_HEADS, gpb, log_sig)
    gxt = gx.T
    outs = []
    for h in range(ML_HEADS):
        q = qk[:, h * ML_QK_DIM:(h + 1) * ML_QK_DIM].astype(BF16)
        k = qk[:, ML_QK_WIDTH + h * ML_QK_DIM:ML_QK_WIDTH + (h + 1) * ML_QK_DIM] * k_scale
        v = v_in[:, h * ML_V_DIM:(h + 1) * ML_V_DIM]
        ig_col = gx[:, h:h + 1]
        lf_col = gx[:, ML_HEADS + h:ML_HEADS + h + 1]
        ig_row = gxt[h:h + 1, :]
        lf_row = gxt[ML_HEADS + h:ML_HEADS + h + 1, :]
        b_col = jnp.sum(jnp.where(causal, lf_row, 0.0), axis=-1, keepdims=True)
        b_row = jnp.sum(jnp.where(ii <= jj, lf_col, 0.0), axis=0, keepdims=True)
        m_prev = m_ref[h:h + 1, 0:1]
        log_d = jnp.where(causal, b_col - b_row + ig_row, NEG_INF)
        m_inter = b_col + m_prev
        m_out = jnp.maximum(m_inter, log_d.max(-1, keepdims=True))
        d = jnp.exp(log_d - m_out)
        inter = jnp.exp(m_inter - m_out)
        ct = ct_ref[h]
        n_prev = n_ref[h:h + 1, :]
        qk_scores = _dot_nt(q, k.astype(BF16))
        q_state = _dot(q, ct.astype(BF16))
        q_norm = jnp.sum(q.astype(F32) * n_prev, axis=-1, keepdims=True)
        b_last = b_col[L - 1:L, :]
        log_w = b_last - b_col + ig_col
        m_new = jnp.maximum(b_last + m_prev, log_w.max(0, keepdims=True))
        decay = jnp.exp(b_last + m_prev - m_new)
        kw = k * jnp.exp(log_w - m_new)
        s = qk_scores * d
        vcols = slice(h * ML_V_DIM, (h + 1) * ML_V_DIM)
        num = _dot(s.astype(BF16), v) + inter * q_state
        den = s.sum(-1, keepdims=True) + inter * q_norm
        hh = num / jnp.maximum(jnp.abs(den), jnp.exp(-m_out))
        ct_ref[h] = decay * ct + _dot_tn(kw.astype(BF16), v)
        n_ref[h:h + 1, :] = decay * n_prev + kw.sum(0, keepdims=True)
        m_ref[h:h + 1, :] = jnp.broadcast_to(m_new, (1, LANES))
        hn = hh * lax.rsqrt(jnp.mean(hh * hh, axis=-1, keepdims=True) + EPS) * ng_ref[:, vcols]
        outs.append(_sigmoid(mo_in[:, vcols].astype(F32)) * hn)
    return jnp.concatenate(outs, axis=1)


def _mlstm_kernel(blocks_per_seq, q_ref, k_ref, va_ref, vb_ref, h_ref, mo_ref, wgate_ref, cw_ref, cb_ref, gb_ref, ng_ref,
                  o_ref, gate_ref, xa_ref, ct_ref, n_ref, m_ref, xa0_ref, ct0_ref, n0_ref, m0_ref):
    L = ML_CHUNK
    step = pl.program_id(0)
    params = (cw_ref, cb_ref, gb_ref, ng_ref, xa_ref, ct_ref, n_ref, m_ref)
    gate_ref[...] = _dot_nt(h_ref[...], wgate_ref[...])

    def chunk_out(rows, prefix_chunk):
        qk = jnp.concatenate([q_ref[rows, :], k_ref[rows, :]], axis=1)
        v = jnp.concatenate([va_ref[rows, :], vb_ref[rows, :]], axis=1)
        return _mlstm_chunk(qk, v, gate_ref[rows, :], mo_ref[rows, :], prefix_chunk, *params).astype(o_ref.dtype)

    @pl.when(step == 0)
    def _():
        xa_ref[0:CONV_HALO, :] = jnp.zeros((CONV_HALO, 2 * ML_QK_WIDTH), xa_ref.dtype)
        ct_ref[...] = jnp.zeros_like(ct_ref)
        n_ref[...] = jnp.zeros_like(n_ref)
        m_ref[...] = jnp.zeros_like(m_ref)
        o_ref[0:L, :] = chunk_out(slice(0, L), True)
        o_ref[L:, :] = jnp.zeros((o_ref.shape[0] - L, o_ref.shape[1]), o_ref.dtype)
        xa0_ref[...] = xa_ref[0:CONV_HALO, :]
        ct0_ref[...] = ct_ref[...]
        n0_ref[...] = n_ref[...]
        m0_ref[...] = m_ref[...]

    @pl.when(step > 0)
    def _():
        @pl.when((step - 1) % blocks_per_seq == 0)
        def _():
            xa_ref[0:CONV_HALO, :] = xa0_ref[...]
            ct_ref[...] = ct0_ref[...]
            n_ref[...] = n0_ref[...]
            m_ref[...] = m0_ref[...]

        def chunk(c, carry):
            rows = pl.ds(pl.multiple_of(c * L, L), L)
            o_ref[rows, :] = chunk_out(rows, False)
            return carry

        lax.fori_loop(0, TAIL_ROWS // L, chunk, 0)


def _mlstm(za, zb, h, w_gates, conv_w, conv_b, igate_b, fgate_b, norm_g, b, seq):
    m = za.shape[0]
    assert seq % TAIL_ROWS == 0 and TAIL_ROWS % ML_CHUNK == 0
    n_seq_blocks = b * seq // TAIL_ROWS
    gate_bias = jnp.zeros((1, LANES), F32).at[0, :ML_HEADS].set(igate_b).at[0, ML_HEADS:2 * ML_HEADS].set(fgate_b)
    row_blk = lambda s: jnp.where(s == 0, n_seq_blocks, s - 1)
    rows = lambda w, c: pl.BlockSpec((TAIL_ROWS, w), lambda s: (row_blk(s), c))
    full = lambda r, c: pl.BlockSpec((r, c), lambda s: (0, 0))
    return pl.pallas_call(
        functools.partial(_mlstm_kernel, seq // TAIL_ROWS),
        out_shape=jax.ShapeDtypeStruct((m, ML_V_WIDTH), BF16),
        grid=(n_seq_blocks + 1,),
        in_specs=[rows(ML_QK_WIDTH, ZA_MQ // ML_QK_WIDTH), rows(ML_QK_WIDTH, ZA_MK // ML_QK_WIDTH),
                  rows(ML_V_WIDTH // 2, ZA_MV // (ML_V_WIDTH // 2)), rows(ML_V_WIDTH // 2, ZA_MV // (ML_V_WIDTH // 2) + 1),
                  rows(D_MODEL, 0), rows(ML_V_WIDTH, ZB_MO // ML_V_WIDTH), full(LANES, D_MODEL),
                  full(CONV_WIDTH, 2 * ML_QK_WIDTH), full(1, 2 * ML_QK_WIDTH), full(1, LANES), full(1, ML_V_WIDTH)],
        out_specs=rows(ML_V_WIDTH, 0),
        scratch_shapes=[pltpu.VMEM((TAIL_ROWS, LANES), F32),
                        pltpu.VMEM((ML_CHUNK + CONV_HALO, 2 * ML_QK_WIDTH), BF16),
                        pltpu.VMEM((ML_HEADS, ML_QK_DIM, ML_V_DIM), F32),
                        pltpu.VMEM((SUBLANES, LANES), F32),
                        pltpu.VMEM((SUBLANES, LANES), F32),
                        pltpu.VMEM((CONV_HALO, 2 * ML_QK_WIDTH), BF16),
                        pltpu.VMEM((ML_HEADS, ML_QK_DIM, ML_V_DIM), F32),
                        pltpu.VMEM((SUBLANES, LANES), F32),
                        pltpu.VMEM((SUBLANES, LANES), F32)],
        compiler_params=_params("arbitrary"),
        name="mlstm",
    )(za, za, za, za, h, zb, w_gates, conv_w.astype(F32), conv_b.reshape(1, -1).astype(F32), gate_bias,
      norm_g.reshape(1, -1).astype(F32))


def _merge_kernel(n_routed, n_body_tiles, attn_ref, ml_ref, ga0_ref, ga1_ref, gm0_ref, gm1_ref, *rest):
    with_router = n_routed is not None
    if n_body_tiles is None:
        xs_ref, wa_ref, wm_ref, wo_ref, g_ref, *rest = rest
        residual = xs_ref[...]
    else:
        xs_ref, tail_ref, wa_ref, wm_ref, wo_ref, g_ref, *rest = rest
        residual = _split_rows(pl.program_id(0), n_body_tiles, xs_ref, tail_ref)
    if with_router:
        wr_hi_ref, wr_lo_ref, br_ref, xs_out, h_out, route_out, counts_out = rest
    else:
        xs_out, h_out = rest
    a = _dot(attn_ref[...], wa_ref[...])
    m = _dot(ml_ref[...], wm_ref[...])
    ga = jnp.concatenate([ga0_ref[...], ga1_ref[...]], axis=1)
    gm = jnp.concatenate([gm0_ref[...], gm1_ref[...]], axis=1)
    y = _sigmoid(ga.astype(F32)) * a + _sigmoid(gm.astype(F32)) * m
    xs_new = residual + _dot(y.astype(BF16), wo_ref[...])
    xs_out[...] = xs_new
    hn = _rms(xs_new, g_ref[...])
    if not with_router:
        h_out[...] = hn.astype(h_out.dtype)
    else:
        half = D_MODEL // 2
        lo = lax.bitcast_convert_type(hn[:, :half].astype(BF16).astype(F32), jnp.uint32) >> 16
        hi = lax.bitcast_convert_type(hn[:, half:].astype(BF16).astype(F32), jnp.uint32) & jnp.uint32(0xFFFF0000)
        h_out[...] = lo | hi
        hn_hi = hn.astype(BF16)
        hn_lo = (hn - hn_hi.astype(F32)).astype(BF16)
        logits = (_dot(hn_hi, wr_hi_ref[...]) + (_dot(hn_lo, wr_hi_ref[...]) + _dot(hn_hi, wr_lo_ref[...]))
                  + br_ref[...])
        lane = lax.broadcasted_iota(jnp.int32, logits.shape, 1).astype(F32)
        l1 = logits.max(-1, keepdims=True)
        i1 = jnp.min(jnp.where(logits == l1, lane, float(LANES)), axis=-1, keepdims=True)
        rest_logits = jnp.where(lane == i1, NEG_INF, logits)
        l2 = rest_logits.max(-1, keepdims=True)
        i2 = jnp.min(jnp.where(rest_logits == l2, lane, float(LANES)), axis=-1, keepdims=True)
        e = jnp.exp(l2 - l1)
        w1 = 1.0 / (1.0 + e)
        w2 = e / (1.0 + e)
        bm = logits.shape[0]
        row = lax.broadcasted_iota(jnp.int32, (bm, 1), 0) + pl.program_id(0) * bm
        routed = row < n_routed
        pick1 = jnp.where(jnp.logical_and(lane == i1, routed), 1.0, 0.0)
        pick2 = jnp.where(jnp.logical_and(lane == i2, routed), 1.0, 0.0)
        picks = pick1 + pick2
        earlier = (lax.broadcasted_iota(jnp.int32, (bm, bm), 1) < lax.broadcasted_iota(jnp.int32, (bm, bm), 0))
        before = _dot(jnp.where(earlier, 1.0, 0.0).astype(BF16), picks.astype(BF16))
        r1 = jnp.sum(before * pick1, axis=-1, keepdims=True)
        r2 = jnp.sum(before * pick2, axis=-1, keepdims=True)
        route_out[...] = jnp.where(lane == 0, i1, jnp.where(lane == 1, i2, jnp.where(lane == 2, w1,
                                   jnp.where(lane == 3, w2, jnp.where(lane == 4, r1, jnp.where(lane == 5, r2, 0.0))))))
        counts_out[0] = jnp.broadcast_to(jnp.sum(picks, axis=0, keepdims=True), (SUBLANES, LANES))


def _merge(attn, ml, zb, xs, wa, wm, wo, g_next, router=None):
    m = zb.shape[0]
    bm = _tile(TAIL_ROWS, 256)
    row = lambda w, c: pl.BlockSpec((bm, w), lambda i: (i, c))
    const = lambda r, c: pl.BlockSpec((r, c), lambda i: (0, 0), pipeline_mode=pl.Buffered(1))
    if isinstance(xs, tuple):
        n_body_tiles = xs[0].shape[0] // bm
        xs_specs, xs_args = _split_specs(bm, n_body_tiles), list(xs)
    else:
        n_body_tiles = None
        xs_specs, xs_args = [row(D_MODEL, 0)], [xs]
    in_specs = [row(ATTN_WIDTH, 0), row(ML_V_WIDTH, 0),
                row(HALF_D, ZB_GA // HALF_D), row(HALF_D, ZB_GA // HALF_D + 1),
                row(HALF_D, ZB_GM // HALF_D), row(HALF_D, ZB_GM // HALF_D + 1),
                *xs_specs, const(ATTN_WIDTH, D_MODEL), const(ML_V_WIDTH, D_MODEL), const(D_MODEL, D_MODEL),
                const(1, D_MODEL)]
    args = [attn, ml, zb, zb, zb, zb, *xs_args, wa, wm, wo, g_next.reshape(1, D_MODEL)]
    if router is None:
        out_shape = [jax.ShapeDtypeStruct((m, D_MODEL), F32), jax.ShapeDtypeStruct((m, D_MODEL), BF16)]
        out_specs = [row(D_MODEL, 0), row(D_MODEL, 0)]
    else:
        out_shape = [jax.ShapeDtypeStruct((m, D_MODEL), F32), jax.ShapeDtypeStruct((m, D_MODEL // 2), jnp.uint32)]
        out_specs = [row(D_MODEL, 0), row(D_MODEL // 2, 0)]
    n_routed = None
    if router is not None:
        w_router, b_router, n_routed = router
        wr = jnp.zeros((D_MODEL, LANES), F32).at[:, :N_EXPERTS].set(w_router.astype(F32))
        br = jnp.full((1, LANES), NEG_INF, F32).at[0, :N_EXPERTS].set(b_router.astype(F32))
        wr_hi = wr.astype(BF16)
        wr_lo = (wr - wr_hi.astype(F32)).astype(BF16)
        in_specs += [const(D_MODEL, LANES), const(D_MODEL, LANES), const(1, LANES)]
        args += [wr_hi, wr_lo, br]
        out_shape += [jax.ShapeDtypeStruct((m, LANES), F32), jax.ShapeDtypeStruct((m // bm, SUBLANES, LANES), F32)]
        out_specs += [row(LANES, 0), pl.BlockSpec((1, SUBLANES, LANES), lambda i: (i, 0, 0))]
    return pl.pallas_call(
        functools.partial(_merge_kernel, n_routed, n_body_tiles),
        out_shape=out_shape,
        grid=(m // bm,),
        in_specs=in_specs,
        out_specs=out_specs,
        compiler_params=_params("parallel"),
        name="merge_out_proj",
    )(*args)


def _ffn_kernel(h_ref, xs_hbm, wg_ref, wu_ref, wd_ref, g_ref, xs_out, h_out, sem):
    i = pl.program_id(0)
    f = pl.program_id(1)
    bm = xs_out.shape[0]
    residual = pltpu.make_async_copy(xs_hbm.at[pl.ds(pl.multiple_of(i * bm, bm), bm), :], xs_out, sem)

    @pl.when(f == 0)
    def _():
        residual.start()

    hb = h_ref[...]
    g = _dot(hb, wg_ref[...])
    u = _dot(hb, wu_ref[...])
    act = (g * _sigmoid(g) * u).astype(BF16)

    @pl.when(f == 0)
    def _():
        residual.wait()

    xs_out[...] += _dot(act, wd_ref[...])

    @pl.when(f == pl.num_programs(1) - 1)
    def _():
        h_out[...] = _rms(xs_out[...], g_ref[...]).astype(h_out.dtype)


def _dense_ffn(h, xs, wg, wu, wd, g_next, bf=FFN_TILE):
    m = xs.shape[0]
    bm = _tile(m, 1024)
    return pl.pallas_call(
        _ffn_kernel,
        out_shape=[jax.ShapeDtypeStruct((m, D_MODEL), F32), jax.ShapeDtypeStruct((m, D_MODEL), BF16)],
        grid=(m // bm, D_FF // bf),
        in_specs=[pl.BlockSpec((bm, D_MODEL), lambda i, f: (i, 0)),
                  pl.BlockSpec(memory_space=pl.ANY),
                  pl.BlockSpec((D_MODEL, bf), lambda i, f: (0, f)),
                  pl.BlockSpec((D_MODEL, bf), lambda i, f: (0, f)),
                  pl.BlockSpec((bf, D_MODEL), lambda i, f: (f, 0)),
                  pl.BlockSpec((1, D_MODEL), lambda i, f: (0, 0))],
        out_specs=[pl.BlockSpec((bm, D_MODEL), lambda i, f: (i, 0)),
                   pl.BlockSpec((bm, D_MODEL), lambda i, f: (i, 0))],
        scratch_shapes=[pltpu.SemaphoreType.DMA(())],
        compiler_params=_params("arbitrary", "arbitrary"),
        name="dense_swiglu",
    )(h, xs, wg, wu, wd, g_next.reshape(1, D_MODEL))


def _row_copy(src_hbm, dst_vmem, sem, src_row, dst_row):
    return pltpu.make_async_copy(src_hbm.at[pl.ds(src_row, 1), :], dst_vmem.at[pl.ds(dst_row, 1), :], sem)


def _moe_gather_copy(h_hbm, gbuf, sem, src_row, dst_row):
    return pltpu.make_async_copy(h_hbm.at[pl.ds(src_row, 1), :], gbuf.at[pl.ds(dst_row, 1), :], sem)


def _moe_gather_wait(h_hbm, gbuf, sem):
    pltpu.make_async_copy(h_hbm.at[pl.ds(0, MOE_GATHER_ROWS), :], gbuf, sem).wait()


def _moe_kernel(sb_e_ref, sb_rows_ref, row_tok_ref, h_hbm, wg_ref, wu_ref, wd_ref, y_ref, gbuf, xb16, sem):
    sb = pl.program_id(0)
    f = pl.program_id(1)
    n_sb = pl.num_programs(0)
    nf = pl.num_programs(1)
    kind = sb_rows_ref[sb]
    half = D_MODEL // 2

    @pl.when(jnp.logical_and(sb == 0, f == 0))
    def _():
        def issue(r, c):
            _moe_gather_copy(h_hbm, gbuf, sem, row_tok_ref[r], r).start()
            return c
        lax.fori_loop(0, MOE_GATHER_ROWS, issue, 0)

    @pl.when(f == 0)
    def _():
        y_ref[...] = jnp.zeros_like(y_ref)
        prev_kind = sb_rows_ref[jnp.maximum(sb - 1, 0)]

        @pl.when(jnp.logical_or(sb == 0, prev_kind > 0))
        def _():
            _moe_gather_wait(h_hbm, gbuf, sem)

        @pl.when(kind > 0)
        def _():
            w = gbuf[0:MOE_SB, :]
            xb16[:, :half] = lax.bitcast_convert_type(w << 16, F32).astype(BF16)
            xb16[:, half:] = lax.bitcast_convert_type(w & jnp.uint32(0xFFFF0000), F32).astype(BF16)

    def compute(row0, rows, prefetch):
        if prefetch:
            base = (sb + 1) * MOE_SB + f * MOE_STEP_ROWS
            for r in range(MOE_STEP_ROWS):
                _moe_gather_copy(h_hbm, gbuf, sem, row_tok_ref[base + r], f * MOE_STEP_ROWS + r).start()
        xb = xb16[row0:row0 + rows, :]
        g = _dot(xb, wg_ref[0].astype(BF16))
        u = _dot(xb, wu_ref[0].astype(BF16))
        act = (g * _sigmoid(g) * u).astype(BF16)
        y_ref[row0:row0 + rows, :] += _dot(act, wd_ref[0].astype(BF16))

    @pl.when(kind == MOE_PARTS)
    def _():
        compute(0, MOE_SB, True)

    for part in range(MOE_PARTS - 1):
        @pl.when(jnp.logical_and(kind > part, kind < MOE_PARTS))
        def _():
            compute(part * MOE_PART_ROWS, MOE_PART_ROWS, part == 0)

    @pl.when(jnp.logical_and(jnp.logical_and(sb == n_sb - 1, f == nf - 1), kind > 0))
    def _():
        _moe_gather_wait(h_hbm, gbuf, sem)


def _moe_experts(h_packed, sb_e, sb_rows, row_tok, wg, wu, wd, n_sb, bf=MOE_FF_TILE):
    nf = D_FF // bf
    assert nf * MOE_STEP_ROWS == MOE_GATHER_ROWS
    ftile = lambda i, f, rows: jnp.where(rows[i] > 0, f, nf - 1)
    grid_spec = pltpu.PrefetchScalarGridSpec(
        num_scalar_prefetch=3,
        grid=(n_sb, nf),
        in_specs=[pl.BlockSpec(memory_space=pl.ANY),
                  pl.BlockSpec((1, D_MODEL, bf), lambda i, f, se, sr, rt: (se[i], 0, ftile(i, f, sr))),
                  pl.BlockSpec((1, D_MODEL, bf), lambda i, f, se, sr, rt: (se[i], 0, ftile(i, f, sr))),
                  pl.BlockSpec((1, bf, D_MODEL), lambda i, f, se, sr, rt: (se[i], ftile(i, f, sr), 0))],
        out_specs=pl.BlockSpec((MOE_SB, D_MODEL), lambda i, f, se, sr, rt: (i, 0)),
        scratch_shapes=[pltpu.VMEM((MOE_GATHER_ROWS, D_MODEL // 2), jnp.uint32),
                        pltpu.VMEM((MOE_SB, D_MODEL), BF16),
                        pltpu.SemaphoreType.DMA(())],
    )
    return pl.pallas_call(
        _moe_kernel,
        out_shape=jax.ShapeDtypeStruct((n_sb * MOE_SB, D_MODEL), F32),
        grid_spec=grid_spec,
        compiler_params=pltpu.CompilerParams(dimension_semantics=("arbitrary", "arbitrary"),
                                             vmem_limit_bytes=MOE_VMEM_LIMIT),
        name="moe_experts",
    )(sb_e, sb_rows, row_tok, h_packed, wg, wu, wd)


def _combine_kernel(pos_ref, xs_ref, route_ref, y_hbm, g_ref, o_ref, ybuf, sem):
    step = pl.program_id(0)
    n_steps = pl.num_programs(0)
    slot = step % 2

    rows = o_ref.shape[0]

    def gather(s, dst_slot):
        base = s * rows

        def issue(r, c):
            for k in range(TOP_K):
                _row_copy(y_hbm, ybuf.at[dst_slot, k], sem.at[dst_slot], pos_ref[TOP_K * (base + r) + k], r).start()
            return c
        lax.fori_loop(0, rows, issue, 0, unroll=4)

    @pl.when(step == 0)
    def _():
        gather(step, slot)

    @pl.when(step + 1 < n_steps)
    def _():
        gather(step + 1, 1 - slot)

    for k in range(TOP_K):
        pltpu.make_async_copy(y_hbm.at[pl.ds(0, rows), :], ybuf.at[slot, k], sem.at[slot]).wait()
    route = route_ref[...]
    moe = ybuf[slot, 0] * route[:, 2:3] + ybuf[slot, 1] * route[:, 3:4]
    o_ref[...] = _rms(xs_ref[...] + moe, g_ref[...])


def _moe_combine(pos, xs, route, yb, g_final, m_real):
    rows = _tile(m_real, 256)
    grid_spec = pltpu.PrefetchScalarGridSpec(
        num_scalar_prefetch=1,
        grid=(m_real // rows,),
        in_specs=[pl.BlockSpec((rows, D_MODEL), lambda t, p: (t, 0)),
                  pl.BlockSpec((rows, LANES), lambda t, p: (t, 0)),
                  pl.BlockSpec(memory_space=pl.ANY),
                  pl.BlockSpec((1, D_MODEL), lambda t, p: (0, 0))],
        out_specs=pl.BlockSpec((rows, D_MODEL), lambda t, p: (t, 0)),
        scratch_shapes=[pltpu.VMEM((2, TOP_K, rows, D_MODEL), F32), pltpu.SemaphoreType.DMA((2,))],
    )
    return pl.pallas_call(
        _combine_kernel,
        out_shape=jax.ShapeDtypeStruct((m_real, D_MODEL), F32),
        grid_spec=grid_spec,
        compiler_params=_params("arbitrary"),
        name="moe_combine_final_norm",
    )(pos, xs, route, yb, g_final.reshape(1, D_MODEL))


def _moe_routing(route, tile_counts, n_tok):
    n_assign = n_tok * TOP_K
    n_sb = n_assign // MOE_SB + N_EXPERTS
    rows_per_tile = route.shape[0] // tile_counts.shape[0]
    e_flat = route[:n_tok, :TOP_K].astype(jnp.int32).reshape(n_assign)
    onehot = (e_flat[:, None] == jnp.arange(N_EXPERTS, dtype=jnp.int32)[None, :]).astype(jnp.int32)
    tile_counts = tile_counts[:, 0, :N_EXPERTS].astype(jnp.int32)
    tile_base = jnp.cumsum(tile_counts, axis=0) - tile_counts
    base = jnp.repeat(tile_base, rows_per_tile * TOP_K, axis=0)[:n_assign]
    rank = jnp.sum(base * onehot, axis=1) + route[:n_tok, 4:4 + TOP_K].astype(jnp.int32).reshape(n_assign)
    counts = jnp.sum(tile_counts, axis=0)
    sb_count = (counts + MOE_SB - 1) // MOE_SB
    sb_end = jnp.cumsum(sb_count)
    sb_start = sb_end - sb_count
    dest = (jnp.sum((sb_start * MOE_SB)[None, :] * onehot, axis=1) + rank).astype(jnp.int32)
    row_tok = jnp.zeros(((n_sb + 2) * MOE_SB,), jnp.int32).at[dest].set(
        jnp.arange(n_assign, dtype=jnp.int32) // TOP_K, unique_indices=True)
    sb = jnp.arange(n_sb, dtype=jnp.int32)
    sb_e = jnp.sum((sb[:, None] >= sb_end[None, :]).astype(jnp.int32), axis=1)
    valid = sb < sb_end[-1]
    last_e = jnp.sum((sb_end[-1] - 1 >= sb_end).astype(jnp.int32))
    sb_e = jnp.where(valid, sb_e, last_e).astype(jnp.int32)
    rows_here = counts[sb_e] - (sb - sb_start[sb_e]) * MOE_SB
    parts = (jnp.minimum(rows_here, MOE_SB) + MOE_PART_ROWS - 1) // MOE_PART_ROWS
    sb_rows = jnp.where(valid, parts, 0).astype(jnp.int32)
    return dest, row_tok, sb_e, sb_rows, n_sb


def _split_w_in(w):
    n_gates = 2 * ML_HEADS
    assert w.shape[1] == ZA_WIDTH + n_gates + ZB_WIDTH
    wt = jnp.swapaxes(w, 0, 1)
    part_a = wt[:W_IN_GATES].astype(BF16)
    part_b = wt[W_IN_GATES + n_gates:].astype(BF16)
    gates = jnp.pad(wt[W_IN_GATES:W_IN_GATES + n_gates], ((0, LANES - n_gates), (0, 0))).astype(BF16)
    return part_a, part_b, gates


def kernel(x, meta_tokens, rel_bias_table, w_in, attn_sinks, conv_w, conv_b, igate_b, fgate_b, mlstm_norm_g,
           w_attn_up, w_mlstm_up, w_out, norm_mix_g, norm_ffn_g, w_ffn_gate, w_ffn_up, w_ffn_down, w_router,
           b_router, w_moe_gate, w_moe_up, w_moe_down, final_norm_g):
    b, seq, _ = x.shape
    depth = w_in.shape[0]
    m_real = b * seq
    m = m_real + TAIL_ROWS
    assert depth == 2 and seq % ML_CHUNK == 0 and seq % BLOCK == 0 and m_real % TAIL_ROWS == 0
    tail = jnp.concatenate([jnp.zeros((PAD, D_MODEL), x.dtype), meta_tokens.astype(x.dtype),
                            jnp.zeros((TAIL_ROWS - PREFIX, D_MODEL), x.dtype)], axis=0)
    xs = (x.reshape(m_real, D_MODEL), tail)
    band, meta = _attn_bias(rel_bias_table, seq // BLOCK + 1)
    h = _rmsnorm(*xs, norm_mix_g[0])
    out = None
    for layer in range(depth):
        w_a, w_b, w_gates = _split_w_in(w_in[layer])
        za = _matmul_nt(h, w_a, BF16, 1024, ZA_WIDTH // 2, "in_proj_a")
        zb = _matmul_nt(h, w_b, BF16, 1024, ZB_WIDTH // 2, "in_proj_b")
        attn = _attention(za, band, meta, attn_sinks[layer], b, seq)
        ml = _mlstm(za, zb, h, w_gates, conv_w[layer], conv_b[layer], igate_b[layer], fgate_b[layer],
                    mlstm_norm_g[layer], b, seq)
        wa, wm, wo = (w_attn_up[layer].astype(BF16), w_mlstm_up[layer].astype(BF16), w_out[layer].astype(BF16))
        i = layer // 2
        if layer % 2 == 0:
            xs, h = _merge(attn, ml, zb, xs, wa, wm, wo, norm_ffn_g[layer])
            xs, h = _dense_ffn(h, xs, w_ffn_gate[i].astype(BF16), w_ffn_up[i].astype(BF16),
                               w_ffn_down[i].astype(BF16), norm_mix_g[layer + 1])
        else:
            n_tok = m_real + PREFIX
            xs, h_packed, route, tile_counts = _merge(attn, ml, zb, xs, wa, wm, wo, norm_ffn_g[layer],
                                                      router=(w_router[i], b_router[i], n_tok))
            dest, row_tok, sb_e, sb_rows, n_sb = _moe_routing(route, tile_counts, n_tok)
            yb = _moe_experts(h_packed, sb_e, sb_rows, row_tok, w_moe_gate[i], w_moe_up[i], w_moe_down[i], n_sb)
            out = _moe_combine(dest, xs, route, yb, final_norm_g, m_real).reshape(b, seq, D_MODEL)
    return out
```

```python
import functools
import math

import jax
import jax.numpy as jnp
from jax import lax
from jax.experimental import pallas as pl
from jax.experimental.pallas import tpu as pltpu

D_MODEL = 2048
N_META = 16
BLOCK = 128
PREFIX = BLOCK
PAD = PREFIX - N_META
HEAD_DIM = 64
N_Q_HEADS = 16
N_KV_HEADS = 4
GQA_GROUP = 4
ATTN_WIDTH = N_Q_HEADS * HEAD_DIM
KV_WIDTH = N_KV_HEADS * HEAD_DIM
WINDOW = 128
NUM_BUCKETS = 32
MAX_DISTANCE = 128
ML_HEADS = 4
ML_V_WIDTH = D_MODEL // 2
ML_V_DIM = ML_V_WIDTH // ML_HEADS
ML_QK_DIM = ML_V_DIM // 2
ML_QK_WIDTH = ML_HEADS * ML_QK_DIM
CONV_WIDTH = 4
D_FF = 11 * D_MODEL // 4
N_EXPERTS = 8
TOP_K = 2
EPS = 1e-6

LANES = 128
SUBLANES = 8
BF16_ROWS = 16
VMEM_LIMIT = 56 * 1024 * 1024

ZA_AQ, ZA_AK, ZA_AV, ZA_MQ, ZA_MK, ZA_MV, ZA_WIDTH = 0, 1024, 1280, 1536, 2048, 2560, 3584
ZB_MO, ZB_GA, ZB_GM, ZB_WIDTH = 0, 1024, 3072, 5120
W_IN_GATES = ZA_WIDTH
HALF_D = D_MODEL // 2
TAIL_ROWS = 512
ML_CHUNK = 128
CONV_HALO = 16
FFN_UP_TILES = 4
FFN_DOWN_TILES = 4
MOE_SB = 1024
MOE_PARTS = 4
MOE_PART_ROWS = MOE_SB // MOE_PARTS
MOE_FF_TILE = 512
MOE_STEP_ROWS = 96
MOE_GATHER_ROWS = MOE_STEP_ROWS * (D_FF // MOE_FF_TILE)
MOE_VMEM_LIMIT = 60 * 1024 * 1024

F32 = jnp.float32
BF16 = jnp.bfloat16
NEG_INF = float("-inf")


def _tile(m, target):
    best = LANES
    for t in range(LANES, min(m, target) + 1, LANES):
        if m % t == 0:
            best = t
    assert m % best == 0
    return best


def _params(*sem):
    return pltpu.CompilerParams(dimension_semantics=sem, vmem_limit_bytes=VMEM_LIMIT)


def _rms(x, g):
    return x * lax.rsqrt(jnp.mean(x * x, axis=-1, keepdims=True) + EPS) * g


def _sigmoid(x):
    return 1.0 / (1.0 + jnp.exp(-x))


def _dot(a, b):
    return jnp.dot(a, b, preferred_element_type=F32)


def _dot_nt(a, b):
    return lax.dot_general(a, b, (((1,), (1,)), ((), ())), preferred_element_type=F32)


def _dot_tn(a, b):
    return lax.dot_general(a, b, (((0,), (0,)), ((), ())), preferred_element_type=F32)


def _split_rows(i, n_body_tiles, body_ref, tail_ref):
    return jnp.where(i < n_body_tiles, body_ref[...], tail_ref[...])


def _split_specs(bm, n_body_tiles):
    return [pl.BlockSpec((bm, D_MODEL), lambda i: (jnp.minimum(i, n_body_tiles - 1), 0)),
            pl.BlockSpec((bm, D_MODEL), lambda i: (jnp.maximum(i - n_body_tiles, 0), 0))]


def _norm_kernel(n_body_tiles, x_ref, tail_ref, g_ref, o_ref):
    x = _split_rows(pl.program_id(0), n_body_tiles, x_ref, tail_ref)
    o_ref[...] = _rms(x, g_ref[...]).astype(o_ref.dtype)


def _rmsnorm(x_body, x_tail, g):
    m = x_body.shape[0] + x_tail.shape[0]
    bm = _tile(x_tail.shape[0], 512)
    assert x_body.shape[0] % bm == 0
    n_body_tiles = x_body.shape[0] // bm
    return pl.pallas_call(
        functools.partial(_norm_kernel, n_body_tiles),
        out_shape=jax.ShapeDtypeStruct((m, D_MODEL), BF16),
        grid=(m // bm,),
        in_specs=_split_specs(bm, n_body_tiles) + [pl.BlockSpec((1, D_MODEL), lambda i: (0, 0))],
        out_specs=pl.BlockSpec((bm, D_MODEL), lambda i: (i, 0)),
        compiler_params=_params("parallel"),
        name="rmsnorm",
    )(x_body, x_tail, g.reshape(1, D_MODEL))


def _mm_nt_kernel(a_ref, wt_ref, o_ref):
    o_ref[...] = _dot_nt(a_ref[...], wt_ref[...]).astype(o_ref.dtype)


def _matmul_nt(a, wt, out_dtype, bm, bn, name):
    m, k = a.shape
    n = wt.shape[0]
    bm = _tile(m, bm)
    return pl.pallas_call(
        _mm_nt_kernel,
        out_shape=jax.ShapeDtypeStruct((m, n), out_dtype),
        grid=(n // bn, m // bm),
        in_specs=[pl.BlockSpec((bm, k), lambda j, i: (i, 0)),
                  pl.BlockSpec((bn, k), lambda j, i: (j, 0))],
        out_specs=pl.BlockSpec((bm, bn), lambda j, i: (i, j)),
        compiler_params=_params("parallel", "parallel"),
        name=name,
    )(a, wt)


def _t5_bucket(rel):
    n = jnp.maximum(rel, 0)
    max_exact = NUM_BUCKETS // 2
    large = max_exact + (jnp.log(jnp.maximum(n, 1).astype(F32) / max_exact)
                         / math.log(MAX_DISTANCE / max_exact) * (NUM_BUCKETS - max_exact)).astype(jnp.int32)
    large = jnp.minimum(large, NUM_BUCKETS - 1)
    return jnp.where(n < max_exact, n, large)


def _bias_lookup(table, rel):
    onehot = (_t5_bucket(rel)[..., None] == jnp.arange(NUM_BUCKETS)).astype(F32)
    return jnp.einsum("...b,bh->h...", onehot, table.astype(F32), precision=lax.Precision.HIGHEST)


ATTN_GROUP_ORDER = (0, 2, 1, 3)


def _stack_group_rows(a):
    lead = a.shape[:-3]
    a = a.reshape(*lead, N_KV_HEADS, GQA_GROUP, BLOCK, a.shape[-1])
    a = jnp.take(a, jnp.array(ATTN_GROUP_ORDER), axis=len(lead) + 1)
    return a.reshape(*lead, N_KV_HEADS, GQA_GROUP * BLOCK, a.shape[-1])


def _attn_bias(table, nb):
    qi = jnp.arange(BLOCK)[:, None]
    ki = jnp.arange(2 * BLOCK)[None, :]
    rel_band = qi + BLOCK - ki
    blk3 = jnp.arange(3)[:, None, None]
    mask_band = (rel_band >= 0) & (rel_band < WINDOW) & ((blk3 - 1) * BLOCK + ki >= PAD)
    band = jnp.where(mask_band[:, None], _bias_lookup(table, rel_band)[None], NEG_INF)
    blk = jnp.arange(nb)[:, None, None]
    rel_meta = blk * BLOCK + qi[None] - (PAD + jnp.arange(N_META))
    meta = jnp.where((rel_meta >= WINDOW)[None], _bias_lookup(table, rel_meta), NEG_INF)
    meta = jnp.pad(jnp.moveaxis(meta, 1, 0), ((0, 0), (0, 0), (0, 0), (PAD, 0)), constant_values=NEG_INF)
    return _stack_group_rows(band), _stack_group_rows(meta)


def _swap_halves(x):
    return pltpu.roll(x.astype(F32), HEAD_DIM, axis=1).astype(x.dtype)


def _attn_kernel(q_ref, kp_ref, kc_ref, km_ref, vp_ref, vc_ref, vm_ref, bb_ref, bm_ref, sink_ref, o_ref):
    scale = HEAD_DIM ** -0.5
    assert math.frexp(scale)[0] == 0.5 and 2 * HEAD_DIM == LANES
    lane = lax.broadcasted_iota(jnp.int32, (1, LANES), 1)
    keep = (jnp.where(lane < HEAD_DIM, scale, 0.0).astype(BF16),
            jnp.where(lane < HEAD_DIM, 0.0, scale).astype(BF16))
    low = lax.broadcasted_iota(jnp.int32, (2 * BLOCK, LANES), 1) < HEAD_DIM
    chains = []
    for col in range(N_KV_HEADS // 2):
        kcols = slice(col * LANES, (col + 1) * LANES)
        k_nat = jnp.concatenate([kp_ref[:, kcols], kc_ref[:, kcols], km_ref[:, kcols]], axis=0)
        v_nat = jnp.concatenate([vp_ref[:, kcols], vc_ref[:, kcols], vm_ref[:, kcols]], axis=0)
        keys = (k_nat, _swap_halves(k_nat))
        vals = (v_nat, _swap_halves(v_nat))
        for half in range(2):
            h = 2 * col + half
            q0 = h * GQA_GROUP * HEAD_DIM
            q2 = jnp.concatenate([q_ref[:, q0:q0 + LANES], q_ref[:, q0 + LANES:q0 + 2 * LANES]], axis=0)
            for lane_half in range(2):
                which = 0 if lane_half == half else 1
                chains.append((h, lane_half, _dot_nt(q2 * keep[lane_half], keys[which]), vals[which]))
    probs = []
    for h, lane_half, s, _ in chains:
        rows = slice(lane_half * 2 * BLOCK, (lane_half + 1) * 2 * BLOCK)
        s0 = s[:, :BLOCK] + bb_ref[0, h, rows, :BLOCK]
        s1 = s[:, BLOCK:2 * BLOCK] + bb_ref[0, h, rows, BLOCK:]
        s2 = s[:, 2 * BLOCK:] + bm_ref[0, h, rows, :]
        sink = sink_ref[h, rows, :]
        mx = jnp.maximum(jnp.maximum(jnp.maximum(s0, s1), s2).max(-1, keepdims=True), sink)
        p0 = jnp.exp(s0 - mx)
        p1 = jnp.exp(s1 - mx)
        p2 = jnp.exp(s2 - mx)
        den = (p0 + p1 + p2).sum(-1, keepdims=True) + jnp.exp(sink - mx)
        probs.append((jnp.concatenate([p0, p1, p2], axis=1).astype(BF16), 1.0 / den))
    outs = [_dot(p, chain[3]) * rden for (p, rden), chain in zip(probs, chains)]
    out_cols = []
    for h in range(N_KV_HEADS):
        o = jnp.where(low, outs[2 * h], outs[2 * h + 1]).astype(o_ref.dtype)
        out_cols += [o[:BLOCK], o[BLOCK:]]
    o_ref[...] = jnp.concatenate(out_cols, axis=1)


def _attention(z, band, meta, sinks, b, seq):
    m = z.shape[0]
    per_seq = seq // BLOCK
    n_real = b * per_seq
    kcol, vcol = ZA_AK // KV_WIDTH, ZA_AV // KV_WIDTH
    sink_col = _stack_group_rows(jnp.broadcast_to(sinks.astype(F32)[:, None, None], (N_Q_HEADS, BLOCK, LANES)))
    kv = lambda col, f: pl.BlockSpec((BLOCK, KV_WIDTH), lambda s: (f(s), col))
    cur = lambda s: s
    prev = lambda s: jnp.where(s < n_real, jnp.where(s % per_seq == 0, n_real, s - 1), s)
    first = lambda s: n_real
    query_block = lambda s: jnp.where(s < n_real, s % per_seq + 1, 0)
    return pl.pallas_call(
        _attn_kernel,
        out_shape=jax.ShapeDtypeStruct((m, ATTN_WIDTH), BF16),
        grid=(m // BLOCK,),
        in_specs=[pl.BlockSpec((BLOCK, ATTN_WIDTH), lambda s: (s, ZA_AQ // ATTN_WIDTH)),
                  kv(kcol, prev), kv(kcol, cur), kv(kcol, first),
                  kv(vcol, prev), kv(vcol, cur), kv(vcol, first),
                  pl.BlockSpec((1, N_KV_HEADS, GQA_GROUP * BLOCK, 2 * BLOCK),
                               lambda s: (jnp.minimum(query_block(s), 2), 0, 0, 0)),
                  pl.BlockSpec((1, N_KV_HEADS, GQA_GROUP * BLOCK, BLOCK), lambda s: (query_block(s), 0, 0, 0)),
                  pl.BlockSpec((N_KV_HEADS, GQA_GROUP * BLOCK, LANES), lambda s: (0, 0, 0))],
        out_specs=pl.BlockSpec((BLOCK, ATTN_WIDTH), lambda s: (s, 0)),
        compiler_params=_params("parallel"),
        name="swa_attention",
    )(z, z, z, z, z, z, z, band, meta, sink_col)


def _mlstm_chunk(qk_in, v_in, gate_in, mo_in, prefix_chunk, cw_ref, cb_ref, gb_ref, ng_ref, xa_ref, ct_ref, n_ref, m_ref):
    L = ML_CHUNK
    ii = lax.broadcasted_iota(jnp.int32, (L, L), 0)
    jj = lax.broadcasted_iota(jnp.int32, (L, L), 1)
    causal = jj <= ii
    lane = lax.broadcasted_iota(jnp.int32, (L, LANES), 1)
    k_scale = ML_QK_DIM ** -0.5
    xa_ref[CONV_HALO:CONV_HALO + L, :] = qk_in
    window = xa_ref[...]
    sel_row = lax.broadcasted_iota(jnp.int32, (L, CONV_HALO + L), 0)
    sel_col = lax.broadcasted_iota(jnp.int32, (L, CONV_HALO + L), 1)
    acc = cb_ref[...] + cw_ref[CONV_WIDTH - 1:CONV_WIDTH, :] * qk_in.astype(F32)
    for j in range(CONV_WIDTH - 1):
        back = CONV_WIDTH - 1 - j
        shift = jnp.where(sel_col == sel_row + (CONV_HALO - back), 1.0, 0.0).astype(BF16)
        acc = acc + cw_ref[j:j + 1, :] * _dot(shift, window)
    qk = acc * _sigmoid(acc)
    xa_ref[0:CONV_HALO, :] = xa_ref[L:L + CONV_HALO, :]
    gpb = gate_in + gb_ref[...]
    log_sig = jnp.minimum(gpb, 0.0) - jnp.log1p(jnp.exp(-jnp.abs(gpb)))
    if prefix_chunk:
        valid = lax.broadcasted_iota(jnp.int32, (L, LANES), 0) >= PAD
        gx = jnp.where(lane < ML_HEADS, jnp.where(valid, gpb, NEG_INF), jnp.where(valid, log_sig, 0.0))
    else:
        gx = jnp.where(lane < ML_HEADS, gpb, log_sig)
    gxt = gx.T
    outs = []
    for h in range(ML_HEADS):
        q = qk[:, h * ML_QK_DIM:(h + 1) * ML_QK_DIM].astype(BF16)
        k = qk[:, ML_QK_WIDTH + h * ML_QK_DIM:ML_QK_WIDTH + (h + 1) * ML_QK_DIM] * k_scale
        v = v_in[:, h * ML_V_DIM:(h + 1) * ML_V_DIM]
        ig_col = gx[:, h:h + 1]
        lf_col = gx[:, ML_HEADS + h:ML_HEADS + h + 1]
        ig_row = gxt[h:h + 1, :]
        lf_row = gxt[ML_HEADS + h:ML_HEADS + h + 1, :]
        b_col = jnp.sum(jnp.where(causal, lf_row, 0.0), axis=-1, keepdims=True)
        b_row = jnp.sum(jnp.where(ii <= jj, lf_col, 0.0), axis=0, keepdims=True)
        m_prev = m_ref[h:h + 1, 0:1]
        log_d = jnp.where(causal, b_col - b_row + ig_row, NEG_INF)
        m_inter = b_col + m_prev
        m_out = jnp.maximum(m_inter, log_d.max(-1, keepdims=True))
        d = jnp.exp(log_d - m_out)
        inter = jnp.exp(m_inter - m_out)
        ct = ct_ref[h]
        n_prev = n_ref[h:h + 1, :]
        qk_scores = _dot_nt(q, k.astype(BF16))
        q_state = _dot(q, ct.astype(BF16))
        q_norm = jnp.sum(q.astype(F32) * n_prev, axis=-1, keepdims=True)
        b_last = b_col[L - 1:L, :]
        log_w = b_last - b_col + ig_col
        m_new = jnp.maximum(b_last + m_prev, log_w.max(0, keepdims=True))
        decay = jnp.exp(b_last + m_prev - m_new)
        kw = k * jnp.exp(log_w - m_new)
        s = qk_scores * d
        vcols = slice(h * ML_V_DIM, (h + 1) * ML_V_DIM)
        num = _dot(s.astype(BF16), v) + inter * q_state
        den = s.sum(-1, keepdims=True) + inter * q_norm
        hh = num / jnp.maximum(jnp.abs(den), jnp.exp(-m_out))
        ct_ref[h] = decay * ct + _dot_tn(kw.astype(BF16), v)
        n_ref[h:h + 1, :] = decay * n_prev + kw.sum(0, keepdims=True)
        m_ref[h:h + 1, :] = jnp.broadcast_to(m_new, (1, LANES))
        hn = hh * lax.rsqrt(jnp.mean(hh * hh, axis=-1, keepdims=True) + EPS) * ng_ref[:, vcols]
        outs.append(_sigmoid(mo_in[:, vcols].astype(F32)) * hn)
    return jnp.concatenate(outs, axis=1)


def _mlstm_kernel(blocks_per_seq, q_ref, k_ref, va_ref, vb_ref, h_ref, mo_ref, wgate_ref, cw_ref, cb_ref, gb_ref, ng_ref,
                  o_ref, gate_ref, xa_ref, ct_ref, n_ref, m_ref, xa0_ref, ct0_ref, n0_ref, m0_ref):
    L = ML_CHUNK
    step = pl.program_id(0)
    params = (cw_ref, cb_ref, gb_ref, ng_ref, xa_ref, ct_ref, n_ref, m_ref)
    gate_ref[...] = _dot_nt(h_ref[...], wgate_ref[...])

    def chunk_out(rows, prefix_chunk):
        qk = jnp.concatenate([q_ref[rows, :], k_ref[rows, :]], axis=1)
        v = jnp.concatenate([va_ref[rows, :], vb_ref[rows, :]], axis=1)
        return _mlstm_chunk(qk, v, gate_ref[rows, :], mo_ref[rows, :], prefix_chunk, *params).astype(o_ref.dtype)

    @pl.when(step == 0)
    def _():
        xa_ref[0:CONV_HALO, :] = jnp.zeros((CONV_HALO, 2 * ML_QK_WIDTH), xa_ref.dtype)
        ct_ref[...] = jnp.zeros_like(ct_ref)
        n_ref[...] = jnp.zeros_like(n_ref)
        m_ref[...] = jnp.zeros_like(m_ref)
        o_ref[0:L, :] = chunk_out(slice(0, L), True)
        o_ref[L:, :] = jnp.zeros((o_ref.shape[0] - L, o_ref.shape[1]), o_ref.dtype)
        xa0_ref[...] = xa_ref[0:CONV_HALO, :]
        ct0_ref[...] = ct_ref[...]
        n0_ref[...] = n_ref[...]
        m0_ref[...] = m_ref[...]

    @pl.when(step > 0)
    def _():
        @pl.when((step - 1) % blocks_per_seq == 0)
        def _():
            xa_ref[0:CONV_HALO, :] = xa0_ref[...]
            ct_ref[...] = ct0_ref[...]
            n_ref[...] = n0_ref[...]
            m_ref[...] = m0_ref[...]

        def chunk(c, carry):
            rows = pl.ds(pl.multiple_of(c * L, L), L)
            o_ref[rows, :] = chunk_out(rows, False)
            return carry

        lax.fori_loop(0, TAIL_ROWS // L, chunk, 0)


def _mlstm(za, zb, h, w_gates, conv_w, conv_b, igate_b, fgate_b, norm_g, b, seq):
    m = za.shape[0]
    assert seq % TAIL_ROWS == 0 and TAIL_ROWS % ML_CHUNK == 0
    n_seq_blocks = b * seq // TAIL_ROWS
    gate_bias = jnp.zeros((1, LANES), F32).at[0, :ML_HEADS].set(igate_b).at[0, ML_HEADS:2 * ML_HEADS].set(fgate_b)
    row_blk = lambda s: jnp.where(s == 0, n_seq_blocks, s - 1)
    rows = lambda w, c: pl.BlockSpec((TAIL_ROWS, w), lambda s: (row_blk(s), c))
    full = lambda r, c: pl.BlockSpec((r, c), lambda s: (0, 0))
    return pl.pallas_call(
        functools.partial(_mlstm_kernel, seq // TAIL_ROWS),
        out_shape=jax.ShapeDtypeStruct((m, ML_V_WIDTH), BF16),
        grid=(n_seq_blocks + 1,),
        in_specs=[rows(ML_QK_WIDTH, ZA_MQ // ML_QK_WIDTH), rows(ML_QK_WIDTH, ZA_MK // ML_QK_WIDTH),
                  rows(ML_V_WIDTH // 2, ZA_MV // (ML_V_WIDTH // 2)), rows(ML_V_WIDTH // 2, ZA_MV // (ML_V_WIDTH // 2) + 1),
                  rows(D_MODEL, 0), rows(ML_V_WIDTH, ZB_MO // ML_V_WIDTH), full(LANES, D_MODEL),
                  full(CONV_WIDTH, 2 * ML_QK_WIDTH), full(1, 2 * ML_QK_WIDTH), full(1, LANES), full(1, ML_V_WIDTH)],
        out_specs=rows(ML_V_WIDTH, 0),
        scratch_shapes=[pltpu.VMEM((TAIL_ROWS, LANES), F32),
                        pltpu.VMEM((ML_CHUNK + CONV_HALO, 2 * ML_QK_WIDTH), BF16),
                        pltpu.VMEM((ML_HEADS, ML_QK_DIM, ML_V_DIM), F32),
                        pltpu.VMEM((SUBLANES, LANES), F32),
                        pltpu.VMEM((SUBLANES, LANES), F32),
                        pltpu.VMEM((CONV_HALO, 2 * ML_QK_WIDTH), BF16),
                        pltpu.VMEM((ML_HEADS, ML_QK_DIM, ML_V_DIM), F32),
                        pltpu.VMEM((SUBLANES, LANES), F32),
                        pltpu.VMEM((SUBLANES, LANES), F32)],
        compiler_params=_params("arbitrary"),
        name="mlstm",
    )(za, za, za, za, h, zb, w_gates, conv_w.astype(F32), conv_b.reshape(1, -1).astype(F32), gate_bias,
      norm_g.reshape(1, -1).astype(F32))


def _merge_kernel(n_routed, n_body_tiles, attn_ref, ml_ref, ga0_ref, ga1_ref, gm0_ref, gm1_ref, *rest):
    with_router = n_routed is not None
    if n_body_tiles is None:
        xs_ref, wa_ref, wm_ref, wo_ref, g_ref, *rest = rest
        residual = xs_ref[...]
    else:
        xs_ref, tail_ref, wa_ref, wm_ref, wo_ref, g_ref, *rest = rest
        residual = _split_rows(pl.program_id(0), n_body_tiles, xs_ref, tail_ref)
    if with_router:
        wr_hi_ref, wr_lo_ref, br_ref, xs_out, h_out, route_out, counts_out = rest
    else:
        xs_out, h_out = rest
    a = _dot(attn_ref[...], wa_ref[...])
    m = _dot(ml_ref[...], wm_ref[...])
    ga = jnp.concatenate([ga0_ref[...], ga1_ref[...]], axis=1)
    gm = jnp.concatenate([gm0_ref[...], gm1_ref[...]], axis=1)
    y = _sigmoid(ga.astype(F32)) * a + _sigmoid(gm.astype(F32)) * m
    xs_new = residual + _dot(y.astype(BF16), wo_ref[...])
    xs_out[...] = xs_new
    hn = _rms(xs_new, g_ref[...])
    if not with_router:
        h_out[...] = hn.astype(h_out.dtype)
    else:
        half = D_MODEL // 2
        lo = lax.bitcast_convert_type(hn[:, :half].astype(BF16).astype(F32), jnp.uint32) >> 16
        hi = lax.bitcast_convert_type(hn[:, half:].astype(BF16).astype(F32), jnp.uint32) & jnp.uint32(0xFFFF0000)
        h_out[...] = lo | hi
        hn_hi = hn.astype(BF16)
        hn_lo = (hn - hn_hi.astype(F32)).astype(BF16)
        logits = (_dot(hn_hi, wr_hi_ref[...]) + (_dot(hn_lo, wr_hi_ref[...]) + _dot(hn_hi, wr_lo_ref[...]))
                  + br_ref[...])
        lane = lax.broadcasted_iota(jnp.int32, logits.shape, 1).astype(F32)
        l1 = logits.max(-1, keepdims=True)
        i1 = jnp.min(jnp.where(logits == l1, lane, float(LANES)), axis=-1, keepdims=True)
        rest_logits = jnp.where(lane == i1, NEG_INF, logits)
        l2 = rest_logits.max(-1, keepdims=True)
        i2 = jnp.min(jnp.where(rest_logits == l2, lane, float(LANES)), axis=-1, keepdims=True)
        e = jnp.exp(l2 - l1)
        w1 = 1.0 / (1.0 + e)
        w2 = e / (1.0 + e)
        bm = logits.shape[0]
        row = lax.broadcasted_iota(jnp.int32, (bm, 1), 0) + pl.program_id(0) * bm
        routed = row < n_routed
        pick1 = jnp.where(jnp.logical_and(lane == i1, routed), 1.0, 0.0)
        pick2 = jnp.where(jnp.logical_and(lane == i2, routed), 1.0, 0.0)
        picks = pick1 + pick2
        earlier = (lax.broadcasted_iota(jnp.int32, (bm, bm), 1) < lax.broadcasted_iota(jnp.int32, (bm, bm), 0))
        before = _dot(jnp.where(earlier, 1.0, 0.0).astype(BF16), picks.astype(BF16))
        r1 = jnp.sum(before * pick1, axis=-1, keepdims=True)
        r2 = jnp.sum(before * pick2, axis=-1, keepdims=True)
        route_out[...] = jnp.where(lane == 0, i1, jnp.where(lane == 1, i2, jnp.where(lane == 2, w1,
                                   jnp.where(lane == 3, w2, jnp.where(lane == 4, r1, jnp.where(lane == 5, r2, 0.0))))))
        counts_out[0] = jnp.broadcast_to(jnp.sum(picks, axis=0, keepdims=True), (SUBLANES, LANES))


def _merge(attn, ml, zb, xs, wa, wm, wo, g_next, router=None):
    m = zb.shape[0]
    bm = _tile(TAIL_ROWS, 256)
    row = lambda w, c: pl.BlockSpec((bm, w), lambda i: (i, c))
    const = lambda r, c: pl.BlockSpec((r, c), lambda i: (0, 0), pipeline_mode=pl.Buffered(1))
    if isinstance(xs, tuple):
        n_body_tiles = xs[0].shape[0] // bm
        xs_specs, xs_args = _split_specs(bm, n_body_tiles), list(xs)
    else:
        n_body_tiles = None
        xs_specs, xs_args = [row(D_MODEL, 0)], [xs]
    in_specs = [row(ATTN_WIDTH, 0), row(ML_V_WIDTH, 0),
                row(HALF_D, ZB_GA // HALF_D), row(HALF_D, ZB_GA // HALF_D + 1),
                row(HALF_D, ZB_GM // HALF_D), row(HALF_D, ZB_GM // HALF_D + 1),
                *xs_specs, const(ATTN_WIDTH, D_MODEL), const(ML_V_WIDTH, D_MODEL), const(D_MODEL, D_MODEL),
                const(1, D_MODEL)]
    args = [attn, ml, zb, zb, zb, zb, *xs_args, wa, wm, wo, g_next.reshape(1, D_MODEL)]
    if router is None:
        out_shape = [jax.ShapeDtypeStruct((m, D_MODEL), F32), jax.ShapeDtypeStruct((m, D_MODEL), BF16)]
        out_specs = [row(D_MODEL, 0), row(D_MODEL, 0)]
    else:
        out_shape = [jax.ShapeDtypeStruct((m, D_MODEL), F32), jax.ShapeDtypeStruct((m, D_MODEL // 2), jnp.uint32)]
        out_specs = [row(D_MODEL, 0), row(D_MODEL // 2, 0)]
    n_routed = None
    if router is not None:
        w_router, b_router, n_routed = router
        wr = jnp.zeros((D_MODEL, LANES), F32).at[:, :N_EXPERTS].set(w_router.astype(F32))
        br = jnp.full((1, LANES), NEG_INF, F32).at[0, :N_EXPERTS].set(b_router.astype(F32))
        wr_hi = wr.astype(BF16)
        wr_lo = (wr - wr_hi.astype(F32)).astype(BF16)
        in_specs += [const(D_MODEL, LANES), const(D_MODEL, LANES), const(1, LANES)]
        args += [wr_hi, wr_lo, br]
        out_shape += [jax.ShapeDtypeStruct((m, LANES), F32), jax.ShapeDtypeStruct((m // bm, SUBLANES, LANES), F32)]
        out_specs += [row(LANES, 0), pl.BlockSpec((1, SUBLANES, LANES), lambda i: (i, 0, 0))]
    return pl.pallas_call(
        functools.partial(_merge_kernel, n_routed, n_body_tiles),
        out_shape=out_shape,
        grid=(m // bm,),
        in_specs=in_specs,
        out_specs=out_specs,
        compiler_params=_params("parallel"),
        name="merge_out_proj",
    )(*args)


def _ffn_up_kernel(h_ref, wg_ref, wu_ref, act_ref):
    hb = h_ref[...]
    g = _dot(hb, wg_ref[...])
    u = _dot(hb, wu_ref[...])
    act_ref[...] = (g * _sigmoid(g) * u).astype(act_ref.dtype)


def _ffn_down_kernel(act_ref, xs_ref, wd_ref, g_ref, xs_out, h_out, rows_ref):
    j = pl.program_id(1)
    xs_new = xs_ref[...] + _dot(act_ref[...], wd_ref[...])
    xs_out[...] = xs_new
    rows_ref[j] = xs_new

    @pl.when(j == pl.num_programs(1) - 1)
    def _():
        full = jnp.concatenate([rows_ref[t] for t in range(rows_ref.shape[0])], axis=1)
        h_out[...] = _rms(full, g_ref[...]).astype(h_out.dtype)


def _dense_ffn(h, xs, wg, wu, wd, g_next):
    m = xs.shape[0]
    bm = _tile(m, 1024)
    bf = D_FF // FFN_UP_TILES
    act = pl.pallas_call(
        _ffn_up_kernel,
        out_shape=jax.ShapeDtypeStruct((m, D_FF), BF16),
        grid=(FFN_UP_TILES, m // bm),
        in_specs=[pl.BlockSpec((bm, D_MODEL), lambda f, i: (i, 0)),
                  pl.BlockSpec((D_MODEL, bf), lambda f, i: (0, f)),
                  pl.BlockSpec((D_MODEL, bf), lambda f, i: (0, f))],
        out_specs=pl.BlockSpec((bm, bf), lambda f, i: (i, f)),
        compiler_params=_params("parallel", "parallel"),
        name="dense_swiglu_up",
    )(h, wg, wu)
    bn = D_MODEL // FFN_DOWN_TILES
    return pl.pallas_call(
        _ffn_down_kernel,
        out_shape=[jax.ShapeDtypeStruct((m, D_MODEL), F32), jax.ShapeDtypeStruct((m, D_MODEL), BF16)],
        grid=(m // bm, FFN_DOWN_TILES),
        in_specs=[pl.BlockSpec((bm, D_FF), lambda i, j: (i, 0)),
                  pl.BlockSpec((bm, bn), lambda i, j: (i, j)),
                  pl.BlockSpec((D_FF, bn), lambda i, j: (0, j)),
                  pl.BlockSpec((1, D_MODEL), lambda i, j: (0, 0))],
        out_specs=[pl.BlockSpec((bm, bn), lambda i, j: (i, j)),
                   pl.BlockSpec((bm, D_MODEL), lambda i, j: (i, 0))],
        scratch_shapes=[pltpu.VMEM((FFN_DOWN_TILES, bm, bn), F32)],
        compiler_params=_params("parallel", "arbitrary"),
        name="dense_swiglu_down",
    )(act, xs, wd, g_next.reshape(1, D_MODEL))


def _row_copy(src_hbm, dst_vmem, sem, src_row, dst_row):
    return pltpu.make_async_copy(src_hbm.at[pl.ds(src_row, 1), :], dst_vmem.at[pl.ds(dst_row, 1), :], sem)


def _moe_gather_copy(h_hbm, gbuf, sem, src_row, dst_row):
    return pltpu.make_async_copy(h_hbm.at[pl.ds(src_row, 1), :], gbuf.at[pl.ds(dst_row, 1), :], sem)


def _moe_gather_wait(h_hbm, gbuf, sem):
    pltpu.make_async_copy(h_hbm.at[pl.ds(0, MOE_GATHER_ROWS), :], gbuf, sem).wait()


def _moe_kernel(sb_e_ref, sb_rows_ref, row_tok_ref, h_hbm, wg_ref, wu_ref, wd_ref, y_ref, gbuf, xb16, sem):
    sb = pl.program_id(0)
    f = pl.program_id(1)
    n_sb = pl.num_programs(0)
    nf = pl.num_programs(1)
    kind = sb_rows_ref[sb]
    half = D_MODEL // 2

    @pl.when(jnp.logical_and(sb == 0, f == 0))
    def _():
        def issue(r, c):
            _moe_gather_copy(h_hbm, gbuf, sem, row_tok_ref[r], r).start()
            return c
        lax.fori_loop(0, MOE_GATHER_ROWS, issue, 0)

    @pl.when(f == 0)
    def _():
        y_ref[...] = jnp.zeros_like(y_ref)
        prev_kind = sb_rows_ref[jnp.maximum(sb - 1, 0)]

        @pl.when(jnp.logical_or(sb == 0, prev_kind > 0))
        def _():
            _moe_gather_wait(h_hbm, gbuf, sem)

        @pl.when(kind > 0)
        def _():
            w = gbuf[0:MOE_SB, :]
            xb16[:, :half] = lax.bitcast_convert_type(w << 16, F32).astype(BF16)
            xb16[:, half:] = lax.bitcast_convert_type(w & jnp.uint32(0xFFFF0000), F32).astype(BF16)

    def compute(row0, rows, prefetch):
        if prefetch:
            base = (sb + 1) * MOE_SB + f * MOE_STEP_ROWS
            for r in range(MOE_STEP_ROWS):
                _moe_gather_copy(h_hbm, gbuf, sem, row_tok_ref[base + r], f * MOE_STEP_ROWS + r).start()
        xb = xb16[row0:row0 + rows, :]
        g = _dot(xb, wg_ref[0].astype(BF16))
        u = _dot(xb, wu_ref[0].astype(BF16))
        act = (g * _sigmoid(g) * u).astype(BF16)
        y_ref[row0:row0 + rows, :] += _dot(act, wd_ref[0].astype(BF16))

    @pl.when(kind == MOE_PARTS)
    def _():
        compute(0, MOE_SB, True)

    for part in range(MOE_PARTS - 1):
        @pl.when(jnp.logical_and(kind > part, kind < MOE_PARTS))
        def _():
            compute(part * MOE_PART_ROWS, MOE_PART_ROWS, part == 0)

    @pl.when(jnp.logical_and(jnp.logical_and(sb == n_sb - 1, f == nf - 1), kind > 0))
    def _():
        _moe_gather_wait(h_hbm, gbuf, sem)


def _moe_experts(h_packed, sb_e, sb_rows, row_tok, wg, wu, wd, n_sb, bf=MOE_FF_TILE):
    nf = D_FF // bf
    assert nf * MOE_STEP_ROWS == MOE_GATHER_ROWS
    ftile = lambda i, f, rows: jnp.where(rows[i] > 0, f, nf - 1)
    grid_spec = pltpu.PrefetchScalarGridSpec(
        num_scalar_prefetch=3,
        grid=(n_sb, nf),
        in_specs=[pl.BlockSpec(memory_space=pl.ANY),
                  pl.BlockSpec((1, D_MODEL, bf), lambda i, f, se, sr, rt: (se[i], 0, ftile(i, f, sr))),
                  pl.BlockSpec((1, D_MODEL, bf), lambda i, f, se, sr, rt: (se[i], 0, ftile(i, f, sr))),
                  pl.BlockSpec((1, bf, D_MODEL), lambda i, f, se, sr, rt: (se[i], ftile(i, f, sr), 0))],
        out_specs=pl.BlockSpec((MOE_SB, D_MODEL), lambda i, f, se, sr, rt: (i, 0)),
        scratch_shapes=[pltpu.VMEM((MOE_GATHER_ROWS, D_MODEL // 2), jnp.uint32),
                        pltpu.VMEM((MOE_SB, D_MODEL), BF16),
                        pltpu.SemaphoreType.DMA(())],
    )
    return pl.pallas_call(
        _moe_kernel,
        out_shape=jax.ShapeDtypeStruct((n_sb * MOE_SB, D_MODEL), F32),
        grid_spec=grid_spec,
        compiler_params=pltpu.CompilerParams(dimension_semantics=("arbitrary", "arbitrary"),
                                             vmem_limit_bytes=MOE_VMEM_LIMIT),
        name="moe_experts",
    )(sb_e, sb_rows, row_tok, h_packed, wg, wu, wd)


def _combine_kernel(pos_ref, xs_ref, route_ref, y_hbm, g_ref, o_ref, ybuf, sem):
    step = pl.program_id(0)
    n_steps = pl.num_programs(0)
    slot = step % 2

    rows = o_ref.shape[0]

    def gather(s, dst_slot):
        base = s * rows

        def issue(r, c):
            for k in range(TOP_K):
                _row_copy(y_hbm, ybuf.at[dst_slot, k], sem.at[dst_slot], pos_ref[TOP_K * (base + r) + k], r).start()
            return c
        lax.fori_loop(0, rows, issue, 0, unroll=4)

    @pl.when(step == 0)
    def _():
        gather(step, slot)

    @pl.when(step + 1 < n_steps)
    def _():
        gather(step + 1, 1 - slot)

    for k in range(TOP_K):
        pltpu.make_async_copy(y_hbm.at[pl.ds(0, rows), :], ybuf.at[slot, k], sem.at[slot]).wait()
    route = route_ref[...]
    moe = ybuf[slot, 0] * route[:, 2:3] + ybuf[slot, 1] * route[:, 3:4]
    o_ref[...] = _rms(xs_ref[...] + moe, g_ref[...])


def _moe_combine(pos, xs, route, yb, g_final, m_real):
    rows = _tile(m_real, 256)
    grid_spec = pltpu.PrefetchScalarGridSpec(
        num_scalar_prefetch=1,
        grid=(m_real // rows,),
        in_specs=[pl.BlockSpec((rows, D_MODEL), lambda t, p: (t, 0)),
                  pl.BlockSpec((rows, LANES), lambda t, p: (t, 0)),
                  pl.BlockSpec(memory_space=pl.ANY),
                  pl.BlockSpec((1, D_MODEL), lambda t, p: (0, 0))],
        out_specs=pl.BlockSpec((rows, D_MODEL), lambda t, p: (t, 0)),
        scratch_shapes=[pltpu.VMEM((2, TOP_K, rows, D_MODEL), F32), pltpu.SemaphoreType.DMA((2,))],
    )
    return pl.pallas_call(
        _combine_kernel,
        out_shape=jax.ShapeDtypeStruct((m_real, D_MODEL), F32),
        grid_spec=grid_spec,
        compiler_params=_params("arbitrary"),
        name="moe_combine_final_norm",
    )(pos, xs, route, yb, g_final.reshape(1, D_MODEL))


def _moe_routing(route, tile_counts, n_tok):
    n_assign = n_tok * TOP_K
    n_sb = n_assign // MOE_SB + N_EXPERTS
    rows_per_tile = route.shape[0] // tile_counts.shape[0]
    e_flat = route[:n_tok, :TOP_K].astype(jnp.int32).reshape(n_assign)
    onehot = (e_flat[:, None] == jnp.arange(N_EXPERTS, dtype=jnp.int32)[None, :]).astype(jnp.int32)
    tile_counts = tile_counts[:, 0, :N_EXPERTS].astype(jnp.int32)
    tile_base = jnp.cumsum(tile_counts, axis=0) - tile_counts
    base = jnp.repeat(tile_base, rows_per_tile * TOP_K, axis=0)[:n_assign]
    rank = jnp.sum(base * onehot, axis=1) + route[:n_tok, 4:4 + TOP_K].astype(jnp.int32).reshape(n_assign)
    counts = jnp.sum(tile_counts, axis=0)
    sb_count = (counts + MOE_SB - 1) // MOE_SB
    sb_end = jnp.cumsum(sb_count)
    sb_start = sb_end - sb_count
    dest = (jnp.sum((sb_start * MOE_SB)[None, :] * onehot, axis=1) + rank).astype(jnp.int32)
    row_tok = jnp.zeros(((n_sb + 2) * MOE_SB,), jnp.int32).at[dest].set(
        jnp.arange(n_assign, dtype=jnp.int32) // TOP_K, unique_indices=True)
    sb = jnp.arange(n_sb, dtype=jnp.int32)
    sb_e = jnp.sum((sb[:, None] >= sb_end[None, :]).astype(jnp.int32), axis=1)
    valid = sb < sb_end[-1]
    last_e = jnp.sum((sb_end[-1] - 1 >= sb_end).astype(jnp.int32))
    sb_e = jnp.where(valid, sb_e, last_e).astype(jnp.int32)
    rows_here = counts[sb_e] - (sb - sb_start[sb_e]) * MOE_SB
    parts = (jnp.minimum(rows_here, MOE_SB) + MOE_PART_ROWS - 1) // MOE_PART_ROWS
    sb_rows = jnp.where(valid, parts, 0).astype(jnp.int32)
    return dest, row_tok, sb_e, sb_rows, n_sb


def _split_w_in(w):
    n_gates = 2 * ML_HEADS
    assert w.shape[1] == ZA_WIDTH + n_gates + ZB_WIDTH
    wt = jnp.swapaxes(w, 0, 1)
    part_a = wt[:W_IN_GATES].astype(BF16)
    part_b = wt[W_IN_GATES + n_gates:].astype(BF16)
    gates = jnp.pad(wt[W_IN_GATES:W_IN_GATES + n_gates], ((0, LANES - n_gates), (0, 0))).astype(BF16)
    return part_a, part_b, gates


def kernel(x, meta_tokens, rel_bias_table, w_in, attn_sinks, conv_w, conv_b, igate_b, fgate_b, mlstm_norm_g,
           w_attn_up, w_mlstm_up, w_out, norm_mix_g, norm_ffn_g, w_ffn_gate, w_ffn_up, w_ffn_down, w_router,
           b_router, w_moe_gate, w_moe_up, w_moe_down, final_norm_g):
    b, seq, _ = x.shape
    depth = w_in.shape[0]
    m_real = b * seq
    m = m_real + TAIL_ROWS
    assert depth == 2 and seq % ML_CHUNK == 0 and seq % BLOCK == 0 and m_real % TAIL_ROWS == 0
    tail = jnp.concatenate([jnp.zeros((PAD, D_MODEL), x.dtype), meta_tokens.astype(x.dtype),
                            jnp.zeros((TAIL_ROWS - PREFIX, D_MODEL), x.dtype)], axis=0)
    xs = (x.reshape(m_real, D_MODEL), tail)
    band, meta = _attn_bias(rel_bias_table, seq // BLOCK + 1)
    h = _rmsnorm(*xs, norm_mix_g[0])
    out = None
    for layer in range(depth):
        w_a, w_b, w_gates = _split_w_in(w_in[layer])
        za = _matmul_nt(h, w_a, BF16, 1024, ZA_WIDTH // 2, "in_proj_a")
        zb = _matmul_nt(h, w_b, BF16, 1024, ZB_WIDTH // 2, "in_proj_b")
        attn = _attention(za, band, meta, attn_sinks[layer], b, seq)
        ml = _mlstm(za, zb, h, w_gates, conv_w[layer], conv_b[layer], igate_b[layer], fgate_b[layer],
                    mlstm_norm_g[layer], b, seq)
        wa, wm, wo = (w_attn_up[layer].astype(BF16), w_mlstm_up[layer].astype(BF16), w_out[layer].astype(BF16))
        i = layer // 2
        if layer % 2 == 0:
            xs, h = _merge(attn, ml, zb, xs, wa, wm, wo, norm_ffn_g[layer])
            xs, h = _dense_ffn(h, xs, w_ffn_gate[i].astype(BF16), w_ffn_up[i].astype(BF16),
                               w_ffn_down[i].astype(BF16), norm_mix_g[layer + 1])
        else:
            n_tok = m_real + PREFIX
            xs, h_packed, route, tile_counts = _merge(attn, ml, zb, xs, wa, wm, wo, norm_ffn_g[layer],
                                                      router=(w_router[i], b_router[i], n_tok))
            dest, row_tok, sb_e, sb_rows, n_sb = _moe_routing(route, tile_counts, n_tok)
            yb = _moe_experts(h_packed, sb_e, sb_rows, row_tok, w_moe_gate[i], w_moe_up[i], w_moe_down[i], n_sb)
            out = _moe_combine(dest, xs, route, yb, final_norm_g, m_real).reshape(b, seq, D_MODEL)
    return out
```

```python
import functools
import math

import jax
import jax.numpy as jnp
from jax import lax
from jax.experimental import pallas as pl
from jax.experimental.pallas import tpu as pltpu

D_MODEL = 2048
N_META = 16
BLOCK = 128
PREFIX = BLOCK
PAD = PREFIX - N_META
HEAD_DIM = 64
N_Q_HEADS = 16
N_KV_HEADS = 4
GQA_GROUP = 4
ATTN_WIDTH = N_Q_HEADS * HEAD_DIM
KV_WIDTH = N_KV_HEADS * HEAD_DIM
WINDOW = 128
NUM_BUCKETS = 32
MAX_DISTANCE = 128
ML_HEADS = 4
ML_V_WIDTH = D_MODEL // 2
ML_V_DIM = ML_V_WIDTH // ML_HEADS
ML_QK_DIM = ML_V_DIM // 2
ML_QK_WIDTH = ML_HEADS * ML_QK_DIM
CONV_WIDTH = 4
D_FF = 11 * D_MODEL // 4
N_EXPERTS = 8
TOP_K = 2
EPS = 1e-6

LANES = 128
SUBLANES = 8
BF16_ROWS = 16
VMEM_LIMIT = 56 * 1024 * 1024

ZA_AQ, ZA_AK, ZA_AV, ZA_MQ, ZA_MK, ZA_MV, ZA_WIDTH = 0, 1024, 1280, 1536, 2048, 2560, 3584
ZB_MO, ZB_GA, ZB_GM, ZB_WIDTH = 0, 1024, 3072, 5120
W_IN_GATES = ZA_WIDTH
HALF_D = D_MODEL // 2
TAIL_ROWS = 512
ATTN_STEP_BLOCKS = 2
ML_CHUNK = 128
CONV_HALO = 16
FFN_UP_TILES = 4
FFN_DOWN_TILES = 4
MOE_SB = 1024
MOE_PARTS = 4
MOE_PART_ROWS = MOE_SB // MOE_PARTS
MOE_FF_TILE = 512
MOE_STEP_ROWS = 96
MOE_GATHER_ROWS = MOE_STEP_ROWS * (D_FF // MOE_FF_TILE)
MOE_VMEM_LIMIT = 60 * 1024 * 1024

F32 = jnp.float32
BF16 = jnp.bfloat16
NEG_INF = float("-inf")


def _tile(m, target):
    best = LANES
    for t in range(LANES, min(m, target) + 1, LANES):
        if m % t == 0:
            best = t
    assert m % best == 0
    return best


def _params(*sem):
    return pltpu.CompilerParams(dimension_semantics=sem, vmem_limit_bytes=VMEM_LIMIT)


def _rms(x, g):
    return x * lax.rsqrt(jnp.mean(x * x, axis=-1, keepdims=True) + EPS) * g


def _sigmoid(x):
    return 1.0 / (1.0 + jnp.exp(-x))


def _dot(a, b):
    return jnp.dot(a, b, preferred_element_type=F32)


def _dot_nt(a, b):
    return lax.dot_general(a, b, (((1,), (1,)), ((), ())), preferred_element_type=F32)


def _dot_tn(a, b):
    return lax.dot_general(a, b, (((0,), (0,)), ((), ())), preferred_element_type=F32)


def _split_rows(i, n_body_tiles, body_ref, tail_ref):
    return jnp.where(i < n_body_tiles, body_ref[...], tail_ref[...])


def _split_specs(bm, n_body_tiles):
    return [pl.BlockSpec((bm, D_MODEL), lambda i: (jnp.minimum(i, n_body_tiles - 1), 0)),
            pl.BlockSpec((bm, D_MODEL), lambda i: (jnp.maximum(i - n_body_tiles, 0), 0))]


def _norm_kernel(n_body_tiles, x_ref, tail_ref, g_ref, o_ref):
    x = _split_rows(pl.program_id(0), n_body_tiles, x_ref, tail_ref)
    o_ref[...] = _rms(x, g_ref[...]).astype(o_ref.dtype)


def _rmsnorm(x_body, x_tail, g):
    m = x_body.shape[0] + x_tail.shape[0]
    bm = _tile(x_tail.shape[0], 512)
    assert x_body.shape[0] % bm == 0
    n_body_tiles = x_body.shape[0] // bm
    return pl.pallas_call(
        functools.partial(_norm_kernel, n_body_tiles),
        out_shape=jax.ShapeDtypeStruct((m, D_MODEL), BF16),
        grid=(m // bm,),
        in_specs=_split_specs(bm, n_body_tiles) + [pl.BlockSpec((1, D_MODEL), lambda i: (0, 0))],
        out_specs=pl.BlockSpec((bm, D_MODEL), lambda i: (i, 0)),
        compiler_params=_params("parallel"),
        name="rmsnorm",
    )(x_body, x_tail, g.reshape(1, D_MODEL))


def _mm_nt_kernel(a_ref, wt_ref, o_ref):
    o_ref[...] = _dot_nt(a_ref[...], wt_ref[...]).astype(o_ref.dtype)


def _matmul_nt(a, wt, out_dtype, bm, bn, name):
    m, k = a.shape
    n = wt.shape[0]
    bm = _tile(m, bm)
    return pl.pallas_call(
        _mm_nt_kernel,
        out_shape=jax.ShapeDtypeStruct((m, n), out_dtype),
        grid=(n // bn, m // bm),
        in_specs=[pl.BlockSpec((bm, k), lambda j, i: (i, 0)),
                  pl.BlockSpec((bn, k), lambda j, i: (j, 0))],
        out_specs=pl.BlockSpec((bm, bn), lambda j, i: (i, j)),
        compiler_params=_params("parallel", "parallel"),
        name=name,
    )(a, wt)


def _t5_bucket(rel):
    n = jnp.maximum(rel, 0)
    max_exact = NUM_BUCKETS // 2
    large = max_exact + (jnp.log(jnp.maximum(n, 1).astype(F32) / max_exact)
                         / math.log(MAX_DISTANCE / max_exact) * (NUM_BUCKETS - max_exact)).astype(jnp.int32)
    large = jnp.minimum(large, NUM_BUCKETS - 1)
    return jnp.where(n < max_exact, n, large)


def _bias_lookup(table, rel):
    onehot = (_t5_bucket(rel)[..., None] == jnp.arange(NUM_BUCKETS)).astype(F32)
    return jnp.einsum("...b,bh->h...", onehot, table.astype(F32), precision=lax.Precision.HIGHEST)


ATTN_GROUP_ORDER = (0, 2, 1, 3)


def _stack_group_rows(a):
    lead = a.shape[:-3]
    a = a.reshape(*lead, N_KV_HEADS, GQA_GROUP, BLOCK, a.shape[-1])
    a = jnp.take(a, jnp.array(ATTN_GROUP_ORDER), axis=len(lead) + 1)
    return a.reshape(*lead, N_KV_HEADS, GQA_GROUP * BLOCK, a.shape[-1])


def _attn_bias(table, nb):
    qi = jnp.arange(BLOCK)[:, None]
    ki = jnp.arange(2 * BLOCK)[None, :]
    rel_band = qi + BLOCK - ki
    blk3 = jnp.arange(3)[:, None, None]
    mask_band = (rel_band >= 0) & (rel_band < WINDOW) & ((blk3 - 1) * BLOCK + ki >= PAD)
    band = jnp.where(mask_band[:, None], _bias_lookup(table, rel_band)[None], NEG_INF)
    blk = jnp.arange(nb)[:, None, None]
    rel_meta = blk * BLOCK + qi[None] - (PAD + jnp.arange(N_META))
    meta = jnp.where((rel_meta >= WINDOW)[None], _bias_lookup(table, rel_meta), NEG_INF)
    meta = jnp.pad(jnp.moveaxis(meta, 1, 0), ((0, 0), (0, 0), (0, 0), (PAD, 0)), constant_values=NEG_INF)
    return _stack_group_rows(band), _stack_group_rows(meta)


def _swap_halves(x):
    return pltpu.roll(x.astype(F32), HEAD_DIM, axis=1).astype(x.dtype)


def _attn_kernel(q_ref, kp_ref, kc_ref, km_ref, vp_ref, vc_ref, vm_ref, bb0_ref, bb1_ref, bm0_ref, bm1_ref, sink_ref,
                 o_ref):
    band_bias = (bb0_ref, bb1_ref)
    meta_bias = (bm0_ref, bm1_ref)
    scale = HEAD_DIM ** -0.5
    assert math.frexp(scale)[0] == 0.5 and 2 * HEAD_DIM == LANES
    lane = lax.broadcasted_iota(jnp.int32, (1, LANES), 1)
    keep = (jnp.where(lane < HEAD_DIM, scale, 0.0).astype(BF16),
            jnp.where(lane < HEAD_DIM, 0.0, scale).astype(BF16))
    low = lax.broadcasted_iota(jnp.int32, (2 * BLOCK, LANES), 1) < HEAD_DIM
    chains = []
    for blk in range(ATTN_STEP_BLOCKS):
        qrows = slice(blk * BLOCK, (blk + 1) * BLOCK)
        for col in range(N_KV_HEADS // 2):
            kcols = slice(col * LANES, (col + 1) * LANES)
            k_prev = kp_ref[:, kcols] if blk == 0 else kc_ref[(blk - 1) * BLOCK:blk * BLOCK, kcols]
            v_prev = vp_ref[:, kcols] if blk == 0 else vc_ref[(blk - 1) * BLOCK:blk * BLOCK, kcols]
            k_nat = jnp.concatenate([k_prev, kc_ref[qrows, kcols], km_ref[:, kcols]], axis=0)
            v_nat = jnp.concatenate([v_prev, vc_ref[qrows, kcols], vm_ref[:, kcols]], axis=0)
            keys = (k_nat, _swap_halves(k_nat))
            vals = (v_nat, _swap_halves(v_nat))
            for half in range(2):
                h = 2 * col + half
                q0 = h * GQA_GROUP * HEAD_DIM
                q2 = jnp.concatenate([q_ref[qrows, q0:q0 + LANES], q_ref[qrows, q0 + LANES:q0 + 2 * LANES]], axis=0)
                for lane_half in range(2):
                    which = 0 if lane_half == half else 1
                    chains.append((blk, h, lane_half, _dot_nt(q2 * keep[lane_half], keys[which]), vals[which]))
    probs = []
    for blk, h, lane_half, s, _ in chains:
        rows = slice(lane_half * 2 * BLOCK, (lane_half + 1) * 2 * BLOCK)
        s0 = s[:, :BLOCK] + band_bias[blk][0, h, rows, :BLOCK]
        s1 = s[:, BLOCK:2 * BLOCK] + band_bias[blk][0, h, rows, BLOCK:]
        s2 = s[:, 2 * BLOCK:] + meta_bias[blk][0, h, rows, :]
        sink = sink_ref[h, rows, :]
        mx = jnp.maximum(jnp.maximum(jnp.maximum(s0, s1), s2).max(-1, keepdims=True), sink)
        p0 = jnp.exp(s0 - mx)
        p1 = jnp.exp(s1 - mx)
        p2 = jnp.exp(s2 - mx)
        den = (p0 + p1 + p2).sum(-1, keepdims=True) + jnp.exp(sink - mx)
        probs.append((jnp.concatenate([p0, p1, p2], axis=1).astype(BF16), 1.0 / den))
    outs = [_dot(p, chain[4]) * rden for (p, rden), chain in zip(probs, chains)]
    out_rows = []
    for blk in range(ATTN_STEP_BLOCKS):
        out_cols = []
        for h in range(N_KV_HEADS):
            first = (blk * N_KV_HEADS + h) * 2
            o = jnp.where(low, outs[first], outs[first + 1]).astype(o_ref.dtype)
            out_cols += [o[:BLOCK], o[BLOCK:]]
        out_rows.append(jnp.concatenate(out_cols, axis=1))
    o_ref[...] = jnp.concatenate(out_rows, axis=0)


def _attention(z, band, meta, sinks, b, seq):
    m = z.shape[0]
    per_seq = seq // BLOCK
    n_real = b * per_seq
    nb = ATTN_STEP_BLOCKS
    assert nb == 2 and per_seq % nb == 0 and (m // BLOCK) % nb == 0
    kcol, vcol = ZA_AK // KV_WIDTH, ZA_AV // KV_WIDTH
    sink_col = _stack_group_rows(jnp.broadcast_to(sinks.astype(F32)[:, None, None], (N_Q_HEADS, BLOCK, LANES)))
    first_blk = lambda s: s * nb
    prev = lambda s: jnp.where(first_blk(s) < n_real,
                               jnp.where(first_blk(s) % per_seq == 0, n_real, first_blk(s) - 1), first_blk(s))
    query_block = lambda s, j: jnp.where(first_blk(s) + j < n_real, (first_blk(s) + j) % per_seq + 1, 0)
    kv_one = lambda col, f: pl.BlockSpec((BLOCK, KV_WIDTH), lambda s: (f(s), col))
    kv_cur = lambda col: pl.BlockSpec((nb * BLOCK, KV_WIDTH), lambda s: (s, col))
    band_spec = lambda j: pl.BlockSpec((1, N_KV_HEADS, GQA_GROUP * BLOCK, 2 * BLOCK),
                                       lambda s: (jnp.minimum(query_block(s, j), 2), 0, 0, 0))
    meta_spec = lambda j: pl.BlockSpec((1, N_KV_HEADS, GQA_GROUP * BLOCK, BLOCK),
                                       lambda s: (query_block(s, j), 0, 0, 0))
    return pl.pallas_call(
        _attn_kernel,
        out_shape=jax.ShapeDtypeStruct((m, ATTN_WIDTH), BF16),
        grid=(m // (nb * BLOCK),),
        in_specs=[pl.BlockSpec((nb * BLOCK, ATTN_WIDTH), lambda s: (s, ZA_AQ // ATTN_WIDTH)),
                  kv_one(kcol, prev), kv_cur(kcol), kv_one(kcol, lambda s: n_real),
                  kv_one(vcol, prev), kv_cur(vcol), kv_one(vcol, lambda s: n_real),
                  band_spec(0), band_spec(1), meta_spec(0), meta_spec(1),
                  pl.BlockSpec((N_KV_HEADS, GQA_GROUP * BLOCK, LANES), lambda s: (0, 0, 0))],
        out_specs=pl.BlockSpec((nb * BLOCK, ATTN_WIDTH), lambda s: (s, 0)),
        compiler_params=_params("parallel"),
        name="swa_attention",
    )(z, z, z, z, z, z, z, band, band, meta, meta, sink_col)


def _mlstm_chunk(qk_in, v_in, gate_in, mo_in, prefix_chunk, cw_ref, cb_ref, gb_ref, ng_ref, xa_ref, ct_ref, n_ref, m_ref):
    L = ML_CHUNK
    ii = lax.broadcasted_iota(jnp.int32, (L, L), 0)
    jj = lax.broadcasted_iota(jnp.int32, (L, L), 1)
    causal = jj <= ii
    lane = lax.broadcasted_iota(jnp.int32, (L, LANES), 1)
    k_scale = ML_QK_DIM ** -0.5
    xa_ref[CONV_HALO:CONV_HALO + L, :] = qk_in
    window = xa_ref[...]
    sel_row = lax.broadcasted_iota(jnp.int32, (L, CONV_HALO + L), 0)
    sel_col = lax.broadcasted_iota(jnp.int32, (L, CONV_HALO + L), 1)
    acc = cb_ref[...] + cw_ref[CONV_WIDTH - 1:CONV_WIDTH, :] * qk_in.astype(F32)
    for j in range(CONV_WIDTH - 1):
        back = CONV_WIDTH - 1 - j
        shift = jnp.where(sel_col == sel_row + (CONV_HALO - back), 1.0, 0.0).astype(BF16)
        acc = acc + cw_ref[j:j + 1, :] * _dot(shift, window)
    qk = acc * _sigmoid(acc)
    xa_ref[0:CONV_HALO, :] = xa_ref[L:L + CONV_HALO, :]
    gpb = gate_in + gb_ref[...]
    log_sig = jnp.minimum(gpb, 0.0) - jnp.log1p(jnp.exp(-jnp.abs(gpb)))
    if prefix_chunk:
        valid = lax.broadcasted_iota(jnp.int32, (L, LANES), 0) >= PAD
        gx = jnp.where(lane < ML_HEADS, jnp.where(valid, gpb, NEG_INF), jnp.where(valid, log_sig, 0.0))
    else:
        gx = jnp.where(lane < ML_HEADS, gpb, log_sig)
    gxt = gx.T
    outs = []
    for h in range(ML_HEADS):
        q = qk[:, h * ML_QK_DIM:(h + 1) * ML_QK_DIM].astype(BF16)
        k = qk[:, ML_QK_WIDTH + h * ML_QK_DIM:ML_QK_WIDTH + (h + 1) * ML_QK_DIM] * k_scale
        v = v_in[:, h * ML_V_DIM:(h + 1) * ML_V_DIM]
        ig_col = gx[:, h:h + 1]
        lf_col = gx[:, ML_HEADS + h:ML_HEADS + h + 1]
        ig_row = gxt[h:h + 1, :]
        lf_row = gxt[ML_HEADS + h:ML_HEADS + h + 1, :]
        b_col = jnp.sum(jnp.where(causal, lf_row, 0.0), axis=-1, keepdims=True)
        b_row = jnp.sum(jnp.where(ii <= jj, lf_col, 0.0), axis=0, keepdims=True)
        m_prev = m_ref[h:h + 1, 0:1]
        log_d = jnp.where(causal, b_col - b_row + ig_row, NEG_INF)
        m_inter = b_col + m_prev
        m_out = jnp.maximum(m_inter, log_d.max(-1, keepdims=True))
        d = jnp.exp(log_d - m_out)
        inter = jnp.exp(m_inter - m_out)
        ct = ct_ref[h]
        n_prev = n_ref[h:h + 1, :]
        qk_scores = _dot_nt(q, k.astype(BF16))
        q_state = _dot(q, ct.astype(BF16))
        q_norm = jnp.sum(q.astype(F32) * n_prev, axis=-1, keepdims=True)
        b_last = b_col[L - 1:L, :]
        log_w = b_last - b_col + ig_col
        m_new = jnp.maximum(b_last + m_prev, log_w.max(0, keepdims=True))
        decay = jnp.exp(b_last + m_prev - m_new)
        kw = k * jnp.exp(log_w - m_new)
        s = qk_scores * d
        vcols = slice(h * ML_V_DIM, (h + 1) * ML_V_DIM)
        num = _dot(s.astype(BF16), v) + inter * q_state
        den = s.sum(-1, keepdims=True) + inter * q_norm
        hh = num / jnp.maximum(jnp.abs(den), jnp.exp(-m_out))
        ct_ref[h] = decay * ct + _dot_tn(kw.astype(BF16), v)
        n_ref[h:h + 1, :] = decay * n_prev + kw.sum(0, keepdims=True)
        m_ref[h:h + 1, :] = jnp.broadcast_to(m_new, (1, LANES))
        hn = hh * lax.rsqrt(jnp.mean(hh * hh, axis=-1, keepdims=True) + EPS) * ng_ref[:, vcols]
        outs.append(_sigmoid(mo_in[:, vcols].astype(F32)) * hn)
    return jnp.concatenate(outs, axis=1)


def _mlstm_kernel(blocks_per_seq, q_ref, k_ref, va_ref, vb_ref, h_ref, mo_ref, wgate_ref, cw_ref, cb_ref, gb_ref, ng_ref,
                  o_ref, gate_ref, xa_ref, ct_ref, n_ref, m_ref, xa0_ref, ct0_ref, n0_ref, m0_ref):
    L = ML_CHUNK
    step = pl.program_id(0)
    params = (cw_ref, cb_ref, gb_ref, ng_ref, xa_ref, ct_ref, n_ref, m_ref)
    gate_ref[...] = _dot_nt(h_ref[...], wgate_ref[...])

    def chunk_out(rows, prefix_chunk):
        qk = jnp.concatenate([q_ref[rows, :], k_ref[rows, :]], axis=1)
        v = jnp.concatenate([va_ref[rows, :], vb_ref[rows, :]], axis=1)
        return _mlstm_chunk(qk, v, gate_ref[rows, :], mo_ref[rows, :], prefix_chunk, *params).astype(o_ref.dtype)

    @pl.when(step == 0)
    def _():
        xa_ref[0:CONV_HALO, :] = jnp.zeros((CONV_HALO, 2 * ML_QK_WIDTH), xa_ref.dtype)
        ct_ref[...] = jnp.zeros_like(ct_ref)
        n_ref[...] = jnp.zeros_like(n_ref)
        m_ref[...] = jnp.zeros_like(m_ref)
        o_ref[0:L, :] = chunk_out(slice(0, L), True)
        o_ref[L:, :] = jnp.zeros((o_ref.shape[0] - L, o_ref.shape[1]), o_ref.dtype)
        xa0_ref[...] = xa_ref[0:CONV_HALO, :]
        ct0_ref[...] = ct_ref[...]
        n0_ref[...] = n_ref[...]
        m0_ref[...] = m_ref[...]

    @pl.when(step > 0)
    def _():
        @pl.when((step - 1) % blocks_per_seq == 0)
        def _():
            xa_ref[0:CONV_HALO, :] = xa0_ref[...]
            ct_ref[...] = ct0_ref[...]
            n_ref[...] = n0_ref[...]
            m_ref[...] = m0_ref[...]

        def chunk(c, carry):
            rows = pl.ds(pl.multiple_of(c * L, L), L)
            o_ref[rows, :] = chunk_out(rows, False)
            return carry

        lax.fori_loop(0, TAIL_ROWS // L, chunk, 0)


def _mlstm(za, zb, h, w_gates, conv_w, conv_b, igate_b, fgate_b, norm_g, b, seq):
    m = za.shape[0]
    assert seq % TAIL_ROWS == 0 and TAIL_ROWS % ML_CHUNK == 0
    n_seq_blocks = b * seq // TAIL_ROWS
    gate_bias = jnp.zeros((1, LANES), F32).at[0, :ML_HEADS].set(igate_b).at[0, ML_HEADS:2 * ML_HEADS].set(fgate_b)
    row_blk = lambda s: jnp.where(s == 0, n_seq_blocks, s - 1)
    rows = lambda w, c: pl.BlockSpec((TAIL_ROWS, w), lambda s: (row_blk(s), c))
    full = lambda r, c: pl.BlockSpec((r, c), lambda s: (0, 0))
    return pl.pallas_call(
        functools.partial(_mlstm_kernel, seq // TAIL_ROWS),
        out_shape=jax.ShapeDtypeStruct((m, ML_V_WIDTH), BF16),
        grid=(n_seq_blocks + 1,),
        in_specs=[rows(ML_QK_WIDTH, ZA_MQ // ML_QK_WIDTH), rows(ML_QK_WIDTH, ZA_MK // ML_QK_WIDTH),
                  rows(ML_V_WIDTH // 2, ZA_MV // (ML_V_WIDTH // 2)), rows(ML_V_WIDTH // 2, ZA_MV // (ML_V_WIDTH // 2) + 1),
                  rows(D_MODEL, 0), rows(ML_V_WIDTH, ZB_MO // ML_V_WIDTH), full(LANES, D_MODEL),
                  full(CONV_WIDTH, 2 * ML_QK_WIDTH), full(1, 2 * ML_QK_WIDTH), full(1, LANES), full(1, ML_V_WIDTH)],
        out_specs=rows(ML_V_WIDTH, 0),
        scratch_shapes=[pltpu.VMEM((TAIL_ROWS, LANES), F32),
                        pltpu.VMEM((ML_CHUNK + CONV_HALO, 2 * ML_QK_WIDTH), BF16),
                        pltpu.VMEM((ML_HEADS, ML_QK_DIM, ML_V_DIM), F32),
                        pltpu.VMEM((SUBLANES, LANES), F32),
                        pltpu.VMEM((SUBLANES, LANES), F32),
                        pltpu.VMEM((CONV_HALO, 2 * ML_QK_WIDTH), BF16),
                        pltpu.VMEM((ML_HEADS, ML_QK_DIM, ML_V_DIM), F32),
                        pltpu.VMEM((SUBLANES, LANES), F32),
                        pltpu.VMEM((SUBLANES, LANES), F32)],
        compiler_params=_params("arbitrary"),
        name="mlstm",
    )(za, za, za, za, h, zb, w_gates, conv_w.astype(F32), conv_b.reshape(1, -1).astype(F32), gate_bias,
      norm_g.reshape(1, -1).astype(F32))


def _merge_kernel(n_routed, n_body_tiles, attn_ref, ml_ref, ga0_ref, ga1_ref, gm0_ref, gm1_ref, *rest):
    with_router = n_routed is not None
    if n_body_tiles is None:
        xs_ref, wa_ref, wm_ref, wo_ref, g_ref, *rest = rest
        residual = xs_ref[...]
    else:
        xs_ref, tail_ref, wa_ref, wm_ref, wo_ref, g_ref, *rest = rest
        residual = _split_rows(pl.program_id(0), n_body_tiles, xs_ref, tail_ref)
    if with_router:
        wr_ref, br_ref, xs_out, h_out, route_out, counts_out = rest
    else:
        xs_out, h_out = rest
    a = _dot(attn_ref[...], wa_ref[...])
    m = _dot(ml_ref[...], wm_ref[...])
    ga = jnp.concatenate([ga0_ref[...], ga1_ref[...]], axis=1)
    gm = jnp.concatenate([gm0_ref[...], gm1_ref[...]], axis=1)
    y = _sigmoid(ga.astype(F32)) * a + _sigmoid(gm.astype(F32)) * m
    xs_new = residual + _dot(y.astype(BF16), wo_ref[...])
    xs_out[...] = xs_new
    hn = _rms(xs_new, g_ref[...])
    if not with_router:
        h_out[...] = hn.astype(h_out.dtype)
    else:
        half = D_MODEL // 2
        lo = lax.bitcast_convert_type(hn[:, :half].astype(BF16).astype(F32), jnp.uint32) >> 16
        hi = lax.bitcast_convert_type(hn[:, half:].astype(BF16).astype(F32), jnp.uint32) & jnp.uint32(0xFFFF0000)
        h_out[...] = lo | hi
        lane = lax.broadcasted_iota(jnp.int32, (hn.shape[0], LANES), 1).astype(F32)
        logits = jnp.broadcast_to(br_ref[...], (hn.shape[0], LANES))
        for e in range(N_EXPERTS):
            logit_e = jnp.sum(hn * wr_ref[e:e + 1, :], axis=-1, keepdims=True)
            logits = jnp.where(lane == e, logit_e + logits, logits)
        l1 = logits.max(-1, keepdims=True)
        i1 = jnp.min(jnp.where(logits == l1, lane, float(LANES)), axis=-1, keepdims=True)
        rest_logits = jnp.where(lane == i1, NEG_INF, logits)
        l2 = rest_logits.max(-1, keepdims=True)
        i2 = jnp.min(jnp.where(rest_logits == l2, lane, float(LANES)), axis=-1, keepdims=True)
        e = jnp.exp(l2 - l1)
        w1 = 1.0 / (1.0 + e)
        w2 = e / (1.0 + e)
        bm = logits.shape[0]
        row = lax.broadcasted_iota(jnp.int32, (bm, 1), 0) + pl.program_id(0) * bm
        routed = row < n_routed
        pick1 = jnp.where(jnp.logical_and(lane == i1, routed), 1.0, 0.0)
        pick2 = jnp.where(jnp.logical_and(lane == i2, routed), 1.0, 0.0)
        picks = pick1 + pick2
        earlier = (lax.broadcasted_iota(jnp.int32, (bm, bm), 1) < lax.broadcasted_iota(jnp.int32, (bm, bm), 0))
        before = _dot(jnp.where(earlier, 1.0, 0.0).astype(BF16), picks.astype(BF16))
        r1 = jnp.sum(before * pick1, axis=-1, keepdims=True)
        r2 = jnp.sum(before * pick2, axis=-1, keepdims=True)
        route_out[...] = jnp.where(lane == 0, i1, jnp.where(lane == 1, i2, jnp.where(lane == 2, w1,
                                   jnp.where(lane == 3, w2, jnp.where(lane == 4, r1, jnp.where(lane == 5, r2, 0.0))))))
        counts_out[0] = jnp.broadcast_to(jnp.sum(picks, axis=0, keepdims=True), (SUBLANES, LANES))


def _merge(attn, ml, zb, xs, wa, wm, wo, g_next, router=None):
    m = zb.shape[0]
    bm = _tile(TAIL_ROWS, 256)
    row = lambda w, c: pl.BlockSpec((bm, w), lambda i: (i, c))
    const = lambda r, c: pl.BlockSpec((r, c), lambda i: (0, 0), pipeline_mode=pl.Buffered(1))
    if isinstance(xs, tuple):
        n_body_tiles = xs[0].shape[0] // bm
        xs_specs, xs_args = _split_specs(bm, n_body_tiles), list(xs)
    else:
        n_body_tiles = None
        xs_specs, xs_args = [row(D_MODEL, 0)], [xs]
    in_specs = [row(ATTN_WIDTH, 0), row(ML_V_WIDTH, 0),
                row(HALF_D, ZB_GA // HALF_D), row(HALF_D, ZB_GA // HALF_D + 1),
                row(HALF_D, ZB_GM // HALF_D), row(HALF_D, ZB_GM // HALF_D + 1),
                *xs_specs, const(ATTN_WIDTH, D_MODEL), const(ML_V_WIDTH, D_MODEL), const(D_MODEL, D_MODEL),
                const(1, D_MODEL)]
    args = [attn, ml, zb, zb, zb, zb, *xs_args, wa, wm, wo, g_next.reshape(1, D_MODEL)]
    if router is None:
        out_shape = [jax.ShapeDtypeStruct((m, D_MODEL), F32), jax.ShapeDtypeStruct((m, D_MODEL), BF16)]
        out_specs = [row(D_MODEL, 0), row(D_MODEL, 0)]
    else:
        out_shape = [jax.ShapeDtypeStruct((m, D_MODEL), F32), jax.ShapeDtypeStruct((m, D_MODEL // 2), jnp.uint32)]
        out_specs = [row(D_MODEL, 0), row(D_MODEL // 2, 0)]
    n_routed = None
    if router is not None:
        w_router, b_router, n_routed = router
        br = jnp.full((1, LANES), NEG_INF, F32).at[0, :N_EXPERTS].set(b_router.astype(F32))
        in_specs += [const(N_EXPERTS, D_MODEL), const(1, LANES)]
        args += [jnp.swapaxes(w_router.astype(F32), 0, 1), br]
        out_shape += [jax.ShapeDtypeStruct((m, LANES), F32), jax.ShapeDtypeStruct((m // bm, SUBLANES, LANES), F32)]
        out_specs += [row(LANES, 0), pl.BlockSpec((1, SUBLANES, LANES), lambda i: (i, 0, 0))]
    return pl.pallas_call(
        functools.partial(_merge_kernel, n_routed, n_body_tiles),
        out_shape=out_shape,
        grid=(m // bm,),
        in_specs=in_specs,
        out_specs=out_specs,
        compiler_params=_params("parallel"),
        name="merge_out_proj",
    )(*args)


def _ffn_up_kernel(h_ref, wg_ref, wu_ref, act_ref):
    hb = h_ref[...]
    g = _dot(hb, wg_ref[...])
    u = _dot(hb, wu_ref[...])
    act_ref[...] = (g * _sigmoid(g) * u).astype(act_ref.dtype)


def _ffn_down_kernel(act_ref, xs_ref, wd_ref, g_ref, xs_out, h_out, rows_ref):
    j = pl.program_id(1)
    xs_new = xs_ref[...] + _dot(act_ref[...], wd_ref[...])
    xs_out[...] = xs_new
    rows_ref[j] = xs_new

    @pl.when(j == pl.num_programs(1) - 1)
    def _():
        full = jnp.concatenate([rows_ref[t] for t in range(rows_ref.shape[0])], axis=1)
        h_out[...] = _rms(full, g_ref[...]).astype(h_out.dtype)


def _dense_ffn(h, xs, wg, wu, wd, g_next):
    m = xs.shape[0]
    bm = _tile(m, 1024)
    bf = D_FF // FFN_UP_TILES
    act = pl.pallas_call(
        _ffn_up_kernel,
        out_shape=jax.ShapeDtypeStruct((m, D_FF), BF16),
        grid=(FFN_UP_TILES, m // bm),
        in_specs=[pl.BlockSpec((bm, D_MODEL), lambda f, i: (i, 0)),
                  pl.BlockSpec((D_MODEL, bf), lambda f, i: (0, f)),
                  pl.BlockSpec((D_MODEL, bf), lambda f, i: (0, f))],
        out_specs=pl.BlockSpec((bm, bf), lambda f, i: (i, f)),
        compiler_params=_params("parallel", "parallel"),
        name="dense_swiglu_up",
    )(h, wg, wu)
    bn = D_MODEL // FFN_DOWN_TILES
    return pl.pallas_call(
        _ffn_down_kernel,
        out_shape=[jax.ShapeDtypeStruct((m, D_MODEL), F32), jax.ShapeDtypeStruct((m, D_MODEL), BF16)],
        grid=(m // bm, FFN_DOWN_TILES),
        in_specs=[pl.BlockSpec((bm, D_FF), lambda i, j: (i, 0)),
                  pl.BlockSpec((bm, bn), lambda i, j: (i, j)),
                  pl.BlockSpec((D_FF, bn), lambda i, j: (0, j)),
                  pl.BlockSpec((1, D_MODEL), lambda i, j: (0, 0))],
        out_specs=[pl.BlockSpec((bm, bn), lambda i, j: (i, j)),
                   pl.BlockSpec((bm, D_MODEL), lambda i, j: (i, 0))],
        scratch_shapes=[pltpu.VMEM((FFN_DOWN_TILES, bm, bn), F32)],
        compiler_params=_params("parallel", "arbitrary"),
        name="dense_swiglu_down",
    )(act, xs, wd, g_next.reshape(1, D_MODEL))


def _row_copy(src_hbm, dst_vmem, sem, src_row, dst_row):
    return pltpu.make_async_copy(src_hbm.at[pl.ds(src_row, 1), :], dst_vmem.at[pl.ds(dst_row, 1), :], sem)


def _moe_gather_copy(h_hbm, gbuf, sem, src_row, dst_row):
    return pltpu.make_async_copy(h_hbm.at[pl.ds(src_row, 1), :], gbuf.at[pl.ds(dst_row, 1), :], sem)


def _moe_gather_wait(h_hbm, gbuf, sem):
    pltpu.make_async_copy(h_hbm.at[pl.ds(0, MOE_GATHER_ROWS), :], gbuf, sem).wait()


def _moe_kernel(sb_e_ref, sb_rows_ref, row_tok_ref, h_hbm, wg_ref, wu_ref, wd_ref, y_ref, gbuf, xb16, sem):
    sb = pl.program_id(0)
    f = pl.program_id(1)
    n_sb = pl.num_programs(0)
    nf = pl.num_programs(1)
    kind = sb_rows_ref[sb]
    half = D_MODEL // 2

    @pl.when(jnp.logical_and(sb == 0, f == 0))
    def _():
        def issue(r, c):
            _moe_gather_copy(h_hbm, gbuf, sem, row_tok_ref[r], r).start()
            return c
        lax.fori_loop(0, MOE_GATHER_ROWS, issue, 0)

    @pl.when(f == 0)
    def _():
        y_ref[...] = jnp.zeros_like(y_ref)
        prev_kind = sb_rows_ref[jnp.maximum(sb - 1, 0)]

        @pl.when(jnp.logical_or(sb == 0, prev_kind > 0))
        def _():
            _moe_gather_wait(h_hbm, gbuf, sem)

        @pl.when(kind > 0)
        def _():
            w = gbuf[0:MOE_SB, :]
            xb16[:, :half] = lax.bitcast_convert_type(w << 16, F32).astype(BF16)
            xb16[:, half:] = lax.bitcast_convert_type(w & jnp.uint32(0xFFFF0000), F32).astype(BF16)

    def compute(row0, rows, prefetch):
        if prefetch:
            base = (sb + 1) * MOE_SB + f * MOE_STEP_ROWS
            for r in range(MOE_STEP_ROWS):
                _moe_gather_copy(h_hbm, gbuf, sem, row_tok_ref[base + r], f * MOE_STEP_ROWS + r).start()
        xb = xb16[row0:row0 + rows, :]
        g = _dot(xb, wg_ref[0].astype(BF16))
        u = _dot(xb, wu_ref[0].astype(BF16))
        act = (g * _sigmoid(g) * u).astype(BF16)
        y_ref[row0:row0 + rows, :] += _dot(act, wd_ref[0].astype(BF16))

    @pl.when(kind == MOE_PARTS)
    def _():
        compute(0, MOE_SB, True)

    for part in range(MOE_PARTS - 1):
        @pl.when(jnp.logical_and(kind > part, kind < MOE_PARTS))
        def _():
            compute(part * MOE_PART_ROWS, MOE_PART_ROWS, part == 0)

    @pl.when(jnp.logical_and(jnp.logical_and(sb == n_sb - 1, f == nf - 1), kind > 0))
    def _():
        _moe_gather_wait(h_hbm, gbuf, sem)


def _moe_experts(h_packed, sb_e, sb_rows, row_tok, wg, wu, wd, n_sb, bf=MOE_FF_TILE):
    nf = D_FF // bf
    assert nf * MOE_STEP_ROWS == MOE_GATHER_ROWS
    ftile = lambda i, f, rows: jnp.where(rows[i] > 0, f, nf - 1)
    grid_spec = pltpu.PrefetchScalarGridSpec(
        num_scalar_prefetch=3,
        grid=(n_sb, nf),
        in_specs=[pl.BlockSpec(memory_space=pl.ANY),
                  pl.BlockSpec((1, D_MODEL, bf), lambda i, f, se, sr, rt: (se[i], 0, ftile(i, f, sr))),
                  pl.BlockSpec((1, D_MODEL, bf), lambda i, f, se, sr, rt: (se[i], 0, ftile(i, f, sr))),
                  pl.BlockSpec((1, bf, D_MODEL), lambda i, f, se, sr, rt: (se[i], ftile(i, f, sr), 0))],
        out_specs=pl.BlockSpec((MOE_SB, D_MODEL), lambda i, f, se, sr, rt: (i, 0)),
        scratch_shapes=[pltpu.VMEM((MOE_GATHER_ROWS, D_MODEL // 2), jnp.uint32),
                        pltpu.VMEM((MOE_SB, D_MODEL), BF16),
                        pltpu.SemaphoreType.DMA(())],
    )
    return pl.pallas_call(
        _moe_kernel,
        out_shape=jax.ShapeDtypeStruct((n_sb * MOE_SB, D_MODEL), F32),
        grid_spec=grid_spec,
        compiler_params=pltpu.CompilerParams(dimension_semantics=("arbitrary", "arbitrary"),
                                             vmem_limit_bytes=MOE_VMEM_LIMIT),
        name="moe_experts",
    )(sb_e, sb_rows, row_tok, h_packed, wg, wu, wd)


def _combine_kernel(pos_ref, xs_ref, route_ref, y_hbm, g_ref, o_ref, ybuf, sem):
    step = pl.program_id(0)
    n_steps = pl.num_programs(0)
    slot = step % 2

    rows = o_ref.shape[0]

    def gather(s, dst_slot):
        base = s * rows

        def issue(r, c):
            for k in range(TOP_K):
                _row_copy(y_hbm, ybuf.at[dst_slot, k], sem.at[dst_slot], pos_ref[TOP_K * (base + r) + k], r).start()
            return c
        lax.fori_loop(0, rows, issue, 0, unroll=4)

    @pl.when(step == 0)
    def _():
        gather(step, slot)

    @pl.when(step + 1 < n_steps)
    def _():
        gather(step + 1, 1 - slot)

    for k in range(TOP_K):
        pltpu.make_async_copy(y_hbm.at[pl.ds(0, rows), :], ybuf.at[slot, k], sem.at[slot]).wait()
    route = route_ref[...]
    moe = ybuf[slot, 0] * route[:, 2:3] + ybuf[slot, 1] * route[:, 3:4]
    o_ref[...] = _rms(xs_ref[...] + moe, g_ref[...])


def _moe_combine(pos, xs, route, yb, g_final, m_real):
    rows = _tile(m_real, 256)
    grid_spec = pltpu.PrefetchScalarGridSpec(
        num_scalar_prefetch=1,
        grid=(m_real // rows,),
        in_specs=[pl.BlockSpec((rows, D_MODEL), lambda t, p: (t, 0)),
                  pl.BlockSpec((rows, LANES), lambda t, p: (t, 0)),
                  pl.BlockSpec(memory_space=pl.ANY),
                  pl.BlockSpec((1, D_MODEL), lambda t, p: (0, 0))],
        out_specs=pl.BlockSpec((rows, D_MODEL), lambda t, p: (t, 0)),
        scratch_shapes=[pltpu.VMEM((2, TOP_K, rows, D_MODEL), F32), pltpu.SemaphoreType.DMA((2,))],
    )
    return pl.pallas_call(
        _combine_kernel,
        out_shape=jax.ShapeDtypeStruct((m_real, D_MODEL), F32),
        grid_spec=grid_spec,
        compiler_params=_params("arbitrary"),
        name="moe_combine_final_norm",
    )(pos, xs, route, yb, g_final.reshape(1, D_MODEL))


def _moe_routing(route, tile_counts, n_tok):
    n_assign = n_tok * TOP_K
    n_sb = n_assign // MOE_SB + N_EXPERTS
    rows_per_tile = route.shape[0] // tile_counts.shape[0]
    e_flat = route[:n_tok, :TOP_K].astype(jnp.int32).reshape(n_assign)
    onehot = (e_flat[:, None] == jnp.arange(N_EXPERTS, dtype=jnp.int32)[None, :]).astype(jnp.int32)
    tile_counts = tile_counts[:, 0, :N_EXPERTS].astype(jnp.int32)
    tile_base = jnp.cumsum(tile_counts, axis=0) - tile_counts
    base = jnp.repeat(tile_base, rows_per_tile * TOP_K, axis=0)[:n_assign]
    rank = jnp.sum(base * onehot, axis=1) + route[:n_tok, 4:4 + TOP_K].astype(jnp.int32).reshape(n_assign)
    counts = jnp.sum(tile_counts, axis=0)
    sb_count = (counts + MOE_SB - 1) // MOE_SB
    sb_end = jnp.cumsum(sb_count)
    sb_start = sb_end - sb_count
    dest = (jnp.sum((sb_start * MOE_SB)[None, :] * onehot, axis=1) + rank).astype(jnp.int32)
    row_tok = jnp.zeros(((n_sb + 2) * MOE_SB,), jnp.int32).at[dest].set(
        jnp.arange(n_assign, dtype=jnp.int32) // TOP_K, unique_indices=True)
    sb = jnp.arange(n_sb, dtype=jnp.int32)
    sb_e = jnp.sum((sb[:, None] >= sb_end[None, :]).astype(jnp.int32), axis=1)
    valid = sb < sb_end[-1]
    last_e = jnp.sum((sb_end[-1] - 1 >= sb_end).astype(jnp.int32))
    sb_e = jnp.where(valid, sb_e, last_e).astype(jnp.int32)
    rows_here = counts[sb_e] - (sb - sb_start[sb_e]) * MOE_SB
    parts = (jnp.minimum(rows_here, MOE_SB) + MOE_PART_ROWS - 1) // MOE_PART_ROWS
    sb_rows = jnp.where(valid, parts, 0).astype(jnp.int32)
    return dest, row_tok, sb_e, sb_rows, n_sb


def _split_w_in(w):
    n_gates = 2 * ML_HEADS
    assert w.shape[1] == ZA_WIDTH + n_gates + ZB_WIDTH
    wt = jnp.swapaxes(w, 0, 1)
    part_a = wt[:W_IN_GATES].astype(BF16)
    part_b = wt[W_IN_GATES + n_gates:].astype(BF16)
    gates = jnp.pad(wt[W_IN_GATES:W_IN_GATES + n_gates], ((0, LANES - n_gates), (0, 0))).astype(BF16)
    return part_a, part_b, gates


def kernel(x, meta_tokens, rel_bias_table, w_in, attn_sinks, conv_w, conv_b, igate_b, fgate_b, mlstm_norm_g,
           w_attn_up, w_mlstm_up, w_out, norm_mix_g, norm_ffn_g, w_ffn_gate, w_ffn_up, w_ffn_down, w_router,
           b_router, w_moe_gate, w_moe_up, w_moe_down, final_norm_g):
    b, seq, _ = x.shape
    depth = w_in.shape[0]
    m_real = b * seq
    m = m_real + TAIL_ROWS
    assert depth == 2 and seq % ML_CHUNK == 0 and seq % BLOCK == 0 and m_real % TAIL_ROWS == 0
    tail = jnp.concatenate([jnp.zeros((PAD, D_MODEL), x.dtype), meta_tokens.astype(x.dtype),
                            jnp.zeros((TAIL_ROWS - PREFIX, D_MODEL), x.dtype)], axis=0)
    xs = (x.reshape(m_real, D_MODEL), tail)
    band, meta = _attn_bias(rel_bias_table, seq // BLOCK + 1)
    h = _rmsnorm(*xs, norm_mix_g[0])
    out = None
    for layer in range(depth):
        w_a, w_b, w_gates = _split_w_in(w_in[layer])
        za = _matmul_nt(h, w_a, BF16, 1024, ZA_WIDTH // 2, "in_proj_a")
        zb = _matmul_nt(h, w_b, BF16, 1024, ZB_WIDTH // 2, "in_proj_b")
        attn = _attention(za, band, meta, attn_sinks[layer], b, seq)
        ml = _mlstm(za, zb, h, w_gates, conv_w[layer], conv_b[layer], igate_b[layer], fgate_b[layer],
                    mlstm_norm_g[layer], b, seq)
        wa, wm, wo = (w_attn_up[layer].astype(BF16), w_mlstm_up[layer].astype(BF16), w_out[layer].astype(BF16))
        i = layer // 2
        if layer % 2 == 0:
            xs, h = _merge(attn, ml, zb, xs, wa, wm, wo, norm_ffn_g[layer])
            xs, h = _dense_ffn(h, xs, w_ffn_gate[i].astype(BF16), w_ffn_up[i].astype(BF16),
                               w_ffn_down[i].astype(BF16), norm_mix_g[layer + 1])
        else:
            n_tok = m_real + PREFIX
            xs, h_packed, route, tile_counts = _merge(attn, ml, zb, xs, wa, wm, wo, norm_ffn_g[layer],
                                                      router=(w_router[i], b_router[i], n_tok))
            dest, row_tok, sb_e, sb_rows, n_sb = _moe_routing(route, tile_counts, n_tok)
            yb = _moe_experts(h_packed, sb_e, sb_rows, row_tok, w_moe_gate[i], w_moe_up[i], w_moe_down[i], n_sb)
            out = _moe_combine(dest, xs, route, yb, final_norm_g, m_real).reshape(b, seq, D_MODEL)
    return out
```

```python
import functools
import math

import jax
import jax.numpy as jnp
from jax import lax
from jax.experimental import pallas as pl
from jax.experimental.pallas import tpu as pltpu

D_MODEL = 2048
N_META = 16
BLOCK = 128
PREFIX = BLOCK
PAD = PREFIX - N_META
HEAD_DIM = 64
N_Q_HEADS = 16
N_KV_HEADS = 4
GQA_GROUP = 4
ATTN_WIDTH = N_Q_HEADS * HEAD_DIM
KV_WIDTH = N_KV_HEADS * HEAD_DIM
WINDOW = 128
NUM_BUCKETS = 32
MAX_DISTANCE = 128
ML_HEADS = 4
ML_V_WIDTH = D_MODEL // 2
ML_V_DIM = ML_V_WIDTH // ML_HEADS
ML_QK_DIM = ML_V_DIM // 2
ML_QK_WIDTH = ML_HEADS * ML_QK_DIM
CONV_WIDTH = 4
D_FF = 11 * D_MODEL // 4
N_EXPERTS = 8
TOP_K = 2
EPS = 1e-6

LANES = 128
SUBLANES = 8
BF16_ROWS = 16
VMEM_LIMIT = 56 * 1024 * 1024

ZA_AQ, ZA_AK, ZA_AV, ZA_MQ, ZA_MK, ZA_MV, ZA_WIDTH = 0, 1024, 1280, 1536, 2048, 2560, 3584
ZB_MO, ZB_GA, ZB_GM, ZB_WIDTH = 0, 1024, 3072, 5120
W_IN_GATES = ZA_WIDTH
HALF_D = D_MODEL // 2
TAIL_ROWS = 512
ATTN_STEP_BLOCKS = 2
ML_CHUNK = 128
CONV_HALO = 16
FFN_UP_TILES = 4
FFN_DOWN_TILES = 4
MOE_SB = 1024
MOE_PARTS = 4
MOE_PART_ROWS = MOE_SB // MOE_PARTS
MOE_FF_TILE = 512
MOE_STEP_ROWS = 96
MOE_GATHER_ROWS = MOE_STEP_ROWS * (D_FF // MOE_FF_TILE)
MOE_VMEM_LIMIT = 60 * 1024 * 1024

F32 = jnp.float32
BF16 = jnp.bfloat16
NEG_INF = float("-inf")


def _tile(m, target):
    best = LANES
    for t in range(LANES, min(m, target) + 1, LANES):
        if m % t == 0:
            best = t
    assert m % best == 0
    return best


def _params(*sem):
    return pltpu.CompilerParams(dimension_semantics=sem, vmem_limit_bytes=VMEM_LIMIT)


def _rms(x, g):
    return x * lax.rsqrt(jnp.mean(x * x, axis=-1, keepdims=True) + EPS) * g


def _sigmoid(x):
    return 1.0 / (1.0 + jnp.exp(-x))


def _dot(a, b):
    return jnp.dot(a, b, preferred_element_type=F32)


def _dot_nt(a, b):
    return lax.dot_general(a, b, (((1,), (1,)), ((), ())), preferred_element_type=F32)


def _dot_tn(a, b):
    return lax.dot_general(a, b, (((0,), (0,)), ((), ())), preferred_element_type=F32)


def _split_rows(i, n_body_tiles, body_ref, tail_ref):
    return jnp.where(i < n_body_tiles, body_ref[...], tail_ref[...])


def _split_specs(bm, n_body_tiles):
    return [pl.BlockSpec((bm, D_MODEL), lambda i: (jnp.minimum(i, n_body_tiles - 1), 0)),
            pl.BlockSpec((bm, D_MODEL), lambda i: (jnp.maximum(i - n_body_tiles, 0), 0))]


def _norm_kernel(n_body_tiles, x_ref, tail_ref, g_ref, o_ref):
    x = _split_rows(pl.program_id(0), n_body_tiles, x_ref, tail_ref)
    o_ref[...] = _rms(x, g_ref[...]).astype(o_ref.dtype)


def _rmsnorm(x_body, x_tail, g):
    m = x_body.shape[0] + x_tail.shape[0]
    bm = _tile(x_tail.shape[0], 512)
    assert x_body.shape[0] % bm == 0
    n_body_tiles = x_body.shape[0] // bm
    return pl.pallas_call(
        functools.partial(_norm_kernel, n_body_tiles),
        out_shape=jax.ShapeDtypeStruct((m, D_MODEL), BF16),
        grid=(m // bm,),
        in_specs=_split_specs(bm, n_body_tiles) + [pl.BlockSpec((1, D_MODEL), lambda i: (0, 0))],
        out_specs=pl.BlockSpec((bm, D_MODEL), lambda i: (i, 0)),
        compiler_params=_params("parallel"),
        name="rmsnorm",
    )(x_body, x_tail, g.reshape(1, D_MODEL))


def _mm_nt_kernel(a_ref, wt_ref, o_ref):
    o_ref[...] = _dot_nt(a_ref[...], wt_ref[...]).astype(o_ref.dtype)


def _matmul_nt(a, wt, out_dtype, bm, bn, name):
    m, k = a.shape
    n = wt.shape[0]
    bm = _tile(m, bm)
    return pl.pallas_call(
        _mm_nt_kernel,
        out_shape=jax.ShapeDtypeStruct((m, n), out_dtype),
        grid=(n // bn, m // bm),
        in_specs=[pl.BlockSpec((bm, k), lambda j, i: (i, 0)),
                  pl.BlockSpec((bn, k), lambda j, i: (j, 0))],
        out_specs=pl.BlockSpec((bm, bn), lambda j, i: (i, j)),
        compiler_params=_params("parallel", "parallel"),
        name=name,
    )(a, wt)


def _t5_bucket(rel):
    n = jnp.maximum(rel, 0)
    max_exact = NUM_BUCKETS // 2
    large = max_exact + (jnp.log(jnp.maximum(n, 1).astype(F32) / max_exact)
                         / math.log(MAX_DISTANCE / max_exact) * (NUM_BUCKETS - max_exact)).astype(jnp.int32)
    large = jnp.minimum(large, NUM_BUCKETS - 1)
    return jnp.where(n < max_exact, n, large)


def _bias_lookup(table, rel):
    onehot = (_t5_bucket(rel)[..., None] == jnp.arange(NUM_BUCKETS)).astype(F32)
    return jnp.einsum("...b,bh->h...", onehot, table.astype(F32), precision=lax.Precision.HIGHEST)


ATTN_GROUP_ORDER = (0, 2, 1, 3)


def _stack_group_rows(a):
    lead = a.shape[:-3]
    a = a.reshape(*lead, N_KV_HEADS, GQA_GROUP, BLOCK, a.shape[-1])
    a = jnp.take(a, jnp.array(ATTN_GROUP_ORDER), axis=len(lead) + 1)
    return a.reshape(*lead, N_KV_HEADS, GQA_GROUP * BLOCK, a.shape[-1])


def _attn_bias(table, nb):
    qi = jnp.arange(BLOCK)[:, None]
    ki = jnp.arange(2 * BLOCK)[None, :]
    rel_band = qi + BLOCK - ki
    blk3 = jnp.arange(3)[:, None, None]
    mask_band = (rel_band >= 0) & (rel_band < WINDOW) & ((blk3 - 1) * BLOCK + ki >= PAD)
    band = jnp.where(mask_band[:, None], _bias_lookup(table, rel_band)[None], NEG_INF)
    blk = jnp.arange(nb)[:, None, None]
    rel_meta = blk * BLOCK + qi[None] - (PAD + jnp.arange(N_META))
    meta = jnp.where((rel_meta >= WINDOW)[None], _bias_lookup(table, rel_meta), NEG_INF)
    meta = jnp.pad(jnp.moveaxis(meta, 1, 0), ((0, 0), (0, 0), (0, 0), (PAD, 0)), constant_values=NEG_INF)
    return _stack_group_rows(band), _stack_group_rows(meta)


def _swap_halves(x):
    return pltpu.roll(x.astype(F32), HEAD_DIM, axis=1).astype(x.dtype)


def _attn_kernel(q_ref, kp_ref, kc_ref, km_ref, vp_ref, vc_ref, vm_ref, bb0_ref, bb1_ref, bm0_ref, bm1_ref, sink_ref,
                 o_ref):
    band_bias = (bb0_ref, bb1_ref)
    meta_bias = (bm0_ref, bm1_ref)
    scale = HEAD_DIM ** -0.5
    assert math.frexp(scale)[0] == 0.5 and 2 * HEAD_DIM == LANES
    lane = lax.broadcasted_iota(jnp.int32, (1, LANES), 1)
    keep = (jnp.where(lane < HEAD_DIM, scale, 0.0).astype(BF16),
            jnp.where(lane < HEAD_DIM, 0.0, scale).astype(BF16))
    low = lax.broadcasted_iota(jnp.int32, (2 * BLOCK, LANES), 1) < HEAD_DIM
    chains = []
    for blk in range(ATTN_STEP_BLOCKS):
        qrows = slice(blk * BLOCK, (blk + 1) * BLOCK)
        for col in range(N_KV_HEADS // 2):
            kcols = slice(col * LANES, (col + 1) * LANES)
            k_prev = kp_ref[:, kcols] if blk == 0 else kc_ref[(blk - 1) * BLOCK:blk * BLOCK, kcols]
            v_prev = vp_ref[:, kcols] if blk == 0 else vc_ref[(blk - 1) * BLOCK:blk * BLOCK, kcols]
            k_nat = jnp.concatenate([k_prev, kc_ref[qrows, kcols], km_ref[:, kcols]], axis=0)
            v_nat = jnp.concatenate([v_prev, vc_ref[qrows, kcols], vm_ref[:, kcols]], axis=0)
            keys = (k_nat, _swap_halves(k_nat))
            vals = (v_nat, _swap_halves(v_nat))
            for half in range(2):
                h = 2 * col + half
                q0 = h * GQA_GROUP * HEAD_DIM
                q2 = jnp.concatenate([q_ref[qrows, q0:q0 + LANES], q_ref[qrows, q0 + LANES:q0 + 2 * LANES]], axis=0)
                for lane_half in range(2):
                    which = 0 if lane_half == half else 1
                    chains.append((blk, h, lane_half, _dot_nt(q2 * keep[lane_half], keys[which]), vals[which]))
    probs = []
    for blk, h, lane_half, s, _ in chains:
        rows = slice(lane_half * 2 * BLOCK, (lane_half + 1) * 2 * BLOCK)
        s0 = s[:, :BLOCK] + band_bias[blk][0, h, rows, :BLOCK]
        s1 = s[:, BLOCK:2 * BLOCK] + band_bias[blk][0, h, rows, BLOCK:]
        s2 = s[:, 2 * BLOCK:] + meta_bias[blk][0, h, rows, :]
        sink = sink_ref[h, rows, :]
        mx = jnp.maximum(jnp.maximum(jnp.maximum(s0, s1), s2).max(-1, keepdims=True), sink)
        p0 = jnp.exp(s0 - mx)
        p1 = jnp.exp(s1 - mx)
        p2 = jnp.exp(s2 - mx)
        den = (p0 + p1 + p2).sum(-1, keepdims=True) + jnp.exp(sink - mx)
        probs.append((jnp.concatenate([p0, p1, p2], axis=1).astype(BF16), 1.0 / den))
    outs = [_dot(p, chain[4]) * rden for (p, rden), chain in zip(probs, chains)]
    out_rows = []
    for blk in range(ATTN_STEP_BLOCKS):
        out_cols = []
        for h in range(N_KV_HEADS):
            first = (blk * N_KV_HEADS + h) * 2
            o = jnp.where(low, outs[first], outs[first + 1]).astype(o_ref.dtype)
            out_cols += [o[:BLOCK], o[BLOCK:]]
        out_rows.append(jnp.concatenate(out_cols, axis=1))
    o_ref[...] = jnp.concatenate(out_rows, axis=0)


def _attention(z, band, meta, sinks, b, seq):
    m = z.shape[0]
    per_seq = seq // BLOCK
    n_real = b * per_seq
    nb = ATTN_STEP_BLOCKS
    assert nb == 2 and per_seq % nb == 0 and (m // BLOCK) % nb == 0
    kcol, vcol = ZA_AK // KV_WIDTH, ZA_AV // KV_WIDTH
    sink_col = _stack_group_rows(jnp.broadcast_to(sinks.astype(F32)[:, None, None], (N_Q_HEADS, BLOCK, LANES)))
    first_blk = lambda s: s * nb
    prev = lambda s: jnp.where(first_blk(s) < n_real,
                               jnp.where(first_blk(s) % per_seq == 0, n_real, first_blk(s) - 1), first_blk(s))
    query_block = lambda s, j: jnp.where(first_blk(s) + j < n_real, (first_blk(s) + j) % per_seq + 1, 0)
    kv_one = lambda col, f: pl.BlockSpec((BLOCK, KV_WIDTH), lambda s: (f(s), col))
    kv_cur = lambda col: pl.BlockSpec((nb * BLOCK, KV_WIDTH), lambda s: (s, col))
    band_spec = lambda j: pl.BlockSpec((1, N_KV_HEADS, GQA_GROUP * BLOCK, 2 * BLOCK),
                                       lambda s: (jnp.minimum(query_block(s, j), 2), 0, 0, 0))
    meta_spec = lambda j: pl.BlockSpec((1, N_KV_HEADS, GQA_GROUP * BLOCK, BLOCK),
                                       lambda s: (query_block(s, j), 0, 0, 0))
    return pl.pallas_call(
        _attn_kernel,
        out_shape=jax.ShapeDtypeStruct((m, ATTN_WIDTH), BF16),
        grid=(m // (nb * BLOCK),),
        in_specs=[pl.BlockSpec((nb * BLOCK, ATTN_WIDTH), lambda s: (s, ZA_AQ // ATTN_WIDTH)),
                  kv_one(kcol, prev), kv_cur(kcol), kv_one(kcol, lambda s: n_real),
                  kv_one(vcol, prev), kv_cur(vcol), kv_one(vcol, lambda s: n_real),
                  band_spec(0), band_spec(1), meta_spec(0), meta_spec(1),
                  pl.BlockSpec((N_KV_HEADS, GQA_GROUP * BLOCK, LANES), lambda s: (0, 0, 0))],
        out_specs=pl.BlockSpec((nb * BLOCK, ATTN_WIDTH), lambda s: (s, 0)),
        compiler_params=_params("parallel"),
        name="swa_attention",
    )(z, z, z, z, z, z, z, band, band, meta, meta, sink_col)


def _mlstm_chunk(qk_in, v_in, gate_in, mo_in, prefix_chunk, cw_ref, cb_ref, gb_ref, ng_ref, xa_ref, ct_ref, n_ref, m_ref):
    L = ML_CHUNK
    ii = lax.broadcasted_iota(jnp.int32, (L, L), 0)
    jj = lax.broadcasted_iota(jnp.int32, (L, L), 1)
    causal = jj <= ii
    lane = lax.broadcasted_iota(jnp.int32, (L, LANES), 1)
    k_scale = ML_QK_DIM ** -0.5
    xa_ref[CONV_HALO:CONV_HALO + L, :] = qk_in
    window = xa_ref[...]
    sel_row = lax.broadcasted_iota(jnp.int32, (L, CONV_HALO + L), 0)
    sel_col = lax.broadcasted_iota(jnp.int32, (L, CONV_HALO + L), 1)
    acc = cb_ref[...] + cw_ref[CONV_WIDTH - 1:CONV_WIDTH, :] * qk_in.astype(F32)
    for j in range(CONV_WIDTH - 1):
        back = CONV_WIDTH - 1 - j
        shift = jnp.where(sel_col == sel_row + (CONV_HALO - back), 1.0, 0.0).astype(BF16)
        acc = acc + cw_ref[j:j + 1, :] * _dot(shift, window)
    qk = acc * _sigmoid(acc)
    xa_ref[0:CONV_HALO, :] = xa_ref[L:L + CONV_HALO, :]
    gpb = gate_in + gb_ref[...]
    log_sig = jnp.minimum(gpb, 0.0) - jnp.log1p(jnp.exp(-jnp.abs(gpb)))
    if prefix_chunk:
        valid = lax.broadcasted_iota(jnp.int32, (L, LANES), 0) >= PAD
        gx = jnp.where(lane < ML_HEADS, jnp.where(valid, gpb, NEG_INF), jnp.where(valid, log_sig, 0.0))
    else:
        gx = jnp.where(lane < ML_HEADS, gpb, log_sig)
    gxt = gx.T
    outs = []
    for h in range(ML_HEADS):
        q = qk[:, h * ML_QK_DIM:(h + 1) * ML_QK_DIM].astype(BF16)
        k = qk[:, ML_QK_WIDTH + h * ML_QK_DIM:ML_QK_WIDTH + (h + 1) * ML_QK_DIM] * k_scale
        v = v_in[:, h * ML_V_DIM:(h + 1) * ML_V_DIM]
        ig_col = gx[:, h:h + 1]
        lf_col = gx[:, ML_HEADS + h:ML_HEADS + h + 1]
        ig_row = gxt[h:h + 1, :]
        lf_row = gxt[ML_HEADS + h:ML_HEADS + h + 1, :]
        b_col = jnp.sum(jnp.where(causal, lf_row, 0.0), axis=-1, keepdims=True)
        b_row = jnp.sum(jnp.where(ii <= jj, lf_col, 0.0), axis=0, keepdims=True)
        m_prev = m_ref[h:h + 1, 0:1]
        log_d = jnp.where(causal, b_col - b_row + ig_row, NEG_INF)
        m_inter = b_col + m_prev
        m_out = jnp.maximum(m_inter, log_d.max(-1, keepdims=True))
        d = jnp.exp(log_d - m_out)
        inter = jnp.exp(m_inter - m_out)
        ct = ct_ref[h]
        n_prev = n_ref[h:h + 1, :]
        qk_scores = _dot_nt(q, k.astype(BF16))
        q_state = _dot(q, ct.astype(BF16))
        q_norm = jnp.sum(q.astype(F32) * n_prev, axis=-1, keepdims=True)
        b_last = b_col[L - 1:L, :]
        log_w = b_last - b_col + ig_col
        m_new = jnp.maximum(b_last + m_prev, log_w.max(0, keepdims=True))
        decay = jnp.exp(b_last + m_prev - m_new)
        kw = k * jnp.exp(log_w - m_new)
        s = qk_scores * d
        vcols = slice(h * ML_V_DIM, (h + 1) * ML_V_DIM)
        num = _dot(s.astype(BF16), v) + inter * q_state
        den = s.sum(-1, keepdims=True) + inter * q_norm
        hh = num / jnp.maximum(jnp.abs(den), jnp.exp(-m_out))
        ct_ref[h] = decay * ct + _dot_tn(kw.astype(BF16), v)
        n_ref[h:h + 1, :] = decay * n_prev + kw.sum(0, keepdims=True)
        m_ref[h:h + 1, :] = jnp.broadcast_to(m_new, (1, LANES))
        hn = hh * lax.rsqrt(jnp.mean(hh * hh, axis=-1, keepdims=True) + EPS) * ng_ref[:, vcols]
        outs.append(_sigmoid(mo_in[:, vcols].astype(F32)) * hn)
    return jnp.concatenate(outs, axis=1)


def _mlstm_kernel(blocks_per_seq, q_ref, k_ref, va_ref, vb_ref, h_ref, mo_ref, wgate_ref, cw_ref, cb_ref, gb_ref, ng_ref,
                  o_ref, gate_ref, xa_ref, ct_ref, n_ref, m_ref, xa0_ref, ct0_ref, n0_ref, m0_ref):
    L = ML_CHUNK
    step = pl.program_id(0)
    params = (cw_ref, cb_ref, gb_ref, ng_ref, xa_ref, ct_ref, n_ref, m_ref)
    gate_ref[...] = _dot_nt(h_ref[...], wgate_ref[...])

    def chunk_out(rows, prefix_chunk):
        qk = jnp.concatenate([q_ref[rows, :], k_ref[rows, :]], axis=1)
        v = jnp.concatenate([va_ref[rows, :], vb_ref[rows, :]], axis=1)
        return _mlstm_chunk(qk, v, gate_ref[rows, :], mo_ref[rows, :], prefix_chunk, *params).astype(o_ref.dtype)

    @pl.when(step == 0)
    def _():
        xa_ref[0:CONV_HALO, :] = jnp.zeros((CONV_HALO, 2 * ML_QK_WIDTH), xa_ref.dtype)
        ct_ref[...] = jnp.zeros_like(ct_ref)
        n_ref[...] = jnp.zeros_like(n_ref)
        m_ref[...] = jnp.zeros_like(m_ref)
        o_ref[0:L, :] = chunk_out(slice(0, L), True)
        o_ref[L:, :] = jnp.zeros((o_ref.shape[0] - L, o_ref.shape[1]), o_ref.dtype)
        xa0_ref[...] = xa_ref[0:CONV_HALO, :]
        ct0_ref[...] = ct_ref[...]
        n0_ref[...] = n_ref[...]
        m0_ref[...] = m_ref[...]

    @pl.when(step > 0)
    def _():
        @pl.when((step - 1) % blocks_per_seq == 0)
        def _():
            xa_ref[0:CONV_HALO, :] = xa0_ref[...]
            ct_ref[...] = ct0_ref[...]
            n_ref[...] = n0_ref[...]
            m_ref[...] = m0_ref[...]

        def chunk(c, carry):
            rows = pl.ds(pl.multiple_of(c * L, L), L)
            o_ref[rows, :] = chunk_out(rows, False)
            return carry

        lax.fori_loop(0, TAIL_ROWS // L, chunk, 0)


def _mlstm(za, zb, h, w_gates, conv_w, conv_b, igate_b, fgate_b, norm_g, b, seq):
    m = za.shape[0]
    assert seq % TAIL_ROWS == 0 and TAIL_ROWS % ML_CHUNK == 0
    n_seq_blocks = b * seq // TAIL_ROWS
    gate_bias = jnp.zeros((1, LANES), F32).at[0, :ML_HEADS].set(igate_b).at[0, ML_HEADS:2 * ML_HEADS].set(fgate_b)
    row_blk = lambda s: jnp.where(s == 0, n_seq_blocks, s - 1)
    rows = lambda w, c: pl.BlockSpec((TAIL_ROWS, w), lambda s: (row_blk(s), c))
    full = lambda r, c: pl.BlockSpec((r, c), lambda s: (0, 0))
    return pl.pallas_call(
        functools.partial(_mlstm_kernel, seq // TAIL_ROWS),
        out_shape=jax.ShapeDtypeStruct((m, ML_V_WIDTH), BF16),
        grid=(n_seq_blocks + 1,),
        in_specs=[rows(ML_QK_WIDTH, ZA_MQ // ML_QK_WIDTH), rows(ML_QK_WIDTH, ZA_MK // ML_QK_WIDTH),
                  rows(ML_V_WIDTH // 2, ZA_MV // (ML_V_WIDTH // 2)), rows(ML_V_WIDTH // 2, ZA_MV // (ML_V_WIDTH // 2) + 1),
                  rows(D_MODEL, 0), rows(ML_V_WIDTH, ZB_MO // ML_V_WIDTH), full(LANES, D_MODEL),
                  full(CONV_WIDTH, 2 * ML_QK_WIDTH), full(1, 2 * ML_QK_WIDTH), full(1, LANES), full(1, ML_V_WIDTH)],
        out_specs=rows(ML_V_WIDTH, 0),
        scratch_shapes=[pltpu.VMEM((TAIL_ROWS, LANES), F32),
                        pltpu.VMEM((ML_CHUNK + CONV_HALO, 2 * ML_QK_WIDTH), BF16),
                        pltpu.VMEM((ML_HEADS, ML_QK_DIM, ML_V_DIM), F32),
                        pltpu.VMEM((SUBLANES, LANES), F32),
                        pltpu.VMEM((SUBLANES, LANES), F32),
                        pltpu.VMEM((CONV_HALO, 2 * ML_QK_WIDTH), BF16),
                        pltpu.VMEM((ML_HEADS, ML_QK_DIM, ML_V_DIM), F32),
                        pltpu.VMEM((SUBLANES, LANES), F32),
                        pltpu.VMEM((SUBLANES, LANES), F32)],
        compiler_params=_params("arbitrary"),
        name="mlstm",
    )(za, za, za, za, h, zb, w_gates, conv_w.astype(F32), conv_b.reshape(1, -1).astype(F32), gate_bias,
      norm_g.reshape(1, -1).astype(F32))


def _merge_kernel(n_routed, n_body_tiles, attn_ref, ml_ref, ga0_ref, ga1_ref, gm0_ref, gm1_ref, *rest):
    with_router = n_routed is not None
    if n_body_tiles is None:
        xs_ref, wa_ref, wm_ref, wo_ref, g_ref, *rest = rest
        residual = xs_ref[...]
    else:
        xs_ref, tail_ref, wa_ref, wm_ref, wo_ref, g_ref, *rest = rest
        residual = _split_rows(pl.program_id(0), n_body_tiles, xs_ref, tail_ref)
    if with_router:
        wr_hi_ref, wr_lo_ref, br_ref, xs_out, h_out, route_out, counts_out = rest
    else:
        xs_out, h_out = rest
    a = _dot(attn_ref[...], wa_ref[...])
    m = _dot(ml_ref[...], wm_ref[...])
    ga = jnp.concatenate([ga0_ref[...], ga1_ref[...]], axis=1)
    gm = jnp.concatenate([gm0_ref[...], gm1_ref[...]], axis=1)
    y = _sigmoid(ga.astype(F32)) * a + _sigmoid(gm.astype(F32)) * m
    xs_new = residual + _dot(y.astype(BF16), wo_ref[...])
    xs_out[...] = xs_new
    hn = _rms(xs_new, g_ref[...])
    if not with_router:
        h_out[...] = hn.astype(h_out.dtype)
    else:
        half = D_MODEL // 2
        lo = lax.bitcast_convert_type(hn[:, :half].astype(BF16).astype(F32), jnp.uint32) >> 16
        hi = lax.bitcast_convert_type(hn[:, half:].astype(BF16).astype(F32), jnp.uint32) & jnp.uint32(0xFFFF0000)
        h_out[...] = lo | hi
        hn_hi = hn.astype(BF16)
        hn_lo = (hn - hn_hi.astype(F32)).astype(BF16)
        logits = (_dot(hn_hi, wr_hi_ref[...]) + (_dot(hn_lo, wr_hi_ref[...]) + _dot(hn_hi, wr_lo_ref[...]))
                  + br_ref[...])
        lane = lax.broadcasted_iota(jnp.int32, logits.shape, 1).astype(F32)
        l1 = logits.max(-1, keepdims=True)
        i1 = jnp.min(jnp.where(logits == l1, lane, float(LANES)), axis=-1, keepdims=True)
        rest_logits = jnp.where(lane == i1, NEG_INF, logits)
        l2 = rest_logits.max(-1, keepdims=True)
        i2 = jnp.min(jnp.where(rest_logits == l2, lane, float(LANES)), axis=-1, keepdims=True)
        e = jnp.exp(l2 - l1)
        w1 = 1.0 / (1.0 + e)
        w2 = e / (1.0 + e)
        bm = logits.shape[0]
        row = lax.broadcasted_iota(jnp.int32, (bm, 1), 0) + pl.program_id(0) * bm
        routed = row < n_routed
        pick1 = jnp.where(jnp.logical_and(lane == i1, routed), 1.0, 0.0)
        pick2 = jnp.where(jnp.logical_and(lane == i2, routed), 1.0, 0.0)
        picks = pick1 + pick2
        earlier = (lax.broadcasted_iota(jnp.int32, (bm, bm), 1) < lax.broadcasted_iota(jnp.int32, (bm, bm), 0))
        before = _dot(jnp.where(earlier, 1.0, 0.0).astype(BF16), picks.astype(BF16))
        r1 = jnp.sum(before * pick1, axis=-1, keepdims=True)
        r2 = jnp.sum(before * pick2, axis=-1, keepdims=True)
        route_out[...] = jnp.where(lane == 0, i1, jnp.where(lane == 1, i2, jnp.where(lane == 2, w1,
                                   jnp.where(lane == 3, w2, jnp.where(lane == 4, r1, jnp.where(lane == 5, r2, 0.0))))))
        counts_out[0] = jnp.broadcast_to(jnp.sum(picks, axis=0, keepdims=True), (SUBLANES, LANES))


def _merge(attn, ml, zb, xs, wa, wm, wo, g_next, router=None):
    m = zb.shape[0]
    bm = _tile(TAIL_ROWS, 256)
    row = lambda w, c: pl.BlockSpec((bm, w), lambda i: (i, c))
    const = lambda r, c: pl.BlockSpec((r, c), lambda i: (0, 0), pipeline_mode=pl.Buffered(1))
    if isinstance(xs, tuple):
        n_body_tiles = xs[0].shape[0] // bm
        xs_specs, xs_args = _split_specs(bm, n_body_tiles), list(xs)
    else:
        n_body_tiles = None
        xs_specs, xs_args = [row(D_MODEL, 0)], [xs]
    in_specs = [row(ATTN_WIDTH, 0), row(ML_V_WIDTH, 0),
                row(HALF_D, ZB_GA // HALF_D), row(HALF_D, ZB_GA // HALF_D + 1),
                row(HALF_D, ZB_GM // HALF_D), row(HALF_D, ZB_GM // HALF_D + 1),
                *xs_specs, const(ATTN_WIDTH, D_MODEL), const(ML_V_WIDTH, D_MODEL), const(D_MODEL, D_MODEL),
                const(1, D_MODEL)]
    args = [attn, ml, zb, zb, zb, zb, *xs_args, wa, wm, wo, g_next.reshape(1, D_MODEL)]
    if router is None:
        out_shape = [jax.ShapeDtypeStruct((m, D_MODEL), F32), jax.ShapeDtypeStruct((m, D_MODEL), BF16)]
        out_specs = [row(D_MODEL, 0), row(D_MODEL, 0)]
    else:
        out_shape = [jax.ShapeDtypeStruct((m, D_MODEL), F32), jax.ShapeDtypeStruct((m, D_MODEL // 2), jnp.uint32)]
        out_specs = [row(D_MODEL, 0), row(D_MODEL // 2, 0)]
    n_routed = None
    if router is not None:
        w_router, b_router, n_routed = router
        wr = jnp.zeros((D_MODEL, LANES), F32).at[:, :N_EXPERTS].set(w_router.astype(F32))
        br = jnp.full((1, LANES), NEG_INF, F32).at[0, :N_EXPERTS].set(b_router.astype(F32))
        wr_hi = wr.astype(BF16)
        wr_lo = (wr - wr_hi.astype(F32)).astype(BF16)
        in_specs += [const(D_MODEL, LANES), const(D_MODEL, LANES), const(1, LANES)]
        args += [wr_hi, wr_lo, br]
        out_shape += [jax.ShapeDtypeStruct((m, LANES), F32), jax.ShapeDtypeStruct((m // bm, SUBLANES, LANES), F32)]
        out_specs += [row(LANES, 0), pl.BlockSpec((1, SUBLANES, LANES), lambda i: (i, 0, 0))]
    return pl.pallas_call(
        functools.partial(_merge_kernel, n_routed, n_body_tiles),
        out_shape=out_shape,
        grid=(m // bm,),
        in_specs=in_specs,
        out_specs=out_specs,
        compiler_params=_params("parallel"),
        name="merge_out_proj",
    )(*args)


def _ffn_up_kernel(h_ref, wg_ref, wu_ref, act_ref):
    hb = h_ref[...]
    g = _dot(hb, wg_ref[...])
    u = _dot(hb, wu_ref[...])
    act_ref[...] = (g * _sigmoid(g) * u).astype(act_ref.dtype)


def _ffn_down_kernel(act_ref, xs_ref, wd_ref, g_ref, xs_out, h_out, rows_ref):
    j = pl.program_id(1)
    xs_new = xs_ref[...] + _dot(act_ref[...], wd_ref[...])
    xs_out[...] = xs_new
    rows_ref[j] = xs_new

    @pl.when(j == pl.num_programs(1) - 1)
    def _():
        full = jnp.concatenate([rows_ref[t] for t in range(rows_ref.shape[0])], axis=1)
        h_out[...] = _rms(full, g_ref[...]).astype(h_out.dtype)


def _dense_ffn(h, xs, wg, wu, wd, g_next):
    m = xs.shape[0]
    bm = _tile(m, 1024)
    bf = D_FF // FFN_UP_TILES
    act = pl.pallas_call(
        _ffn_up_kernel,
        out_shape=jax.ShapeDtypeStruct((m, D_FF), BF16),
        grid=(FFN_UP_TILES, m // bm),
        in_specs=[pl.BlockSpec((bm, D_MODEL), lambda f, i: (i, 0)),
                  pl.BlockSpec((D_MODEL, bf), lambda f, i: (0, f)),
                  pl.BlockSpec((D_MODEL, bf), lambda f, i: (0, f))],
        out_specs=pl.BlockSpec((bm, bf), lambda f, i: (i, f)),
        compiler_params=_params("parallel", "parallel"),
        name="dense_swiglu_up",
    )(h, wg, wu)
    bn = D_MODEL // FFN_DOWN_TILES
    return pl.pallas_call(
        _ffn_down_kernel,
        out_shape=[jax.ShapeDtypeStruct((m, D_MODEL), F32), jax.ShapeDtypeStruct((m, D_MODEL), BF16)],
        grid=(m // bm, FFN_DOWN_TILES),
        in_specs=[pl.BlockSpec((bm, D_FF), lambda i, j: (i, 0)),
                  pl.BlockSpec((bm, bn), lambda i, j: (i, j)),
                  pl.BlockSpec((D_FF, bn), lambda i, j: (0, j)),
                  pl.BlockSpec((1, D_MODEL), lambda i, j: (0, 0))],
        out_specs=[pl.BlockSpec((bm, bn), lambda i, j: (i, j)),
                   pl.BlockSpec((bm, D_MODEL), lambda i, j: (i, 0))],
        scratch_shapes=[pltpu.VMEM((FFN_DOWN_TILES, bm, bn), F32)],
        compiler_params=_params("parallel", "arbitrary"),
        name="dense_swiglu_down",
    )(act, xs, wd, g_next.reshape(1, D_MODEL))


def _row_copy(src_hbm, dst_vmem, sem, src_row, dst_row):
    return pltpu.make_async_copy(src_hbm.at[pl.ds(src_row, 1), :], dst_vmem.at[pl.ds(dst_row, 1), :], sem)


def _moe_gather_copy(h_hbm, gbuf, sem, src_row, dst_row):
    return pltpu.make_async_copy(h_hbm.at[pl.ds(src_row, 1), :], gbuf.at[pl.ds(dst_row, 1), :], sem)


def _moe_gather_wait(h_hbm, gbuf, sem):
    pltpu.make_async_copy(h_hbm.at[pl.ds(0, MOE_GATHER_ROWS), :], gbuf, sem).wait()


def _moe_kernel(sb_e_ref, sb_rows_ref, row_tok_ref, h_hbm, wg_ref, wu_ref, wd_ref, y_ref, gbuf, xb16, sem):
    sb = pl.program_id(0)
    f = pl.program_id(1)
    n_sb = pl.num_programs(0)
    nf = pl.num_programs(1)
    kind = sb_rows_ref[sb]
    half = D_MODEL // 2

    @pl.when(jnp.logical_and(sb == 0, f == 0))
    def _():
        def issue(r, c):
            _moe_gather_copy(h_hbm, gbuf, sem, row_tok_ref[r], r).start()
            return c
        lax.fori_loop(0, MOE_GATHER_ROWS, issue, 0)

    @pl.when(f == 0)
    def _():
        y_ref[...] = jnp.zeros_like(y_ref)
        prev_kind = sb_rows_ref[jnp.maximum(sb - 1, 0)]

        @pl.when(jnp.logical_or(sb == 0, prev_kind > 0))
        def _():
            _moe_gather_wait(h_hbm, gbuf, sem)

        @pl.when(kind > 0)
        def _():
            w = gbuf[0:MOE_SB, :]
            xb16[:, :half] = lax.bitcast_convert_type(w << 16, F32).astype(BF16)
            xb16[:, half:] = lax.bitcast_convert_type(w & jnp.uint32(0xFFFF0000), F32).astype(BF16)

    def compute(row0, rows, prefetch):
        if prefetch:
            base = (sb + 1) * MOE_SB + f * MOE_STEP_ROWS
            for r in range(MOE_STEP_ROWS):
                _moe_gather_copy(h_hbm, gbuf, sem, row_tok_ref[base + r], f * MOE_STEP_ROWS + r).start()
        xb = xb16[row0:row0 + rows, :]
        g = _dot(xb, wg_ref[0].astype(BF16))
        u = _dot(xb, wu_ref[0].astype(BF16))
        act = (g * _sigmoid(g) * u).astype(BF16)
        y_ref[row0:row0 + rows, :] += _dot(act, wd_ref[0].astype(BF16))

    @pl.when(kind == MOE_PARTS)
    def _():
        compute(0, MOE_SB, True)

    for part in range(MOE_PARTS - 1):
        @pl.when(jnp.logical_and(kind > part, kind < MOE_PARTS))
        def _():
            compute(part * MOE_PART_ROWS, MOE_PART_ROWS, part == 0)

    @pl.when(jnp.logical_and(jnp.logical_and(sb == n_sb - 1, f == nf - 1), kind > 0))
    def _():
        _moe_gather_wait(h_hbm, gbuf, sem)


def _moe_experts(h_packed, sb_e, sb_rows, row_tok, wg, wu, wd, n_sb, bf=MOE_FF_TILE):
    nf = D_FF // bf
    assert nf * MOE_STEP_ROWS == MOE_GATHER_ROWS
    ftile = lambda i, f, rows: jnp.where(rows[i] > 0, f, nf - 1)
    grid_spec = pltpu.PrefetchScalarGridSpec(
        num_scalar_prefetch=3,
        grid=(n_sb, nf),
        in_specs=[pl.BlockSpec(memory_space=pl.ANY),
                  pl.BlockSpec((1, D_MODEL, bf), lambda i, f, se, sr, rt: (se[i], 0, ftile(i, f, sr))),
                  pl.BlockSpec((1, D_MODEL, bf), lambda i, f, se, sr, rt: (se[i], 0, ftile(i, f, sr))),
                  pl.BlockSpec((1, bf, D_MODEL), lambda i, f, se, sr, rt: (se[i], ftile(i, f, sr), 0))],
        out_specs=pl.BlockSpec((MOE_SB, D_MODEL), lambda i, f, se, sr, rt: (i, 0)),
        scratch_shapes=[pltpu.VMEM((MOE_GATHER_ROWS, D_MODEL // 2), jnp.uint32),
                        pltpu.VMEM((MOE_SB, D_MODEL), BF16),
                        pltpu.SemaphoreType.DMA(())],
    )
    return pl.pallas_call(
        _moe_kernel,
        out_shape=jax.ShapeDtypeStruct((n_sb * MOE_SB, D_MODEL), F32),
        grid_spec=grid_spec,
        compiler_params=pltpu.CompilerParams(dimension_semantics=("arbitrary", "arbitrary"),
                                             vmem_limit_bytes=MOE_VMEM_LIMIT),
        name="moe_experts",
    )(sb_e, sb_rows, row_tok, h_packed, wg, wu, wd)


def _combine_kernel(pos_ref, xs_ref, route_ref, y_hbm, g_ref, o_ref, ybuf, sem):
    step = pl.program_id(0)
    n_steps = pl.num_programs(0)
    slot = step % 2

    rows = o_ref.shape[0]

    def gather(s, dst_slot):
        base = s * rows

        def issue(r, c):
            for k in range(TOP_K):
                _row_copy(y_hbm, ybuf.at[dst_slot, k], sem.at[dst_slot], pos_ref[TOP_K * (base + r) + k], r).start()
            return c
        lax.fori_loop(0, rows, issue, 0, unroll=4)

    @pl.when(step == 0)
    def _():
        gather(step, slot)

    @pl.when(step + 1 < n_steps)
    def _():
        gather(step + 1, 1 - slot)

    for k in range(TOP_K):
        pltpu.make_async_copy(y_hbm.at[pl.ds(0, rows), :], ybuf.at[slot, k], sem.at[slot]).wait()
    route = route_ref[...]
    moe = ybuf[slot, 0] * route[:, 2:3] + ybuf[slot, 1] * route[:, 3:4]
    o_ref[...] = _rms(xs_ref[...] + moe, g_ref[...])


def _moe_combine(pos, xs, route, yb, g_final, m_real):
    rows = _tile(m_real, 512)
    grid_spec = pltpu.PrefetchScalarGridSpec(
        num_scalar_prefetch=1,
        grid=(m_real // rows,),
        in_specs=[pl.BlockSpec((rows, D_MODEL), lambda t, p: (t, 0)),
                  pl.BlockSpec((rows, LANES), lambda t, p: (t, 0)),
                  pl.BlockSpec(memory_space=pl.ANY),
                  pl.BlockSpec((1, D_MODEL), lambda t, p: (0, 0))],
        out_specs=pl.BlockSpec((rows, D_MODEL), lambda t, p: (t, 0)),
        scratch_shapes=[pltpu.VMEM((2, TOP_K, rows, D_MODEL), F32), pltpu.SemaphoreType.DMA((2,))],
    )
    return pl.pallas_call(
        _combine_kernel,
        out_shape=jax.ShapeDtypeStruct((m_real, D_MODEL), F32),
        grid_spec=grid_spec,
        compiler_params=_params("arbitrary"),
        name="moe_combine_final_norm",
    )(pos, xs, route, yb, g_final.reshape(1, D_MODEL))


def _moe_routing(route, tile_counts, n_tok):
    n_assign = n_tok * TOP_K
    n_sb = n_assign // MOE_SB + N_EXPERTS
    rows_per_tile = route.shape[0] // tile_counts.shape[0]
    e_flat = route[:n_tok, :TOP_K].astype(jnp.int32).reshape(n_assign)
    onehot = (e_flat[:, None] == jnp.arange(N_EXPERTS, dtype=jnp.int32)[None, :]).astype(jnp.int32)
    tile_counts = tile_counts[:, 0, :N_EXPERTS].astype(jnp.int32)
    tile_base = jnp.cumsum(tile_counts, axis=0) - tile_counts
    base = jnp.repeat(tile_base, rows_per_tile * TOP_K, axis=0)[:n_assign]
    rank = jnp.sum(base * onehot, axis=1) + route[:n_tok, 4:4 + TOP_K].astype(jnp.int32).reshape(n_assign)
    counts = jnp.sum(tile_counts, axis=0)
    sb_count = (counts + MOE_SB - 1) // MOE_SB
    sb_end = jnp.cumsum(sb_count)
    sb_start = sb_end - sb_count
    dest = (jnp.sum((sb_start * MOE_SB)[None, :] * onehot, axis=1) + rank).astype(jnp.int32)
    row_tok = jnp.zeros(((n_sb + 2) * MOE_SB,), jnp.int32).at[dest].set(
        jnp.arange(n_assign, dtype=jnp.int32) // TOP_K, unique_indices=True)
    sb = jnp.arange(n_sb, dtype=jnp.int32)
    sb_e = jnp.sum((sb[:, None] >= sb_end[None, :]).astype(jnp.int32), axis=1)
    valid = sb < sb_end[-1]
    last_e = jnp.sum((sb_end[-1] - 1 >= sb_end).astype(jnp.int32))
    sb_e = jnp.where(valid, sb_e, last_e).astype(jnp.int32)
    rows_here = counts[sb_e] - (sb - sb_start[sb_e]) * MOE_SB
    parts = (jnp.minimum(rows_here, MOE_SB) + MOE_PART_ROWS - 1) // MOE_PART_ROWS
    sb_rows = jnp.where(valid, parts, 0).astype(jnp.int32)
    return dest, row_tok, sb_e, sb_rows, n_sb


def _split_w_in(w):
    n_gates = 2 * ML_HEADS
    assert w.shape[1] == ZA_WIDTH + n_gates + ZB_WIDTH
    wt = jnp.swapaxes(w, 0, 1)
    part_a = wt[:W_IN_GATES].astype(BF16)
    part_b = wt[W_IN_GATES + n_gates:].astype(BF16)
    gates = jnp.pad(wt[W_IN_GATES:W_IN_GATES + n_gates], ((0, LANES - n_gates), (0, 0))).astype(BF16)
    return part_a, part_b, gates


def kernel(x, meta_tokens, rel_bias_table, w_in, attn_sinks, conv_w, conv_b, igate_b, fgate_b, mlstm_norm_g,
           w_attn_up, w_mlstm_up, w_out, norm_mix_g, norm_ffn_g, w_ffn_gate, w_ffn_up, w_ffn_down, w_router,
           b_router, w_moe_gate, w_moe_up, w_moe_down, final_norm_g):
    b, seq, _ = x.shape
    depth = w_in.shape[0]
    m_real = b * seq
    m = m_real + TAIL_ROWS
    assert depth == 2 and seq % ML_CHUNK == 0 and seq % BLOCK == 0 and m_real % TAIL_ROWS == 0
    tail = jnp.concatenate([jnp.zeros((PAD, D_MODEL), x.dtype), meta_tokens.astype(x.dtype),
                            jnp.zeros((TAIL_ROWS - PREFIX, D_MODEL), x.dtype)], axis=0)
    xs = (x.reshape(m_real, D_MODEL), tail)
    band, meta = _attn_bias(rel_bias_table, seq // BLOCK + 1)
    h = _rmsnorm(*xs, norm_mix_g[0])
    out = None
    for layer in range(depth):
        w_a, w_b, w_gates = _split_w_in(w_in[layer])
        za = _matmul_nt(h, w_a, BF16, 1024, ZA_WIDTH // 2, "in_proj_a")
        zb = _matmul_nt(h, w_b, BF16, 1024, ZB_WIDTH // 2, "in_proj_b")
        attn = _attention(za, band, meta, attn_sinks[layer], b, seq)
        ml = _mlstm(za, zb, h, w_gates, conv_w[layer], conv_b[layer], igate_b[layer], fgate_b[layer],
                    mlstm_norm_g[layer], b, seq)
        wa, wm, wo = (w_attn_up[layer].astype(BF16), w_mlstm_up[layer].astype(BF16), w_out[layer].astype(BF16))
        i = layer // 2
        if layer % 2 == 0:
            xs, h = _merge(attn, ml, zb, xs, wa, wm, wo, norm_ffn_g[layer])
            xs, h = _dense_ffn(h, xs, w_ffn_gate[i].astype(BF16), w_ffn_up[i].astype(BF16),
                               w_ffn_down[i].astype(BF16), norm_mix_g[layer + 1])
        else:
            n_tok = m_real + PREFIX
            xs, h_packed, route, tile_counts = _merge(attn, ml, zb, xs, wa, wm, wo, norm_ffn_g[layer],
                                                      router=(w_router[i], b_router[i], n_tok))
            dest, row_tok, sb_e, sb_rows, n_sb = _moe_routing(route, tile_counts, n_tok)
            yb = _moe_experts(h_packed, sb_e, sb_rows, row_tok, w_moe_gate[i], w_moe_up[i], w_moe_down[i], n_sb)
            out = _moe_combine(dest, xs, route, yb, final_norm_g, m_real).reshape(b, seq, D_MODEL)
    return out
```

```python
import functools
import math

import jax
import jax.numpy as jnp
from jax import lax
from jax.experimental import pallas as pl
from jax.experimental.pallas import tpu as pltpu

D_MODEL = 2048
N_META = 16
BLOCK = 128
PREFIX = BLOCK
PAD = PREFIX - N_META
HEAD_DIM = 64
N_Q_HEADS = 16
N_KV_HEADS = 4
GQA_GROUP = 4
ATTN_WIDTH = N_Q_HEADS * HEAD_DIM
KV_WIDTH = N_KV_HEADS * HEAD_DIM
WINDOW = 128
NUM_BUCKETS = 32
MAX_DISTANCE = 128
ML_HEADS = 4
ML_V_WIDTH = D_MODEL // 2
ML_V_DIM = ML_V_WIDTH // ML_HEADS
ML_QK_DIM = ML_V_DIM // 2
ML_QK_WIDTH = ML_HEADS * ML_QK_DIM
CONV_WIDTH = 4
D_FF = 11 * D_MODEL // 4
N_EXPERTS = 8
TOP_K = 2
EPS = 1e-6

LANES = 128
SUBLANES = 8
BF16_ROWS = 16
VMEM_LIMIT = 56 * 1024 * 1024

ZA_AQ = 0
ZA_AK = ZA_AQ + ATTN_WIDTH
ZA_AV = ZA_AK + KV_WIDTH
ZA_MQ = ZA_AV + KV_WIDTH
ZA_MK = ZA_MQ + ML_QK_WIDTH
ZA_MV = ZA_MK + ML_QK_WIDTH
ZA_WIDTH = ZA_MV + ML_V_WIDTH
ZB_MO = 0
ZB_GA = ZB_MO + ML_V_WIDTH
ZB_GM = ZB_GA + D_MODEL
ZB_WIDTH = ZB_GM + D_MODEL
W_IN_GATES = ZA_WIDTH
HALF_D = D_MODEL // 2
TAIL_ROWS = 512
ATTN_STEP_BLOCKS = 2
ML_CHUNK = 128
CONV_HALO = BF16_ROWS
FFN_UP_TILES = 4
FFN_DOWN_TILES = 4
MOE_SB = 1024
MOE_PARTS = 4
MOE_PART_ROWS = MOE_SB // MOE_PARTS
MOE_FF_TILE = 512
MOE_STEP_ROWS = 96
MOE_GATHER_ROWS = MOE_STEP_ROWS * (D_FF // MOE_FF_TILE)
MOE_VMEM_LIMIT = 60 * 1024 * 1024

F32 = jnp.float32
BF16 = jnp.bfloat16
NEG_INF = float("-inf")


def _tile(m, target):
    best = LANES
    for t in range(LANES, min(m, target) + 1, LANES):
        if m % t == 0:
            best = t
    assert m % best == 0
    return best


def _params(*sem):
    return pltpu.CompilerParams(dimension_semantics=sem, vmem_limit_bytes=VMEM_LIMIT)


def _rms(x, g):
    return x * lax.rsqrt(jnp.mean(x * x, axis=-1, keepdims=True) + EPS) * g


def _sigmoid(x):
    return 1.0 / (1.0 + jnp.exp(-x))


def _dot(a, b):
    return jnp.dot(a, b, preferred_element_type=F32)


def _dot_nt(a, b):
    return lax.dot_general(a, b, (((1,), (1,)), ((), ())), preferred_element_type=F32)


def _dot_tn(a, b):
    return lax.dot_general(a, b, (((0,), (0,)), ((), ())), preferred_element_type=F32)


def _split_rows(i, n_body_tiles, body_ref, tail_ref):
    return jnp.where(i < n_body_tiles, body_ref[...], tail_ref[...])


def _split_specs(bm, n_body_tiles):
    return [pl.BlockSpec((bm, D_MODEL), lambda i: (jnp.minimum(i, n_body_tiles - 1), 0)),
            pl.BlockSpec((bm, D_MODEL), lambda i: (jnp.maximum(i - n_body_tiles, 0), 0))]


def _norm_kernel(n_body_tiles, x_ref, tail_ref, g_ref, o_ref):
    x = _split_rows(pl.program_id(0), n_body_tiles, x_ref, tail_ref)
    o_ref[...] = _rms(x, g_ref[...]).astype(o_ref.dtype)


def _rmsnorm(x_body, x_tail, g):
    m = x_body.shape[0] + x_tail.shape[0]
    bm = _tile(x_tail.shape[0], 512)
    assert x_body.shape[0] % bm == 0
    n_body_tiles = x_body.shape[0] // bm
    return pl.pallas_call(
        functools.partial(_norm_kernel, n_body_tiles),
        out_shape=jax.ShapeDtypeStruct((m, D_MODEL), BF16),
        grid=(m // bm,),
        in_specs=_split_specs(bm, n_body_tiles) + [pl.BlockSpec((1, D_MODEL), lambda i: (0, 0))],
        out_specs=pl.BlockSpec((bm, D_MODEL), lambda i: (i, 0)),
        compiler_params=_params("parallel"),
        name="rmsnorm",
    )(x_body, x_tail, g.reshape(1, D_MODEL))


def _mm_nt_kernel(a_ref, wt_ref, o_ref):
    o_ref[...] = _dot_nt(a_ref[...], wt_ref[...]).astype(o_ref.dtype)


def _matmul_nt(a, wt, out_dtype, bm, bn, name):
    m, k = a.shape
    n = wt.shape[0]
    bm = _tile(m, bm)
    return pl.pallas_call(
        _mm_nt_kernel,
        out_shape=jax.ShapeDtypeStruct((m, n), out_dtype),
        grid=(n // bn, m // bm),
        in_specs=[pl.BlockSpec((bm, k), lambda j, i: (i, 0)),
                  pl.BlockSpec((bn, k), lambda j, i: (j, 0))],
        out_specs=pl.BlockSpec((bm, bn), lambda j, i: (i, j)),
        compiler_params=_params("parallel", "parallel"),
        name=name,
    )(a, wt)


def _t5_bucket(rel):
    n = jnp.maximum(rel, 0)
    max_exact = NUM_BUCKETS // 2
    large = max_exact + (jnp.log(jnp.maximum(n, 1).astype(F32) / max_exact)
                         / math.log(MAX_DISTANCE / max_exact) * (NUM_BUCKETS - max_exact)).astype(jnp.int32)
    large = jnp.minimum(large, NUM_BUCKETS - 1)
    return jnp.where(n < max_exact, n, large)


def _bias_lookup(table, rel):
    onehot = (_t5_bucket(rel)[..., None] == jnp.arange(NUM_BUCKETS)).astype(F32)
    return jnp.einsum("...b,bh->h...", onehot, table.astype(F32), precision=lax.Precision.HIGHEST)


ATTN_GROUP_ORDER = (0, 2, 1, 3)


def _stack_group_rows(a):
    lead = a.shape[:-3]
    a = a.reshape(*lead, N_KV_HEADS, GQA_GROUP, BLOCK, a.shape[-1])
    a = jnp.take(a, jnp.array(ATTN_GROUP_ORDER), axis=len(lead) + 1)
    return a.reshape(*lead, N_KV_HEADS, GQA_GROUP * BLOCK, a.shape[-1])


def _attn_bias(table, nb):
    qi = jnp.arange(BLOCK)[:, None]
    ki = jnp.arange(2 * BLOCK)[None, :]
    rel_band = qi + BLOCK - ki
    blk3 = jnp.arange(3)[:, None, None]
    mask_band = (rel_band >= 0) & (rel_band < WINDOW) & ((blk3 - 1) * BLOCK + ki >= PAD)
    band = jnp.where(mask_band[:, None], _bias_lookup(table, rel_band)[None], NEG_INF)
    blk = jnp.arange(nb)[:, None, None]
    rel_meta = blk * BLOCK + qi[None] - (PAD + jnp.arange(N_META))
    meta = jnp.where((rel_meta >= WINDOW)[None], _bias_lookup(table, rel_meta), NEG_INF)
    meta = jnp.pad(jnp.moveaxis(meta, 1, 0), ((0, 0), (0, 0), (0, 0), (PAD, 0)), constant_values=NEG_INF)
    return _stack_group_rows(band), _stack_group_rows(meta)


def _swap_halves(x):
    return pltpu.roll(x.astype(F32), HEAD_DIM, axis=1).astype(x.dtype)


def _attn_kernel(q_ref, kp_ref, kc_ref, km_ref, vp_ref, vc_ref, vm_ref, bb0_ref, bb1_ref, bm0_ref, bm1_ref, sink_ref,
                 o_ref):
    band_bias = (bb0_ref, bb1_ref)
    meta_bias = (bm0_ref, bm1_ref)
    scale = HEAD_DIM ** -0.5
    assert math.frexp(scale)[0] == 0.5 and 2 * HEAD_DIM == LANES
    lane = lax.broadcasted_iota(jnp.int32, (1, LANES), 1)
    keep = (jnp.where(lane < HEAD_DIM, scale, 0.0).astype(BF16),
            jnp.where(lane < HEAD_DIM, 0.0, scale).astype(BF16))
    low = lax.broadcasted_iota(jnp.int32, (2 * BLOCK, LANES), 1) < HEAD_DIM
    chains = []
    for blk in range(ATTN_STEP_BLOCKS):
        qrows = slice(blk * BLOCK, (blk + 1) * BLOCK)
        for col in range(N_KV_HEADS // 2):
            kcols = slice(col * LANES, (col + 1) * LANES)
            k_prev = kp_ref[:, kcols] if blk == 0 else kc_ref[(blk - 1) * BLOCK:blk * BLOCK, kcols]
            v_prev = vp_ref[:, kcols] if blk == 0 else vc_ref[(blk - 1) * BLOCK:blk * BLOCK, kcols]
            k_nat = jnp.concatenate([k_prev, kc_ref[qrows, kcols], km_ref[:, kcols]], axis=0)
            v_nat = jnp.concatenate([v_prev, vc_ref[qrows, kcols], vm_ref[:, kcols]], axis=0)
            keys = (k_nat, _swap_halves(k_nat))
            vals = (v_nat, _swap_halves(v_nat))
            for half in range(2):
                h = 2 * col + half
                q0 = h * GQA_GROUP * HEAD_DIM
                q2 = jnp.concatenate([q_ref[qrows, q0:q0 + LANES], q_ref[qrows, q0 + LANES:q0 + 2 * LANES]], axis=0)
                for lane_half in range(2):
                    which = 0 if lane_half == half else 1
                    chains.append((blk, h, lane_half, _dot_nt(q2 * keep[lane_half], keys[which]), vals[which]))
    probs = []
    for blk, h, lane_half, s, _ in chains:
        rows = slice(lane_half * 2 * BLOCK, (lane_half + 1) * 2 * BLOCK)
        s0 = s[:, :BLOCK] + band_bias[blk][0, h, rows, :BLOCK]
        s1 = s[:, BLOCK:2 * BLOCK] + band_bias[blk][0, h, rows, BLOCK:]
        s2 = s[:, 2 * BLOCK:] + meta_bias[blk][0, h, rows, :]
        sink = sink_ref[h, rows, :]
        mx = jnp.maximum(jnp.maximum(jnp.maximum(s0, s1), s2).max(-1, keepdims=True), sink)
        p0 = jnp.exp(s0 - mx)
        p1 = jnp.exp(s1 - mx)
        p2 = jnp.exp(s2 - mx)
        den = (p0 + p1 + p2).sum(-1, keepdims=True) + jnp.exp(sink - mx)
        probs.append((jnp.concatenate([p0, p1, p2], axis=1).astype(BF16), 1.0 / den))
    outs = [_dot(p, chain[4]) * rden for (p, rden), chain in zip(probs, chains)]
    out_rows = []
    for blk in range(ATTN_STEP_BLOCKS):
        out_cols = []
        for h in range(N_KV_HEADS):
            first = (blk * N_KV_HEADS + h) * 2
            o = jnp.where(low, outs[first], outs[first + 1]).astype(o_ref.dtype)
            out_cols += [o[:BLOCK], o[BLOCK:]]
        out_rows.append(jnp.concatenate(out_cols, axis=1))
    o_ref[...] = jnp.concatenate(out_rows, axis=0)


def _attention(z, band, meta, sinks, b, seq):
    m = z.shape[0]
    per_seq = seq // BLOCK
    n_real = b * per_seq
    nb = ATTN_STEP_BLOCKS
    assert nb == 2 and per_seq % nb == 0 and (m // BLOCK) % nb == 0
    kcol, vcol = ZA_AK // KV_WIDTH, ZA_AV // KV_WIDTH
    sink_col = _stack_group_rows(jnp.broadcast_to(sinks.astype(F32)[:, None, None], (N_Q_HEADS, BLOCK, LANES)))
    first_blk = lambda s: s * nb
    prev = lambda s: jnp.where(first_blk(s) < n_real,
                               jnp.where(first_blk(s) % per_seq == 0, n_real, first_blk(s) - 1), first_blk(s))
    query_block = lambda s, j: jnp.where(first_blk(s) + j < n_real, (first_blk(s) + j) % per_seq + 1, 0)
    kv_one = lambda col, f: pl.BlockSpec((BLOCK, KV_WIDTH), lambda s: (f(s), col))
    kv_cur = lambda col: pl.BlockSpec((nb * BLOCK, KV_WIDTH), lambda s: (s, col))
    band_spec = lambda j: pl.BlockSpec((1, N_KV_HEADS, GQA_GROUP * BLOCK, 2 * BLOCK),
                                       lambda s: (jnp.minimum(query_block(s, j), 2), 0, 0, 0))
    meta_spec = lambda j: pl.BlockSpec((1, N_KV_HEADS, GQA_GROUP * BLOCK, BLOCK),
                                       lambda s: (query_block(s, j), 0, 0, 0))
    return pl.pallas_call(
        _attn_kernel,
        out_shape=jax.ShapeDtypeStruct((m, ATTN_WIDTH), BF16),
        grid=(m // (nb * BLOCK),),
        in_specs=[pl.BlockSpec((nb * BLOCK, ATTN_WIDTH), lambda s: (s, ZA_AQ // ATTN_WIDTH)),
                  kv_one(kcol, prev), kv_cur(kcol), kv_one(kcol, lambda s: n_real),
                  kv_one(vcol, prev), kv_cur(vcol), kv_one(vcol, lambda s: n_real),
                  band_spec(0), band_spec(1), meta_spec(0), meta_spec(1),
                  pl.BlockSpec((N_KV_HEADS, GQA_GROUP * BLOCK, LANES), lambda s: (0, 0, 0))],
        out_specs=pl.BlockSpec((nb * BLOCK, ATTN_WIDTH), lambda s: (s, 0)),
        compiler_params=_params("parallel"),
        name="swa_attention",
    )(z, z, z, z, z, z, z, band, band, meta, meta, sink_col)


def _mlstm_chunk(qk_in, v_in, gate_in, mo_in, prefix_chunk, cw_ref, cb_ref, gb_ref, ng_ref, xa_ref, ct_ref, n_ref, m_ref):
    L = ML_CHUNK
    ii = lax.broadcasted_iota(jnp.int32, (L, L), 0)
    jj = lax.broadcasted_iota(jnp.int32, (L, L), 1)
    causal = jj <= ii
    lane = lax.broadcasted_iota(jnp.int32, (L, LANES), 1)
    k_scale = ML_QK_DIM ** -0.5
    xa_ref[CONV_HALO:CONV_HALO + L, :] = qk_in
    window = xa_ref[...]
    sel_row = lax.broadcasted_iota(jnp.int32, (L, CONV_HALO + L), 0)
    sel_col = lax.broadcasted_iota(jnp.int32, (L, CONV_HALO + L), 1)
    acc = cb_ref[...] + cw_ref[CONV_WIDTH - 1:CONV_WIDTH, :] * qk_in.astype(F32)
    for j in range(CONV_WIDTH - 1):
        back = CONV_WIDTH - 1 - j
        shift = jnp.where(sel_col == sel_row + (CONV_HALO - back), 1.0, 0.0).astype(BF16)
        acc = acc + cw_ref[j:j + 1, :] * _dot(shift, window)
    qk = acc * _sigmoid(acc)
    xa_ref[0:CONV_HALO, :] = xa_ref[L:L + CONV_HALO, :]
    gpb = gate_in + gb_ref[...]
    log_sig = jnp.minimum(gpb, 0.0) - jnp.log1p(jnp.exp(-jnp.abs(gpb)))
    if prefix_chunk:
        valid = lax.broadcasted_iota(jnp.int32, (L, LANES), 0) >= PAD
        gx = jnp.where(lane < ML_HEADS, jnp.where(valid, gpb, NEG_INF), jnp.where(valid, log_sig, 0.0))
    else:
        gx = jnp.where(lane < ML_HEADS, gpb, log_sig)
    gxt = gx.T
    outs = []
    for h in range(ML_HEADS):
        q = qk[:, h * ML_QK_DIM:(h + 1) * ML_QK_DIM].astype(BF16)
        k = qk[:, ML_QK_WIDTH + h * ML_QK_DIM:ML_QK_WIDTH + (h + 1) * ML_QK_DIM] * k_scale
        v = v_in[:, h * ML_V_DIM:(h + 1) * ML_V_DIM]
        ig_col = gx[:, h:h + 1]
        lf_col = gx[:, ML_HEADS + h:ML_HEADS + h + 1]
        ig_row = gxt[h:h + 1, :]
        lf_row = gxt[ML_HEADS + h:ML_HEADS + h + 1, :]
        b_col = jnp.sum(jnp.where(causal, lf_row, 0.0), axis=-1, keepdims=True)
        b_row = jnp.sum(jnp.where(ii <= jj, lf_col, 0.0), axis=0, keepdims=True)
        m_prev = m_ref[h:h + 1, 0:1]
        log_d = jnp.where(causal, b_col - b_row + ig_row, NEG_INF)
        m_inter = b_col + m_prev
        m_out = jnp.maximum(m_inter, log_d.max(-1, keepdims=True))
        d = jnp.exp(log_d - m_out)
        inter = jnp.exp(m_inter - m_out)
        ct = ct_ref[h]
        n_prev = n_ref[h:h + 1, :]
        qk_scores = _dot_nt(q, k.astype(BF16))
        q_state = _dot(q, ct.astype(BF16))
        q_norm = jnp.sum(q.astype(F32) * n_prev, axis=-1, keepdims=True)
        b_last = b_col[L - 1:L, :]
        log_w = b_last - b_col + ig_col
        m_new = jnp.maximum(b_last + m_prev, log_w.max(0, keepdims=True))
        decay = jnp.exp(b_last + m_prev - m_new)
        kw = k * jnp.exp(log_w - m_new)
        s = qk_scores * d
        vcols = slice(h * ML_V_DIM, (h + 1) * ML_V_DIM)
        num = _dot(s.astype(BF16), v) + inter * q_state
        den = s.sum(-1, keepdims=True) + inter * q_norm
        hh = num / jnp.maximum(jnp.abs(den), jnp.exp(-m_out))
        ct_ref[h] = decay * ct + _dot_tn(kw.astype(BF16), v)
        n_ref[h:h + 1, :] = decay * n_prev + kw.sum(0, keepdims=True)
        m_ref[h:h + 1, :] = jnp.broadcast_to(m_new, (1, LANES))
        hn = hh * lax.rsqrt(jnp.mean(hh * hh, axis=-1, keepdims=True) + EPS) * ng_ref[:, vcols]
        outs.append(_sigmoid(mo_in[:, vcols].astype(F32)) * hn)
    return jnp.concatenate(outs, axis=1)


def _mlstm_kernel(blocks_per_seq, q_ref, k_ref, va_ref, vb_ref, h_ref, mo_ref, wgate_ref, cw_ref, cb_ref, gb_ref, ng_ref,
                  o_ref, gate_ref, xa_ref, ct_ref, n_ref, m_ref, xa0_ref, ct0_ref, n0_ref, m0_ref):
    L = ML_CHUNK
    step = pl.program_id(0)
    params = (cw_ref, cb_ref, gb_ref, ng_ref, xa_ref, ct_ref, n_ref, m_ref)
    gate_ref[...] = _dot_nt(h_ref[...], wgate_ref[...])

    def chunk_out(rows, prefix_chunk):
        qk = jnp.concatenate([q_ref[rows, :], k_ref[rows, :]], axis=1)
        v = jnp.concatenate([va_ref[rows, :], vb_ref[rows, :]], axis=1)
        return _mlstm_chunk(qk, v, gate_ref[rows, :], mo_ref[rows, :], prefix_chunk, *params).astype(o_ref.dtype)

    @pl.when(step == 0)
    def _():
        xa_ref[0:CONV_HALO, :] = jnp.zeros((CONV_HALO, 2 * ML_QK_WIDTH), xa_ref.dtype)
        ct_ref[...] = jnp.zeros_like(ct_ref)
        n_ref[...] = jnp.zeros_like(n_ref)
        m_ref[...] = jnp.zeros_like(m_ref)
        o_ref[0:L, :] = chunk_out(slice(0, L), True)
        o_ref[L:, :] = jnp.zeros((o_ref.shape[0] - L, o_ref.shape[1]), o_ref.dtype)
        xa0_ref[...] = xa_ref[0:CONV_HALO, :]
        ct0_ref[...] = ct_ref[...]
        n0_ref[...] = n_ref[...]
        m0_ref[...] = m_ref[...]

    @pl.when(step > 0)
    def _():
        @pl.when((step - 1) % blocks_per_seq == 0)
        def _():
            xa_ref[0:CONV_HALO, :] = xa0_ref[...]
            ct_ref[...] = ct0_ref[...]
            n_ref[...] = n0_ref[...]
            m_ref[...] = m0_ref[...]

        def chunk(c, carry):
            rows = pl.ds(pl.multiple_of(c * L, L), L)
            o_ref[rows, :] = chunk_out(rows, False)
            return carry

        lax.fori_loop(0, TAIL_ROWS // L, chunk, 0)


def _mlstm(za, zb, h, w_gates, conv_w, conv_b, igate_b, fgate_b, norm_g, b, seq):
    m = za.shape[0]
    assert seq % TAIL_ROWS == 0 and TAIL_ROWS % ML_CHUNK == 0
    n_seq_blocks = b * seq // TAIL_ROWS
    gate_bias = jnp.zeros((1, LANES), F32).at[0, :ML_HEADS].set(igate_b).at[0, ML_HEADS:2 * ML_HEADS].set(fgate_b)
    row_blk = lambda s: jnp.where(s == 0, n_seq_blocks, s - 1)
    rows = lambda w, c: pl.BlockSpec((TAIL_ROWS, w), lambda s: (row_blk(s), c))
    full = lambda r, c: pl.BlockSpec((r, c), lambda s: (0, 0))
    return pl.pallas_call(
        functools.partial(_mlstm_kernel, seq // TAIL_ROWS),
        out_shape=jax.ShapeDtypeStruct((m, ML_V_WIDTH), BF16),
        grid=(n_seq_blocks + 1,),
        in_specs=[rows(ML_QK_WIDTH, ZA_MQ // ML_QK_WIDTH), rows(ML_QK_WIDTH, ZA_MK // ML_QK_WIDTH),
                  rows(ML_V_WIDTH // 2, ZA_MV // (ML_V_WIDTH // 2)), rows(ML_V_WIDTH // 2, ZA_MV // (ML_V_WIDTH // 2) + 1),
                  rows(D_MODEL, 0), rows(ML_V_WIDTH, ZB_MO // ML_V_WIDTH), full(LANES, D_MODEL),
                  full(CONV_WIDTH, 2 * ML_QK_WIDTH), full(1, 2 * ML_QK_WIDTH), full(1, LANES), full(1, ML_V_WIDTH)],
        out_specs=rows(ML_V_WIDTH, 0),
        scratch_shapes=[pltpu.VMEM((TAIL_ROWS, LANES), F32),
                        pltpu.VMEM((ML_CHUNK + CONV_HALO, 2 * ML_QK_WIDTH), BF16),
                        pltpu.VMEM((ML_HEADS, ML_QK_DIM, ML_V_DIM), F32),
                        pltpu.VMEM((SUBLANES, LANES), F32),
                        pltpu.VMEM((SUBLANES, LANES), F32),
                        pltpu.VMEM((CONV_HALO, 2 * ML_QK_WIDTH), BF16),
                        pltpu.VMEM((ML_HEADS, ML_QK_DIM, ML_V_DIM), F32),
                        pltpu.VMEM((SUBLANES, LANES), F32),
                        pltpu.VMEM((SUBLANES, LANES), F32)],
        compiler_params=_params("arbitrary"),
        name="mlstm",
    )(za, za, za, za, h, zb, w_gates, conv_w.astype(F32), conv_b.reshape(1, -1).astype(F32), gate_bias,
      norm_g.reshape(1, -1).astype(F32))


def _merge_kernel(n_routed, n_body_tiles, attn_ref, ml_ref, ga0_ref, ga1_ref, gm0_ref, gm1_ref, *rest):
    with_router = n_routed is not None
    if n_body_tiles is None:
        xs_ref, wa_ref, wm_ref, wo_ref, g_ref, *rest = rest
        residual = xs_ref[...]
    else:
        xs_ref, tail_ref, wa_ref, wm_ref, wo_ref, g_ref, *rest = rest
        residual = _split_rows(pl.program_id(0), n_body_tiles, xs_ref, tail_ref)
    if with_router:
        wr_hi_ref, wr_lo_ref, br_ref, xs_out, h_out, route_out, counts_out = rest
    else:
        xs_out, h_out = rest
    a = _dot(attn_ref[...], wa_ref[...])
    m = _dot(ml_ref[...], wm_ref[...])
    ga = jnp.concatenate([ga0_ref[...], ga1_ref[...]], axis=1)
    gm = jnp.concatenate([gm0_ref[...], gm1_ref[...]], axis=1)
    y = _sigmoid(ga.astype(F32)) * a + _sigmoid(gm.astype(F32)) * m
    xs_new = residual + _dot(y.astype(BF16), wo_ref[...])
    xs_out[...] = xs_new
    hn = _rms(xs_new, g_ref[...])
    if not with_router:
        h_out[...] = hn.astype(h_out.dtype)
    else:
        half = D_MODEL // 2
        lo = lax.bitcast_convert_type(hn[:, :half].astype(BF16).astype(F32), jnp.uint32) >> 16
        hi = lax.bitcast_convert_type(hn[:, half:].astype(BF16).astype(F32), jnp.uint32) & jnp.uint32(0xFFFF0000)
        h_out[...] = lo | hi
        hn_hi = hn.astype(BF16)
        hn_lo = (hn - hn_hi.astype(F32)).astype(BF16)
        logits = (_dot(hn_hi, wr_hi_ref[...]) + (_dot(hn_lo, wr_hi_ref[...]) + _dot(hn_hi, wr_lo_ref[...]))
                  + br_ref[...])
        lane = lax.broadcasted_iota(jnp.int32, logits.shape, 1).astype(F32)
        l1 = logits.max(-1, keepdims=True)
        i1 = jnp.min(jnp.where(logits == l1, lane, float(LANES)), axis=-1, keepdims=True)
        rest_logits = jnp.where(lane == i1, NEG_INF, logits)
        l2 = rest_logits.max(-1, keepdims=True)
        i2 = jnp.min(jnp.where(rest_logits == l2, lane, float(LANES)), axis=-1, keepdims=True)
        e = jnp.exp(l2 - l1)
        w1 = 1.0 / (1.0 + e)
        w2 = e / (1.0 + e)
        bm = logits.shape[0]
        row = lax.broadcasted_iota(jnp.int32, (bm, 1), 0) + pl.program_id(0) * bm
        routed = row < n_routed
        pick1 = jnp.where(jnp.logical_and(lane == i1, routed), 1.0, 0.0)
        pick2 = jnp.where(jnp.logical_and(lane == i2, routed), 1.0, 0.0)
        picks = pick1 + pick2
        earlier = (lax.broadcasted_iota(jnp.int32, (bm, bm), 1) < lax.broadcasted_iota(jnp.int32, (bm, bm), 0))
        before = _dot(jnp.where(earlier, 1.0, 0.0).astype(BF16), picks.astype(BF16))
        r1 = jnp.sum(before * pick1, axis=-1, keepdims=True)
        r2 = jnp.sum(before * pick2, axis=-1, keepdims=True)
        route_out[...] = jnp.where(lane == 0, i1, jnp.where(lane == 1, i2, jnp.where(lane == 2, w1,
                                   jnp.where(lane == 3, w2, jnp.where(lane == 4, r1, jnp.where(lane == 5, r2, 0.0))))))
        counts_out[0] = jnp.broadcast_to(jnp.sum(picks, axis=0, keepdims=True), (SUBLANES, LANES))


def _merge(attn, ml, zb, xs, wa, wm, wo, g_next, router=None):
    m = zb.shape[0]
    bm = _tile(TAIL_ROWS, 256)
    row = lambda w, c: pl.BlockSpec((bm, w), lambda i: (i, c))
    const = lambda r, c: pl.BlockSpec((r, c), lambda i: (0, 0), pipeline_mode=pl.Buffered(1))
    if isinstance(xs, tuple):
        n_body_tiles = xs[0].shape[0] // bm
        xs_specs, xs_args = _split_specs(bm, n_body_tiles), list(xs)
    else:
        n_body_tiles = None
        xs_specs, xs_args = [row(D_MODEL, 0)], [xs]
    in_specs = [row(ATTN_WIDTH, 0), row(ML_V_WIDTH, 0),
                row(HALF_D, ZB_GA // HALF_D), row(HALF_D, ZB_GA // HALF_D + 1),
                row(HALF_D, ZB_GM // HALF_D), row(HALF_D, ZB_GM // HALF_D + 1),
                *xs_specs, const(ATTN_WIDTH, D_MODEL), const(ML_V_WIDTH, D_MODEL), const(D_MODEL, D_MODEL),
                const(1, D_MODEL)]
    args = [attn, ml, zb, zb, zb, zb, *xs_args, wa, wm, wo, g_next.reshape(1, D_MODEL)]
    if router is None:
        out_shape = [jax.ShapeDtypeStruct((m, D_MODEL), F32), jax.ShapeDtypeStruct((m, D_MODEL), BF16)]
        out_specs = [row(D_MODEL, 0), row(D_MODEL, 0)]
    else:
        out_shape = [jax.ShapeDtypeStruct((m, D_MODEL), F32), jax.ShapeDtypeStruct((m, D_MODEL // 2), jnp.uint32)]
        out_specs = [row(D_MODEL, 0), row(D_MODEL // 2, 0)]
    n_routed = None
    if router is not None:
        w_router, b_router, n_routed = router
        wr = jnp.zeros((D_MODEL, LANES), F32).at[:, :N_EXPERTS].set(w_router.astype(F32))
        br = jnp.full((1, LANES), NEG_INF, F32).at[0, :N_EXPERTS].set(b_router.astype(F32))
        wr_hi = wr.astype(BF16)
        wr_lo = (wr - wr_hi.astype(F32)).astype(BF16)
        in_specs += [const(D_MODEL, LANES), const(D_MODEL, LANES), const(1, LANES)]
        args += [wr_hi, wr_lo, br]
        out_shape += [jax.ShapeDtypeStruct((m, LANES), F32), jax.ShapeDtypeStruct((m // bm, SUBLANES, LANES), F32)]
        out_specs += [row(LANES, 0), pl.BlockSpec((1, SUBLANES, LANES), lambda i: (i, 0, 0))]
    return pl.pallas_call(
        functools.partial(_merge_kernel, n_routed, n_body_tiles),
        out_shape=out_shape,
        grid=(m // bm,),
        in_specs=in_specs,
        out_specs=out_specs,
        compiler_params=_params("parallel"),
        name="merge_out_proj",
    )(*args)


def _ffn_up_kernel(h_ref, wg_ref, wu_ref, act_ref):
    hb = h_ref[...]
    g = _dot(hb, wg_ref[...])
    u = _dot(hb, wu_ref[...])
    act_ref[...] = (g * _sigmoid(g) * u).astype(act_ref.dtype)


def _ffn_down_kernel(act_ref, xs_ref, wd_ref, g_ref, xs_out, h_out, rows_ref):
    j = pl.program_id(1)
    xs_new = xs_ref[...] + _dot(act_ref[...], wd_ref[...])
    xs_out[...] = xs_new
    rows_ref[j] = xs_new

    @pl.when(j == pl.num_programs(1) - 1)
    def _():
        full = jnp.concatenate([rows_ref[t] for t in range(rows_ref.shape[0])], axis=1)
        h_out[...] = _rms(full, g_ref[...]).astype(h_out.dtype)


def _dense_ffn(h, xs, wg, wu, wd, g_next):
    m = xs.shape[0]
    bm = _tile(m, 1024)
    bf = D_FF // FFN_UP_TILES
    act = pl.pallas_call(
        _ffn_up_kernel,
        out_shape=jax.ShapeDtypeStruct((m, D_FF), BF16),
        grid=(FFN_UP_TILES, m // bm),
        in_specs=[pl.BlockSpec((bm, D_MODEL), lambda f, i: (i, 0)),
                  pl.BlockSpec((D_MODEL, bf), lambda f, i: (0, f)),
                  pl.BlockSpec((D_MODEL, bf), lambda f, i: (0, f))],
        out_specs=pl.BlockSpec((bm, bf), lambda f, i: (i, f)),
        compiler_params=_params("parallel", "parallel"),
        name="dense_swiglu_up",
    )(h, wg, wu)
    bn = D_MODEL // FFN_DOWN_TILES
    return pl.pallas_call(
        _ffn_down_kernel,
        out_shape=[jax.ShapeDtypeStruct((m, D_MODEL), F32), jax.ShapeDtypeStruct((m, D_MODEL), BF16)],
        grid=(m // bm, FFN_DOWN_TILES),
        in_specs=[pl.BlockSpec((bm, D_FF), lambda i, j: (i, 0)),
                  pl.BlockSpec((bm, bn), lambda i, j: (i, j)),
                  pl.BlockSpec((D_FF, bn), lambda i, j: (0, j)),
                  pl.BlockSpec((1, D_MODEL), lambda i, j: (0, 0))],
        out_specs=[pl.BlockSpec((bm, bn), lambda i, j: (i, j)),
                   pl.BlockSpec((bm, D_MODEL), lambda i, j: (i, 0))],
        scratch_shapes=[pltpu.VMEM((FFN_DOWN_TILES, bm, bn), F32)],
        compiler_params=_params("parallel", "arbitrary"),
        name="dense_swiglu_down",
    )(act, xs, wd, g_next.reshape(1, D_MODEL))


def _row_copy(src_hbm, dst_vmem, sem, src_row, dst_row):
    return pltpu.make_async_copy(src_hbm.at[pl.ds(src_row, 1), :], dst_vmem.at[pl.ds(dst_row, 1), :], sem)


def _moe_gather_copy(h_hbm, gbuf, sem, src_row, dst_row):
    return pltpu.make_async_copy(h_hbm.at[pl.ds(src_row, 1), :], gbuf.at[pl.ds(dst_row, 1), :], sem)


def _moe_gather_wait(h_hbm, gbuf, sem):
    pltpu.make_async_copy(h_hbm.at[pl.ds(0, MOE_GATHER_ROWS), :], gbuf, sem).wait()


def _moe_kernel(sb_e_ref, sb_rows_ref, row_tok_ref, h_hbm, wg_ref, wu_ref, wd_ref, y_ref, gbuf, xb16, sem):
    sb = pl.program_id(0)
    f = pl.program_id(1)
    n_sb = pl.num_programs(0)
    nf = pl.num_programs(1)
    kind = sb_rows_ref[sb]
    half = D_MODEL // 2

    @pl.when(jnp.logical_and(sb == 0, f == 0))
    def _():
        def issue(r, c):
            _moe_gather_copy(h_hbm, gbuf, sem, row_tok_ref[r], r).start()
            return c
        lax.fori_loop(0, MOE_GATHER_ROWS, issue, 0)

    @pl.when(f == 0)
    def _():
        y_ref[...] = jnp.zeros_like(y_ref)
        prev_kind = sb_rows_ref[jnp.maximum(sb - 1, 0)]

        @pl.when(jnp.logical_or(sb == 0, prev_kind > 0))
        def _():
            _moe_gather_wait(h_hbm, gbuf, sem)

        @pl.when(kind > 0)
        def _():
            w = gbuf[0:MOE_SB, :]
            xb16[:, :half] = lax.bitcast_convert_type(w << 16, F32).astype(BF16)
            xb16[:, half:] = lax.bitcast_convert_type(w & jnp.uint32(0xFFFF0000), F32).astype(BF16)

    def compute(row0, rows, prefetch):
        if prefetch:
            base = (sb + 1) * MOE_SB + f * MOE_STEP_ROWS
            for r in range(MOE_STEP_ROWS):
                _moe_gather_copy(h_hbm, gbuf, sem, row_tok_ref[base + r], f * MOE_STEP_ROWS + r).start()
        xb = xb16[row0:row0 + rows, :]
        g = _dot(xb, wg_ref[0].astype(BF16))
        u = _dot(xb, wu_ref[0].astype(BF16))
        act = (g * _sigmoid(g) * u).astype(BF16)
        y_ref[row0:row0 + rows, :] += _dot(act, wd_ref[0].astype(BF16))

    @pl.when(kind == MOE_PARTS)
    def _():
        compute(0, MOE_SB, True)

    for part in range(MOE_PARTS - 1):
        @pl.when(jnp.logical_and(kind > part, kind < MOE_PARTS))
        def _():
            compute(part * MOE_PART_ROWS, MOE_PART_ROWS, part == 0)

    @pl.when(jnp.logical_and(jnp.logical_and(sb == n_sb - 1, f == nf - 1), kind > 0))
    def _():
        _moe_gather_wait(h_hbm, gbuf, sem)


def _moe_experts(h_packed, sb_e, sb_rows, row_tok, wg, wu, wd, n_sb, bf=MOE_FF_TILE):
    nf = D_FF // bf
    assert nf * MOE_STEP_ROWS == MOE_GATHER_ROWS
    ftile = lambda i, f, rows: jnp.where(rows[i] > 0, f, nf - 1)
    grid_spec = pltpu.PrefetchScalarGridSpec(
        num_scalar_prefetch=3,
        grid=(n_sb, nf),
        in_specs=[pl.BlockSpec(memory_space=pl.ANY),
                  pl.BlockSpec((1, D_MODEL, bf), lambda i, f, se, sr, rt: (se[i], 0, ftile(i, f, sr))),
                  pl.BlockSpec((1, D_MODEL, bf), lambda i, f, se, sr, rt: (se[i], 0, ftile(i, f, sr))),
                  pl.BlockSpec((1, bf, D_MODEL), lambda i, f, se, sr, rt: (se[i], ftile(i, f, sr), 0))],
        out_specs=pl.BlockSpec((MOE_SB, D_MODEL), lambda i, f, se, sr, rt: (i, 0)),
        scratch_shapes=[pltpu.VMEM((MOE_GATHER_ROWS, D_MODEL // 2), jnp.uint32),
                        pltpu.VMEM((MOE_SB, D_MODEL), BF16),
                        pltpu.SemaphoreType.DMA(())],
    )
    return pl.pallas_call(
        _moe_kernel,
        out_shape=jax.ShapeDtypeStruct((n_sb * MOE_SB, D_MODEL), F32),
        grid_spec=grid_spec,
        compiler_params=pltpu.CompilerParams(dimension_semantics=("arbitrary", "arbitrary"),
                                             vmem_limit_bytes=MOE_VMEM_LIMIT),
        name="moe_experts",
    )(sb_e, sb_rows, row_tok, h_packed, wg, wu, wd)


def _combine_kernel(pos_ref, xs_ref, route_ref, y_hbm, g_ref, o_ref, ybuf, sem):
    step = pl.program_id(0)
    n_steps = pl.num_programs(0)
    slot = step % 2

    rows = o_ref.shape[0]

    def gather(s, dst_slot):
        base = s * rows

        def issue(r, c):
            for k in range(TOP_K):
                _row_copy(y_hbm, ybuf.at[dst_slot, k], sem.at[dst_slot], pos_ref[TOP_K * (base + r) + k], r).start()
            return c
        lax.fori_loop(0, rows, issue, 0, unroll=4)

    @pl.when(step == 0)
    def _():
        gather(step, slot)

    @pl.when(step + 1 < n_steps)
    def _():
        gather(step + 1, 1 - slot)

    for k in range(TOP_K):
        pltpu.make_async_copy(y_hbm.at[pl.ds(0, rows), :], ybuf.at[slot, k], sem.at[slot]).wait()
    route = route_ref[...]
    moe = ybuf[slot, 0] * route[:, 2:3] + ybuf[slot, 1] * route[:, 3:4]
    o_ref[...] = _rms(xs_ref[...] + moe, g_ref[...])


def _moe_combine(pos, xs, route, yb, g_final, m_real):
    rows = _tile(m_real, 512)
    grid_spec = pltpu.PrefetchScalarGridSpec(
        num_scalar_prefetch=1,
        grid=(m_real // rows,),
        in_specs=[pl.BlockSpec((rows, D_MODEL), lambda t, p: (t, 0)),
                  pl.BlockSpec((rows, LANES), lambda t, p: (t, 0)),
                  pl.BlockSpec(memory_space=pl.ANY),
                  pl.BlockSpec((1, D_MODEL), lambda t, p: (0, 0))],
        out_specs=pl.BlockSpec((rows, D_MODEL), lambda t, p: (t, 0)),
        scratch_shapes=[pltpu.VMEM((2, TOP_K, rows, D_MODEL), F32), pltpu.SemaphoreType.DMA((2,))],
    )
    return pl.pallas_call(
        _combine_kernel,
        out_shape=jax.ShapeDtypeStruct((m_real, D_MODEL), F32),
        grid_spec=grid_spec,
        compiler_params=_params("arbitrary"),
        name="moe_combine_final_norm",
    )(pos, xs, route, yb, g_final.reshape(1, D_MODEL))


def _moe_routing(route, tile_counts, n_tok):
    n_assign = n_tok * TOP_K
    n_sb = n_assign // MOE_SB + N_EXPERTS
    rows_per_tile = route.shape[0] // tile_counts.shape[0]
    e_flat = route[:n_tok, :TOP_K].astype(jnp.int32).reshape(n_assign)
    onehot = (e_flat[:, None] == jnp.arange(N_EXPERTS, dtype=jnp.int32)[None, :]).astype(jnp.int32)
    tile_counts = tile_counts[:, 0, :N_EXPERTS].astype(jnp.int32)
    tile_base = jnp.cumsum(tile_counts, axis=0) - tile_counts
    base = jnp.repeat(tile_base, rows_per_tile * TOP_K, axis=0)[:n_assign]
    rank = jnp.sum(base * onehot, axis=1) + route[:n_tok, 4:4 + TOP_K].astype(jnp.int32).reshape(n_assign)
    counts = jnp.sum(tile_counts, axis=0)
    sb_count = (counts + MOE_SB - 1) // MOE_SB
    sb_end = jnp.cumsum(sb_count)
    sb_start = sb_end - sb_count
    dest = (jnp.sum((sb_start * MOE_SB)[None, :] * onehot, axis=1) + rank).astype(jnp.int32)
    row_tok = jnp.zeros(((n_sb + 2) * MOE_SB,), jnp.int32).at[dest].set(
        jnp.arange(n_assign, dtype=jnp.int32) // TOP_K, unique_indices=True)
    sb = jnp.arange(n_sb, dtype=jnp.int32)
    sb_e = jnp.sum((sb[:, None] >= sb_end[None, :]).astype(jnp.int32), axis=1)
    valid = sb < sb_end[-1]
    last_e = jnp.sum((sb_end[-1] - 1 >= sb_end).astype(jnp.int32))
    sb_e = jnp.where(valid, sb_e, last_e).astype(jnp.int32)
    rows_here = counts[sb_e] - (sb - sb_start[sb_e]) * MOE_SB
    parts = (jnp.minimum(rows_here, MOE_SB) + MOE_PART_ROWS - 1) // MOE_PART_ROWS
    sb_rows = jnp.where(valid, parts, 0).astype(jnp.int32)
    return dest, row_tok, sb_e, sb_rows, n_sb


def _split_w_in(w):
    n_gates = 2 * ML_HEADS
    assert w.shape[1] == ZA_WIDTH + n_gates + ZB_WIDTH
    wt = jnp.swapaxes(w, 0, 1)
    part_a = wt[:W_IN_GATES].astype(BF16)
    part_b = wt[W_IN_GATES + n_gates:].astype(BF16)
    gates = jnp.pad(wt[W_IN_GATES:W_IN_GATES + n_gates], ((0, LANES - n_gates), (0, 0))).astype(BF16)
    return part_a, part_b, gates


def kernel(x, meta_tokens, rel_bias_table, w_in, attn_sinks, conv_w, conv_b, igate_b, fgate_b, mlstm_norm_g,
           w_attn_up, w_mlstm_up, w_out, norm_mix_g, norm_ffn_g, w_ffn_gate, w_ffn_up, w_ffn_down, w_router,
           b_router, w_moe_gate, w_moe_up, w_moe_down, final_norm_g):
    b, seq, _ = x.shape
    depth = w_in.shape[0]
    m_real = b * seq
    m = m_real + TAIL_ROWS
    assert depth == 2 and seq % ML_CHUNK == 0 and seq % BLOCK == 0 and m_real % TAIL_ROWS == 0
    tail = jnp.concatenate([jnp.zeros((PAD, D_MODEL), x.dtype), meta_tokens.astype(x.dtype),
                            jnp.zeros((TAIL_ROWS - PREFIX, D_MODEL), x.dtype)], axis=0)
    xs = (x.reshape(m_real, D_MODEL), tail)
    band, meta = _attn_bias(rel_bias_table, seq // BLOCK + 1)
    h = _rmsnorm(*xs, norm_mix_g[0])
    out = None
    for layer in range(depth):
        w_a, w_b, w_gates = _split_w_in(w_in[layer])
        za = _matmul_nt(h, w_a, BF16, 1024, ZA_WIDTH // 2, "in_proj_a")
        zb = _matmul_nt(h, w_b, BF16, 1024, ZB_WIDTH // 2, "in_proj_b")
        attn = _attention(za, band, meta, attn_sinks[layer], b, seq)
        ml = _mlstm(za, zb, h, w_gates, conv_w[layer], conv_b[layer], igate_b[layer], fgate_b[layer],
                    mlstm_norm_g[layer], b, seq)
        wa, wm, wo = (w_attn_up[layer].astype(BF16), w_mlstm_up[layer].astype(BF16), w_out[layer].astype(BF16))
        i = layer // 2
        if layer % 2 == 0:
            xs, h = _merge(attn, ml, zb, xs, wa, wm, wo, norm_ffn_g[layer])
            xs, h = _dense_ffn(h, xs, w_ffn_gate[i].astype(BF16), w_ffn_up[i].astype(BF16),
                               w_ffn_down[i].astype(BF16), norm_mix_g[layer + 1])
        else:
            n_tok = m_real + PREFIX
            xs, h_packed, route, tile_counts = _merge(attn, ml, zb, xs, wa, wm, wo, norm_ffn_g[layer],
                                                      router=(w_router[i], b_router[i], n_tok))
            dest, row_tok, sb_e, sb_rows, n_sb = _moe_routing(route, tile_counts, n_tok)
            yb = _moe_experts(h_packed, sb_e, sb_rows, row_tok, w_moe_gate[i], w_moe_up[i], w_moe_down[i], n_sb)
            out = _moe_combine(dest, xs, route, yb, final_norm_g, m_real).reshape(b, seq, D_MODEL)
    return out
```

```python
import functools
import math

import jax
import jax.numpy as jnp
from jax import lax
from jax.experimental import pallas as pl
from jax.experimental.pallas import tpu as pltpu

D_MODEL = 2048
N_META = 16
BLOCK = 128
PREFIX = BLOCK
PAD = PREFIX - N_META
HEAD_DIM = 64
N_Q_HEADS = 16
N_KV_HEADS = 4
GQA_GROUP = 4
ATTN_WIDTH = N_Q_HEADS * HEAD_DIM
KV_WIDTH = N_KV_HEADS * HEAD_DIM
WINDOW = 128
NUM_BUCKETS = 32
MAX_DISTANCE = 128
ML_HEADS = 4
ML_V_WIDTH = D_MODEL // 2
ML_V_DIM = ML_V_WIDTH // ML_HEADS
ML_QK_DIM = ML_V_DIM // 2
ML_QK_WIDTH = ML_HEADS * ML_QK_DIM
CONV_WIDTH = 4
D_FF = 11 * D_MODEL // 4
N_EXPERTS = 8
TOP_K = 2
EPS = 1e-6

LANES = 128
SUBLANES = 8
BF16_ROWS = 16
VMEM_LIMIT = 56 * 1024 * 1024

ZA_AQ = 0
ZA_AK = ZA_AQ + ATTN_WIDTH
ZA_AV = ZA_AK + KV_WIDTH
ZA_MQ = ZA_AV + KV_WIDTH
ZA_MK = ZA_MQ + ML_QK_WIDTH
ZA_MV = ZA_MK + ML_QK_WIDTH
ZA_WIDTH = ZA_MV + ML_V_WIDTH
ZB_MO = 0
ZB_GA = ZB_MO + ML_V_WIDTH
ZB_GM = ZB_GA + D_MODEL
ZB_WIDTH = ZB_GM + D_MODEL
W_IN_GATES = ZA_WIDTH
HALF_D = D_MODEL // 2
TAIL_ROWS = 512
ATTN_STEP_BLOCKS = 2
ML_CHUNK = 128
CONV_HALO = BF16_ROWS
FFN_UP_TILES = 4
FFN_DOWN_TILES = 4
MOE_SB = 1024
MOE_PARTS = 4
MOE_PART_ROWS = MOE_SB // MOE_PARTS
MOE_FF_TILE = 512
MOE_STEP_ROWS = 96
MOE_GATHER_ROWS = MOE_STEP_ROWS * (D_FF // MOE_FF_TILE)
MOE_VMEM_LIMIT = 60 * 1024 * 1024

F32 = jnp.float32
BF16 = jnp.bfloat16
NEG_INF = float("-inf")


def _tile(m, target):
    best = LANES
    for t in range(LANES, min(m, target) + 1, LANES):
        if m % t == 0:
            best = t
    assert m % best == 0
    return best


def _params(*sem):
    return pltpu.CompilerParams(dimension_semantics=sem, vmem_limit_bytes=VMEM_LIMIT)


def _rms(x, g):
    return x * lax.rsqrt(jnp.mean(x * x, axis=-1, keepdims=True) + EPS) * g


def _sigmoid(x):
    return 1.0 / (1.0 + jnp.exp(-x))


def _dot(a, b):
    return jnp.dot(a, b, preferred_element_type=F32)


def _dot_nt(a, b):
    return lax.dot_general(a, b, (((1,), (1,)), ((), ())), preferred_element_type=F32)


def _dot_tn(a, b):
    return lax.dot_general(a, b, (((0,), (0,)), ((), ())), preferred_element_type=F32)


def _split_rows(i, n_body_tiles, body_ref, tail_ref):
    return jnp.where(i < n_body_tiles, body_ref[...], tail_ref[...])


def _split_specs(bm, n_body_tiles):
    return [pl.BlockSpec((bm, D_MODEL), lambda i: (jnp.minimum(i, n_body_tiles - 1), 0)),
            pl.BlockSpec((bm, D_MODEL), lambda i: (jnp.maximum(i - n_body_tiles, 0), 0))]


def _norm_kernel(n_body_tiles, x_ref, tail_ref, g_ref, o_ref):
    x = _split_rows(pl.program_id(0), n_body_tiles, x_ref, tail_ref)
    o_ref[...] = _rms(x, g_ref[...]).astype(o_ref.dtype)


def _rmsnorm(x_body, x_tail, g):
    m = x_body.shape[0] + x_tail.shape[0]
    bm = _tile(x_tail.shape[0], 512)
    assert x_body.shape[0] % bm == 0
    n_body_tiles = x_body.shape[0] // bm
    return pl.pallas_call(
        functools.partial(_norm_kernel, n_body_tiles),
        out_shape=jax.ShapeDtypeStruct((m, D_MODEL), BF16),
        grid=(m // bm,),
        in_specs=_split_specs(bm, n_body_tiles) + [pl.BlockSpec((1, D_MODEL), lambda i: (0, 0))],
        out_specs=pl.BlockSpec((bm, D_MODEL), lambda i: (i, 0)),
        compiler_params=_params("parallel"),
        name="rmsnorm",
    )(x_body, x_tail, g.reshape(1, D_MODEL))


def _mm_nt_kernel(a_ref, wt_ref, o_ref):
    o_ref[...] = _dot_nt(a_ref[...], wt_ref[...]).astype(o_ref.dtype)


def _matmul_nt(a, wt, out_dtype, bm, bn, name):
    m, k = a.shape
    n = wt.shape[0]
    bm = _tile(m, bm)
    return pl.pallas_call(
        _mm_nt_kernel,
        out_shape=jax.ShapeDtypeStruct((m, n), out_dtype),
        grid=(n // bn, m // bm),
        in_specs=[pl.BlockSpec((bm, k), lambda j, i: (i, 0)),
                  pl.BlockSpec((bn, k), lambda j, i: (j, 0))],
        out_specs=pl.BlockSpec((bm, bn), lambda j, i: (i, j)),
        compiler_params=_params("parallel", "parallel"),
        name=name,
    )(a, wt)


def _t5_bucket(rel):
    n = jnp.maximum(rel, 0)
    max_exact = NUM_BUCKETS // 2
    large = max_exact + (jnp.log(jnp.maximum(n, 1).astype(F32) / max_exact)
                         / math.log(MAX_DISTANCE / max_exact) * (NUM_BUCKETS - max_exact)).astype(jnp.int32)
    large = jnp.minimum(large, NUM_BUCKETS - 1)
    return jnp.where(n < max_exact, n, large)


def _bias_lookup(table, rel):
    onehot = (_t5_bucket(rel)[..., None] == jnp.arange(NUM_BUCKETS)).astype(F32)
    return jnp.einsum("...b,bh->h...", onehot, table.astype(F32), precision=lax.Precision.HIGHEST)


ATTN_GROUP_ORDER = (0, 2, 1, 3)


def _stack_group_rows(a):
    lead = a.shape[:-3]
    a = a.reshape(*lead, N_KV_HEADS, GQA_GROUP, BLOCK, a.shape[-1])
    a = jnp.take(a, jnp.array(ATTN_GROUP_ORDER), axis=len(lead) + 1)
    return a.reshape(*lead, N_KV_HEADS, GQA_GROUP * BLOCK, a.shape[-1])


def _attn_bias(table, nb):
    qi = jnp.arange(BLOCK)[:, None]
    ki = jnp.arange(2 * BLOCK)[None, :]
    rel_band = qi + BLOCK - ki
    blk3 = jnp.arange(3)[:, None, None]
    mask_band = (rel_band >= 0) & (rel_band < WINDOW) & ((blk3 - 1) * BLOCK + ki >= PAD)
    band = jnp.where(mask_band[:, None], _bias_lookup(table, rel_band)[None], NEG_INF)
    blk = jnp.arange(nb)[:, None, None]
    rel_meta = blk * BLOCK + qi[None] - (PAD + jnp.arange(N_META))
    meta = jnp.where((rel_meta >= WINDOW)[None], _bias_lookup(table, rel_meta), NEG_INF)
    meta = jnp.pad(jnp.moveaxis(meta, 1, 0), ((0, 0), (0, 0), (0, 0), (PAD, 0)), constant_values=NEG_INF)
    return _stack_group_rows(band), _stack_group_rows(meta)


def _swap_halves(x):
    return pltpu.roll(x.astype(F32), HEAD_DIM, axis=1).astype(x.dtype)


def _attn_kernel(q_ref, kp_ref, kc_ref, km_ref, vp_ref, vc_ref, vm_ref, bb0_ref, bb1_ref, bm0_ref, bm1_ref, sink_ref,
                 o_ref):
    band_bias = (bb0_ref, bb1_ref)
    meta_bias = (bm0_ref, bm1_ref)
    scale = HEAD_DIM ** -0.5
    assert math.frexp(scale)[0] == 0.5 and 2 * HEAD_DIM == LANES
    lane = lax.broadcasted_iota(jnp.int32, (1, LANES), 1)
    keep = (jnp.where(lane < HEAD_DIM, scale, 0.0).astype(BF16),
            jnp.where(lane < HEAD_DIM, 0.0, scale).astype(BF16))
    low = lax.broadcasted_iota(jnp.int32, (2 * BLOCK, LANES), 1) < HEAD_DIM
    chains = []
    for blk in range(ATTN_STEP_BLOCKS):
        qrows = slice(blk * BLOCK, (blk + 1) * BLOCK)
        for col in range(N_KV_HEADS // 2):
            kcols = slice(col * LANES, (col + 1) * LANES)
            k_prev = kp_ref[:, kcols] if blk == 0 else kc_ref[(blk - 1) * BLOCK:blk * BLOCK, kcols]
            v_prev = vp_ref[:, kcols] if blk == 0 else vc_ref[(blk - 1) * BLOCK:blk * BLOCK, kcols]
            k_nat = jnp.concatenate([k_prev, kc_ref[qrows, kcols], km_ref[:, kcols]], axis=0)
            v_nat = jnp.concatenate([v_prev, vc_ref[qrows, kcols], vm_ref[:, kcols]], axis=0)
            keys = (k_nat, _swap_halves(k_nat))
            vals = (v_nat, _swap_halves(v_nat))
            for half in range(2):
                h = 2 * col + half
                q0 = h * GQA_GROUP * HEAD_DIM
                q2 = jnp.concatenate([q_ref[qrows, q0:q0 + LANES], q_ref[qrows, q0 + LANES:q0 + 2 * LANES]], axis=0)
                for lane_half in range(2):
                    which = 0 if lane_half == half else 1
                    chains.append((blk, h, lane_half, _dot_nt(q2 * keep[lane_half], keys[which]), vals[which]))
    probs = []
    for blk, h, lane_half, s, _ in chains:
        rows = slice(lane_half * 2 * BLOCK, (lane_half + 1) * 2 * BLOCK)
        s0 = s[:, :BLOCK] + band_bias[blk][0, h, rows, :BLOCK]
        s1 = s[:, BLOCK:2 * BLOCK] + band_bias[blk][0, h, rows, BLOCK:]
        s2 = s[:, 2 * BLOCK:] + meta_bias[blk][0, h, rows, :]
        sink = sink_ref[h, rows, :]
        mx = jnp.maximum(jnp.maximum(jnp.maximum(s0, s1), s2).max(-1, keepdims=True), sink)
        p0 = jnp.exp(s0 - mx)
        p1 = jnp.exp(s1 - mx)
        p2 = jnp.exp(s2 - mx)
        den = (p0 + p1 + p2).sum(-1, keepdims=True) + jnp.exp(sink - mx)
        probs.append((jnp.concatenate([p0, p1, p2], axis=1).astype(BF16), 1.0 / den))
    outs = [_dot(p, chain[4]) * rden for (p, rden), chain in zip(probs, chains)]
    out_rows = []
    for blk in range(ATTN_STEP_BLOCKS):
        out_cols = []
        for h in range(N_KV_HEADS):
            first = (blk * N_KV_HEADS + h) * 2
            o = jnp.where(low, outs[first], outs[first + 1]).astype(o_ref.dtype)
            out_cols += [o[:BLOCK], o[BLOCK:]]
        out_rows.append(jnp.concatenate(out_cols, axis=1))
    o_ref[...] = jnp.concatenate(out_rows, axis=0)


def _attention(z, band, meta, sinks, b, seq):
    m = z.shape[0]
    per_seq = seq // BLOCK
    n_real = b * per_seq
    nb = ATTN_STEP_BLOCKS
    assert nb == 2 and per_seq % nb == 0 and (m // BLOCK) % nb == 0
    kcol, vcol = ZA_AK // KV_WIDTH, ZA_AV // KV_WIDTH
    sink_col = _stack_group_rows(jnp.broadcast_to(sinks.astype(F32)[:, None, None], (N_Q_HEADS, BLOCK, LANES)))
    first_blk = lambda s: s * nb
    prev = lambda s: jnp.where(first_blk(s) < n_real,
                               jnp.where(first_blk(s) % per_seq == 0, n_real, first_blk(s) - 1), first_blk(s))
    query_block = lambda s, j: jnp.where(first_blk(s) + j < n_real, (first_blk(s) + j) % per_seq + 1, 0)
    kv_one = lambda col, f: pl.BlockSpec((BLOCK, KV_WIDTH), lambda s: (f(s), col))
    kv_cur = lambda col: pl.BlockSpec((nb * BLOCK, KV_WIDTH), lambda s: (s, col))
    band_spec = lambda j: pl.BlockSpec((1, N_KV_HEADS, GQA_GROUP * BLOCK, 2 * BLOCK),
                                       lambda s: (jnp.minimum(query_block(s, j), 2), 0, 0, 0))
    meta_spec = lambda j: pl.BlockSpec((1, N_KV_HEADS, GQA_GROUP * BLOCK, BLOCK),
                                       lambda s: (query_block(s, j), 0, 0, 0))
    return pl.pallas_call(
        _attn_kernel,
        out_shape=jax.ShapeDtypeStruct((m, ATTN_WIDTH), BF16),
        grid=(m // (nb * BLOCK),),
        in_specs=[pl.BlockSpec((nb * BLOCK, ATTN_WIDTH), lambda s: (s, ZA_AQ // ATTN_WIDTH)),
                  kv_one(kcol, prev), kv_cur(kcol), kv_one(kcol, lambda s: n_real),
                  kv_one(vcol, prev), kv_cur(vcol), kv_one(vcol, lambda s: n_real),
                  band_spec(0), band_spec(1), meta_spec(0), meta_spec(1),
                  pl.BlockSpec((N_KV_HEADS, GQA_GROUP * BLOCK, LANES), lambda s: (0, 0, 0))],
        out_specs=pl.BlockSpec((nb * BLOCK, ATTN_WIDTH), lambda s: (s, 0)),
        compiler_params=_params("parallel"),
        name="swa_attention",
    )(z, z, z, z, z, z, z, band, band, meta, meta, sink_col)


def _mlstm_chunk(qk_in, v_in, gate_in, mo_in, prefix_chunk, cw_ref, cb_ref, gb_ref, ng_ref, xa_ref, ct_ref, n_ref, m_ref):
    L = ML_CHUNK
    ii = lax.broadcasted_iota(jnp.int32, (L, L), 0)
    jj = lax.broadcasted_iota(jnp.int32, (L, L), 1)
    causal = jj <= ii
    lane = lax.broadcasted_iota(jnp.int32, (L, LANES), 1)
    k_scale = ML_QK_DIM ** -0.5
    xa_ref[CONV_HALO:CONV_HALO + L, :] = qk_in
    window = xa_ref[...]
    sel_row = lax.broadcasted_iota(jnp.int32, (L, CONV_HALO + L), 0)
    sel_col = lax.broadcasted_iota(jnp.int32, (L, CONV_HALO + L), 1)
    acc = cb_ref[...] + cw_ref[CONV_WIDTH - 1:CONV_WIDTH, :] * qk_in.astype(F32)
    for j in range(CONV_WIDTH - 1):
        back = CONV_WIDTH - 1 - j
        shift = jnp.where(sel_col == sel_row + (CONV_HALO - back), 1.0, 0.0).astype(BF16)
        acc = acc + cw_ref[j:j + 1, :] * _dot(shift, window)
    qk = acc * _sigmoid(acc)
    xa_ref[0:CONV_HALO, :] = xa_ref[L:L + CONV_HALO, :]
    gpb = gate_in + gb_ref[...]
    log_sig = jnp.minimum(gpb, 0.0) - jnp.log1p(jnp.exp(-jnp.abs(gpb)))
    if prefix_chunk:
        valid = lax.broadcasted_iota(jnp.int32, (L, LANES), 0) >= PAD
        gx = jnp.where(lane < ML_HEADS, jnp.where(valid, gpb, NEG_INF), jnp.where(valid, log_sig, 0.0))
    else:
        gx = jnp.where(lane < ML_HEADS, gpb, log_sig)
    gxt = gx.T
    outs = []
    for h in range(ML_HEADS):
        q = qk[:, h * ML_QK_DIM:(h + 1) * ML_QK_DIM].astype(BF16)
        k = qk[:, ML_QK_WIDTH + h * ML_QK_DIM:ML_QK_WIDTH + (h + 1) * ML_QK_DIM] * k_scale
        v = v_in[:, h * ML_V_DIM:(h + 1) * ML_V_DIM]
        ig_col = gx[:, h:h + 1]
        lf_col = gx[:, ML_HEADS + h:ML_HEADS + h + 1]
        ig_row = gxt[h:h + 1, :]
        lf_row = gxt[ML_HEADS + h:ML_HEADS + h + 1, :]
        b_col = jnp.sum(jnp.where(causal, lf_row, 0.0), axis=-1, keepdims=True)
        b_row = jnp.sum(jnp.where(ii <= jj, lf_col, 0.0), axis=0, keepdims=True)
        m_prev = m_ref[h:h + 1, 0:1]
        log_d = jnp.where(causal, b_col - b_row + ig_row, NEG_INF)
        m_inter = b_col + m_prev
        m_out = jnp.maximum(m_inter, log_d.max(-1, keepdims=True))
        d = jnp.exp(log_d - m_out)
        inter = jnp.exp(m_inter - m_out)
        ct = ct_ref[h]
        n_prev = n_ref[h:h + 1, :]
        qk_scores = _dot_nt(q, k.astype(BF16))
        q_state = _dot(q, ct.astype(BF16))
        q_norm = jnp.sum(q.astype(F32) * n_prev, axis=-1, keepdims=True)
        b_last = b_col[L - 1:L, :]
        log_w = b_last - b_col + ig_col
        m_new = jnp.maximum(b_last + m_prev, log_w.max(0, keepdims=True))
        decay = jnp.exp(b_last + m_prev - m_new)
        kw = k * jnp.exp(log_w - m_new)
        s = qk_scores * d
        vcols = slice(h * ML_V_DIM, (h + 1) * ML_V_DIM)
        num = _dot(s.astype(BF16), v) + inter * q_state
        den = s.sum(-1, keepdims=True) + inter * q_norm
        hh = num / jnp.maximum(jnp.abs(den), jnp.exp(-m_out))
        ct_ref[h] = decay * ct + _dot_tn(kw.astype(BF16), v)
        n_ref[h:h + 1, :] = decay * n_prev + kw.sum(0, keepdims=True)
        m_ref[h:h + 1, :] = jnp.broadcast_to(m_new, (1, LANES))
        hn = hh * lax.rsqrt(jnp.mean(hh * hh, axis=-1, keepdims=True) + EPS) * ng_ref[:, vcols]
        outs.append(_sigmoid(mo_in[:, vcols].astype(F32)) * hn)
    return jnp.concatenate(outs, axis=1)


def _mlstm_kernel(blocks_per_seq, q_ref, k_ref, va_ref, vb_ref, h_ref, mo_ref, wgate_ref, cw_ref, cb_ref, gb_ref, ng_ref,
                  o_ref, gate_ref, xa_ref, ct_ref, n_ref, m_ref, xa0_ref, ct0_ref, n0_ref, m0_ref):
    L = ML_CHUNK
    step = pl.program_id(0)
    params = (cw_ref, cb_ref, gb_ref, ng_ref, xa_ref, ct_ref, n_ref, m_ref)
    gate_ref[...] = _dot_nt(h_ref[...], wgate_ref[...])

    def chunk_out(rows, prefix_chunk):
        qk = jnp.concatenate([q_ref[rows, :], k_ref[rows, :]], axis=1)
        v = jnp.concatenate([va_ref[rows, :], vb_ref[rows, :]], axis=1)
        return _mlstm_chunk(qk, v, gate_ref[rows, :], mo_ref[rows, :], prefix_chunk, *params).astype(o_ref.dtype)

    @pl.when(step == 0)
    def _():
        xa_ref[0:CONV_HALO, :] = jnp.zeros((CONV_HALO, 2 * ML_QK_WIDTH), xa_ref.dtype)
        ct_ref[...] = jnp.zeros_like(ct_ref)
        n_ref[...] = jnp.zeros_like(n_ref)
        m_ref[...] = jnp.zeros_like(m_ref)
        o_ref[0:L, :] = chunk_out(slice(0, L), True)
        o_ref[L:, :] = jnp.zeros((o_ref.shape[0] - L, o_ref.shape[1]), o_ref.dtype)
        xa0_ref[...] = xa_ref[0:CONV_HALO, :]
        ct0_ref[...] = ct_ref[...]
        n0_ref[...] = n_ref[...]
        m0_ref[...] = m_ref[...]

    @pl.when(step > 0)
    def _():
        @pl.when((step - 1) % blocks_per_seq == 0)
        def _():
            xa_ref[0:CONV_HALO, :] = xa0_ref[...]
            ct_ref[...] = ct0_ref[...]
            n_ref[...] = n0_ref[...]
            m_ref[...] = m0_ref[...]

        def chunk(c, carry):
            rows = pl.ds(pl.multiple_of(c * L, L), L)
            o_ref[rows, :] = chunk_out(rows, False)
            return carry

        lax.fori_loop(0, TAIL_ROWS // L, chunk, 0)


def _mlstm(za, zb, h, w_gates, conv_w, conv_b, igate_b, fgate_b, norm_g, b, seq):
    m = za.shape[0]
    assert seq % TAIL_ROWS == 0 and TAIL_ROWS % ML_CHUNK == 0
    n_seq_blocks = b * seq // TAIL_ROWS
    gate_bias = jnp.zeros((1, LANES), F32).at[0, :ML_HEADS].set(igate_b).at[0, ML_HEADS:2 * ML_HEADS].set(fgate_b)
    row_blk = lambda s: jnp.where(s == 0, n_seq_blocks, s - 1)
    rows = lambda w, c: pl.BlockSpec((TAIL_ROWS, w), lambda s: (row_blk(s), c))
    full = lambda r, c: pl.BlockSpec((r, c), lambda s: (0, 0))
    return pl.pallas_call(
        functools.partial(_mlstm_kernel, seq // TAIL_ROWS),
        out_shape=jax.ShapeDtypeStruct((m, ML_V_WIDTH), BF16),
        grid=(n_seq_blocks + 1,),
        in_specs=[rows(ML_QK_WIDTH, ZA_MQ // ML_QK_WIDTH), rows(ML_QK_WIDTH, ZA_MK // ML_QK_WIDTH),
                  rows(ML_V_WIDTH // 2, ZA_MV // (ML_V_WIDTH // 2)), rows(ML_V_WIDTH // 2, ZA_MV // (ML_V_WIDTH // 2) + 1),
                  rows(D_MODEL, 0), rows(ML_V_WIDTH, ZB_MO // ML_V_WIDTH), full(LANES, D_MODEL),
                  full(CONV_WIDTH, 2 * ML_QK_WIDTH), full(1, 2 * ML_QK_WIDTH), full(1, LANES), full(1, ML_V_WIDTH)],
        out_specs=rows(ML_V_WIDTH, 0),
        scratch_shapes=[pltpu.VMEM((TAIL_ROWS, LANES), F32),
                        pltpu.VMEM((ML_CHUNK + CONV_HALO, 2 * ML_QK_WIDTH), BF16),
                        pltpu.VMEM((ML_HEADS, ML_QK_DIM, ML_V_DIM), F32),
                        pltpu.VMEM((SUBLANES, LANES), F32),
                        pltpu.VMEM((SUBLANES, LANES), F32),
                        pltpu.VMEM((CONV_HALO, 2 * ML_QK_WIDTH), BF16),
                        pltpu.VMEM((ML_HEADS, ML_QK_DIM, ML_V_DIM), F32),
                        pltpu.VMEM((SUBLANES, LANES), F32),
                        pltpu.VMEM((SUBLANES, LANES), F32)],
        compiler_params=_params("arbitrary"),
        name="mlstm",
    )(za, za, za, za, h, zb, w_gates, conv_w.astype(F32), conv_b.reshape(1, -1).astype(F32), gate_bias,
      norm_g.reshape(1, -1).astype(F32))


def _merge_kernel(n_routed, n_body_tiles, attn_ref, ml_ref, ga0_ref, ga1_ref, gm0_ref, gm1_ref, *rest):
    with_router = n_routed is not None
    if n_body_tiles is None:
        xs_ref, wa_ref, wm_ref, wo_ref, g_ref, *rest = rest
        residual = xs_ref[...]
    else:
        xs_ref, tail_ref, wa_ref, wm_ref, wo_ref, g_ref, *rest = rest
        residual = _split_rows(pl.program_id(0), n_body_tiles, xs_ref, tail_ref)
    if with_router:
        wr_hi_ref, wr_lo_ref, br_ref, xs_out, h_out, route_out, counts_out = rest
    else:
        xs_out, h_out = rest
    a = _dot(attn_ref[...], wa_ref[...])
    m = _dot(ml_ref[...], wm_ref[...])
    ga = jnp.concatenate([ga0_ref[...], ga1_ref[...]], axis=1)
    gm = jnp.concatenate([gm0_ref[...], gm1_ref[...]], axis=1)
    y = _sigmoid(ga.astype(F32)) * a + _sigmoid(gm.astype(F32)) * m
    xs_new = residual + _dot(y.astype(BF16), wo_ref[...])
    xs_out[...] = xs_new
    hn = _rms(xs_new, g_ref[...])
    if not with_router:
        h_out[...] = hn.astype(h_out.dtype)
    else:
        half = D_MODEL // 2
        lo = lax.bitcast_convert_type(hn[:, :half].astype(BF16).astype(F32), jnp.uint32) >> 16
        hi = lax.bitcast_convert_type(hn[:, half:].astype(BF16).astype(F32), jnp.uint32) & jnp.uint32(0xFFFF0000)
        h_out[...] = lo | hi
        hn_hi = hn.astype(BF16)
        hn_lo = (hn - hn_hi.astype(F32)).astype(BF16)
        logits = (_dot(hn_hi, wr_hi_ref[...]) + (_dot(hn_lo, wr_hi_ref[...]) + _dot(hn_hi, wr_lo_ref[...]))
                  + br_ref[...])
        lane = lax.broadcasted_iota(jnp.int32, logits.shape, 1).astype(F32)
        l1 = logits.max(-1, keepdims=True)
        i1 = jnp.min(jnp.where(logits == l1, lane, float(LANES)), axis=-1, keepdims=True)
        rest_logits = jnp.where(lane == i1, NEG_INF, logits)
        l2 = rest_logits.max(-1, keepdims=True)
        i2 = jnp.min(jnp.where(rest_logits == l2, lane, float(LANES)), axis=-1, keepdims=True)
        e = jnp.exp(l2 - l1)
        w1 = 1.0 / (1.0 + e)
        w2 = e / (1.0 + e)
        bm = logits.shape[0]
        row = lax.broadcasted_iota(jnp.int32, (bm, 1), 0) + pl.program_id(0) * bm
        routed = row < n_routed
        pick1 = jnp.where(jnp.logical_and(lane == i1, routed), 1.0, 0.0)
        pick2 = jnp.where(jnp.logical_and(lane == i2, routed), 1.0, 0.0)
        picks = pick1 + pick2
        earlier = (lax.broadcasted_iota(jnp.int32, (bm, bm), 1) < lax.broadcasted_iota(jnp.int32, (bm, bm), 0))
        before = _dot(jnp.where(earlier, 1.0, 0.0).astype(BF16), picks.astype(BF16))
        r1 = jnp.sum(before * pick1, axis=-1, keepdims=True)
        r2 = jnp.sum(before * pick2, axis=-1, keepdims=True)
        route_out[...] = jnp.where(lane == 0, i1, jnp.where(lane == 1, i2, jnp.where(lane == 2, w1,
                                   jnp.where(lane == 3, w2, jnp.where(lane == 4, r1, jnp.where(lane == 5, r2, 0.0))))))
        counts_out[0] = jnp.broadcast_to(jnp.sum(picks, axis=0, keepdims=True), (SUBLANES, LANES))


def _merge(attn, ml, zb, xs, wa, wm, wo, g_next, router=None):
    m = zb.shape[0]
    bm = _tile(TAIL_ROWS, 256)
    row = lambda w, c: pl.BlockSpec((bm, w), lambda i: (i, c))
    const = lambda r, c: pl.BlockSpec((r, c), lambda i: (0, 0), pipeline_mode=pl.Buffered(1))
    if isinstance(xs, tuple):
        n_body_tiles = xs[0].shape[0] // bm
        xs_specs, xs_args = _split_specs(bm, n_body_tiles), list(xs)
    else:
        n_body_tiles = None
        xs_specs, xs_args = [row(D_MODEL, 0)], [xs]
    in_specs = [row(ATTN_WIDTH, 0), row(ML_V_WIDTH, 0),
                row(HALF_D, ZB_GA // HALF_D), row(HALF_D, ZB_GA // HALF_D + 1),
                row(HALF_D, ZB_GM // HALF_D), row(HALF_D, ZB_GM // HALF_D + 1),
                *xs_specs, const(ATTN_WIDTH, D_MODEL), const(ML_V_WIDTH, D_MODEL), const(D_MODEL, D_MODEL),
                const(1, D_MODEL)]
    args = [attn, ml, zb, zb, zb, zb, *xs_args, wa, wm, wo, g_next.reshape(1, D_MODEL)]
    if router is None:
        out_shape = [jax.ShapeDtypeStruct((m, D_MODEL), F32), jax.ShapeDtypeStruct((m, D_MODEL), BF16)]
        out_specs = [row(D_MODEL, 0), row(D_MODEL, 0)]
    else:
        out_shape = [jax.ShapeDtypeStruct((m, D_MODEL), F32), jax.ShapeDtypeStruct((m, D_MODEL // 2), jnp.uint32)]
        out_specs = [row(D_MODEL, 0), row(D_MODEL // 2, 0)]
    n_routed = None
    if router is not None:
        w_router, b_router, n_routed = router
        wr = jnp.zeros((D_MODEL, LANES), F32).at[:, :N_EXPERTS].set(w_router.astype(F32))
        br = jnp.full((1, LANES), NEG_INF, F32).at[0, :N_EXPERTS].set(b_router.astype(F32))
        wr_hi = wr.astype(BF16)
        wr_lo = (wr - wr_hi.astype(F32)).astype(BF16)
        in_specs += [const(D_MODEL, LANES), const(D_MODEL, LANES), const(1, LANES)]
        args += [wr_hi, wr_lo, br]
        out_shape += [jax.ShapeDtypeStruct((m, LANES), F32), jax.ShapeDtypeStruct((m // bm, SUBLANES, LANES), F32)]
        out_specs += [row(LANES, 0), pl.BlockSpec((1, SUBLANES, LANES), lambda i: (i, 0, 0))]
    return pl.pallas_call(
        functools.partial(_merge_kernel, n_routed, n_body_tiles),
        out_shape=out_shape,
        grid=(m // bm,),
        in_specs=in_specs,
        out_specs=out_specs,
        compiler_params=_params("parallel"),
        name="merge_out_proj",
    )(*args)


def _ffn_up_kernel(h_ref, wg_ref, wu_ref, act_ref):
    hb = h_ref[...]
    g = _dot(hb, wg_ref[...])
    u = _dot(hb, wu_ref[...])
    act_ref[...] = (g * _sigmoid(g) * u).astype(act_ref.dtype)


def _ffn_down_kernel(act_ref, xs_ref, wd_ref, g_ref, xs_out, h_out, rows_ref):
    j = pl.program_id(1)
    xs_new = xs_ref[...] + _dot(act_ref[...], wd_ref[...])
    xs_out[...] = xs_new
    rows_ref[j] = xs_new

    @pl.when(j == pl.num_programs(1) - 1)
    def _():
        full = jnp.concatenate([rows_ref[t] for t in range(rows_ref.shape[0])], axis=1)
        h_out[...] = _rms(full, g_ref[...]).astype(h_out.dtype)


def _dense_ffn(h, xs, wg, wu, wd, g_next):
    m = xs.shape[0]
    bm = _tile(m, 1024)
    bf = D_FF // FFN_UP_TILES
    act = pl.pallas_call(
        _ffn_up_kernel,
        out_shape=jax.ShapeDtypeStruct((m, D_FF), BF16),
        grid=(FFN_UP_TILES, m // bm),
        in_specs=[pl.BlockSpec((bm, D_MODEL), lambda f, i: (i, 0)),
                  pl.BlockSpec((D_MODEL, bf), lambda f, i: (0, f)),
                  pl.BlockSpec((D_MODEL, bf), lambda f, i: (0, f))],
        out_specs=pl.BlockSpec((bm, bf), lambda f, i: (i, f)),
        compiler_params=_params("parallel", "parallel"),
        name="dense_swiglu_up",
    )(h, wg, wu)
    bn = D_MODEL // FFN_DOWN_TILES
    return pl.pallas_call(
        _ffn_down_kernel,
        out_shape=[jax.ShapeDtypeStruct((m, D_MODEL), F32), jax.ShapeDtypeStruct((m, D_MODEL), BF16)],
        grid=(m // bm, FFN_DOWN_TILES),
        in_specs=[pl.BlockSpec((bm, D_FF), lambda i, j: (i, 0)),
                  pl.BlockSpec((bm, bn), lambda i, j: (i, j)),
                  pl.BlockSpec((D_FF, bn), lambda i, j: (0, j)),
                  pl.BlockSpec((1, D_MODEL), lambda i, j: (0, 0))],
        out_specs=[pl.BlockSpec((bm, bn), lambda i, j: (i, j)),
                   pl.BlockSpec((bm, D_MODEL), lambda i, j: (i, 0))],
        scratch_shapes=[pltpu.VMEM((FFN_DOWN_TILES, bm, bn), F32)],
        compiler_params=_params("parallel", "arbitrary"),
        name="dense_swiglu_down",
    )(act, xs, wd, g_next.reshape(1, D_MODEL))


def _row_copy(src_hbm, dst_vmem, sem, src_row, dst_row):
    return pltpu.make_async_copy(src_hbm.at[pl.ds(src_row, 1), :], dst_vmem.at[pl.ds(dst_row, 1), :], sem)


def _moe_gather_copy(h_hbm, gbuf, sem, src_row, dst_row):
    return pltpu.make_async_copy(h_hbm.at[pl.ds(src_row, 1), :], gbuf.at[pl.ds(dst_row, 1), :], sem)


def _moe_gather_wait(h_hbm, gbuf, sem):
    pltpu.make_async_copy(h_hbm.at[pl.ds(0, MOE_GATHER_ROWS), :], gbuf, sem).wait()


def _moe_kernel(sb_e_ref, sb_rows_ref, row_tok_ref, h_hbm, wg_ref, wu_ref, wd_ref, y_ref, gbuf, xb16, sem):
    sb = pl.program_id(0)
    f = pl.program_id(1)
    n_sb = pl.num_programs(0)
    nf = pl.num_programs(1)
    kind = sb_rows_ref[sb]
    half = D_MODEL // 2

    @pl.when(jnp.logical_and(sb == 0, f == 0))
    def _():
        def issue(r, c):
            _moe_gather_copy(h_hbm, gbuf, sem, row_tok_ref[r], r).start()
            return c
        lax.fori_loop(0, MOE_GATHER_ROWS, issue, 0)

    @pl.when(f == 0)
    def _():
        @pl.when(kind < MOE_PARTS)
        def _():
            y_ref[...] = jnp.zeros_like(y_ref)

        prev_kind = sb_rows_ref[jnp.maximum(sb - 1, 0)]

        @pl.when(jnp.logical_or(sb == 0, prev_kind > 0))
        def _():
            _moe_gather_wait(h_hbm, gbuf, sem)

        @pl.when(kind > 0)
        def _():
            w = gbuf[0:MOE_SB, :]
            xb16[:, :half] = lax.bitcast_convert_type(w << 16, F32).astype(BF16)
            xb16[:, half:] = lax.bitcast_convert_type(w & jnp.uint32(0xFFFF0000), F32).astype(BF16)

    def compute(row0, rows, prefetch, assign=False):
        if prefetch:
            base = (sb + 1) * MOE_SB + f * MOE_STEP_ROWS
            for r in range(MOE_STEP_ROWS):
                _moe_gather_copy(h_hbm, gbuf, sem, row_tok_ref[base + r], f * MOE_STEP_ROWS + r).start()
        xb = xb16[row0:row0 + rows, :]
        g = _dot(xb, wg_ref[0].astype(BF16))
        u = _dot(xb, wu_ref[0].astype(BF16))
        act = (g * _sigmoid(g) * u).astype(BF16)
        down = _dot(act, wd_ref[0].astype(BF16))
        if assign:
            y_ref[row0:row0 + rows, :] = down
        else:
            y_ref[row0:row0 + rows, :] += down

    @pl.when(jnp.logical_and(kind == MOE_PARTS, f == 0))
    def _():
        compute(0, MOE_SB, True, assign=True)

    @pl.when(jnp.logical_and(kind == MOE_PARTS, f > 0))
    def _():
        compute(0, MOE_SB, True)

    for part in range(MOE_PARTS - 1):
        @pl.when(jnp.logical_and(kind > part, kind < MOE_PARTS))
        def _():
            compute(part * MOE_PART_ROWS, MOE_PART_ROWS, part == 0)

    @pl.when(jnp.logical_and(jnp.logical_and(sb == n_sb - 1, f == nf - 1), kind > 0))
    def _():
        _moe_gather_wait(h_hbm, gbuf, sem)


def _moe_experts(h_packed, sb_e, sb_rows, row_tok, wg, wu, wd, n_sb, bf=MOE_FF_TILE):
    nf = D_FF // bf
    assert nf * MOE_STEP_ROWS == MOE_GATHER_ROWS
    ftile = lambda i, f, rows: jnp.where(rows[i] > 0, f, nf - 1)
    grid_spec = pltpu.PrefetchScalarGridSpec(
        num_scalar_prefetch=3,
        grid=(n_sb, nf),
        in_specs=[pl.BlockSpec(memory_space=pl.ANY),
                  pl.BlockSpec((1, D_MODEL, bf), lambda i, f, se, sr, rt: (se[i], 0, ftile(i, f, sr))),
                  pl.BlockSpec((1, D_MODEL, bf), lambda i, f, se, sr, rt: (se[i], 0, ftile(i, f, sr))),
                  pl.BlockSpec((1, bf, D_MODEL), lambda i, f, se, sr, rt: (se[i], ftile(i, f, sr), 0))],
        out_specs=pl.BlockSpec((MOE_SB, D_MODEL), lambda i, f, se, sr, rt: (i, 0)),
        scratch_shapes=[pltpu.VMEM((MOE_GATHER_ROWS, D_MODEL // 2), jnp.uint32),
                        pltpu.VMEM((MOE_SB, D_MODEL), BF16),
                        pltpu.SemaphoreType.DMA(())],
    )
    return pl.pallas_call(
        _moe_kernel,
        out_shape=jax.ShapeDtypeStruct((n_sb * MOE_SB, D_MODEL), F32),
        grid_spec=grid_spec,
        compiler_params=pltpu.CompilerParams(dimension_semantics=("arbitrary", "arbitrary"),
                                             vmem_limit_bytes=MOE_VMEM_LIMIT),
        name="moe_experts",
    )(sb_e, sb_rows, row_tok, h_packed, wg, wu, wd)


def _combine_kernel(pos_ref, xs_ref, route_ref, y_hbm, g_ref, o_ref, ybuf, sem):
    step = pl.program_id(0)
    n_steps = pl.num_programs(0)
    slot = step % 2

    rows = o_ref.shape[0]

    def gather(s, dst_slot):
        base = s * rows

        def issue(r, c):
            for k in range(TOP_K):
                _row_copy(y_hbm, ybuf.at[dst_slot, k], sem.at[dst_slot], pos_ref[TOP_K * (base + r) + k], r).start()
            return c
        lax.fori_loop(0, rows, issue, 0, unroll=4)

    @pl.when(step == 0)
    def _():
        gather(step, slot)

    @pl.when(step + 1 < n_steps)
    def _():
        gather(step + 1, 1 - slot)

    for k in range(TOP_K):
        pltpu.make_async_copy(y_hbm.at[pl.ds(0, rows), :], ybuf.at[slot, k], sem.at[slot]).wait()
    route = route_ref[...]
    moe = ybuf[slot, 0] * route[:, 2:3] + ybuf[slot, 1] * route[:, 3:4]
    o_ref[...] = _rms(xs_ref[...] + moe, g_ref[...])


def _moe_combine(pos, xs, route, yb, g_final, m_real):
    rows = _tile(m_real, 512)
    grid_spec = pltpu.PrefetchScalarGridSpec(
        num_scalar_prefetch=1,
        grid=(m_real // rows,),
        in_specs=[pl.BlockSpec((rows, D_MODEL), lambda t, p: (t, 0)),
                  pl.BlockSpec((rows, LANES), lambda t, p: (t, 0)),
                  pl.BlockSpec(memory_space=pl.ANY),
                  pl.BlockSpec((1, D_MODEL), lambda t, p: (0, 0))],
        out_specs=pl.BlockSpec((rows, D_MODEL), lambda t, p: (t, 0)),
        scratch_shapes=[pltpu.VMEM((2, TOP_K, rows, D_MODEL), F32), pltpu.SemaphoreType.DMA((2,))],
    )
    return pl.pallas_call(
        _combine_kernel,
        out_shape=jax.ShapeDtypeStruct((m_real, D_MODEL), F32),
        grid_spec=grid_spec,
        compiler_params=_params("arbitrary"),
        name="moe_combine_final_norm",
    )(pos, xs, route, yb, g_final.reshape(1, D_MODEL))


def _moe_routing(route, tile_counts, n_tok):
    n_assign = n_tok * TOP_K
    n_sb = n_assign // MOE_SB + N_EXPERTS
    rows_per_tile = route.shape[0] // tile_counts.shape[0]
    e_flat = route[:n_tok, :TOP_K].astype(jnp.int32).reshape(n_assign)
    onehot = (e_flat[:, None] == jnp.arange(N_EXPERTS, dtype=jnp.int32)[None, :]).astype(jnp.int32)
    tile_counts = tile_counts[:, 0, :N_EXPERTS].astype(jnp.int32)
    tile_base = jnp.cumsum(tile_counts, axis=0) - tile_counts
    base = jnp.repeat(tile_base, rows_per_tile * TOP_K, axis=0)[:n_assign]
    rank = jnp.sum(base * onehot, axis=1) + route[:n_tok, 4:4 + TOP_K].astype(jnp.int32).reshape(n_assign)
    counts = jnp.sum(tile_counts, axis=0)
    sb_count = (counts + MOE_SB - 1) // MOE_SB
    sb_end = jnp.cumsum(sb_count)
    sb_start = sb_end - sb_count
    dest = (jnp.sum((sb_start * MOE_SB)[None, :] * onehot, axis=1) + rank).astype(jnp.int32)
    row_tok = jnp.zeros(((n_sb + 2) * MOE_SB,), jnp.int32).at[dest].set(
        jnp.arange(n_assign, dtype=jnp.int32) // TOP_K, unique_indices=True)
    sb = jnp.arange(n_sb, dtype=jnp.int32)
    sb_e = jnp.sum((sb[:, None] >= sb_end[None, :]).astype(jnp.int32), axis=1)
    valid = sb < sb_end[-1]
    last_e = jnp.sum((sb_end[-1] - 1 >= sb_end).astype(jnp.int32))
    sb_e = jnp.where(valid, sb_e, last_e).astype(jnp.int32)
    rows_here = counts[sb_e] - (sb - sb_start[sb_e]) * MOE_SB
    parts = (jnp.minimum(rows_here, MOE_SB) + MOE_PART_ROWS - 1) // MOE_PART_ROWS
    sb_rows = jnp.where(valid, parts, 0).astype(jnp.int32)
    return dest, row_tok, sb_e, sb_rows, n_sb


def _split_w_in(w):
    n_gates = 2 * ML_HEADS
    assert w.shape[1] == ZA_WIDTH + n_gates + ZB_WIDTH
    wt = jnp.swapaxes(w, 0, 1)
    part_a = wt[:W_IN_GATES].astype(BF16)
    part_b = wt[W_IN_GATES + n_gates:].astype(BF16)
    gates = jnp.pad(wt[W_IN_GATES:W_IN_GATES + n_gates], ((0, LANES - n_gates), (0, 0))).astype(BF16)
    return part_a, part_b, gates


def kernel(x, meta_tokens, rel_bias_table, w_in, attn_sinks, conv_w, conv_b, igate_b, fgate_b, mlstm_norm_g,
           w_attn_up, w_mlstm_up, w_out, norm_mix_g, norm_ffn_g, w_ffn_gate, w_ffn_up, w_ffn_down, w_router,
           b_router, w_moe_gate, w_moe_up, w_moe_down, final_norm_g):
    b, seq, _ = x.shape
    depth = w_in.shape[0]
    m_real = b * seq
    m = m_real + TAIL_ROWS
    assert depth == 2 and seq % ML_CHUNK == 0 and seq % BLOCK == 0 and m_real % TAIL_ROWS == 0
    tail = jnp.concatenate([jnp.zeros((PAD, D_MODEL), x.dtype), meta_tokens.astype(x.dtype),
                            jnp.zeros((TAIL_ROWS - PREFIX, D_MODEL), x.dtype)], axis=0)
    xs = (x.reshape(m_real, D_MODEL), tail)
    band, meta = _attn_bias(rel_bias_table, seq // BLOCK + 1)
    h = _rmsnorm(*xs, norm_mix_g[0])
    out = None
    for layer in range(depth):
        w_a, w_b, w_gates = _split_w_in(w_in[layer])
        za = _matmul_nt(h, w_a, BF16, 1024, ZA_WIDTH // 2, "in_proj_a")
        zb = _matmul_nt(h, w_b, BF16, 1024, ZB_WIDTH // 2, "in_proj_b")
        attn = _attention(za, band, meta, attn_sinks[layer], b, seq)
        ml = _mlstm(za, zb, h, w_gates, conv_w[layer], conv_b[layer], igate_b[layer], fgate_b[layer],
                    mlstm_norm_g[layer], b, seq)
        wa, wm, wo = (w_attn_up[layer].astype(BF16), w_mlstm_up[layer].astype(BF16), w_out[layer].astype(BF16))
        i = layer // 2
        if layer % 2 == 0:
            xs, h = _merge(attn, ml, zb, xs, wa, wm, wo, norm_ffn_g[layer])
            xs, h = _dense_ffn(h, xs, w_ffn_gate[i].astype(BF16), w_ffn_up[i].astype(BF16),
                               w_ffn_down[i].astype(BF16), norm_mix_g[layer + 1])
        else:
            n_tok = m_real + PREFIX
            xs, h_packed, route, tile_counts = _merge(attn, ml, zb, xs, wa, wm, wo, norm_ffn_g[layer],
                                                      router=(w_router[i], b_router[i], n_tok))
            dest, row_tok, sb_e, sb_rows, n_sb = _moe_routing(route, tile_counts, n_tok)
            yb = _moe_experts(h_packed, sb_e, sb_rows, row_tok, w_moe_gate[i], w_moe_up[i], w_moe_down[i], n_sb)
            out = _moe_combine(dest, xs, route, yb, final_norm_g, m_real).reshape(b, seq, D_MODEL)
    return out
```

```python
import functools
import math

import jax
import jax.numpy as jnp
from jax import lax
from jax.experimental import pallas as pl
from jax.experimental.pallas import tpu as pltpu

D_MODEL = 2048
N_META = 16
BLOCK = 128
PREFIX = BLOCK
PAD = PREFIX - N_META
HEAD_DIM = 64
N_Q_HEADS = 16
N_KV_HEADS = 4
GQA_GROUP = 4
ATTN_WIDTH = N_Q_HEADS * HEAD_DIM
KV_WIDTH = N_KV_HEADS * HEAD_DIM
WINDOW = 128
NUM_BUCKETS = 32
MAX_DISTANCE = 128
ML_HEADS = 4
ML_V_WIDTH = D_MODEL // 2
ML_V_DIM = ML_V_WIDTH // ML_HEADS
ML_QK_DIM = ML_V_DIM // 2
ML_QK_WIDTH = ML_HEADS * ML_QK_DIM
CONV_WIDTH = 4
D_FF = 11 * D_MODEL // 4
N_EXPERTS = 8
TOP_K = 2
EPS = 1e-6

LANES = 128
SUBLANES = 8
BF16_ROWS = 16
VMEM_LIMIT = 56 * 1024 * 1024

ZA_AQ = 0
ZA_AK = ZA_AQ + ATTN_WIDTH
ZA_AV = ZA_AK + KV_WIDTH
ZA_MQ = ZA_AV + KV_WIDTH
ZA_MK = ZA_MQ + ML_QK_WIDTH
ZA_MV = ZA_MK + ML_QK_WIDTH
ZA_WIDTH = ZA_MV + ML_V_WIDTH
ZB_MO = 0
ZB_GA = ZB_MO + ML_V_WIDTH
ZB_GM = ZB_GA + D_MODEL
ZB_WIDTH = ZB_GM + D_MODEL
W_IN_GATES = ZA_WIDTH
HALF_D = D_MODEL // 2
TAIL_ROWS = 512
ATTN_STEP_BLOCKS = 2
ML_CHUNK = 128
CONV_HALO = BF16_ROWS
FFN_UP_TILES = 4
FFN_DOWN_TILES = 4
MOE_SB = 1024
MOE_PARTS = 4
MOE_PART_ROWS = MOE_SB // MOE_PARTS
MOE_FF_TILE = 512
MOE_STEP_ROWS = 96
MOE_GATHER_ROWS = MOE_STEP_ROWS * (D_FF // MOE_FF_TILE)
MOE_VMEM_LIMIT = 60 * 1024 * 1024

F32 = jnp.float32
BF16 = jnp.bfloat16
NEG_INF = float("-inf")


def _tile(m, target):
    best = LANES
    for t in range(LANES, min(m, target) + 1, LANES):
        if m % t == 0:
            best = t
    assert m % best == 0
    return best


def _params(*sem):
    return pltpu.CompilerParams(dimension_semantics=sem, vmem_limit_bytes=VMEM_LIMIT)


def _rms(x, g):
    return x * lax.rsqrt(jnp.mean(x * x, axis=-1, keepdims=True) + EPS) * g


def _sigmoid(x):
    return 1.0 / (1.0 + jnp.exp(-x))


def _dot(a, b):
    return jnp.dot(a, b, preferred_element_type=F32)


def _dot_nt(a, b):
    return lax.dot_general(a, b, (((1,), (1,)), ((), ())), preferred_element_type=F32)


def _dot_tn(a, b):
    return lax.dot_general(a, b, (((0,), (0,)), ((), ())), preferred_element_type=F32)


def _split_rows(i, n_body_tiles, body_ref, tail_ref):
    return jnp.where(i < n_body_tiles, body_ref[...], tail_ref[...])


def _split_specs(bm, n_body_tiles):
    return [pl.BlockSpec((bm, D_MODEL), lambda i: (jnp.minimum(i, n_body_tiles - 1), 0)),
            pl.BlockSpec((bm, D_MODEL), lambda i: (jnp.maximum(i - n_body_tiles, 0), 0))]


def _norm_kernel(n_body_tiles, x_ref, tail_ref, g_ref, o_ref):
    x = _split_rows(pl.program_id(0), n_body_tiles, x_ref, tail_ref)
    o_ref[...] = _rms(x, g_ref[...]).astype(o_ref.dtype)


def _rmsnorm(x_body, x_tail, g):
    m = x_body.shape[0] + x_tail.shape[0]
    bm = _tile(x_tail.shape[0], 512)
    assert x_body.shape[0] % bm == 0
    n_body_tiles = x_body.shape[0] // bm
    return pl.pallas_call(
        functools.partial(_norm_kernel, n_body_tiles),
        out_shape=jax.ShapeDtypeStruct((m, D_MODEL), BF16),
        grid=(m // bm,),
        in_specs=_split_specs(bm, n_body_tiles) + [pl.BlockSpec((1, D_MODEL), lambda i: (0, 0))],
        out_specs=pl.BlockSpec((bm, D_MODEL), lambda i: (i, 0)),
        compiler_params=_params("parallel"),
        name="rmsnorm",
    )(x_body, x_tail, g.reshape(1, D_MODEL))


def _mm_nt_kernel(a_ref, wt_ref, o_ref):
    o_ref[...] = _dot_nt(a_ref[...], wt_ref[...]).astype(o_ref.dtype)


def _matmul_nt(a, wt, out_dtype, bm, bn, name):
    m, k = a.shape
    n = wt.shape[0]
    bm = _tile(m, bm)
    return pl.pallas_call(
        _mm_nt_kernel,
        out_shape=jax.ShapeDtypeStruct((m, n), out_dtype),
        grid=(n // bn, m // bm),
        in_specs=[pl.BlockSpec((bm, k), lambda j, i: (i, 0)),
                  pl.BlockSpec((bn, k), lambda j, i: (j, 0))],
        out_specs=pl.BlockSpec((bm, bn), lambda j, i: (i, j)),
        compiler_params=_params("parallel", "parallel"),
        name=name,
    )(a, wt)


def _t5_bucket(rel):
    n = jnp.maximum(rel, 0)
    max_exact = NUM_BUCKETS // 2
    large = max_exact + (jnp.log(jnp.maximum(n, 1).astype(F32) / max_exact)
                         / math.log(MAX_DISTANCE / max_exact) * (NUM_BUCKETS - max_exact)).astype(jnp.int32)
    large = jnp.minimum(large, NUM_BUCKETS - 1)
    return jnp.where(n < max_exact, n, large)


def _bias_lookup(table, rel):
    onehot = (_t5_bucket(rel)[..., None] == jnp.arange(NUM_BUCKETS)).astype(F32)
    return jnp.einsum("...b,bh->h...", onehot, table.astype(F32), precision=lax.Precision.HIGHEST)


ATTN_GROUP_ORDER = (0, 2, 1, 3)


def _stack_group_rows(a):
    lead = a.shape[:-3]
    a = a.reshape(*lead, N_KV_HEADS, GQA_GROUP, BLOCK, a.shape[-1])
    a = jnp.take(a, jnp.array(ATTN_GROUP_ORDER), axis=len(lead) + 1)
    return a.reshape(*lead, N_KV_HEADS, GQA_GROUP * BLOCK, a.shape[-1])


def _attn_bias(table, nb):
    qi = jnp.arange(BLOCK)[:, None]
    ki = jnp.arange(2 * BLOCK)[None, :]
    rel_band = qi + BLOCK - ki
    blk3 = jnp.arange(3)[:, None, None]
    mask_band = (rel_band >= 0) & (rel_band < WINDOW) & ((blk3 - 1) * BLOCK + ki >= PAD)
    band = jnp.where(mask_band[:, None], _bias_lookup(table, rel_band)[None], NEG_INF)
    blk = jnp.arange(nb)[:, None, None]
    rel_meta = blk * BLOCK + qi[None] - (PAD + jnp.arange(N_META))
    meta = jnp.where((rel_meta >= WINDOW)[None], _bias_lookup(table, rel_meta), NEG_INF)
    meta = jnp.pad(jnp.moveaxis(meta, 1, 0), ((0, 0), (0, 0), (0, 0), (PAD, 0)), constant_values=NEG_INF)
    return _stack_group_rows(band), _stack_group_rows(meta)


def _swap_halves(x):
    return pltpu.roll(x.astype(F32), HEAD_DIM, axis=1).astype(x.dtype)


def _attn_kernel(q_ref, kp_ref, kc_ref, km_ref, vp_ref, vc_ref, vm_ref, bb0_ref, bb1_ref, bm0_ref, bm1_ref, sink_ref,
                 o_ref):
    band_bias = (bb0_ref, bb1_ref)
    meta_bias = (bm0_ref, bm1_ref)
    scale = HEAD_DIM ** -0.5
    assert math.frexp(scale)[0] == 0.5 and 2 * HEAD_DIM == LANES
    lane = lax.broadcasted_iota(jnp.int32, (1, LANES), 1)
    keep = (jnp.where(lane < HEAD_DIM, scale, 0.0).astype(BF16),
            jnp.where(lane < HEAD_DIM, 0.0, scale).astype(BF16))
    low = lax.broadcasted_iota(jnp.int32, (2 * BLOCK, LANES), 1) < HEAD_DIM
    chains = []
    for blk in range(ATTN_STEP_BLOCKS):
        qrows = slice(blk * BLOCK, (blk + 1) * BLOCK)
        for col in range(N_KV_HEADS // 2):
            kcols = slice(col * LANES, (col + 1) * LANES)
            k_prev = kp_ref[:, kcols] if blk == 0 else kc_ref[(blk - 1) * BLOCK:blk * BLOCK, kcols]
            v_prev = vp_ref[:, kcols] if blk == 0 else vc_ref[(blk - 1) * BLOCK:blk * BLOCK, kcols]
            k_nat = jnp.concatenate([k_prev, kc_ref[qrows, kcols], km_ref[:, kcols]], axis=0)
            v_nat = jnp.concatenate([v_prev, vc_ref[qrows, kcols], vm_ref[:, kcols]], axis=0)
            keys = (k_nat, _swap_halves(k_nat))
            vals = (v_nat, _swap_halves(v_nat))
            for half in range(2):
                h = 2 * col + half
                q0 = h * GQA_GROUP * HEAD_DIM
                q2 = jnp.concatenate([q_ref[qrows, q0:q0 + LANES], q_ref[qrows, q0 + LANES:q0 + 2 * LANES]], axis=0)
                for lane_half in range(2):
                    which = 0 if lane_half == half else 1
                    chains.append((blk, h, lane_half, _dot_nt(q2 * keep[lane_half], keys[which]), vals[which]))
    probs = []
    for blk, h, lane_half, s, _ in chains:
        rows = slice(lane_half * 2 * BLOCK, (lane_half + 1) * 2 * BLOCK)
        s0 = s[:, :BLOCK] + band_bias[blk][0, h, rows, :BLOCK]
        s1 = s[:, BLOCK:2 * BLOCK] + band_bias[blk][0, h, rows, BLOCK:]
        s2 = s[:, 2 * BLOCK:] + meta_bias[blk][0, h, rows, :]
        sink = sink_ref[h, rows, :]
        mx = jnp.maximum(jnp.maximum(jnp.maximum(s0, s1), s2).max(-1, keepdims=True), sink)
        p0 = jnp.exp(s0 - mx)
        p1 = jnp.exp(s1 - mx)
        p2 = jnp.exp(s2 - mx)
        den = (p0 + p1 + p2).sum(-1, keepdims=True) + jnp.exp(sink - mx)
        probs.append((jnp.concatenate([p0, p1, p2], axis=1).astype(BF16), 1.0 / den))
    outs = [_dot(p, chain[4]) * rden for (p, rden), chain in zip(probs, chains)]
    out_rows = []
    for blk in range(ATTN_STEP_BLOCKS):
        out_cols = []
        for h in range(N_KV_HEADS):
            first = (blk * N_KV_HEADS + h) * 2
            o = jnp.where(low, outs[first], outs[first + 1]).astype(o_ref.dtype)
            out_cols += [o[:BLOCK], o[BLOCK:]]
        out_rows.append(jnp.concatenate(out_cols, axis=1))
    o_ref[...] = jnp.concatenate(out_rows, axis=0)


def _attention(z, band, meta, sinks, b, seq):
    m = z.shape[0]
    per_seq = seq // BLOCK
    n_real = b * per_seq
    nb = ATTN_STEP_BLOCKS
    assert nb == 2 and per_seq % nb == 0 and (m // BLOCK) % nb == 0
    kcol, vcol = ZA_AK // KV_WIDTH, ZA_AV // KV_WIDTH
    sink_col = _stack_group_rows(jnp.broadcast_to(sinks.astype(F32)[:, None, None], (N_Q_HEADS, BLOCK, LANES)))
    first_blk = lambda s: s * nb
    prev = lambda s: jnp.where(first_blk(s) < n_real,
                               jnp.where(first_blk(s) % per_seq == 0, n_real, first_blk(s) - 1), first_blk(s))
    query_block = lambda s, j: jnp.where(first_blk(s) + j < n_real, (first_blk(s) + j) % per_seq + 1, 0)
    kv_one = lambda col, f: pl.BlockSpec((BLOCK, KV_WIDTH), lambda s: (f(s), col))
    kv_cur = lambda col: pl.BlockSpec((nb * BLOCK, KV_WIDTH), lambda s: (s, col))
    band_spec = lambda j: pl.BlockSpec((1, N_KV_HEADS, GQA_GROUP * BLOCK, 2 * BLOCK),
                                       lambda s: (jnp.minimum(query_block(s, j), 2), 0, 0, 0))
    meta_spec = lambda j: pl.BlockSpec((1, N_KV_HEADS, GQA_GROUP * BLOCK, BLOCK),
                                       lambda s: (query_block(s, j), 0, 0, 0))
    return pl.pallas_call(
        _attn_kernel,
        out_shape=jax.ShapeDtypeStruct((m, ATTN_WIDTH), BF16),
        grid=(m // (nb * BLOCK),),
        in_specs=[pl.BlockSpec((nb * BLOCK, ATTN_WIDTH), lambda s: (s, ZA_AQ // ATTN_WIDTH)),
                  kv_one(kcol, prev), kv_cur(kcol), kv_one(kcol, lambda s: n_real),
                  kv_one(vcol, prev), kv_cur(vcol), kv_one(vcol, lambda s: n_real),
                  band_spec(0), band_spec(1), meta_spec(0), meta_spec(1),
                  pl.BlockSpec((N_KV_HEADS, GQA_GROUP * BLOCK, LANES), lambda s: (0, 0, 0))],
        out_specs=pl.BlockSpec((nb * BLOCK, ATTN_WIDTH), lambda s: (s, 0)),
        compiler_params=_params("parallel"),
        name="swa_attention",
    )(z, z, z, z, z, z, z, band, band, meta, meta, sink_col)


def _mlstm_chunk(qk_in, v_in, gate_in, mo_in, prefix_chunk, cw_ref, cb_ref, gb_ref, ng_ref, xa_ref, ct_ref, n_ref, m_ref):
    L = ML_CHUNK
    ii = lax.broadcasted_iota(jnp.int32, (L, L), 0)
    jj = lax.broadcasted_iota(jnp.int32, (L, L), 1)
    causal = jj <= ii
    lane = lax.broadcasted_iota(jnp.int32, (L, LANES), 1)
    k_scale = ML_QK_DIM ** -0.5
    xa_ref[CONV_HALO:CONV_HALO + L, :] = qk_in
    window = xa_ref[...]
    sel_row = lax.broadcasted_iota(jnp.int32, (L, CONV_HALO + L), 0)
    sel_col = lax.broadcasted_iota(jnp.int32, (L, CONV_HALO + L), 1)
    acc = cb_ref[...] + cw_ref[CONV_WIDTH - 1:CONV_WIDTH, :] * qk_in.astype(F32)
    for j in range(CONV_WIDTH - 1):
        back = CONV_WIDTH - 1 - j
        shift = jnp.where(sel_col == sel_row + (CONV_HALO - back), 1.0, 0.0).astype(BF16)
        acc = acc + cw_ref[j:j + 1, :] * _dot(shift, window)
    qk = acc * _sigmoid(acc)
    xa_ref[0:CONV_HALO, :] = xa_ref[L:L + CONV_HALO, :]
    gpb = gate_in + gb_ref[...]
    log_sig = jnp.minimum(gpb, 0.0) - jnp.log1p(jnp.exp(-jnp.abs(gpb)))
    if prefix_chunk:
        valid = lax.broadcasted_iota(jnp.int32, (L, LANES), 0) >= PAD
        gx = jnp.where(lane < ML_HEADS, jnp.where(valid, gpb, NEG_INF), jnp.where(valid, log_sig, 0.0))
    else:
        gx = jnp.where(lane < ML_HEADS, gpb, log_sig)
    gxt = gx.T
    outs = []
    for h in range(ML_HEADS):
        q = qk[:, h * ML_QK_DIM:(h + 1) * ML_QK_DIM].astype(BF16)
        k = qk[:, ML_QK_WIDTH + h * ML_QK_DIM:ML_QK_WIDTH + (h + 1) * ML_QK_DIM] * k_scale
        v = v_in[:, h * ML_V_DIM:(h + 1) * ML_V_DIM]
        ig_col = gx[:, h:h + 1]
        lf_col = gx[:, ML_HEADS + h:ML_HEADS + h + 1]
        ig_row = gxt[h:h + 1, :]
        lf_row = gxt[ML_HEADS + h:ML_HEADS + h + 1, :]
        b_col = jnp.sum(jnp.where(causal, lf_row, 0.0), axis=-1, keepdims=True)
        b_row = jnp.sum(jnp.where(ii <= jj, lf_col, 0.0), axis=0, keepdims=True)
        m_prev = m_ref[h:h + 1, 0:1]
        log_d = jnp.where(causal, b_col - b_row + ig_row, NEG_INF)
        m_inter = b_col + m_prev
        m_out = jnp.maximum(m_inter, log_d.max(-1, keepdims=True))
        d = jnp.exp(log_d - m_out)
        inter = jnp.exp(m_inter - m_out)
        ct = ct_ref[h]
        n_prev = n_ref[h:h + 1, :]
        qk_scores = _dot_nt(q, k.astype(BF16))
        q_state = _dot(q, ct.astype(BF16))
        q_norm = jnp.sum(q.astype(F32) * n_prev, axis=-1, keepdims=True)
        b_last = b_col[L - 1:L, :]
        log_w = b_last - b_col + ig_col
        m_new = jnp.maximum(b_last + m_prev, log_w.max(0, keepdims=True))
        decay = jnp.exp(b_last + m_prev - m_new)
        kw = k * jnp.exp(log_w - m_new)
        s = qk_scores * d
        vcols = slice(h * ML_V_DIM, (h + 1) * ML_V_DIM)
        num = _dot(s.astype(BF16), v) + inter * q_state
        den = s.sum(-1, keepdims=True) + inter * q_norm
        hh = num / jnp.maximum(jnp.abs(den), jnp.exp(-m_out))
        ct_ref[h] = decay * ct + _dot_tn(kw.astype(BF16), v)
        n_ref[h:h + 1, :] = decay * n_prev + kw.sum(0, keepdims=True)
        m_ref[h:h + 1, :] = jnp.broadcast_to(m_new, (1, LANES))
        hn = hh * lax.rsqrt(jnp.mean(hh * hh, axis=-1, keepdims=True) + EPS) * ng_ref[:, vcols]
        outs.append(_sigmoid(mo_in[:, vcols].astype(F32)) * hn)
    return jnp.concatenate(outs, axis=1)


def _mlstm_kernel(blocks_per_seq, q_ref, k_ref, va_ref, vb_ref, h_ref, mo_ref, wgate_ref, cw_ref, cb_ref, gb_ref, ng_ref,
                  o_ref, gate_ref, xa_ref, ct_ref, n_ref, m_ref, xa0_ref, ct0_ref, n0_ref, m0_ref):
    L = ML_CHUNK
    step = pl.program_id(0)
    params = (cw_ref, cb_ref, gb_ref, ng_ref, xa_ref, ct_ref, n_ref, m_ref)
    gate_ref[...] = _dot_nt(h_ref[...], wgate_ref[...])

    def chunk_out(rows, prefix_chunk):
        qk = jnp.concatenate([q_ref[rows, :], k_ref[rows, :]], axis=1)
        v = jnp.concatenate([va_ref[rows, :], vb_ref[rows, :]], axis=1)
        return _mlstm_chunk(qk, v, gate_ref[rows, :], mo_ref[rows, :], prefix_chunk, *params).astype(o_ref.dtype)

    @pl.when(step == 0)
    def _():
        xa_ref[0:CONV_HALO, :] = jnp.zeros((CONV_HALO, 2 * ML_QK_WIDTH), xa_ref.dtype)
        ct_ref[...] = jnp.zeros_like(ct_ref)
        n_ref[...] = jnp.zeros_like(n_ref)
        m_ref[...] = jnp.zeros_like(m_ref)
        o_ref[0:L, :] = chunk_out(slice(0, L), True)
        o_ref[L:, :] = jnp.zeros((o_ref.shape[0] - L, o_ref.shape[1]), o_ref.dtype)
        xa0_ref[...] = xa_ref[0:CONV_HALO, :]
        ct0_ref[...] = ct_ref[...]
        n0_ref[...] = n_ref[...]
        m0_ref[...] = m_ref[...]

    @pl.when(step > 0)
    def _():
        @pl.when((step - 1) % blocks_per_seq == 0)
        def _():
            xa_ref[0:CONV_HALO, :] = xa0_ref[...]
            ct_ref[...] = ct0_ref[...]
            n_ref[...] = n0_ref[...]
            m_ref[...] = m0_ref[...]

        def chunk(c, carry):
            rows = pl.ds(pl.multiple_of(c * L, L), L)
            o_ref[rows, :] = chunk_out(rows, False)
            return carry

        lax.fori_loop(0, TAIL_ROWS // L, chunk, 0)


def _mlstm(za, zb, h, w_gates, conv_w, conv_b, igate_b, fgate_b, norm_g, b, seq):
    m = za.shape[0]
    assert seq % TAIL_ROWS == 0 and TAIL_ROWS % ML_CHUNK == 0
    n_seq_blocks = b * seq // TAIL_ROWS
    gate_bias = jnp.zeros((1, LANES), F32).at[0, :ML_HEADS].set(igate_b).at[0, ML_HEADS:2 * ML_HEADS].set(fgate_b)
    row_blk = lambda s: jnp.where(s == 0, n_seq_blocks, s - 1)
    rows = lambda w, c: pl.BlockSpec((TAIL_ROWS, w), lambda s: (row_blk(s), c))
    full = lambda r, c: pl.BlockSpec((r, c), lambda s: (0, 0))
    return pl.pallas_call(
        functools.partial(_mlstm_kernel, seq // TAIL_ROWS),
        out_shape=jax.ShapeDtypeStruct((m, ML_V_WIDTH), BF16),
        grid=(n_seq_blocks + 1,),
        in_specs=[rows(ML_QK_WIDTH, ZA_MQ // ML_QK_WIDTH), rows(ML_QK_WIDTH, ZA_MK // ML_QK_WIDTH),
                  rows(ML_V_WIDTH // 2, ZA_MV // (ML_V_WIDTH // 2)), rows(ML_V_WIDTH // 2, ZA_MV // (ML_V_WIDTH // 2) + 1),
                  rows(D_MODEL, 0), rows(ML_V_WIDTH, ZB_MO // ML_V_WIDTH), full(LANES, D_MODEL),
                  full(CONV_WIDTH, 2 * ML_QK_WIDTH), full(1, 2 * ML_QK_WIDTH), full(1, LANES), full(1, ML_V_WIDTH)],
        out_specs=rows(ML_V_WIDTH, 0),
        scratch_shapes=[pltpu.VMEM((TAIL_ROWS, LANES), F32),
                        pltpu.VMEM((ML_CHUNK + CONV_HALO, 2 * ML_QK_WIDTH), BF16),
                        pltpu.VMEM((ML_HEADS, ML_QK_DIM, ML_V_DIM), F32),
                        pltpu.VMEM((SUBLANES, LANES), F32),
                        pltpu.VMEM((SUBLANES, LANES), F32),
                        pltpu.VMEM((CONV_HALO, 2 * ML_QK_WIDTH), BF16),
                        pltpu.VMEM((ML_HEADS, ML_QK_DIM, ML_V_DIM), F32),
                        pltpu.VMEM((SUBLANES, LANES), F32),
                        pltpu.VMEM((SUBLANES, LANES), F32)],
        compiler_params=_params("arbitrary"),
        name="mlstm",
    )(za, za, za, za, h, zb, w_gates, conv_w.astype(F32), conv_b.reshape(1, -1).astype(F32), gate_bias,
      norm_g.reshape(1, -1).astype(F32))


def _merge_kernel(n_routed, n_body_tiles, attn_ref, ml_ref, ga0_ref, ga1_ref, gm0_ref, gm1_ref, *rest):
    with_router = n_routed is not None
    if n_body_tiles is None:
        xs_ref, wa_ref, wm_ref, wo_ref, g_ref, *rest = rest
        residual = xs_ref[...]
    else:
        xs_ref, tail_ref, wa_ref, wm_ref, wo_ref, g_ref, *rest = rest
        residual = _split_rows(pl.program_id(0), n_body_tiles, xs_ref, tail_ref)
    if with_router:
        wr_hi_ref, wr_lo_ref, br_ref, xs_out, h_out, route_out, counts_out = rest
    else:
        xs_out, h_out = rest
    a = _dot(attn_ref[...], wa_ref[...])
    m = _dot(ml_ref[...], wm_ref[...])
    ga = jnp.concatenate([ga0_ref[...], ga1_ref[...]], axis=1)
    gm = jnp.concatenate([gm0_ref[...], gm1_ref[...]], axis=1)
    y = _sigmoid(ga.astype(F32)) * a + _sigmoid(gm.astype(F32)) * m
    xs_new = residual + _dot(y.astype(BF16), wo_ref[...])
    xs_out[...] = xs_new
    hn = _rms(xs_new, g_ref[...])
    if not with_router:
        h_out[...] = hn.astype(h_out.dtype)
    else:
        half = D_MODEL // 2
        lo = lax.bitcast_convert_type(hn[:, :half].astype(BF16).astype(F32), jnp.uint32) >> 16
        hi = lax.bitcast_convert_type(hn[:, half:].astype(BF16).astype(F32), jnp.uint32) & jnp.uint32(0xFFFF0000)
        h_out[...] = lo | hi
        hn_hi = hn.astype(BF16)
        hn_lo = (hn - hn_hi.astype(F32)).astype(BF16)
        logits = (_dot(hn_hi, wr_hi_ref[...]) + (_dot(hn_lo, wr_hi_ref[...]) + _dot(hn_hi, wr_lo_ref[...]))
                  + br_ref[...])
        lane = lax.broadcasted_iota(jnp.int32, logits.shape, 1).astype(F32)
        l1 = logits.max(-1, keepdims=True)
        i1 = jnp.min(jnp.where(logits == l1, lane, float(LANES)), axis=-1, keepdims=True)
        rest_logits = jnp.where(lane == i1, NEG_INF, logits)
        l2 = rest_logits.max(-1, keepdims=True)
        i2 = jnp.min(jnp.where(rest_logits == l2, lane, float(LANES)), axis=-1, keepdims=True)
        e = jnp.exp(l2 - l1)
        w1 = 1.0 / (1.0 + e)
        w2 = e / (1.0 + e)
        bm = logits.shape[0]
        row = lax.broadcasted_iota(jnp.int32, (bm, 1), 0) + pl.program_id(0) * bm
        routed = row < n_routed
        pick1 = jnp.where(jnp.logical_and(lane == i1, routed), 1.0, 0.0)
        pick2 = jnp.where(jnp.logical_and(lane == i2, routed), 1.0, 0.0)
        picks = pick1 + pick2
        earlier = (lax.broadcasted_iota(jnp.int32, (bm, bm), 1) < lax.broadcasted_iota(jnp.int32, (bm, bm), 0))
        before = _dot(jnp.where(earlier, 1.0, 0.0).astype(BF16), picks.astype(BF16))
        r1 = jnp.sum(before * pick1, axis=-1, keepdims=True)
        r2 = jnp.sum(before * pick2, axis=-1, keepdims=True)
        route_out[...] = jnp.where(lane == 0, i1, jnp.where(lane == 1, i2, jnp.where(lane == 2, w1,
                                   jnp.where(lane == 3, w2, jnp.where(lane == 4, r1, jnp.where(lane == 5, r2, 0.0))))))
        counts_out[0] = jnp.broadcast_to(jnp.sum(picks, axis=0, keepdims=True), (SUBLANES, LANES))


def _merge(attn, ml, zb, xs, wa, wm, wo, g_next, router=None):
    m = zb.shape[0]
    bm = _tile(TAIL_ROWS, 256)
    row = lambda w, c: pl.BlockSpec((bm, w), lambda i: (i, c))
    const = lambda r, c: pl.BlockSpec((r, c), lambda i: (0, 0), pipeline_mode=pl.Buffered(1))
    if isinstance(xs, tuple):
        n_body_tiles = xs[0].shape[0] // bm
        xs_specs, xs_args = _split_specs(bm, n_body_tiles), list(xs)
    else:
        n_body_tiles = None
        xs_specs, xs_args = [row(D_MODEL, 0)], [xs]
    in_specs = [row(ATTN_WIDTH, 0), row(ML_V_WIDTH, 0),
                row(HALF_D, ZB_GA // HALF_D), row(HALF_D, ZB_GA // HALF_D + 1),
                row(HALF_D, ZB_GM // HALF_D), row(HALF_D, ZB_GM // HALF_D + 1),
                *xs_specs, const(ATTN_WIDTH, D_MODEL), const(ML_V_WIDTH, D_MODEL), const(D_MODEL, D_MODEL),
                const(1, D_MODEL)]
    args = [attn, ml, zb, zb, zb, zb, *xs_args, wa, wm, wo, g_next.reshape(1, D_MODEL)]
    if router is None:
        out_shape = [jax.ShapeDtypeStruct((m, D_MODEL), F32), jax.ShapeDtypeStruct((m, D_MODEL), BF16)]
        out_specs = [row(D_MODEL, 0), row(D_MODEL, 0)]
    else:
        out_shape = [jax.ShapeDtypeStruct((m, D_MODEL), F32), jax.ShapeDtypeStruct((m, D_MODEL // 2), jnp.uint32)]
        out_specs = [row(D_MODEL, 0), row(D_MODEL // 2, 0)]
    n_routed = None
    if router is not None:
        w_router, b_router, n_routed = router
        wr = jnp.zeros((D_MODEL, LANES), F32).at[:, :N_EXPERTS].set(w_router.astype(F32))
        br = jnp.full((1, LANES), NEG_INF, F32).at[0, :N_EXPERTS].set(b_router.astype(F32))
        wr_hi = wr.astype(BF16)
        wr_lo = (wr - wr_hi.astype(F32)).astype(BF16)
        in_specs += [const(D_MODEL, LANES), const(D_MODEL, LANES), const(1, LANES)]
        args += [wr_hi, wr_lo, br]
        out_shape += [jax.ShapeDtypeStruct((m, LANES), F32), jax.ShapeDtypeStruct((m // bm, SUBLANES, LANES), F32)]
        out_specs += [row(LANES, 0), pl.BlockSpec((1, SUBLANES, LANES), lambda i: (i, 0, 0))]
    return pl.pallas_call(
        functools.partial(_merge_kernel, n_routed, n_body_tiles),
        out_shape=out_shape,
        grid=(m // bm,),
        in_specs=in_specs,
        out_specs=out_specs,
        compiler_params=_params("parallel"),
        name="merge_out_proj",
    )(*args)


def _ffn_up_kernel(h_ref, wg_ref, wu_ref, act_ref):
    hb = h_ref[...]
    g = _dot(hb, wg_ref[...])
    u = _dot(hb, wu_ref[...])
    act_ref[...] = (g * _sigmoid(g) * u).astype(act_ref.dtype)


def _ffn_down_kernel(act_ref, xs_ref, wd_ref, g_ref, xs_out, h_out, rows_ref):
    j = pl.program_id(1)
    xs_new = xs_ref[...] + _dot(act_ref[...], wd_ref[...])
    xs_out[...] = xs_new
    rows_ref[j] = xs_new

    @pl.when(j == pl.num_programs(1) - 1)
    def _():
        full = jnp.concatenate([rows_ref[t] for t in range(rows_ref.shape[0])], axis=1)
        h_out[...] = _rms(full, g_ref[...]).astype(h_out.dtype)


def _dense_ffn(h, xs, wg, wu, wd, g_next):
    m = xs.shape[0]
    bm = _tile(m, 1024)
    bf = D_FF // FFN_UP_TILES
    act = pl.pallas_call(
        _ffn_up_kernel,
        out_shape=jax.ShapeDtypeStruct((m, D_FF), BF16),
        grid=(FFN_UP_TILES, m // bm),
        in_specs=[pl.BlockSpec((bm, D_MODEL), lambda f, i: (i, 0)),
                  pl.BlockSpec((D_MODEL, bf), lambda f, i: (0, f)),
                  pl.BlockSpec((D_MODEL, bf), lambda f, i: (0, f))],
        out_specs=pl.BlockSpec((bm, bf), lambda f, i: (i, f)),
        compiler_params=_params("parallel", "parallel"),
        name="dense_swiglu_up",
    )(h, wg, wu)
    bn = D_MODEL // FFN_DOWN_TILES
    return pl.pallas_call(
        _ffn_down_kernel,
        out_shape=[jax.ShapeDtypeStruct((m, D_MODEL), F32), jax.ShapeDtypeStruct((m, D_MODEL), BF16)],
        grid=(m // bm, FFN_DOWN_TILES),
        in_specs=[pl.BlockSpec((bm, D_FF), lambda i, j: (i, 0)),
                  pl.BlockSpec((bm, bn), lambda i, j: (i, j)),
                  pl.BlockSpec((D_FF, bn), lambda i, j: (0, j)),
                  pl.BlockSpec((1, D_MODEL), lambda i, j: (0, 0))],
        out_specs=[pl.BlockSpec((bm, bn), lambda i, j: (i, j)),
                   pl.BlockSpec((bm, D_MODEL), lambda i, j: (i, 0))],
        scratch_shapes=[pltpu.VMEM((FFN_DOWN_TILES, bm, bn), F32)],
        compiler_params=_params("parallel", "arbitrary"),
        name="dense_swiglu_down",
    )(act, xs, wd, g_next.reshape(1, D_MODEL))


def _row_copy(src_hbm, dst_vmem, sem, src_row, dst_row):
    return pltpu.make_async_copy(src_hbm.at[pl.ds(src_row, 1), :], dst_vmem.at[pl.ds(dst_row, 1), :], sem)


def _moe_gather_copy(h_hbm, gbuf, sem, src_row, dst_row):
    return pltpu.make_async_copy(h_hbm.at[pl.ds(src_row, 1), :], gbuf.at[pl.ds(dst_row, 1), :], sem)


def _moe_gather_wait(h_hbm, gbuf, sem):
    pltpu.make_async_copy(h_hbm.at[pl.ds(0, MOE_GATHER_ROWS), :], gbuf, sem).wait()


def _moe_kernel(sb_e_ref, sb_rows_ref, row_tok_ref, h_hbm, wg_ref, wu_ref, wd_ref, y_ref, gbuf, xb16, sem):
    sb = pl.program_id(0)
    f = pl.program_id(1)
    n_sb = pl.num_programs(0)
    nf = pl.num_programs(1)
    kind = sb_rows_ref[sb]
    half = D_MODEL // 2

    @pl.when(jnp.logical_and(sb == 0, f == 0))
    def _():
        def issue(r, c):
            _moe_gather_copy(h_hbm, gbuf, sem, row_tok_ref[r], r).start()
            return c
        lax.fori_loop(0, MOE_GATHER_ROWS, issue, 0)

    @pl.when(f == 0)
    def _():
        @pl.when(kind < MOE_PARTS)
        def _():
            y_ref[...] = jnp.zeros_like(y_ref)

        prev_kind = sb_rows_ref[jnp.maximum(sb - 1, 0)]

        @pl.when(jnp.logical_or(sb == 0, prev_kind > 0))
        def _():
            _moe_gather_wait(h_hbm, gbuf, sem)

        @pl.when(kind > 0)
        def _():
            w = gbuf[0:MOE_SB, :]
            xb16[:, :half] = lax.bitcast_convert_type(w << 16, F32).astype(BF16)
            xb16[:, half:] = lax.bitcast_convert_type(w & jnp.uint32(0xFFFF0000), F32).astype(BF16)

    def compute(row0, rows, prefetch, assign=False):
        if prefetch:
            base = (sb + 1) * MOE_SB + f * MOE_STEP_ROWS
            for r in range(MOE_STEP_ROWS):
                _moe_gather_copy(h_hbm, gbuf, sem, row_tok_ref[base + r], f * MOE_STEP_ROWS + r).start()
        xb = xb16[row0:row0 + rows, :]
        g = _dot(xb, wg_ref[0].astype(BF16))
        u = _dot(xb, wu_ref[0].astype(BF16))
        act = (g * _sigmoid(g) * u).astype(BF16)
        down = _dot(act, wd_ref[0].astype(BF16))
        if assign:
            y_ref[row0:row0 + rows, :] = down
        else:
            y_ref[row0:row0 + rows, :] += down

    @pl.when(jnp.logical_and(kind == MOE_PARTS, f == 0))
    def _():
        compute(0, MOE_SB, True, assign=True)

    @pl.when(jnp.logical_and(kind == MOE_PARTS, f > 0))
    def _():
        compute(0, MOE_SB, True)

    for part in range(MOE_PARTS - 1):
        @pl.when(jnp.logical_and(kind > part, kind < MOE_PARTS))
        def _():
            compute(part * MOE_PART_ROWS, MOE_PART_ROWS, part == 0)

    @pl.when(jnp.logical_and(jnp.logical_and(sb == n_sb - 1, f == nf - 1), kind > 0))
    def _():
        _moe_gather_wait(h_hbm, gbuf, sem)


def _moe_experts(h_packed, sb_e, sb_rows, row_tok, wg, wu, wd, n_sb, bf=MOE_FF_TILE):
    nf = D_FF // bf
    assert nf * MOE_STEP_ROWS == MOE_GATHER_ROWS
    ftile = lambda i, f, rows: jnp.where(rows[i] > 0, f, nf - 1)
    grid_spec = pltpu.PrefetchScalarGridSpec(
        num_scalar_prefetch=3,
        grid=(n_sb, nf),
        in_specs=[pl.BlockSpec(memory_space=pl.ANY),
                  pl.BlockSpec((1, D_MODEL, bf), lambda i, f, se, sr, rt: (se[i], 0, ftile(i, f, sr))),
                  pl.BlockSpec((1, D_MODEL, bf), lambda i, f, se, sr, rt: (se[i], 0, ftile(i, f, sr))),
                  pl.BlockSpec((1, bf, D_MODEL), lambda i, f, se, sr, rt: (se[i], ftile(i, f, sr), 0))],
        out_specs=pl.BlockSpec((MOE_SB, D_MODEL), lambda i, f, se, sr, rt: (i, 0)),
        scratch_shapes=[pltpu.VMEM((MOE_GATHER_ROWS, D_MODEL // 2), jnp.uint32),
                        pltpu.VMEM((MOE_SB, D_MODEL), BF16),
                        pltpu.SemaphoreType.DMA(())],
    )
    return pl.pallas_call(
        _moe_kernel,
        out_shape=jax.ShapeDtypeStruct((n_sb * MOE_SB, D_MODEL), F32),
        grid_spec=grid_spec,
        compiler_params=pltpu.CompilerParams(dimension_semantics=("arbitrary", "arbitrary"),
                                             vmem_limit_bytes=MOE_VMEM_LIMIT),
        name="moe_experts",
    )(sb_e, sb_rows, row_tok, h_packed, wg, wu, wd)


def _combine_kernel(pos_ref, xs_ref, route_ref, y_hbm, g_ref, o_ref, ybuf, sem):
    step = pl.program_id(0)
    n_steps = pl.num_programs(0)
    slot = step % 2

    rows = o_ref.shape[0]

    def gather(s, dst_slot):
        base = s * rows

        def issue(r, c):
            for k in range(TOP_K):
                _row_copy(y_hbm, ybuf.at[dst_slot, k], sem.at[dst_slot], pos_ref[TOP_K * (base + r) + k],
                          r).start(priority=k % 2)
            return c
        lax.fori_loop(0, rows, issue, 0, unroll=4)

    @pl.when(step == 0)
    def _():
        gather(step, slot)

    @pl.when(step + 1 < n_steps)
    def _():
        gather(step + 1, 1 - slot)

    for k in range(TOP_K):
        pltpu.make_async_copy(y_hbm.at[pl.ds(0, rows), :], ybuf.at[slot, k], sem.at[slot]).wait()
    route = route_ref[...]
    moe = ybuf[slot, 0] * route[:, 2:3] + ybuf[slot, 1] * route[:, 3:4]
    o_ref[...] = _rms(xs_ref[...] + moe, g_ref[...])


def _moe_combine(pos, xs, route, yb, g_final, m_real):
    rows = _tile(m_real, 512)
    grid_spec = pltpu.PrefetchScalarGridSpec(
        num_scalar_prefetch=1,
        grid=(m_real // rows,),
        in_specs=[pl.BlockSpec((rows, D_MODEL), lambda t, p: (t, 0)),
                  pl.BlockSpec((rows, LANES), lambda t, p: (t, 0)),
                  pl.BlockSpec(memory_space=pl.ANY),
                  pl.BlockSpec((1, D_MODEL), lambda t, p: (0, 0))],
        out_specs=pl.BlockSpec((rows, D_MODEL), lambda t, p: (t, 0)),
        scratch_shapes=[pltpu.VMEM((2, TOP_K, rows, D_MODEL), F32), pltpu.SemaphoreType.DMA((2,))],
    )
    return pl.pallas_call(
        _combine_kernel,
        out_shape=jax.ShapeDtypeStruct((m_real, D_MODEL), F32),
        grid_spec=grid_spec,
        compiler_params=_params("arbitrary"),
        name="moe_combine_final_norm",
    )(pos, xs, route, yb, g_final.reshape(1, D_MODEL))


def _moe_routing(route, tile_counts, n_tok):
    n_assign = n_tok * TOP_K
    n_sb = n_assign // MOE_SB + N_EXPERTS
    rows_per_tile = route.shape[0] // tile_counts.shape[0]
    e_flat = route[:n_tok, :TOP_K].astype(jnp.int32).reshape(n_assign)
    onehot = (e_flat[:, None] == jnp.arange(N_EXPERTS, dtype=jnp.int32)[None, :]).astype(jnp.int32)
    tile_counts = tile_counts[:, 0, :N_EXPERTS].astype(jnp.int32)
    tile_base = jnp.cumsum(tile_counts, axis=0) - tile_counts
    base = jnp.repeat(tile_base, rows_per_tile * TOP_K, axis=0)[:n_assign]
    rank = jnp.sum(base * onehot, axis=1) + route[:n_tok, 4:4 + TOP_K].astype(jnp.int32).reshape(n_assign)
    counts = jnp.sum(tile_counts, axis=0)
    sb_count = (counts + MOE_SB - 1) // MOE_SB
    sb_end = jnp.cumsum(sb_count)
    sb_start = sb_end - sb_count
    dest = (jnp.sum((sb_start * MOE_SB)[None, :] * onehot, axis=1) + rank).astype(jnp.int32)
    row_tok = jnp.zeros(((n_sb + 2) * MOE_SB,), jnp.int32).at[dest].set(
        jnp.arange(n_assign, dtype=jnp.int32) // TOP_K, unique_indices=True)
    sb = jnp.arange(n_sb, dtype=jnp.int32)
    sb_e = jnp.sum((sb[:, None] >= sb_end[None, :]).astype(jnp.int32), axis=1)
    valid = sb < sb_end[-1]
    last_e = jnp.sum((sb_end[-1] - 1 >= sb_end).astype(jnp.int32))
    sb_e = jnp.where(valid, sb_e, last_e).astype(jnp.int32)
    rows_here = counts[sb_e] - (sb - sb_start[sb_e]) * MOE_SB
    parts = (jnp.minimum(rows_here, MOE_SB) + MOE_PART_ROWS - 1) // MOE_PART_ROWS
    sb_rows = jnp.where(valid, parts, 0).astype(jnp.int32)
    return dest, row_tok, sb_e, sb_rows, n_sb


def _split_w_in(w):
    n_gates = 2 * ML_HEADS
    assert w.shape[1] == ZA_WIDTH + n_gates + ZB_WIDTH
    wt = jnp.swapaxes(w, 0, 1)
    part_a = wt[:W_IN_GATES].astype(BF16)
    part_b = wt[W_IN_GATES + n_gates:].astype(BF16)
    gates = jnp.pad(wt[W_IN_GATES:W_IN_GATES + n_gates], ((0, LANES - n_gates), (0, 0))).astype(BF16)
    return part_a, part_b, gates


def kernel(x, meta_tokens, rel_bias_table, w_in, attn_sinks, conv_w, conv_b, igate_b, fgate_b, mlstm_norm_g,
           w_attn_up, w_mlstm_up, w_out, norm_mix_g, norm_ffn_g, w_ffn_gate, w_ffn_up, w_ffn_down, w_router,
           b_router, w_moe_gate, w_moe_up, w_moe_down, final_norm_g):
    b, seq, _ = x.shape
    depth = w_in.shape[0]
    m_real = b * seq
    m = m_real + TAIL_ROWS
    assert depth == 2 and seq % ML_CHUNK == 0 and seq % BLOCK == 0 and m_real % TAIL_ROWS == 0
    tail = jnp.concatenate([jnp.zeros((PAD, D_MODEL), x.dtype), meta_tokens.astype(x.dtype),
                            jnp.zeros((TAIL_ROWS - PREFIX, D_MODEL), x.dtype)], axis=0)
    xs = (x.reshape(m_real, D_MODEL), tail)
    band, meta = _attn_bias(rel_bias_table, seq // BLOCK + 1)
    h = _rmsnorm(*xs, norm_mix_g[0])
    out = None
    for layer in range(depth):
        w_a, w_b, w_gates = _split_w_in(w_in[layer])
        za = _matmul_nt(h, w_a, BF16, 1024, ZA_WIDTH // 2, "in_proj_a")
        zb = _matmul_nt(h, w_b, BF16, 1024, ZB_WIDTH // 2, "in_proj_b")
        attn = _attention(za, band, meta, attn_sinks[layer], b, seq)
        ml = _mlstm(za, zb, h, w_gates, conv_w[layer], conv_b[layer], igate_b[layer], fgate_b[layer],
                    mlstm_norm_g[layer], b, seq)
        wa, wm, wo = (w_attn_up[layer].astype(BF16), w_mlstm_up[layer].astype(BF16), w_out[layer].astype(BF16))
        i = layer // 2
        if layer % 2 == 0:
            xs, h = _merge(attn, ml, zb, xs, wa, wm, wo, norm_ffn_g[layer])
            xs, h = _dense_ffn(h, xs, w_ffn_gate[i].astype(BF16), w_ffn_up[i].astype(BF16),
                               w_ffn_down[i].astype(BF16), norm_mix_g[layer + 1])
        else:
            n_tok = m_real + PREFIX
            xs, h_packed, route, tile_counts = _merge(attn, ml, zb, xs, wa, wm, wo, norm_ffn_g[layer],
                                                      router=(w_router[i], b_router[i], n_tok))
            dest, row_tok, sb_e, sb_rows, n_sb = _moe_routing(route, tile_counts, n_tok)
            yb = _moe_experts(h_packed, sb_e, sb_rows, row_tok, w_moe_gate[i], w_moe_up[i], w_moe_down[i], n_sb)
            out = _moe_combine(dest, xs, route, yb, final_norm_g, m_real).reshape(b, seq, D_MODEL)
    return out
```
